```python
import math
import numpy as np
import jax
import jax.numpy as jnp
from jax import lax

D_MODEL = 1024
BATCH = 16
SEQ = 4096
DEPTH = 2

GRID_W = 64
CTX_LEN = 256
EPS = 1e-6

GDN_HEADS = 4
GDN_DK = 64
GDN_DV = 64
GDN_CONV = 5
GDN_CHUNK = 64
GDN_QK = GDN_HEADS * GDN_DK
GDN_W = GDN_HEADS * GDN_DV
GDN_QKV = 2 * GDN_QK + GDN_W

MLA_HEADS = 8
MLA_Q_RANK = 384
MLA_KV_RANK = 256
MLA_NOPE = 64
MLA_ROPE = 32
MLA_DV = 64
MLA_W = MLA_HEADS * MLA_DV
MLA_SCALE = (MLA_NOPE + MLA_ROPE) ** -0.5
ROPE_THETA = 10000.0
Q_BLOCK = 128

GLA_HEADS = 4
GLA_DK = 32
GLA_DV = 64
GLA_QK = GLA_HEADS * GLA_DK
GLA_W = GLA_HEADS * GLA_DV
GLA_GATE_RANK = 16
GLA_GATE_NORM = 16.0
GLA_CHUNK = 64

MIX_W = GDN_W + MLA_W + GLA_W
IN_SPLITS = (GDN_QKV, GDN_W, 2 * GDN_HEADS, 2 * GDN_HEADS,
             MLA_Q_RANK, MLA_KV_RANK, MLA_ROPE, MLA_W,
             GLA_QK, GLA_QK, GLA_W, GLA_W, 2 * GLA_GATE_RANK)
N_IN = sum(IN_SPLITS)

kernel_name = 'hybrid_gdn_mla_gla_prefix_dit'


def rmsnorm(x, w=None):
    xf = x.astype(jnp.float32)
    y = xf * lax.rsqrt(jnp.mean(xf * xf, axis=-1, keepdims=True) + EPS)
    if w is not None:
        y = y * w.astype(jnp.float32)
    return y.astype(x.dtype)


def l2norm(x):
    xf = x.astype(jnp.float32)
    return (xf * lax.rsqrt(jnp.sum(xf * xf, axis=-1, keepdims=True) + EPS)).astype(x.dtype)


def split_cols(p, sizes):
    return jnp.split(p, np.cumsum(sizes)[:-1].tolist(), axis=-1)


def dwconv_centred(x, w):
    k, ch = w.shape
    pad = (k - 1) // 2
    return lax.conv_general_dilated(x, w[:, None, :].astype(x.dtype), window_strides=(1,),
                                    padding=[(pad, pad)], dimension_numbers=('NWC', 'WIO', 'NWC'),
                                    feature_group_count=ch)


def axial_rope(n_tokens):
    rows = n_tokens // GRID_W
    row = jnp.broadcast_to(jnp.arange(rows, dtype=jnp.float32)[:, None], (rows, GRID_W)).reshape(-1)
    col = jnp.broadcast_to(jnp.arange(GRID_W, dtype=jnp.float32)[None, :], (rows, GRID_W)).reshape(-1)
    n_freq = MLA_ROPE // 4
    inv = ROPE_THETA ** (-jnp.arange(n_freq, dtype=jnp.float32) / n_freq)
    ang = jnp.concatenate([row[:, None] * inv, col[:, None] * inv], axis=-1)
    return jnp.cos(ang), jnp.sin(ang)


def apply_rope(x, cos, sin):
    x1, x2 = jnp.split(x.astype(jnp.float32), 2, axis=-1)
    return jnp.concatenate([x1 * cos - x2 * sin, x2 * cos + x1 * sin], axis=-1).astype(x.dtype)


def to_chunks(a, size):
    b, t, h = a.shape[:3]
    a = a.reshape((b, t // size, size, h) + a.shape[3:])
    return jnp.moveaxis(a, 3, 1)


def from_chunks(a):
    a = jnp.moveaxis(a, 1, 3)
    return a.reshape((a.shape[0], -1) + a.shape[3:])


def gdn_chunk_scan(q, k, v, g, beta, s0):
    out_dtype = v.dtype
    q, k, v, g, beta = (to_chunks(a.astype(jnp.float32), GDN_CHUNK) for a in (q, k, v, g, beta))
    size = GDN_CHUNK
    incl = jnp.tril(jnp.ones((size, size), bool))
    strict = jnp.tril(jnp.ones((size, size), bool), -1)
    gc = jnp.cumsum(g, axis=-1)
    decay = jnp.where(incl, jnp.exp(jnp.where(incl, gc[..., :, None] - gc[..., None, :], 0.0)), 0.0)
    kb = k * beta[..., None]
    low = jnp.where(strict, jnp.einsum('bhntd,bhnsd->bhnts', kb, k) * decay, 0.0)
    eye = jnp.eye(size, dtype=jnp.float32)
    t_inv = lax.linalg.triangular_solve(eye + low, jnp.broadcast_to(eye, low.shape), left_side=True, lower=True)
    u = t_inv @ (v * beta[..., None])
    w = t_inv @ (kb * jnp.exp(gc)[..., None])
    intra = jnp.where(incl, jnp.einsum('bhntd,bhnsd->bhnts', q, k) * decay, 0.0)

    def step(state, xs):
        qi, ki, ui, wi, gi, ai = xs
        v_new = ui - wi @ state
        o = (qi * jnp.exp(gi)[..., None]) @ state + ai @ v_new
        g_last = gi[..., -1:]
        state = state * jnp.exp(g_last)[..., None] + jnp.einsum(
            'bhcd,bhce->bhde', ki * jnp.exp(g_last - gi)[..., None], v_new)
        return state, o

    xs = tuple(jnp.moveaxis(a, 2, 0) for a in (q, k, u, w, gc, intra))
    state, o = lax.scan(step, s0.astype(jnp.float32), xs)
    return from_chunks(jnp.moveaxis(o, 0, 2)).astype(out_dtype), state


def gla_chunk_scan(q, k, v, la, s0):
    out_dtype = v.dtype
    q, k, v, la = (to_chunks(a.astype(jnp.float32), GLA_CHUNK) for a in (q, k, v, la))
    incl = jnp.tril(jnp.ones((GLA_CHUNK, GLA_CHUNK), bool))[..., None]
    cum = jnp.cumsum(la, axis=3)

    def step(state, xs):
        qi, ki, vi, bi = xs
        diff = bi[:, :, :, None, :] - bi[:, :, None, :, :]
        dec = jnp.where(incl, jnp.exp(jnp.where(incl, diff, 0.0)), 0.0)
        scores = jnp.einsum('bhtd,bhsd,bhtsd->bhts', qi, ki, dec)
        o = (qi * jnp.exp(bi)) @ state + scores @ vi
        b_last = bi[:, :, -1:]
        state = state * jnp.exp(b_last[:, :, 0])[..., None] + jnp.einsum(
            'bhcd,bhce->bhde', ki * jnp.exp(b_last - bi), vi)
        return state, o

    xs = tuple(jnp.moveaxis(a, 2, 0) for a in (q, k, v, cum))
    state, o = lax.scan(step, s0.astype(jnp.float32), xs)
    return from_chunks(jnp.moveaxis(o, 0, 2)).astype(out_dtype), state


def bidirectional_scan(scan_fn, shared_c, shared_l, dirs_c, dirs_l, s0):
    outs_c, outs_l = [], []
    for d in range(2):
        flip = (lambda a: a[:, ::-1]) if d == 1 else (lambda a: a)
        o_c, s_ctx = scan_fn(*[flip(a) for a in shared_c + dirs_c[d]], s0)
        o_l, _ = scan_fn(*[flip(a) for a in shared_l + dirs_l[d]], s_ctx)
        outs_c.append(flip(o_c))
        outs_l.append(flip(o_l))
    return outs_c[0] + outs_c[1], outs_l[0] + outs_l[1]


def gdn_branch(parts_c, parts_l, conv_w, a_log, dt_bias, norm_w):
    def prep(qkv, a, b):
        bsz, t = qkv.shape[:2]
        h = jax.nn.silu(dwconv_centred(qkv, conv_w))
        q, k, v = jnp.split(h, [GDN_QK, 2 * GDN_QK], axis=-1)
        q = l2norm(q.reshape(bsz, t, GDN_HEADS, GDN_DK)) * GDN_DK ** -0.5
        k = l2norm(k.reshape(bsz, t, GDN_HEADS, GDN_DK))
        v = v.reshape(bsz, t, GDN_HEADS, GDN_DV)
        g = -jnp.exp(a_log.astype(jnp.float32)) * jax.nn.softplus(
            a.reshape(bsz, t, 2, GDN_HEADS).astype(jnp.float32) + dt_bias.astype(jnp.float32))
        beta = jax.nn.sigmoid(b.reshape(bsz, t, 2, GDN_HEADS).astype(jnp.float32))
        return (q, k, v), tuple((g[:, :, d], beta[:, :, d]) for d in range(2))

    qkv_c, z_c, a_c, b_c = parts_c
    qkv_l, z_l, a_l, b_l = parts_l
    shared_c, dirs_c = prep(qkv_c, a_c, b_c)
    shared_l, dirs_l = prep(qkv_l, a_l, b_l)
    s0 = jnp.zeros((qkv_l.shape[0], GDN_HEADS, GDN_DK, GDN_DV), jnp.float32)
    o_c, o_l = bidirectional_scan(gdn_chunk_scan, shared_c, shared_l, dirs_c, dirs_l, s0)

    def gate(o, z):
        return rmsnorm(o, norm_w).reshape(z.shape) * jax.nn.silu(z)
    return gate(o_c, z_c), gate(o_l, z_l)


def attend(q, k, v):
    s = jnp.einsum('bqhd,bkhd->bhqk', q, k).astype(jnp.float32) * MLA_SCALE
    p = jax.nn.softmax(s, axis=-1).astype(v.dtype)
    return jnp.einsum('bhqk,bkhd->bqhd', p, v)


def blocked_attention(q, k, v):
    bsz, t, h, dq = q.shape
    qb = jnp.moveaxis(q.reshape(bsz, t // Q_BLOCK, Q_BLOCK, h, dq), 1, 0)
    ob = lax.map(lambda qi: attend(qi, k, v), qb)
    return jnp.moveaxis(ob, 0, 1).reshape(bsz, t, h, -1)


def mla_q(cq, q_norm_w, w_uq, rope):
    bsz, t = cq.shape[:2]
    q = (rmsnorm(cq, q_norm_w) @ w_uq).reshape(bsz, t, MLA_HEADS, MLA_NOPE + MLA_ROPE)
    q_nope, q_pe = jnp.split(q, [MLA_NOPE], axis=-1)
    if rope is not None:
        cos, sin = rope
        q_pe = apply_rope(q_pe, cos[:, None], sin[:, None])
    return jnp.concatenate([q_nope, q_pe], axis=-1)


def mla_kv(ckv, k_rope, kv_norm_w, w_ukv, rope):
    bsz, t = ckv.shape[:2]
    kv = (rmsnorm(ckv, kv_norm_w) @ w_ukv).reshape(bsz, t, MLA_HEADS, MLA_NOPE + MLA_DV)
    k_nope, v = jnp.split(kv, [MLA_NOPE], axis=-1)
    if rope is not None:
        cos, sin = rope
        k_rope = apply_rope(k_rope, cos, sin)
    k_pe = jnp.broadcast_to(k_rope[:, :, None, :], (bsz, t, MLA_HEADS, MLA_ROPE))
    return jnp.concatenate([k_nope, k_pe], axis=-1), v


def mla_branch(parts_c, parts_l, q_norm_w, w_uq, kv_norm_w, w_ukv, rope, need_ctx_out):
    cq_c, ckv_c, kr_c, z_c = parts_c
    cq_l, ckv_l, kr_l, z_l = parts_l
    k_c, v_c = mla_kv(ckv_c, kr_c, kv_norm_w, w_ukv, None)
    k_l, v_l = mla_kv(ckv_l, kr_l, kv_norm_w, w_ukv, rope)
    q_l = mla_q(cq_l, q_norm_w, w_uq, rope)
    o_l = blocked_attention(q_l, jnp.concatenate([k_c, k_l], axis=1), jnp.concatenate([v_c, v_l], axis=1))
    out_l = o_l.reshape(z_l.shape) * jax.nn.silu(z_l)
    if not need_ctx_out:
        return None, out_l
    o_c = attend(mla_q(cq_c, q_norm_w, w_uq, None), k_c, v_c)
    return o_c.reshape(z_c.shape) * jax.nn.silu(z_c), out_l


def gla_branch(parts_c, parts_l, w_gk, b_gk, norm_w):
    def prep(q, k, v, g_low):
        bsz, t = q.shape[:2]
        q = q.reshape(bsz, t, GLA_HEADS, GLA_DK) * GLA_DK ** -0.5
        k = k.reshape(bsz, t, GLA_HEADS, GLA_DK)
        v = v.reshape(bsz, t, GLA_HEADS, GLA_DV)
        g_low = g_low.reshape(bsz, t, 2, GLA_GATE_RANK)
        gk = jnp.einsum('btgr,grk->btgk', g_low, w_gk) + b_gk
        la = (jax.nn.log_sigmoid(gk.astype(jnp.float32)) / GLA_GATE_NORM).reshape(bsz, t, 2, GLA_HEADS, GLA_DK)
        return (q, k, v), tuple((la[:, :, d],) for d in range(2))

    q_c, k_c, v_c, z_c, g_c = parts_c
    q_l, k_l, v_l, z_l, g_l = parts_l
    shared_c, dirs_c = prep(q_c, k_c, v_c, g_c)
    shared_l, dirs_l = prep(q_l, k_l, v_l, g_l)
    s0 = jnp.zeros((q_l.shape[0], GLA_HEADS, GLA_DK, GLA_DV), jnp.float32)
    o_c, o_l = bidirectional_scan(gla_chunk_scan, shared_c, shared_l, dirs_c, dirs_l, s0)

    def gate(o, z):
        return rmsnorm(o, norm_w).reshape(z.shape) * jax.nn.silu(z)
    return gate(o_c, z_c), gate(o_l, z_l)


def hybrid_layer(x_l, x_c, mod_l, mod_c, w_in, conv_w, a_log, dt_bias, gdn_norm_w, q_norm_w, w_uq,
                 kv_norm_w, w_ukv, w_gk, b_gk, gla_norm_w, w_out, rope, update_ctx):
    shift_l, scale_l, gate_l = jnp.split(mod_l[:, None, :], 3, axis=-1)
    shift_c, scale_c, gate_c = jnp.split(mod_c, 3, axis=-1)
    h_l = rmsnorm(x_l) * (1 + scale_l) + shift_l
    h_c = rmsnorm(x_c) * (1 + scale_c) + shift_c
    p_l = split_cols(h_l @ w_in, IN_SPLITS)
    p_c = split_cols(h_c @ w_in, IN_SPLITS)
    gdn_c, gdn_l = gdn_branch(tuple(p_c[0:4]), tuple(p_l[0:4]), conv_w, a_log, dt_bias, gdn_norm_w)
    mla_c, mla_l = mla_branch(tuple(p_c[4:8]), tuple(p_l[4:8]), q_norm_w, w_uq, kv_norm_w, w_ukv, rope, update_ctx)
    gla_c, gla_l = gla_branch(tuple(p_c[8:13]), tuple(p_l[8:13]), w_gk, b_gk, gla_norm_w)
    x_l = x_l + gate_l * (jnp.concatenate([gdn_l, mla_l, gla_l], axis=-1) @ w_out)
    if update_ctx:
        x_c = x_c + gate_c * (jnp.concatenate([gdn_c, mla_c, gla_c], axis=-1) @ w_out)
    return x_l, x_c


def setup_inputs(seed: int = 0) -> dict:
    key = jax.random.key(seed)
    ks = jax.random.split(key, 20)
    f32 = jnp.float32
    L = DEPTH

    def nrm(k, shape, s):
        return jax.random.normal(k, shape, f32) * s

    def gain(k, shape):
        return 1.0 + 0.02 * jax.random.normal(k, shape, f32)

    dt = jnp.exp(jax.random.uniform(ks[8], (L, 2, GDN_HEADS), f32, math.log(1e-3), math.log(1e-1)))
    return {
        'x': nrm(ks[0], (BATCH, SEQ, D_MODEL), 1.0),
        'c': nrm(ks[1], (BATCH, D_MODEL), 1.0),
        'ctx': nrm(ks[2], (BATCH, CTX_LEN, D_MODEL), 1.0),
        'c_ctx': nrm(ks[3], (D_MODEL,), 1.0),
        'w_ada': nrm(ks[4], (L, D_MODEL, 3 * D_MODEL), 0.5 * D_MODEL ** -0.5),
        'b_ada': nrm(ks[5], (L, 3 * D_MODEL), 0.02),
        'w_in': nrm(ks[6], (L, D_MODEL, N_IN), D_MODEL ** -0.5),
        'gdn_conv_w': nrm(ks[7], (L, GDN_CONV, GDN_QKV), GDN_CONV ** -0.5),
        'gdn_a_log': jnp.log(jax.random.uniform(ks[9], (L, 2, GDN_HEADS), f32, 1.0, 16.0)),
        'gdn_dt_bias': dt + jnp.log(-jnp.expm1(-dt)),
        'gdn_norm_w': gain(ks[10], (L, GDN_DV)),
        'mla_q_norm_w': gain(ks[11], (L, MLA_Q_RANK)),
        'mla_w_uq': nrm(ks[12], (L, MLA_Q_RANK, MLA_HEADS * (MLA_NOPE + MLA_ROPE)), MLA_Q_RANK ** -0.5),
        'mla_kv_norm_w': gain(ks[13], (L, MLA_KV_RANK)),
        'mla_w_ukv': nrm(ks[14], (L, MLA_KV_RANK, MLA_HEADS * (MLA_NOPE + MLA_DV)), MLA_KV_RANK ** -0.5),
        'gla_w_gk': nrm(ks[15], (L, 2, GLA_GATE_RANK, GLA_QK), GLA_GATE_RANK ** -0.5),
        'gla_b_gk': nrm(ks[16], (L, 2, GLA_QK), 0.1),
        'gla_norm_w': gain(ks[17], (L, GLA_DV)),
        'w_out': nrm(ks[18], (L, MIX_W, D_MODEL), MIX_W ** -0.5),
        'final_norm_w': gain(ks[19], (D_MODEL,)),
    }


def reference(x, c, ctx, c_ctx, w_ada, b_ada, w_in, gdn_conv_w, gdn_a_log, gdn_dt_bias, gdn_norm_w,
              mla_q_norm_w, mla_w_uq, mla_kv_norm_w, mla_w_ukv, gla_w_gk, gla_b_gk, gla_norm_w,
              w_out, final_norm_w):
    rope = axial_rope(x.shape[1])
    x_l, x_c = x, ctx
    for layer in range(DEPTH):
        mod_l = jax.nn.silu(c) @ w_ada[layer] + b_ada[layer]
        mod_c = jax.nn.silu(c_ctx) @ w_ada[layer] + b_ada[layer]
        x_l, x_c = hybrid_layer(x_l, x_c, mod_l, mod_c, w_in[layer], gdn_conv_w[layer], gdn_a_log[layer],
                                gdn_dt_bias[layer], gdn_norm_w[layer], mla_q_norm_w[layer], mla_w_uq[layer],
                                mla_kv_norm_w[layer], mla_w_ukv[layer], gla_w_gk[layer], gla_b_gk[layer],
                                gla_norm_w[layer], w_out[layer], rope, layer < DEPTH - 1)
    return rmsnorm(x_l, final_norm_w)
```

```python
import functools

import numpy as np
import jax
import jax.numpy as jnp
from jax import lax
from jax.experimental import pallas as pl
from jax.experimental.pallas import tpu as pltpu

F32 = jnp.float32
BF16 = jnp.bfloat16
EPS = 1e-6

GDN_HEADS, GDN_DK, GDN_DV, GDN_CONV = 4, 64, 64, 5
GDN_QK = GDN_HEADS * GDN_DK
GDN_W = GDN_HEADS * GDN_DV
GDN_QKV = 2 * GDN_QK + GDN_W
MLA_HEADS, MLA_Q_RANK, MLA_KV_RANK = 8, 384, 256
MLA_NOPE, MLA_ROPE, MLA_DV = 64, 32, 64
MLA_W = MLA_HEADS * MLA_DV
MLA_SCALE = (MLA_NOPE + MLA_ROPE) ** -0.5
ROPE_THETA = 10000.0
ROPE_GRID_W = 64
GLA_HEADS, GLA_DK, GLA_DV = 4, 32, 64
GLA_QK = GLA_HEADS * GLA_DK
GLA_W = GLA_HEADS * GLA_DV
GLA_GATE_RANK = 16
GLA_GATE_NORM = 16.0
CHUNK = 64
GLA_LEVELS = (32, 16, 8, 4, 2, 1)

LANES = 128
ROWS = 256
HALO = 8
VMEM_LIMIT = 56 * 1024 * 1024

_O_GDN_QKV, _O_GDN_Z, _O_A, _O_B = 0, 768, 1024, 1032
_O_CQ, _O_CKV, _O_KR, _O_MLA_Z = 1040, 1424, 1680, 1712
_O_GLA_Q, _O_GLA_K, _O_GLA_V, _O_GLA_Z, _O_GLOW = 2224, 2352, 2480, 2736, 2992
_S_A, _S_B, _S_GLOW = 0, 8, 16
_W_GDN, _W_MLA, _W_MLAZ, _W_GLA = 1024, 640, 512, 768
_N_IN_PAD = _W_GDN + _W_MLA + _W_MLAZ + _W_GLA + 2 * LANES


def _dot(a, b):
    return jnp.dot(a, b, preferred_element_type=F32)


def _dot_nt(a, b):
    return lax.dot_general(a, b, (((1,), (1,)), ((), ())), preferred_element_type=F32)


def _dot_tn(a, b):
    return lax.dot_general(a, b, (((0,), (0,)), ((), ())), preferred_element_type=F32)


def _bdot(a, b):
    return _dot(a.astype(BF16), b.astype(BF16))


def _split3(x):
    x1 = x.astype(BF16)
    r1 = x - x1.astype(F32)
    x2 = r1.astype(BF16)
    x3 = (r1 - x2.astype(F32)).astype(BF16)
    return x1, x2, x3


def _sel_dot(m01, x):
    n = x.shape[1]
    y = _dot(m01, jnp.concatenate(_split3(x), axis=1))
    return y[:, :n] + y[:, n:2 * n] + y[:, 2 * n:]


def _dot_sel(x, m01):
    n = x.shape[0]
    y = _dot(jnp.concatenate(_split3(x), axis=0), m01)
    return y[:n] + y[n:2 * n] + y[2 * n:]


def _dot_sel_nt(x, m01):
    n = x.shape[0]
    y = _dot_nt(jnp.concatenate(_split3(x), axis=0), m01)
    return y[:n] + y[n:2 * n] + y[2 * n:]


def _softplus(x):
    return jnp.maximum(x, 0.0) + jnp.log1p(jnp.exp(-jnp.abs(x)))


def _silu(x):
    return x * jax.nn.sigmoid(x)


def _params(*sem):
    return pltpu.CompilerParams(dimension_semantics=sem, vmem_limit_bytes=VMEM_LIMIT)


def _scan_consts(rev):
    t = np.arange(ROWS)
    ch = t // CHUNK
    p = (CHUNK - 1 - t % CHUNK) if rev else (t % CHUNK)
    same = ch[:, None] == ch[None, :]
    tri = same & (p[None, :] <= p[:, None])
    mq, mk = [], []
    lv = np.full((ROWS, ROWS), -1.0, np.float32)
    for li, s in enumerate(GLA_LEVELS):
        blk = p // s
        mq.append(same & (p[None, :] > (blk * s)[:, None]) & (p[None, :] <= p[:, None]))
        mk.append(same & (p[None, :] > p[:, None]) & (p[None, :] <= ((blk + 1) * s)[:, None]))
        pair = same & ((blk % 2) == 1)[:, None] & (blk[None, :] == (blk - 1)[:, None])
        lv[pair] = li
    return dict(
        tri=jnp.asarray(tri, BF16), ones=jnp.asarray(same, BF16),
        incl=jnp.asarray(tri, F32),
        mq=jnp.asarray(np.stack(mq), BF16), mk=jnp.asarray(np.stack(mk), BF16),
        lv=jnp.asarray(lv))


def _head_block_ones(n, width):
    i = np.arange(n)
    return jnp.asarray((i[:, None] // width) == (i[None, :] // width), BF16)


def _inproj_perm():
    perm = np.full((_N_IN_PAD,), -1, np.int64)
    pos = 0

    def put(src, n, at):
        perm[at:at + n] = np.arange(src, src + n)

    put(_O_GDN_QKV, GDN_QKV, 0)
    put(_O_GDN_Z, GDN_W, GDN_QKV)
    pos = _W_GDN
    put(_O_CQ, MLA_Q_RANK, pos)
    put(_O_CKV, MLA_KV_RANK, pos + MLA_Q_RANK)
    pos += _W_MLA
    put(_O_MLA_Z, MLA_W, pos)
    pos += _W_MLAZ
    put(_O_GLA_Q, GLA_QK, pos)
    put(_O_GLA_K, GLA_QK, pos + GLA_QK)
    put(_O_GLA_V, GLA_W, pos + 2 * GLA_QK)
    put(_O_GLA_Z, GLA_W, pos + 2 * GLA_QK + GLA_W)
    pos += _W_GLA
    put(_O_A, 2 * GDN_HEADS, pos + _S_A)
    put(_O_B, 2 * GDN_HEADS, pos + _S_B)
    put(_O_GLOW, 2 * GLA_GATE_RANK, pos + _S_GLOW)
    pos += LANES
    put(_O_KR, MLA_ROPE, pos + MLA_NOPE)
    return perm


def _take_cols(w, perm):
    cols = jnp.take(w, jnp.asarray(np.maximum(perm, 0)), axis=1)
    return jnp.where(jnp.asarray(perm >= 0)[None, :], cols, 0.0)


def _mla_q_perm():
    perm = np.full((MLA_HEADS * LANES,), -1, np.int64)
    d = MLA_NOPE + MLA_ROPE
    for h in range(MLA_HEADS):
        perm[h * LANES:h * LANES + d] = np.arange(h * d, (h + 1) * d)
    return perm


def _mla_kv_perm():
    dk = MLA_NOPE + MLA_DV
    kperm = np.full((MLA_HEADS * LANES,), -1, np.int64)
    vperm = np.zeros((MLA_W,), np.int64)
    for h in range(MLA_HEADS):
        kperm[h * LANES:h * LANES + MLA_NOPE] = np.arange(h * dk, h * dk + MLA_NOPE)
        vperm[h * MLA_DV:(h + 1) * MLA_DV] = np.arange(h * dk + MLA_NOPE, (h + 1) * dk)
    return kperm, vperm


def _rope_tables(n_ctx, n_lat):
    rows = n_lat // ROPE_GRID_W
    row = np.repeat(np.arange(rows, dtype=np.float32), ROPE_GRID_W)
    col = np.tile(np.arange(ROPE_GRID_W, dtype=np.float32), rows)
    n_freq = MLA_ROPE // 4
    inv = (ROPE_THETA ** (-np.arange(n_freq, dtype=np.float32) / n_freq)).astype(np.float32)
    ang = np.concatenate([row[:, None] * inv, col[:, None] * inv], axis=-1)
    cos = np.concatenate([np.ones((n_ctx, 2 * n_freq), np.float32), np.cos(ang)], 0)
    sin = np.concatenate([np.zeros((n_ctx, 2 * n_freq), np.float32), np.sin(ang)], 0)
    n = n_ctx + n_lat
    half = MLA_ROPE // 2
    tab = np.zeros((3, n, LANES), np.float32)
    tab[0, :, :MLA_NOPE] = 1.0
    tab[0, :, MLA_NOPE:MLA_NOPE + half] = cos
    tab[0, :, MLA_NOPE + half:MLA_NOPE + MLA_ROPE] = cos
    tab[1, :, MLA_NOPE:MLA_NOPE + half] = -sin
    tab[2, :, MLA_NOPE + half:MLA_NOPE + MLA_ROPE] = sin
    return jnp.asarray(tab)


def _ada_kernel(c_ref, w_ref, b_ref, o_ref):
    c = _silu(c_ref[...]).astype(BF16)
    o_ref[0] = _dot(c, w_ref[0].astype(BF16)) + b_ref[0]


def _ada(cc, w_ada, b_ada):
    nl, d, n3 = w_ada.shape
    r = cc.shape[0]
    tn = 1024
    return pl.pallas_call(
        _ada_kernel,
        grid=(nl, n3 // tn),
        in_specs=[pl.BlockSpec((r, d), lambda l, j: (0, 0)),
                  pl.BlockSpec((1, d, tn), lambda l, j: (l, 0, j)),
                  pl.BlockSpec((1, 1, tn), lambda l, j: (l, 0, j))],
        out_specs=pl.BlockSpec((1, r, tn), lambda l, j: (l, 0, j)),
        out_shape=jax.ShapeDtypeStruct((nl, r, n3), F32),
        compiler_params=_params("arbitrary", "arbitrary"),
        name="ada_mod",
    )(cc, w_ada, b_ada.reshape(nl, 1, n3))


def _inproj_kernel(x_ref, mod_ref, w_ref, wabt_ref, ogdn, omla, omlaz, ogla, osmall, okr, oabt):
    x = x_ref[0]
    h = x * lax.rsqrt(jnp.mean(x * x, axis=-1, keepdims=True) + EPS)
    h = h * (1.0 + mod_ref[0, 1:2, :]) + mod_ref[0, 0:1, :]
    hb = h.astype(BF16)
    pos = 0
    for ref, n in ((ogdn, _W_GDN), (omla, _W_MLA), (omlaz, _W_MLAZ), (ogla, _W_GLA),
                   (osmall, LANES), (okr, LANES)):
        ref[0] = _dot(hb, w_ref[:, pos:pos + n])
        pos += n
    oabt[0] = _dot_nt(wabt_ref[...], hb)


def _inproj(x_all, mod, w_p, wabt, nct):
    b, ta, d = x_all.shape
    nt = ta // ROWS
    widths = (_W_GDN, _W_MLA, _W_MLAZ, _W_GLA, LANES, LANES)
    row = lambda bi, j: (bi, j, 0)
    return pl.pallas_call(
        _inproj_kernel,
        grid=(b, nt),
        in_specs=[pl.BlockSpec((1, ROWS, d), row),
                  pl.BlockSpec((1, 3, d), lambda bi, j: (jnp.where(j < nct, 0, 1 + bi), 0, 0)),
                  pl.BlockSpec(w_p.shape, lambda bi, j: (0, 0)),
                  pl.BlockSpec(wabt.shape, lambda bi, j: (0, 0))],
        out_specs=[pl.BlockSpec((1, ROWS, n), row) for n in widths]
        + [pl.BlockSpec((1, 4 * GDN_HEADS, ROWS), lambda bi, j: (bi, 0, j))],
        out_shape=[jax.ShapeDtypeStruct((b, ta, n), F32) for n in widths]
        + [jax.ShapeDtypeStruct((b, 4 * GDN_HEADS, ta), F32)],
        compiler_params=_params("arbitrary", "arbitrary"),
        name="in_proj",
    )(x_all, mod, w_p, wabt)


def _scan_block_index(j, nct, nt, rev):
    if not rev:
        return j
    return jnp.where(j < nct, nct - 1 - j, nt - 1 - (j - nct))


def _head_lane_mask(n, width, h):
    lane = lax.broadcasted_iota(jnp.int32, (1, n), 1)
    return (lane >= h * width) & (lane < (h + 1) * width)


def _gdn_kernel(rev, final, nct, *refs):
    if final:
        (x_ref, prev_ref, next_ref, small_ref, abt_ref, convw_ref, prow_ref, pcol_ref,
         tri_ref, ones_ref, incl_ref, lv_ref, exp_ref, ob_ref, oprev_ref, normw_ref,
         o_ref, xe_scr, s_scr) = refs
    else:
        (x_ref, prev_ref, next_ref, small_ref, abt_ref, convw_ref, prow_ref, pcol_ref,
         tri_ref, ones_ref, incl_ref, lv_ref, exp_ref, ob_ref,
         o_ref, xe_scr, s_scr) = refs
    j = pl.program_id(1)
    nt = pl.num_programs(1)
    blk = _scan_block_index(j, nct, nt, rev)
    d = 1 if rev else 0
    nh, dk = GDN_HEADS, GDN_DK
    nchunk = ROWS // CHUNK

    @pl.when(j == 0)
    def _():
        s_scr[...] = jnp.zeros_like(s_scr)

    has_prev = jnp.logical_and(blk != 0, blk != nct)
    has_next = jnp.logical_and(blk != nct - 1, blk != nt - 1)
    xe_scr[0:HALO, :] = jnp.where(has_prev, prev_ref[0], 0.0)
    xe_scr[HALO:HALO + ROWS, :] = x_ref[0, :, :GDN_QKV]
    xe_scr[HALO + ROWS:, :] = jnp.where(has_next, next_ref[0], 0.0)
    pad = (GDN_CONV - 1) // 2
    conv = jnp.zeros((ROWS, GDN_QKV), F32)
    for t in range(GDN_CONV):
        conv = conv + convw_ref[t:t + 1, :] * xe_scr[pl.ds(HALO - pad + t, ROWS), :]
    hqkv = _silu(conv)
    ob = ob_ref[...]
    q = hqkv[:, :GDN_QK]
    k = hqkv[:, GDN_QK:2 * GDN_QK]
    v = hqkv[:, 2 * GDN_QK:]
    q = q * lax.rsqrt(_dot_sel(q * q, ob) + EPS) * (dk ** -0.5)
    k = k * lax.rsqrt(_dot_sel(k * k, ob) + EPS)

    sm = small_ref[0]
    g_all = -jnp.exp(prow_ref[0:1, :]) * _softplus(sm + prow_ref[1:2, :])
    beta_all = jax.nn.sigmoid(sm)
    tri = tri_ref[...]
    gc_all = _sel_dot(tri, g_all)
    gl_all = _sel_dot(ones_ref[...], g_all)
    g_t = -jnp.exp(pcol_ref[:, 0:1]) * _softplus(abt_ref[0] + pcol_ref[:, 1:2])
    gc_t = _dot_sel_nt(g_t, tri)
    e_g = exp_ref[d]
    e_b = exp_ref[2 + d]
    gc_w = _dot_sel(gc_all, e_g)
    gl_w = _dot_sel(gl_all, e_g)
    beta_w = _dot_sel(beta_all, e_b)
    kb = k * beta_w
    vb = v * beta_w
    kbg = kb * jnp.exp(gc_w)
    qg = q * jnp.exp(gc_w)
    kdec = k * jnp.exp(gl_w - gc_w)

    incl_f = incl_ref[...]
    incl = incl_f > 0.5
    lv = lv_ref[...]
    ri = lax.broadcasted_iota(jnp.int32, (ROWS, ROWS), 0)
    ci = lax.broadcasted_iota(jnp.int32, (ROWS, ROWS), 1)
    eye = (ri == ci).astype(F32)
    kbf = k.astype(BF16)
    order = list(range(nchunk))[::-1] if rev else list(range(nchunk))

    outs = []
    for h in range(nh):
        idx = d * nh + h
        hm = _head_lane_mask(GDN_QK, dk, h)
        sl = slice(h * dk, (h + 1) * dk)
        diff = gc_all[:, idx:idx + 1] - gc_t[idx:idx + 1, :]
        decay = incl_f * jnp.exp(jnp.where(incl, diff, 0.0))
        kk = _dot_nt(jnp.where(hm, kb, 0.0).astype(BF16), kbf)
        qk = _dot_nt(jnp.where(hm, q, 0.0).astype(BF16), kbf)
        a_intra = qk * decay
        low = kk * decay
        t_inv = eye
        for li in reversed(range(len(GLA_LEVELS))):
            c_l = jnp.where(lv == float(li), low, 0.0)
            t_inv = t_inv - _bdot(t_inv, _bdot(c_l, t_inv))
        rhs = jnp.concatenate([vb[:, sl], kbg[:, sl]], axis=1)
        uw = _bdot(t_inv, rhs)
        u = uw[:, :GDN_DV]
        w = uw[:, GDN_DV:]
        state = s_scr[h]
        o_parts = [None] * nchunk
        for c in order:
            r0 = c * CHUNK
            rs = slice(r0, r0 + CHUNK)
            sb = state.astype(BF16)
            v_new = u[rs] - _dot(w[rs].astype(BF16), sb)
            vnb = v_new.astype(BF16)
            o_parts[c] = (_dot(qg[rs, sl].astype(BF16), sb)
                          + _dot(a_intra[rs, rs].astype(BF16), vnb))
            state = (state * jnp.exp(gl_all[r0:r0 + 1, idx:idx + 1])
                     + _dot_tn(kdec[rs, sl].astype(BF16), vnb))
        s_scr[h] = state
        outs.append(jnp.concatenate(o_parts, axis=0))
    o = jnp.concatenate(outs, axis=1)
    if final:
        o = o + oprev_ref[0]
        ms = _dot_sel(o * o, ob) * (1.0 / GDN_DV)
        z = x_ref[0, :, GDN_QKV:]
        o = o * lax.rsqrt(ms + EPS) * normw_ref[...] * _silu(z)
    o_ref[0] = o


def _gdn_pass(rev, nct, gdn_in, small, abt, convw, prow, pcol, exp_m, ob, oprev, normw):
    b, ta, _ = gdn_in.shape
    nt = ta // ROWS
    final = oprev is not None
    cst = _scan_consts(rev)
    blk_of = functools.partial(_scan_block_index, nct=nct, nt=nt, rev=rev)
    hb = ROWS // HALO
    last_halo = ta // HALO - 1
    row = lambda bi, j: (bi, blk_of(j), 0)
    full2 = lambda bi, j: (0, 0)
    in_specs = [
        pl.BlockSpec((1, ROWS, _W_GDN), row),
        pl.BlockSpec((1, HALO, GDN_QKV), lambda bi, j: (bi, jnp.maximum(blk_of(j) * hb - 1, 0), 0)),
        pl.BlockSpec((1, HALO, GDN_QKV),
                     lambda bi, j: (bi, jnp.minimum((blk_of(j) + 1) * hb, last_halo), 0)),
        pl.BlockSpec((1, ROWS, LANES), row),
        pl.BlockSpec((1, 4 * GDN_HEADS, ROWS), lambda bi, j: (bi, 0, blk_of(j))),
        pl.BlockSpec(convw.shape, full2),
        pl.BlockSpec(prow.shape, full2),
        pl.BlockSpec(pcol.shape, full2),
        pl.BlockSpec((ROWS, ROWS), full2),
        pl.BlockSpec((ROWS, ROWS), full2),
        pl.BlockSpec((ROWS, ROWS), full2),
        pl.BlockSpec((ROWS, ROWS), full2),
        pl.BlockSpec(exp_m.shape, lambda bi, j: (0, 0, 0)),
        pl.BlockSpec(ob.shape, full2),
    ]
    args = [gdn_in, gdn_in, gdn_in, small, abt, convw, prow, pcol,
            cst["tri"], cst["ones"], cst["incl"], cst["lv"], exp_m, ob]
    if final:
        in_specs += [pl.BlockSpec((1, ROWS, GDN_W), row), pl.BlockSpec(normw.shape, full2)]
        args += [oprev, normw]
    return pl.pallas_call(
        functools.partial(_gdn_kernel, rev, final, nct),
        grid=(b, nt),
        in_specs=in_specs,
        out_specs=pl.BlockSpec((1, ROWS, GDN_W), row),
        out_shape=jax.ShapeDtypeStruct((b, ta, GDN_W), F32),
        scratch_shapes=[pltpu.VMEM((ROWS + 2 * HALO, GDN_QKV), F32),
                        pltpu.VMEM((GDN_HEADS, GDN_DK, GDN_DV), F32)],
        compiler_params=_params("arbitrary", "arbitrary"),
        name="gdn_bwd" if rev else "gdn_fwd",
    )(*args)


def _gla_kernel(rev, final, nct, *refs):
    if final:
        (x_ref, small_ref, wg_ref, bg_ref, tri_ref, ones_ref, mq_ref, mk_ref, lv_ref,
         eqk_ref, sbd_ref, ob_ref, oprev_ref, normw_ref, o_ref, s_scr) = refs
    else:
        (x_ref, small_ref, wg_ref, bg_ref, tri_ref, ones_ref, mq_ref, mk_ref, lv_ref,
         eqk_ref, sbd_ref, ob_ref, o_ref, s_scr) = refs
    j = pl.program_id(1)
    nh, dk, dv = GLA_HEADS, GLA_DK, GLA_DV
    nchunk = ROWS // CHUNK

    @pl.when(j == 0)
    def _():
        s_scr[...] = jnp.zeros_like(s_scr)

    x = x_ref[0]
    q = x[:, :GLA_QK] * (dk ** -0.5)
    k = x[:, GLA_QK:2 * GLA_QK]
    v = x[:, 2 * GLA_QK:2 * GLA_QK + GLA_W]
    gk = _dot(small_ref[0].astype(BF16), wg_ref[...]) + bg_ref[...]
    la = -_softplus(-gk) * (1.0 / GLA_GATE_NORM)
    la3 = jnp.concatenate(_split3(la), axis=1)
    n = GLA_QK

    def cum(m01):
        y = _dot(m01, la3)
        return y[:, :n] + y[:, n:2 * n] + y[:, 2 * n:]

    bcum = cum(tri_ref[...])
    blast = cum(ones_ref[...])
    qg = (q * jnp.exp(bcum)).astype(BF16)
    kdec = (k * jnp.exp(blast - bcum)).astype(BF16)
    vb = v.astype(BF16)

    lv = lv_ref[...]
    lane_head = [_head_lane_mask(GLA_QK, dk, h) for h in range(nh)]
    acc = [jnp.zeros((ROWS, ROWS), F32) for _ in range(nh)]
    for li in range(len(GLA_LEVELS)):
        ql = q * jnp.exp(cum(mq_ref[li]))
        kl = (k * jnp.exp(cum(mk_ref[li]))).astype(BF16)
        sel = lv == float(li)
        for h in range(nh):
            p = _dot_nt(jnp.where(lane_head[h], ql, 0.0).astype(BF16), kl)
            acc[h] = jnp.where(sel, p, acc[h])
    out_head = [_head_lane_mask(GLA_W, dv, h) for h in range(nh)]
    o_intra = jnp.zeros((ROWS, GLA_W), F32)
    for h in range(nh):
        o_intra = o_intra + jnp.where(out_head[h], _dot(acc[h].astype(BF16), vb), 0.0)
    o_diag = _dot_sel(q * k, eqk_ref[...]) * v

    order = list(range(nchunk))[::-1] if rev else list(range(nchunk))
    sbd = sbd_ref[...]
    state = s_scr[...]
    o_parts = [None] * nchunk
    for c in order:
        rs = slice(c * CHUNK, (c + 1) * CHUNK)
        o_parts[c] = _dot_nt(qg[rs], state.astype(BF16))
        state = (state * jnp.exp(blast[c * CHUNK:c * CHUNK + 1, :])
                 + sbd * _dot_tn(vb[rs], kdec[rs]))
    s_scr[...] = state
    o = jnp.concatenate(o_parts, axis=0) + o_intra + o_diag
    if final:
        o = o + oprev_ref[0]
        ms = _dot_sel(o * o, ob_ref[...]) * (1.0 / dv)
        z = x[:, 2 * GLA_QK + GLA_W:]
        o = o * lax.rsqrt(ms + EPS) * normw_ref[...] * _silu(z)
    o_ref[0] = o


def _gla_pass(rev, nct, gla_in, small, wg, bg, eqk, sbd, ob, oprev, normw):
    b, ta, _ = gla_in.shape
    nt = ta // ROWS
    final = oprev is not None
    cst = _scan_consts(rev)
    blk_of = functools.partial(_scan_block_index, nct=nct, nt=nt, rev=rev)
    row = lambda bi, j: (bi, blk_of(j), 0)
    full2 = lambda bi, j: (0, 0)
    full3 = lambda bi, j: (0, 0, 0)
    in_specs = [
        pl.BlockSpec((1, ROWS, _W_GLA), row),
        pl.BlockSpec((1, ROWS, LANES), row),
        pl.BlockSpec(wg.shape, full2),
        pl.BlockSpec(bg.shape, full2),
        pl.BlockSpec((ROWS, ROWS), full2),
        pl.BlockSpec((ROWS, ROWS), full2),
        pl.BlockSpec(cst["mq"].shape, full3),
        pl.BlockSpec(cst["mk"].shape, full3),
        pl.BlockSpec((ROWS, ROWS), full2),
        pl.BlockSpec(eqk.shape, full2),
        pl.BlockSpec(sbd.shape, full2),
        pl.BlockSpec(ob.shape, full2),
    ]
    args = [gla_in, small, wg, bg, cst["tri"], cst["ones"], cst["mq"], cst["mk"], cst["lv"],
            eqk, sbd, ob]
    if final:
        in_specs += [pl.BlockSpec((1, ROWS, GLA_W), row), pl.BlockSpec(normw.shape, full2)]
        args += [oprev, normw]
    return pl.pallas_call(
        functools.partial(_gla_kernel, rev, final, nct),
        grid=(b, nt),
        in_specs=in_specs,
        out_specs=pl.BlockSpec((1, ROWS, GLA_W), row),
        out_shape=jax.ShapeDtypeStruct((b, ta, GLA_W), F32),
        scratch_shapes=[pltpu.VMEM((GLA_W, GLA_QK), F32)],
        compiler_params=_params("arbitrary", "arbitrary"),
        name="gla_bwd" if rev else "gla_fwd",
    )(*args)


def _rope(x, tab_ref):
    half = MLA_ROPE // 2
    return (x * tab_ref[0] + pltpu.roll(x, LANES - half, 1) * tab_ref[1]
            + pltpu.roll(x, half, 1) * tab_ref[2])


def _mla_prep_kernel(x_ref, kr_ref, tab_ref, qw_ref, kvw_ref, wuq_ref, wuk_ref, wuv_ref,
                     q_ref, k_ref, v_ref):
    x = x_ref[0]
    cq = x[:, :MLA_Q_RANK]
    ckv = x[:, MLA_Q_RANK:]
    cq = cq * lax.rsqrt(jnp.mean(cq * cq, axis=-1, keepdims=True) + EPS) * qw_ref[...]
    ckv = ckv * lax.rsqrt(jnp.mean(ckv * ckv, axis=-1, keepdims=True) + EPS) * kvw_ref[...]
    cqb = cq.astype(BF16)
    ckvb = ckv.astype(BF16)
    qf = _dot(cqb, wuq_ref[...])
    kf = _dot(ckvb, wuk_ref[...])
    vf = _dot(ckvb, wuv_ref[...])
    kr = _rope(kr_ref[0], tab_ref)
    for h in range(MLA_HEADS):
        sl = slice(h * LANES, (h + 1) * LANES)
        q_ref[0, h] = (_rope(qf[:, sl], tab_ref) * MLA_SCALE).astype(BF16)
        k_ref[0, h] = (kf[:, sl] + kr).astype(BF16)
    for hp in range(MLA_HEADS // 2):
        v_ref[0, hp] = vf[:, hp * LANES:(hp + 1) * LANES].astype(BF16)


def _mla_prep(mla_in, kr128, tab, qw, kvw, wuq, wuk, wuv):
    b, ta, _ = mla_in.shape
    nt = ta // ROWS
    row = lambda bi, j: (bi, j, 0)
    full2 = lambda bi, j: (0, 0)
    hm = lambda bi, j: (bi, 0, j, 0)
    return pl.pallas_call(
        _mla_prep_kernel,
        grid=(b, nt),
        in_specs=[pl.BlockSpec((1, ROWS, _W_MLA), row),
                  pl.BlockSpec((1, ROWS, LANES), row),
                  pl.BlockSpec((3, ROWS, LANES), lambda bi, j: (0, j, 0)),
                  pl.BlockSpec(qw.shape, full2), pl.BlockSpec(kvw.shape, full2),
                  pl.BlockSpec(wuq.shape, full2), pl.BlockSpec(wuk.shape, full2),
                  pl.BlockSpec(wuv.shape, full2)],
        out_specs=[pl.BlockSpec((1, MLA_HEADS, ROWS, LANES), hm),
                   pl.BlockSpec((1, MLA_HEADS, ROWS, LANES), hm),
                   pl.BlockSpec((1, MLA_HEADS // 2, ROWS, LANES), hm)],
        out_shape=[jax.ShapeDtypeStruct((b, MLA_HEADS, ta, LANES), BF16),
                   jax.ShapeDtypeStruct((b, MLA_HEADS, ta, LANES), BF16),
                   jax.ShapeDtypeStruct((b, MLA_HEADS // 2, ta, LANES), BF16)],
        compiler_params=_params("arbitrary", "arbitrary"),
        name="mla_prep",
    )(mla_in, kr128, tab, qw, kvw, wuq, wuk, wuv)


def _attn_kernel(nct, off, n_ctx, q_ref, k_ref, v_ref, z_ref, o_ref):
    i = pl.program_id(2) + off

    def body(nk):
        parts = []
        for hh in range(2):
            s = _dot_nt(q_ref[0, hh], k_ref[0, hh, :nk, :])
            m = jnp.max(s, axis=-1, keepdims=True)
            p = jnp.exp(s - m)
            l = jnp.sum(p, axis=-1, keepdims=True)
            o = _dot(p.astype(BF16), v_ref[0, 0, :nk, :]) / l
            parts.append(o[:, hh * MLA_DV:(hh + 1) * MLA_DV])
        o_ref[0] = jnp.concatenate(parts, axis=1) * _silu(z_ref[0])

    if off == 0:
        @pl.when(i < nct)
        def _():
            body(n_ctx)

        @pl.when(i >= nct)
        def _():
            body(k_ref.shape[2])
    else:
        body(k_ref.shape[2])


def _attention(q, k, v, z, nct, with_ctx):
    b, nh, ta, _ = q.shape
    nt = ta // ROWS
    off = 0 if with_ctx else nct
    nq = nt - off
    npair = nh // 2
    return pl.pallas_call(
        functools.partial(_attn_kernel, nct, off, nct * ROWS),
        grid=(b, npair, nq),
        in_specs=[pl.BlockSpec((1, 2, ROWS, LANES), lambda bi, hp, i: (bi, hp, i + off, 0)),
                  pl.BlockSpec((1, 2, ta, LANES), lambda bi, hp, i: (bi, hp, 0, 0)),
                  pl.BlockSpec((1, 1, ta, LANES), lambda bi, hp, i: (bi, hp, 0, 0)),
                  pl.BlockSpec((1, ROWS, LANES), lambda bi, hp, i: (bi, i + off, hp))],
        out_specs=pl.BlockSpec((1, ROWS, LANES), lambda bi, hp, i: (bi, i, hp)),
        out_shape=jax.ShapeDtypeStruct((b, nq * ROWS, MLA_W), F32),
        compiler_params=_params("arbitrary", "arbitrary", "arbitrary"),
        name="mla_attn",
    )(q, k, v, z)


def _outproj_kernel(final, g_ref, m_ref, a_ref, x_ref, mod_ref, w_ref, *rest):
    if final:
        fw_ref, o_ref = rest
    else:
        (o_ref,) = rest
    y = _dot(g_ref[0].astype(BF16), w_ref[0:GDN_W, :])
    y = y + _dot(m_ref[0].astype(BF16), w_ref[GDN_W:GDN_W + MLA_W, :])
    y = y + _dot(a_ref[0].astype(BF16), w_ref[GDN_W + MLA_W:, :])
    xn = x_ref[0] + mod_ref[0, 2:3, :] * y
    if final:
        xn = xn * lax.rsqrt(jnp.mean(xn * xn, axis=-1, keepdims=True) + EPS) * fw_ref[...]
    o_ref[0] = xn


def _outproj(gdn_o, mla_o, gla_o, x_all, mod, w_out, nct, final_w):
    b, ta, d = x_all.shape
    nt = ta // ROWS
    final = final_w is not None
    off = nct if final else 0
    moff = off - (ta - mla_o.shape[1]) // ROWS
    nq = nt - off
    row = lambda bi, j: (bi, j + off, 0)
    in_specs = [pl.BlockSpec((1, ROWS, GDN_W), row),
                pl.BlockSpec((1, ROWS, MLA_W), lambda bi, j: (bi, j + moff, 0)),
                pl.BlockSpec((1, ROWS, GLA_W), row),
                pl.BlockSpec((1, ROWS, d), row),
                pl.BlockSpec((1, 3, d), lambda bi, j: (jnp.where(j + off < nct, 0, 1 + bi), 0, 0)),
                pl.BlockSpec(w_out.shape, lambda bi, j: (0, 0))]
    args = [gdn_o, mla_o, gla_o, x_all, mod, w_out]
    if final:
        in_specs.append(pl.BlockSpec(final_w.shape, lambda bi, j: (0, 0)))
        args.append(final_w)
    return pl.pallas_call(
        functools.partial(_outproj_kernel, final),
        grid=(b, nq),
        in_specs=in_specs,
        out_specs=pl.BlockSpec((1, ROWS, d), lambda bi, j: (bi, j, 0)),
        out_shape=jax.ShapeDtypeStruct((b, nq * ROWS, d), F32),
        compiler_params=_params("arbitrary", "arbitrary"),
        name="out_proj",
    )(*args)


def _expand_matrix(src0, n_src, width, n_out):
    m = np.zeros((LANES, n_out), np.float32)
    for h in range(n_src):
        m[src0 + h, h * width:(h + 1) * width] = 1.0
    return m


def kernel(x, c, ctx, c_ctx, w_ada, b_ada, w_in, gdn_conv_w, gdn_a_log, gdn_dt_bias, gdn_norm_w,
           mla_q_norm_w, mla_w_uq, mla_kv_norm_w, mla_w_ukv, gla_w_gk, gla_b_gk, gla_norm_w,
           w_out, final_norm_w):
    b, seq, d = x.shape
    n_ctx = ctx.shape[1]
    depth = w_in.shape[0]
    assert n_ctx % ROWS == 0 and seq % ROWS == 0 and seq % ROPE_GRID_W == 0
    nct = n_ctx // ROWS
    nh = GDN_HEADS

    x_all = jnp.concatenate([ctx, x], axis=1)
    pad_rows = (-(1 + b)) % 8
    cc = jnp.concatenate([c_ctx[None, :], c, jnp.zeros((pad_rows, d), F32)], axis=0)
    mod_all = _ada(cc, w_ada, b_ada).reshape(depth, cc.shape[0], 3, d)

    perm = _inproj_perm()
    qperm = _mla_q_perm()
    kperm, vperm = _mla_kv_perm()
    tab = _rope_tables(n_ctx, seq)
    ob64 = _head_block_ones(GDN_W, GDN_DV)
    exp_m = jnp.asarray(np.stack(
        [_expand_matrix(_S_A + dd * nh, nh, GDN_DK, GDN_QK) for dd in range(2)]
        + [_expand_matrix(_S_B + dd * nh, nh, GDN_DK, GDN_QK) for dd in range(2)]), BF16)
    eqk_np = np.zeros((GLA_QK, GLA_W), np.float32)
    sbd_np = np.zeros((GLA_W, GLA_QK), np.float32)
    for h in range(GLA_HEADS):
        eqk_np[h * GLA_DK:(h + 1) * GLA_DK, h * GLA_DV:(h + 1) * GLA_DV] = 1.0
        sbd_np[h * GLA_DV:(h + 1) * GLA_DV, h * GLA_DK:(h + 1) * GLA_DK] = 1.0
    eqk = jnp.asarray(eqk_np, BF16)
    sbd = jnp.asarray(sbd_np)

    out = None
    for layer in range(depth):
        last = layer == depth - 1
        w_p = _take_cols(w_in[layer], perm).astype(BF16)
        wabt = w_in[layer][:, _O_A:_O_A + 4 * nh].T.astype(BF16)
        mod = mod_all[layer]
        gdn_in, mla_in, mla_z, gla_in, small, kr128, abt = _inproj(x_all, mod, w_p, wabt, nct)

        convw = jnp.concatenate(
            [gdn_conv_w[layer], jnp.zeros((8 - GDN_CONV, GDN_QKV), F32)], axis=0)
        a_flat = gdn_a_log[layer].reshape(-1)
        dt_flat = gdn_dt_bias[layer].reshape(-1)
        prow = jnp.zeros((8, LANES), F32)
        prow = prow.at[0, _S_A:_S_A + 2 * nh].set(a_flat).at[1, _S_A:_S_A + 2 * nh].set(dt_flat)
        pcol = jnp.zeros((4 * nh, LANES), F32)
        pcol = pcol.at[0:2 * nh, 0].set(a_flat).at[0:2 * nh, 1].set(dt_flat)
        gnw = jnp.tile(gdn_norm_w[layer], nh)[None, :]
        o_f = _gdn_pass(False, nct, gdn_in, small, abt, convw, prow, pcol, exp_m, ob64, None, None)
        gdn_o = _gdn_pass(True, nct, gdn_in, small, abt, convw, prow, pcol, exp_m, ob64, o_f, gnw)

        lnw = jnp.tile(gla_norm_w[layer], GLA_HEADS)[None, :]
        gla_dir = []
        for dd in range(2):
            wg = jnp.zeros((LANES, GLA_QK), F32)
            r0 = _S_GLOW + dd * GLA_GATE_RANK
            wg = wg.at[r0:r0 + GLA_GATE_RANK, :].set(gla_w_gk[layer, dd]).astype(BF16)
            gla_dir.append((wg, gla_b_gk[layer, dd][None, :]))
        o_f = _gla_pass(False, nct, gla_in, small, gla_dir[0][0], gla_dir[0][1],
                        eqk, sbd, ob64, None, None)
        gla_o = _gla_pass(True, nct, gla_in, small, gla_dir[1][0], gla_dir[1][1],
                          eqk, sbd, ob64, o_f, lnw)

        wuq = _take_cols(mla_w_uq[layer], qperm).astype(BF16)
        wuk = _take_cols(mla_w_ukv[layer], kperm).astype(BF16)
        wuv = jnp.take(mla_w_ukv[layer], jnp.asarray(vperm), axis=1).astype(BF16)
        qh, kh, vh = _mla_prep(mla_in, kr128, tab, mla_q_norm_w[layer][None, :],
                               mla_kv_norm_w[layer][None, :], wuq, wuk, wuv)
        mla_o = _attention(qh, kh, vh, mla_z, nct, with_ctx=not last)

        res = _outproj(gdn_o, mla_o, gla_o, x_all, mod, w_out[layer].astype(BF16), nct,
                       final_norm_w[None, :] if last else None)
        if last:
            out = res
        else:
            x_all = res
    return out
```

```python
import functools
import math

import numpy as np
import jax
import jax.numpy as jnp
from jax import lax
from jax.experimental import pallas as pl
from jax.experimental.pallas import tpu as pltpu

F32 = jnp.float32
BF16 = jnp.bfloat16
EPS = 1e-6

GDN_HEADS, GDN_DK, GDN_DV, GDN_CONV = 4, 64, 64, 5
GDN_QK = GDN_HEADS * GDN_DK
GDN_W = GDN_HEADS * GDN_DV
GDN_QKV = 2 * GDN_QK + GDN_W
MLA_HEADS, MLA_Q_RANK, MLA_KV_RANK = 8, 384, 256
MLA_NOPE, MLA_ROPE, MLA_DV = 64, 32, 64
MLA_W = MLA_HEADS * MLA_DV
MLA_SCALE = (MLA_NOPE + MLA_ROPE) ** -0.5
ROPE_THETA = 10000.0
ROPE_GRID_W = 64
GLA_HEADS, GLA_DK, GLA_DV = 4, 32, 64
GLA_QK = GLA_HEADS * GLA_DK
GLA_W = GLA_HEADS * GLA_DV
GLA_GATE_RANK = 16
GLA_GATE_NORM = 16.0
CHUNK = 64
LEVELS = (32, 16, 8, 4, 2, 1)

LANES = 128
MXU_N = 256
ROWS = 256
HALO = 8
ATTN_HEADS = 8
VMEM_LIMIT = 56 * 1024 * 1024

_O_GDN_QKV, _O_GDN_Z, _O_A, _O_B = 0, 768, 1024, 1032
_O_CQ, _O_CKV, _O_KR, _O_MLA_Z = 1040, 1424, 1680, 1712
_O_GLA_Q, _O_GLA_K, _O_GLA_V, _O_GLA_Z, _O_GLOW = 2224, 2352, 2480, 2736, 2992
_S_A, _S_B, _S_GLOW = 0, 8, 16
_W_GDN, _W_MLA, _W_MLAZ, _W_GLA = 1024, 640, 512, 768
_N_IN_PAD = _W_GDN + _W_MLA + _W_MLAZ + _W_GLA + 2 * LANES


def _dot(a, b):
    return jnp.dot(a, b, preferred_element_type=F32)


def _dot_nt(a, b):
    return lax.dot_general(a, b, (((1,), (1,)), ((), ())), preferred_element_type=F32)


def _dot_tn(a, b):
    return lax.dot_general(a, b, (((0,), (0,)), ((), ())), preferred_element_type=F32)


def _bdot(a, b):
    return _dot(a.astype(BF16), b.astype(BF16))


def _split2(x):
    x1 = x.astype(BF16)
    return x1, (x - x1.astype(F32)).astype(BF16)


def _split3(x):
    x1 = x.astype(BF16)
    r1 = x - x1.astype(F32)
    x2 = r1.astype(BF16)
    x3 = (r1 - x2.astype(F32)).astype(BF16)
    return x1, x2, x3


def _sel_dot(m01, x):
    n = x.shape[1]
    y = _dot(m01, jnp.concatenate(_split3(x), axis=1))
    return y[:, :n] + y[:, n:2 * n] + y[:, 2 * n:]


def _dot_sel(x, m01):
    n = x.shape[0]
    y = _dot(jnp.concatenate(_split3(x), axis=0), m01)
    return y[:n] + y[n:2 * n] + y[2 * n:]


def _dot_sel_nt(x, m01):
    n = x.shape[0]
    y = _dot_nt(jnp.concatenate(_split3(x), axis=0), m01)
    return y[:n] + y[n:2 * n] + y[2 * n:]


def _softplus(x):
    return jnp.maximum(x, 0.0) + jnp.log1p(jnp.exp(-jnp.abs(x)))


def _silu(x):
    return x * jax.nn.sigmoid(x)


def _params(*sem):
    return pltpu.CompilerParams(dimension_semantics=sem, vmem_limit_bytes=VMEM_LIMIT)


def _scan_consts(rev):
    t = np.arange(ROWS)
    ch = t // CHUNK
    p = (CHUNK - 1 - t % CHUNK) if rev else (t % CHUNK)
    same = ch[:, None] == ch[None, :]
    tri = same & (p[None, :] <= p[:, None])
    mq, mk = [], []
    lv = np.full((ROWS, ROWS), -1.0, np.float32)
    for li, s in enumerate(LEVELS):
        blk = p // s
        mq.append(same & (p[None, :] > (blk * s)[:, None]) & (p[None, :] <= p[:, None]))
        mk.append(same & (p[None, :] > p[:, None]) & (p[None, :] <= ((blk + 1) * s)[:, None]))
        pair = same & ((blk % 2) == 1)[:, None] & (blk[None, :] == (blk - 1)[:, None])
        lv[pair] = li
    stack = np.concatenate([tri, same] + mq + mk, axis=0)
    return dict(tri=jnp.asarray(tri, BF16), ones=jnp.asarray(same, BF16),
                incl=jnp.asarray(tri, F32), lv=jnp.asarray(lv), stack=jnp.asarray(stack, BF16))


def _head_block_ones(n, width):
    i = np.arange(n)
    return jnp.asarray((i[:, None] // width) == (i[None, :] // width), BF16)


def _inproj_perm():
    perm = np.full((_N_IN_PAD,), -1, np.int64)

    def put(src, n, at):
        perm[at:at + n] = np.arange(src, src + n)

    put(_O_GDN_QKV, GDN_QKV, 0)
    put(_O_GDN_Z, GDN_W, GDN_QKV)
    pos = _W_GDN
    put(_O_CQ, MLA_Q_RANK, pos)
    put(_O_CKV, MLA_KV_RANK, pos + MLA_Q_RANK)
    pos += _W_MLA
    put(_O_MLA_Z, MLA_W, pos)
    pos += _W_MLAZ
    put(_O_GLA_Q, GLA_QK, pos)
    put(_O_GLA_K, GLA_QK, pos + GLA_QK)
    put(_O_GLA_V, GLA_W, pos + 2 * GLA_QK)
    put(_O_GLA_Z, GLA_W, pos + 2 * GLA_QK + GLA_W)
    pos += _W_GLA
    put(_O_A, 2 * GDN_HEADS, pos + _S_A)
    put(_O_B, 2 * GDN_HEADS, pos + _S_B)
    put(_O_GLOW, 2 * GLA_GATE_RANK, pos + _S_GLOW)
    pos += LANES
    put(_O_KR, MLA_ROPE, pos + MLA_NOPE)
    return perm


def _take_cols(w, perm):
    cols = jnp.take(w, jnp.asarray(np.maximum(perm, 0)), axis=1)
    return jnp.where(jnp.asarray(perm >= 0)[None, :], cols, 0.0)


def _mla_q_perm():
    perm = np.full((MLA_HEADS * LANES,), -1, np.int64)
    d = MLA_NOPE + MLA_ROPE
    for h in range(MLA_HEADS):
        perm[h * LANES:h * LANES + d] = np.arange(h * d, (h + 1) * d)
    return perm


def _mla_kv_perm():
    dk = MLA_NOPE + MLA_DV
    kperm = np.full((MLA_HEADS * LANES,), -1, np.int64)
    vperm = np.zeros((MLA_W,), np.int64)
    for h in range(MLA_HEADS):
        kperm[h * LANES:h * LANES + MLA_NOPE] = np.arange(h * dk, h * dk + MLA_NOPE)
        vperm[h * MLA_DV:(h + 1) * MLA_DV] = np.arange(h * dk + MLA_NOPE, (h + 1) * dk)
    return kperm, vperm


def _rope_tables(n_ctx, n_lat):
    rows = n_lat // ROPE_GRID_W
    row = np.repeat(np.arange(rows, dtype=np.float32), ROPE_GRID_W)
    col = np.tile(np.arange(ROPE_GRID_W, dtype=np.float32), rows)
    n_freq = MLA_ROPE // 4
    inv = (ROPE_THETA ** (-np.arange(n_freq, dtype=np.float32) / n_freq)).astype(np.float32)
    ang = np.concatenate([row[:, None] * inv, col[:, None] * inv], axis=-1)
    cos = np.concatenate([np.ones((n_ctx, 2 * n_freq), np.float32), np.cos(ang)], 0)
    sin = np.concatenate([np.zeros((n_ctx, 2 * n_freq), np.float32), np.sin(ang)], 0)
    n = n_ctx + n_lat
    half = MLA_ROPE // 2
    tab = np.zeros((3, n, LANES), np.float32)
    tab[0, :, :MLA_NOPE] = 1.0
    tab[0, :, MLA_NOPE:MLA_NOPE + half] = cos
    tab[0, :, MLA_NOPE + half:MLA_NOPE + MLA_ROPE] = cos
    tab[1, :, MLA_NOPE:MLA_NOPE + half] = -sin
    tab[2, :, MLA_NOPE + half:MLA_NOPE + MLA_ROPE] = sin
    return jnp.asarray(tab)


def _expand_matrix(src0, n_src, width, n_out):
    m = np.zeros((LANES, n_out), np.float32)
    for h in range(n_src):
        m[src0 + h, h * width:(h + 1) * width] = 1.0
    return m


def _ada_kernel(c_ref, w_ref, b_ref, o_ref):
    c = _silu(c_ref[...]).astype(BF16)
    o_ref[0] = _dot(c, w_ref[0].astype(BF16)) + b_ref[0]


def _ada(cc, w_ada, b_ada):
    nl, d, n3 = w_ada.shape
    r = cc.shape[0]
    tn = 1024
    return pl.pallas_call(
        _ada_kernel,
        grid=(nl, n3 // tn),
        in_specs=[pl.BlockSpec((r, d), lambda l, j: (0, 0)),
                  pl.BlockSpec((1, d, tn), lambda l, j: (l, 0, j)),
                  pl.BlockSpec((1, 1, tn), lambda l, j: (l, 0, j))],
        out_specs=pl.BlockSpec((1, r, tn), lambda l, j: (l, 0, j)),
        out_shape=jax.ShapeDtypeStruct((nl, r, n3), F32),
        compiler_params=_params("arbitrary", "arbitrary"),
        name="ada_mod",
    )(cc, w_ada, b_ada.reshape(nl, 1, n3))


def _inproj_kernel(x_ref, mod_ref, w_ref, wabt_ref, ogdn, omla, omlaz, ogla, osmall, okr, oabt):
    x = x_ref[0]
    h = x * lax.rsqrt(jnp.mean(x * x, axis=-1, keepdims=True) + EPS)
    h = h * (1.0 + mod_ref[0, 1:2, :]) + mod_ref[0, 0:1, :]
    hb = h.astype(BF16)
    pos = 0
    for ref, n in ((ogdn, _W_GDN), (omla, _W_MLA), (omlaz, _W_MLAZ), (ogla, _W_GLA),
                   (osmall, LANES), (okr, LANES)):
        ref[0] = _dot(hb, w_ref[:, pos:pos + n])
        pos += n
    oabt[0] = _dot_nt(wabt_ref[...], hb)


def _inproj(x_all, mod, w_p, wabt, nct):
    b, ta, d = x_all.shape
    nt = ta // ROWS
    widths = (_W_GDN, _W_MLA, _W_MLAZ, _W_GLA, LANES, LANES)
    row = lambda bi, j: (bi, j, 0)
    return pl.pallas_call(
        _inproj_kernel,
        grid=(b, nt),
        in_specs=[pl.BlockSpec((1, ROWS, d), row),
                  pl.BlockSpec((1, 3, d), lambda bi, j: (jnp.where(j < nct, 0, 1 + bi), 0, 0)),
                  pl.BlockSpec(w_p.shape, lambda bi, j: (0, 0)),
                  pl.BlockSpec(wabt.shape, lambda bi, j: (0, 0))],
        out_specs=[pl.BlockSpec((1, ROWS, n), row) for n in widths]
        + [pl.BlockSpec((1, 4 * GDN_HEADS, ROWS), lambda bi, j: (bi, 0, j))],
        out_shape=[jax.ShapeDtypeStruct((b, ta, n), F32) for n in widths]
        + [jax.ShapeDtypeStruct((b, 4 * GDN_HEADS, ta), F32)],
        compiler_params=_params("arbitrary", "arbitrary"),
        name="in_proj",
    )(x_all, mod, w_p, wabt)


def _scan_block_index(j, nct, nt, rev):
    if not rev:
        return j
    return jnp.where(j < nct, nct - 1 - j, nt - 1 - (j - nct))


def _head_lane_mask(n, width, h):
    lane = lax.broadcasted_iota(jnp.int32, (1, n), 1)
    return (lane >= h * width) & (lane < (h + 1) * width)


def _gdn_prep(d, blk, nct, nt, x_ref, prev_ref, next_ref, small_ref, abt_ref, convw_ref, prow_ref,
              pcol_ref, tri, ones, exp_ref, ob, xe_scr):
    dk = GDN_DK
    has_prev = jnp.logical_and(blk != 0, blk != nct)
    has_next = jnp.logical_and(blk != nct - 1, blk != nt - 1)
    xe_scr[d, 0:HALO, :] = jnp.where(has_prev, prev_ref[0], 0.0)
    xe_scr[d, HALO:HALO + ROWS, :] = x_ref[0, :, :GDN_QKV]
    xe_scr[d, HALO + ROWS:, :] = jnp.where(has_next, next_ref[0], 0.0)
    pad = (GDN_CONV - 1) // 2
    conv = jnp.zeros((ROWS, GDN_QKV), F32)
    for t in range(GDN_CONV):
        conv = conv + convw_ref[t:t + 1, :] * xe_scr[d, pl.ds(HALO - pad + t, ROWS), :]
    hqkv = _silu(conv)
    q = hqkv[:, :GDN_QK]
    k = hqkv[:, GDN_QK:2 * GDN_QK]
    v = hqkv[:, 2 * GDN_QK:]
    q = q * lax.rsqrt(_dot_sel(q * q, ob) + EPS) * (dk ** -0.5)
    k = k * lax.rsqrt(_dot_sel(k * k, ob) + EPS)
    sm = small_ref[0]
    g_all = -jnp.exp(prow_ref[0:1, :]) * _softplus(sm + prow_ref[1:2, :])
    beta_all = jax.nn.sigmoid(sm)
    gc_all = _sel_dot(tri, g_all)
    gl_all = _sel_dot(ones, g_all)
    g_t = -jnp.exp(pcol_ref[:, 0:1]) * _softplus(abt_ref[0] + pcol_ref[:, 1:2])
    gc_t = _dot_sel_nt(g_t, tri)
    gc_w = _dot_sel(gc_all, exp_ref[d])
    gl_w = _dot_sel(gl_all, exp_ref[d])
    beta_w = _dot_sel(beta_all, exp_ref[2 + d])
    kb = k * beta_w
    return dict(q=q, kbf=k.astype(BF16), kb=kb, vb=v * beta_w, kbg=kb * jnp.exp(gc_w),
                qg=q * jnp.exp(gc_w), kdec=k * jnp.exp(gl_w - gc_w),
                gc_all=gc_all, gc_t=gc_t, gl_all=gl_all)


def _gdn_kernel(nct, xf_ref, pf_ref, nf_ref, smf_ref, abtf_ref, xb_ref, pb_ref, nb_ref, smb_ref,
                abtb_ref, convw_ref, prow_ref, pcol_ref, tri_ref, ones_ref, incl_ref, lv_ref,
                exp_ref, ob_ref, of_ref, obk_ref, xe_scr, s_scr):
    j = pl.program_id(1)
    nt = pl.num_programs(1)
    nh, dk = GDN_HEADS, GDN_DK
    nchunk = ROWS // CHUNK
    nlev = len(LEVELS)

    @pl.when(j == 0)
    def _():
        s_scr[...] = jnp.zeros_like(s_scr)

    ob = ob_ref[...]
    ones = ones_ref[...]
    dir_refs = ((xf_ref, pf_ref, nf_ref, smf_ref, abtf_ref), (xb_ref, pb_ref, nb_ref, smb_ref, abtb_ref))
    prep = []
    for d in range(2):
        blk = _scan_block_index(j, nct, nt, d == 1)
        prep.append(_gdn_prep(d, blk, nct, nt, *dir_refs[d], convw_ref, prow_ref, pcol_ref,
                              tri_ref[d], ones, exp_ref, ob, xe_scr))
    ri = lax.broadcasted_iota(jnp.int32, (ROWS, ROWS), 0)
    ci = lax.broadcasted_iota(jnp.int32, (ROWS, ROWS), 1)
    eye = (ri == ci).astype(F32)
    chains = [(d, h) for d in range(2) for h in range(nh)]

    low, a_intra = [], []
    for d, h in chains:
        p = prep[d]
        idx = d * nh + h
        hm = _head_lane_mask(GDN_QK, dk, h)
        incl_f = incl_ref[d]
        diff = p["gc_all"][:, idx:idx + 1] - p["gc_t"][idx:idx + 1, :]
        decay = incl_f * jnp.exp(jnp.where(incl_f > 0.5, diff, 0.0))
        kk = _dot_nt(jnp.where(hm, p["kb"], 0.0).astype(BF16), p["kbf"])
        qk = _dot_nt(jnp.where(hm, p["q"], 0.0).astype(BF16), p["kbf"])
        low.append(kk * decay)
        a_intra.append(qk * decay)

    lvs = [lv_ref[d] for d in range(2)]
    t_inv = [eye - jnp.where(lvs[d] == float(nlev - 1), low[i], 0.0) for i, (d, h) in enumerate(chains)]
    for li in reversed(range(nlev - 1)):
        ys = [_bdot(jnp.where(lvs[d] == float(li), low[i], 0.0), t_inv[i])
              for i, (d, h) in enumerate(chains)]
        t_inv = [t_inv[i] - _bdot(t_inv[i], ys[i]) for i in range(len(chains))]
    uw = []
    for i, (d, h) in enumerate(chains):
        sl = slice(h * dk, (h + 1) * dk)
        rhs = jnp.concatenate([prep[d]["vb"][:, sl], prep[d]["kbg"][:, sl]], axis=1)
        uw.append(_bdot(t_inv[i], rhs))

    states = [s_scr[i] for i in range(len(chains))]
    o_parts = [[None] * nchunk for _ in chains]
    for step in range(nchunk):
        for i, (d, h) in enumerate(chains):
            c = nchunk - 1 - step if d == 1 else step
            p = prep[d]
            idx = d * nh + h
            sl = slice(h * dk, (h + 1) * dk)
            r0 = c * CHUNK
            rs = slice(r0, r0 + CHUNK)
            sb = states[i].astype(BF16)
            v_new = uw[i][rs, :GDN_DV] - _dot(uw[i][rs, GDN_DV:].astype(BF16), sb)
            vnb = v_new.astype(BF16)
            o_parts[i][c] = (_dot(p["qg"][rs, sl].astype(BF16), sb)
                             + _dot(a_intra[i][rs, rs].astype(BF16), vnb))
            states[i] = (states[i] * jnp.exp(p["gl_all"][r0:r0 + 1, idx:idx + 1])
                         + _dot_tn(p["kdec"][rs, sl].astype(BF16), vnb))
    for i in range(len(chains)):
        s_scr[i] = states[i]
    outs = [jnp.concatenate(o_parts[i], axis=0) for i in range(len(chains))]
    of_ref[0] = jnp.concatenate(outs[:nh], axis=1)
    obk_ref[0] = jnp.concatenate(outs[nh:], axis=1)


def _scan_row_specs(nct, nt, rev, width):
    blk_of = functools.partial(_scan_block_index, nct=nct, nt=nt, rev=rev)
    return pl.BlockSpec((1, ROWS, width), lambda bi, j: (bi, blk_of(j), 0))


def _gdn(nct, gdn_in, small, abt, convw, prow, pcol, exp_m, ob):
    b, ta, _ = gdn_in.shape
    nt = ta // ROWS
    cf, cb = _scan_consts(False), _scan_consts(True)
    stack = lambda name: jnp.stack([cf[name], cb[name]])
    hb = ROWS // HALO
    last_halo = ta // HALO - 1
    full2 = lambda bi, j: (0, 0)
    full3 = lambda bi, j: (0, 0, 0)

    def dir_specs(rev):
        blk_of = functools.partial(_scan_block_index, nct=nct, nt=nt, rev=rev)
        return [
            _scan_row_specs(nct, nt, rev, _W_GDN),
            pl.BlockSpec((1, HALO, GDN_QKV),
                         lambda bi, j: (bi, jnp.maximum(blk_of(j) * hb - 1, 0), 0)),
            pl.BlockSpec((1, HALO, GDN_QKV),
                         lambda bi, j: (bi, jnp.minimum((blk_of(j) + 1) * hb, last_halo), 0)),
            _scan_row_specs(nct, nt, rev, LANES),
            pl.BlockSpec((1, 4 * GDN_HEADS, ROWS), lambda bi, j: (bi, 0, blk_of(j))),
        ]

    in_specs = dir_specs(False) + dir_specs(True) + [
        pl.BlockSpec(convw.shape, full2),
        pl.BlockSpec(prow.shape, full2),
        pl.BlockSpec(pcol.shape, full2),
        pl.BlockSpec((2, ROWS, ROWS), full3),
        pl.BlockSpec((ROWS, ROWS), full2),
        pl.BlockSpec((2, ROWS, ROWS), full3),
        pl.BlockSpec((2, ROWS, ROWS), full3),
        pl.BlockSpec(exp_m.shape, full3),
        pl.BlockSpec(ob.shape, full2),
    ]
    dir_args = [gdn_in, gdn_in, gdn_in, small, abt]
    args = dir_args + dir_args + [convw, prow, pcol, stack("tri"), cf["ones"], stack("incl"),
                                  stack("lv"), exp_m, ob]
    return pl.pallas_call(
        functools.partial(_gdn_kernel, nct),
        grid=(b, nt),
        in_specs=in_specs,
        out_specs=[_scan_row_specs(nct, nt, False, GDN_W), _scan_row_specs(nct, nt, True, GDN_W)],
        out_shape=[jax.ShapeDtypeStruct((b, ta, GDN_W), F32)] * 2,
        scratch_shapes=[pltpu.VMEM((2, ROWS + 2 * HALO, GDN_QKV), F32),
                        pltpu.VMEM((2 * GDN_HEADS, GDN_DK, GDN_DV), F32)],
        compiler_params=_params("arbitrary", "arbitrary"),
        name="gdn_scan",
    )(*args)


def _gla_kernel(xf_ref, smf_ref, xb_ref, smb_ref, wg_ref, bg_ref, stack_ref, lv_ref, eqk_ref,
                sbd_ref, of_ref, obk_ref, s_scr):
    j = pl.program_id(1)
    nh, dk, dv = GLA_HEADS, GLA_DK, GLA_DV
    nchunk = ROWS // CHUNK
    nlev = len(LEVELS)
    n = GLA_QK

    @pl.when(j == 0)
    def _():
        s_scr[...] = jnp.zeros_like(s_scr)

    lane_head = [_head_lane_mask(GLA_QK, dk, h) for h in range(nh)]
    out_head = [_head_lane_mask(GLA_W, dv, h) for h in range(nh)]
    eqk = eqk_ref[...]
    sbd = sbd_ref[...]
    dir_refs = ((xf_ref, smf_ref, of_ref), (xb_ref, smb_ref, obk_ref))
    for d in range(2):
        x_ref, sm_ref, o_ref = dir_refs[d]
        x = x_ref[0]
        q = x[:, :GLA_QK] * (dk ** -0.5)
        k = x[:, GLA_QK:2 * GLA_QK]
        v = x[:, 2 * GLA_QK:2 * GLA_QK + GLA_W]
        gk = _dot(sm_ref[0].astype(BF16), wg_ref[d]) + bg_ref[d]
        la = -_softplus(-gk) * (1.0 / GLA_GATE_NORM)
        y = _dot(stack_ref[d], jnp.concatenate(_split2(la), axis=1))
        cums = y[:, :n] + y[:, n:]
        piece = lambda i: cums[i * ROWS:(i + 1) * ROWS]
        bcum = piece(0)
        blast = piece(1)
        qg = (q * jnp.exp(bcum)).astype(BF16)
        kdec = (k * jnp.exp(blast - bcum)).astype(BF16)
        vb = v.astype(BF16)

        lv = lv_ref[d]
        lv4 = jnp.concatenate([lv] * nh, axis=0)
        acc = jnp.zeros((nh * ROWS, ROWS), F32)
        for li in range(nlev):
            ql = q * jnp.exp(piece(2 + li))
            kl = (k * jnp.exp(piece(2 + nlev + li))).astype(BF16)
            qs = jnp.concatenate([jnp.where(lane_head[h], ql, 0.0) for h in range(nh)], axis=0)
            acc = jnp.where(lv4 == float(li), _dot_nt(qs.astype(BF16), kl), acc)
        pv = _dot(acc.astype(BF16), vb)
        o = _dot_sel(q * k, eqk) * v
        for h in range(nh):
            o = o + jnp.where(out_head[h], pv[h * ROWS:(h + 1) * ROWS], 0.0)

        order = list(range(nchunk))[::-1] if d == 1 else list(range(nchunk))
        state = s_scr[d]
        o_parts = [None] * nchunk
        for c in order:
            rs = slice(c * CHUNK, (c + 1) * CHUNK)
            o_parts[c] = _dot_nt(qg[rs], state.astype(BF16))
            state = (state * jnp.exp(blast[c * CHUNK:c * CHUNK + 1, :])
                     + sbd * _dot_tn(vb[rs], kdec[rs]))
        s_scr[d] = state
        o_ref[0] = o + jnp.concatenate(o_parts, axis=0)


def _gla(nct, gla_in, small, wg, bg, eqk, sbd):
    b, ta, _ = gla_in.shape
    nt = ta // ROWS
    cf, cb = _scan_consts(False), _scan_consts(True)
    stack = jnp.stack([cf["stack"], cb["stack"]])
    lv = jnp.stack([cf["lv"], cb["lv"]])
    full2 = lambda bi, j: (0, 0)
    full3 = lambda bi, j: (0, 0, 0)
    in_specs = [
        _scan_row_specs(nct, nt, False, _W_GLA), _scan_row_specs(nct, nt, False, LANES),
        _scan_row_specs(nct, nt, True, _W_GLA), _scan_row_specs(nct, nt, True, LANES),
        pl.BlockSpec(wg.shape, full3),
        pl.BlockSpec(bg.shape, full3),
        pl.BlockSpec(stack.shape, full3),
        pl.BlockSpec(lv.shape, full3),
        pl.BlockSpec(eqk.shape, full2),
        pl.BlockSpec(sbd.shape, full2),
    ]
    return pl.pallas_call(
        _gla_kernel,
        grid=(b, nt),
        in_specs=in_specs,
        out_specs=[_scan_row_specs(nct, nt, False, GLA_W), _scan_row_specs(nct, nt, True, GLA_W)],
        out_shape=[jax.ShapeDtypeStruct((b, ta, GLA_W), F32)] * 2,
        scratch_shapes=[pltpu.VMEM((2, GLA_W, GLA_QK), F32)],
        compiler_params=_params("arbitrary", "arbitrary"),
        name="gla_scan",
    )(gla_in, small, gla_in, small, wg, bg, stack, lv, eqk, sbd)


def _rope(x, tab_ref):
    half = MLA_ROPE // 2
    return (x * tab_ref[0] + pltpu.roll(x, LANES - half, 1) * tab_ref[1]
            + pltpu.roll(x, half, 1) * tab_ref[2])


def _mla_prep_kernel(x_ref, kr_ref, tab_ref, qw_ref, kvw_ref, wuq_ref, wuk_ref, wuv_ref,
                     q_ref, k_ref, v_ref):
    x = x_ref[0]
    cq = x[:, :MLA_Q_RANK]
    ckv = x[:, MLA_Q_RANK:]
    cq = cq * lax.rsqrt(jnp.mean(cq * cq, axis=-1, keepdims=True) + EPS) * qw_ref[...]
    ckv = ckv * lax.rsqrt(jnp.mean(ckv * ckv, axis=-1, keepdims=True) + EPS) * kvw_ref[...]
    cqb = cq.astype(BF16)
    ckvb = ckv.astype(BF16)
    qf = _dot(cqb, wuq_ref[...])
    kf = _dot(ckvb, wuk_ref[...])
    vf = _dot(ckvb, wuv_ref[...])
    kr = _rope(kr_ref[0], tab_ref)
    q_scale = MLA_SCALE * math.log2(math.e)
    for h in range(MLA_HEADS):
        sl = slice(h * LANES, (h + 1) * LANES)
        q_ref[0, h] = (_rope(qf[:, sl], tab_ref) * q_scale).astype(BF16)
        k_ref[0, h] = (kf[:, sl] + kr).astype(BF16)
    ones = jnp.ones((x.shape[0], MXU_N - LANES), BF16)
    for hp in range(MLA_HEADS // 2):
        v_ref[0, hp] = jnp.concatenate(
            [vf[:, hp * LANES:(hp + 1) * LANES].astype(BF16), ones], axis=1)


def _mla_prep(mla_in, kr128, tab, qw, kvw, wuq, wuk, wuv):
    b, ta, _ = mla_in.shape
    nt = ta // ROWS
    row = lambda bi, j: (bi, j, 0)
    full2 = lambda bi, j: (0, 0)
    hm = lambda bi, j: (bi, 0, j, 0)
    return pl.pallas_call(
        _mla_prep_kernel,
        grid=(b, nt),
        in_specs=[pl.BlockSpec((1, ROWS, _W_MLA), row),
                  pl.BlockSpec((1, ROWS, LANES), row),
                  pl.BlockSpec((3, ROWS, LANES), lambda bi, j: (0, j, 0)),
                  pl.BlockSpec(qw.shape, full2), pl.BlockSpec(kvw.shape, full2),
                  pl.BlockSpec(wuq.shape, full2), pl.BlockSpec(wuk.shape, full2),
                  pl.BlockSpec(wuv.shape, full2)],
        out_specs=[pl.BlockSpec((1, MLA_HEADS, ROWS, LANES), hm),
                   pl.BlockSpec((1, MLA_HEADS, ROWS, LANES), hm),
                   pl.BlockSpec((1, MLA_HEADS // 2, ROWS, MXU_N), hm)],
        out_shape=[jax.ShapeDtypeStruct((b, MLA_HEADS, ta, LANES), BF16),
                   jax.ShapeDtypeStruct((b, MLA_HEADS, ta, LANES), BF16),
                   jax.ShapeDtypeStruct((b, MLA_HEADS // 2, ta, MXU_N), BF16)],
        compiler_params=_params("arbitrary", "arbitrary"),
        name="mla_prep",
    )(mla_in, kr128, tab, qw, kvw, wuq, wuk, wuv)


def _attn_kernel(nct, off, n_ctx, q_ref, k_ref, v_ref, z_ref, o_ref):
    i = pl.program_id(2) + off

    def body(nk):
        parts = []
        for hh in range(ATTN_HEADS):
            s = _dot_nt(q_ref[0, hh], k_ref[0, hh, :nk, :])
            m = jnp.max(s, axis=-1, keepdims=True)
            p = jnp.exp2(s - m).astype(BF16)
            pv = _dot(p, v_ref[0, hh // 2, :nk, :])
            o = pv[:, (hh % 2) * MLA_DV:(hh % 2 + 1) * MLA_DV] / pv[:, LANES:LANES + 1]
            parts.append(o)
        o_ref[0] = jnp.concatenate(parts, axis=1) * _silu(z_ref[0])

    if off == 0:
        @pl.when(i < nct)
        def _():
            body(n_ctx)

        @pl.when(i >= nct)
        def _():
            body(k_ref.shape[2])
    else:
        body(k_ref.shape[2])


def _attention(q, k, v, z, nct, with_ctx):
    b, nh, ta, _ = q.shape
    nt = ta // ROWS
    off = 0 if with_ctx else nct
    nq = nt - off
    hg = ATTN_HEADS
    wo = hg * MLA_DV
    return pl.pallas_call(
        functools.partial(_attn_kernel, nct, off, nct * ROWS),
        grid=(b, nh // hg, nq),
        in_specs=[pl.BlockSpec((1, hg, ROWS, LANES), lambda bi, hp, i: (bi, hp, i + off, 0)),
                  pl.BlockSpec((1, hg, ta, LANES), lambda bi, hp, i: (bi, hp, 0, 0)),
                  pl.BlockSpec((1, hg // 2, ta, MXU_N), lambda bi, hp, i: (bi, hp, 0, 0)),
                  pl.BlockSpec((1, ROWS, wo), lambda bi, hp, i: (bi, i + off, hp))],
        out_specs=pl.BlockSpec((1, ROWS, wo), lambda bi, hp, i: (bi, i, hp)),
        out_shape=jax.ShapeDtypeStruct((b, nq * ROWS, MLA_W), F32),
        compiler_params=_params("arbitrary", "arbitrary", "arbitrary"),
        name="mla_attn",
    )(q, k, v, z)


def _outproj_kernel(final, gf_ref, gb_ref, gz_ref, gnw_ref, af_ref, ab_ref, az_ref, anw_ref,
                    ob_ref, m_ref, x_ref, mod_ref, w_ref, *rest):
    if final:
        fw_ref, o_ref = rest
    else:
        (o_ref,) = rest
    ob = ob_ref[...]

    def gated_head_norm(o, nw_ref, z):
        ms = _dot_sel(o * o, ob) * (1.0 / GDN_DV)
        return (o * lax.rsqrt(ms + EPS) * nw_ref[...] * _silu(z)).astype(BF16)

    g = gated_head_norm(gf_ref[0] + gb_ref[0], gnw_ref, gz_ref[0])
    a = gated_head_norm(af_ref[0] + ab_ref[0], anw_ref, az_ref[0])
    y = _dot(g, w_ref[0:GDN_W, :])
    y = y + _dot(m_ref[0].astype(BF16), w_ref[GDN_W:GDN_W + MLA_W, :])
    y = y + _dot(a, w_ref[GDN_W + MLA_W:, :])
    xn = x_ref[0] + mod_ref[0, 2:3, :] * y
    if final:
        xn = xn * lax.rsqrt(jnp.mean(xn * xn, axis=-1, keepdims=True) + EPS) * fw_ref[...]
    o_ref[0] = xn


def _outproj(gdn_f, gdn_b, gdn_in, gnw, gla_f, gla_b, gla_in, anw, ob, mla_o, x_all, mod, w_out,
             nct, final_w):
    assert GDN_DV == GLA_DV and GDN_W == GLA_W
    b, ta, d = x_all.shape
    nt = ta // ROWS
    final = final_w is not None
    off = nct if final else 0
    moff = off - (ta - mla_o.shape[1]) // ROWS
    nq = nt - off
    row = lambda bi, j: (bi, j + off, 0)
    full2 = lambda bi, j: (0, 0)
    wide = pl.BlockSpec((1, ROWS, GDN_W), row)
    gz_blk = GDN_QKV // GDN_W
    az_blk = (2 * GLA_QK + GLA_W) // GLA_W
    in_specs = [wide, wide,
                pl.BlockSpec((1, ROWS, GDN_W), lambda bi, j: (bi, j + off, gz_blk)),
                pl.BlockSpec(gnw.shape, full2),
                wide, wide,
                pl.BlockSpec((1, ROWS, GLA_W), lambda bi, j: (bi, j + off, az_blk)),
                pl.BlockSpec(anw.shape, full2),
                pl.BlockSpec(ob.shape, full2),
                pl.BlockSpec((1, ROWS, MLA_W), lambda bi, j: (bi, j + moff, 0)),
                pl.BlockSpec((1, ROWS, d), row),
                pl.BlockSpec((1, 3, d), lambda bi, j: (jnp.where(j + off < nct, 0, 1 + bi), 0, 0)),
                pl.BlockSpec(w_out.shape, full2)]
    args = [gdn_f, gdn_b, gdn_in, gnw, gla_f, gla_b, gla_in, anw, ob, mla_o, x_all, mod, w_out]
    if final:
        in_specs.append(pl.BlockSpec(final_w.shape, full2))
        args.append(final_w)
    return pl.pallas_call(
        functools.partial(_outproj_kernel, final),
        grid=(b, nq),
        in_specs=in_specs,
        out_specs=pl.BlockSpec((1, ROWS, d), lambda bi, j: (bi, j, 0)),
        out_shape=jax.ShapeDtypeStruct((b, nq * ROWS, d), F32),
        compiler_params=_params("arbitrary", "arbitrary"),
        name="out_proj",
    )(*args)


def kernel(x, c, ctx, c_ctx, w_ada, b_ada, w_in, gdn_conv_w, gdn_a_log, gdn_dt_bias, gdn_norm_w,
           mla_q_norm_w, mla_w_uq, mla_kv_norm_w, mla_w_ukv, gla_w_gk, gla_b_gk, gla_norm_w,
           w_out, final_norm_w):
    b, seq, d = x.shape
    n_ctx = ctx.shape[1]
    depth = w_in.shape[0]
    assert n_ctx % ROWS == 0 and seq % ROWS == 0 and seq % ROPE_GRID_W == 0
    nct = n_ctx // ROWS
    nh = GDN_HEADS

    x_all = jnp.concatenate([ctx, x], axis=1)
    pad_rows = (-(1 + b)) % 8
    cc = jnp.concatenate([c_ctx[None, :], c, jnp.zeros((pad_rows, d), F32)], axis=0)
    mod_all = _ada(cc, w_ada, b_ada).reshape(depth, cc.shape[0], 3, d)

    perm = _inproj_perm()
    qperm = _mla_q_perm()
    kperm, vperm = _mla_kv_perm()
    tab = _rope_tables(n_ctx, seq)
    ob64 = _head_block_ones(GDN_W, GDN_DV)
    exp_m = jnp.asarray(np.stack(
        [_expand_matrix(_S_A + dd * nh, nh, GDN_DK, GDN_QK) for dd in range(2)]
        + [_expand_matrix(_S_B + dd * nh, nh, GDN_DK, GDN_QK) for dd in range(2)]), BF16)
    eqk_np = np.zeros((GLA_QK, GLA_W), np.float32)
    sbd_np = np.zeros((GLA_W, GLA_QK), np.float32)
    for h in range(GLA_HEADS):
        eqk_np[h * GLA_DK:(h + 1) * GLA_DK, h * GLA_DV:(h + 1) * GLA_DV] = 1.0
        sbd_np[h * GLA_DV:(h + 1) * GLA_DV, h * GLA_DK:(h + 1) * GLA_DK] = 1.0
    eqk = jnp.asarray(eqk_np, BF16)
    sbd = jnp.asarray(sbd_np)

    out = None
    for layer in range(depth):
        last = layer == depth - 1
        w_p = _take_cols(w_in[layer], perm).astype(BF16)
        wabt = w_in[layer][:, _O_A:_O_A + 4 * nh].T.astype(BF16)
        mod = mod_all[layer]
        gdn_in, mla_in, mla_z, gla_in, small, kr128, abt = _inproj(x_all, mod, w_p, wabt, nct)

        convw = jnp.concatenate(
            [gdn_conv_w[layer], jnp.zeros((8 - GDN_CONV, GDN_QKV), F32)], axis=0)
        a_flat = gdn_a_log[layer].reshape(-1)
        dt_flat = gdn_dt_bias[layer].reshape(-1)
        prow = jnp.zeros((8, LANES), F32)
        prow = prow.at[0, _S_A:_S_A + 2 * nh].set(a_flat).at[1, _S_A:_S_A + 2 * nh].set(dt_flat)
        pcol = jnp.zeros((4 * nh, LANES), F32)
        pcol = pcol.at[0:2 * nh, 0].set(a_flat).at[0:2 * nh, 1].set(dt_flat)
        gnw = jnp.tile(gdn_norm_w[layer], nh)[None, :]
        gdn_f, gdn_b = _gdn(nct, gdn_in, small, abt, convw, prow, pcol, exp_m, ob64)

        anw = jnp.tile(gla_norm_w[layer], GLA_HEADS)[None, :]
        wg = jnp.zeros((2, LANES, GLA_QK), F32)
        for dd in range(2):
            r0 = _S_GLOW + dd * GLA_GATE_RANK
            wg = wg.at[dd, r0:r0 + GLA_GATE_RANK, :].set(gla_w_gk[layer, dd])
        gla_f, gla_b = _gla(nct, gla_in, small, wg.astype(BF16), gla_b_gk[layer][:, None, :],
                            eqk, sbd)

        wuq = _take_cols(mla_w_uq[layer], qperm).astype(BF16)
        wuk = _take_cols(mla_w_ukv[layer], kperm).astype(BF16)
        wuv = jnp.take(mla_w_ukv[layer], jnp.asarray(vperm), axis=1).astype(BF16)
        qh, kh, vh = _mla_prep(mla_in, kr128, tab, mla_q_norm_w[layer][None, :],
                               mla_kv_norm_w[layer][None, :], wuq, wuk, wuv)
        mla_o = _attention(qh, kh, vh, mla_z, nct, with_ctx=not last)

        res = _outproj(gdn_f, gdn_b, gdn_in, gnw, gla_f, gla_b, gla_in, anw, ob64, mla_o, x_all,
                       mod, w_out[layer].astype(BF16), nct, final_norm_w[None, :] if last else None)
        if last:
            out = res
        else:
            x_all = res
    return out
```

```python
import functools
import math

import numpy as np
import jax
import jax.numpy as jnp
from jax import lax
from jax.experimental import pallas as pl
from jax.experimental.pallas import tpu as pltpu

F32 = jnp.float32
BF16 = jnp.bfloat16
EPS = 1e-6

GDN_HEADS, GDN_DK, GDN_DV, GDN_CONV = 4, 64, 64, 5
GDN_QK = GDN_HEADS * GDN_DK
GDN_W = GDN_HEADS * GDN_DV
GDN_QKV = 2 * GDN_QK + GDN_W
MLA_HEADS, MLA_Q_RANK, MLA_KV_RANK = 8, 384, 256
MLA_NOPE, MLA_ROPE, MLA_DV = 64, 32, 64
MLA_W = MLA_HEADS * MLA_DV
MLA_SCALE = (MLA_NOPE + MLA_ROPE) ** -0.5
ROPE_THETA = 10000.0
ROPE_GRID_W = 64
GLA_HEADS, GLA_DK, GLA_DV = 4, 32, 64
GLA_QK = GLA_HEADS * GLA_DK
GLA_W = GLA_HEADS * GLA_DV
GLA_GATE_RANK = 16
GLA_GATE_NORM = 16.0
CHUNK = 64
LEVELS = (32, 16, 8, 4, 2, 1)

LANES = 128
MXU_N = 256
ROWS = 256
HALO = 8
ATTN_HEADS = 8
VMEM_LIMIT = 56 * 1024 * 1024

_O_GDN_QKV, _O_GDN_Z, _O_A, _O_B = 0, 768, 1024, 1032
_O_CQ, _O_CKV, _O_KR, _O_MLA_Z = 1040, 1424, 1680, 1712
_O_GLA_Q, _O_GLA_K, _O_GLA_V, _O_GLA_Z, _O_GLOW = 2224, 2352, 2480, 2736, 2992
_S_A, _S_B, _S_GLOW = 0, 8, 16
_W_GDN, _W_MLA, _W_MLAZ, _W_GLA = 1024, 640, 512, 768
_N_IN_PAD = _W_GDN + _W_MLA + _W_MLAZ + _W_GLA + 2 * LANES


def _dot(a, b):
    return jnp.dot(a, b, preferred_element_type=F32)


def _dot_nt(a, b):
    return lax.dot_general(a, b, (((1,), (1,)), ((), ())), preferred_element_type=F32)


def _dot_tn(a, b):
    return lax.dot_general(a, b, (((0,), (0,)), ((), ())), preferred_element_type=F32)


def _bdot(a, b):
    return _dot(a.astype(BF16), b.astype(BF16))


def _split2(x):
    x1 = x.astype(BF16)
    return x1, (x - x1.astype(F32)).astype(BF16)


def _split3(x):
    x1 = x.astype(BF16)
    r1 = x - x1.astype(F32)
    x2 = r1.astype(BF16)
    x3 = (r1 - x2.astype(F32)).astype(BF16)
    return x1, x2, x3


def _sel_dot(m01, x):
    n = x.shape[1]
    y = _dot(m01, jnp.concatenate(_split3(x), axis=1))
    return y[:, :n] + y[:, n:2 * n] + y[:, 2 * n:]


def _dot_sel(x, m01):
    n = x.shape[0]
    y = _dot(jnp.concatenate(_split3(x), axis=0), m01)
    return y[:n] + y[n:2 * n] + y[2 * n:]


def _dot_sel2(x, m01):
    n = x.shape[0]
    y = _dot(jnp.concatenate(_split2(x), axis=0), m01)
    return y[:n] + y[n:]


def _dot_sel_nt(x, m01):
    n = x.shape[0]
    y = _dot_nt(jnp.concatenate(_split3(x), axis=0), m01)
    return y[:n] + y[n:2 * n] + y[2 * n:]


def _softplus(x):
    return jnp.maximum(x, 0.0) + jnp.log1p(jnp.exp(-jnp.abs(x)))


def _silu(x):
    return x * jax.nn.sigmoid(x)


def _params(*sem):
    return pltpu.CompilerParams(dimension_semantics=sem, vmem_limit_bytes=VMEM_LIMIT)


def _scan_consts(rev):
    t = np.arange(ROWS)
    ch = t // CHUNK
    p = (CHUNK - 1 - t % CHUNK) if rev else (t % CHUNK)
    same = ch[:, None] == ch[None, :]
    tri = same & (p[None, :] <= p[:, None])
    mq, mk = [], []
    lv = np.full((ROWS, ROWS), -1.0, np.float32)
    for li, s in enumerate(LEVELS):
        blk = p // s
        mq.append(same & (p[None, :] > (blk * s)[:, None]) & (p[None, :] <= p[:, None]))
        mk.append(same & (p[None, :] > p[:, None]) & (p[None, :] <= ((blk + 1) * s)[:, None]))
        pair = same & ((blk % 2) == 1)[:, None] & (blk[None, :] == (blk - 1)[:, None])
        lv[pair] = li
    stack = np.concatenate([tri, same] + mq + mk, axis=0)
    lvm = np.stack([lv == li for li in range(len(LEVELS))])
    negm = np.where(tri, 0.0, -np.inf).astype(np.float32)
    return dict(tri=jnp.asarray(tri, BF16), ones=jnp.asarray(same, BF16), negm=jnp.asarray(negm),
                lv=jnp.asarray(lv), lvm=jnp.asarray(lvm, BF16), stack=jnp.asarray(stack, BF16))


def _head_block_ones(n, width):
    i = np.arange(n)
    return jnp.asarray((i[:, None] // width) == (i[None, :] // width), BF16)


def _inproj_perm():
    perm = np.full((_N_IN_PAD,), -1, np.int64)

    def put(src, n, at):
        perm[at:at + n] = np.arange(src, src + n)

    put(_O_GDN_QKV, GDN_QKV, 0)
    put(_O_GDN_Z, GDN_W, GDN_QKV)
    pos = _W_GDN
    put(_O_CQ, MLA_Q_RANK, pos)
    put(_O_CKV, MLA_KV_RANK, pos + MLA_Q_RANK)
    pos += _W_MLA
    put(_O_MLA_Z, MLA_W, pos)
    pos += _W_MLAZ
    put(_O_GLA_Q, GLA_QK, pos)
    put(_O_GLA_K, GLA_QK, pos + GLA_QK)
    put(_O_GLA_V, GLA_W, pos + 2 * GLA_QK)
    put(_O_GLA_Z, GLA_W, pos + 2 * GLA_QK + GLA_W)
    pos += _W_GLA
    put(_O_A, 2 * GDN_HEADS, pos + _S_A)
    put(_O_B, 2 * GDN_HEADS, pos + _S_B)
    put(_O_GLOW, 2 * GLA_GATE_RANK, pos + _S_GLOW)
    pos += LANES
    put(_O_KR, MLA_ROPE, pos + MLA_NOPE)
    return perm


def _take_cols(w, perm):
    cols = jnp.take(w, jnp.asarray(np.maximum(perm, 0)), axis=1)
    return jnp.where(jnp.asarray(perm >= 0)[None, :], cols, 0.0)


def _mla_q_perm():
    perm = np.full((MLA_HEADS * LANES,), -1, np.int64)
    d = MLA_NOPE + MLA_ROPE
    for h in range(MLA_HEADS):
        perm[h * LANES:h * LANES + d] = np.arange(h * d, (h + 1) * d)
    return perm


def _mla_kv_perm():
    dk = MLA_NOPE + MLA_DV
    kperm = np.full((MLA_HEADS * LANES,), -1, np.int64)
    vperm = np.zeros((MLA_W,), np.int64)
    for h in range(MLA_HEADS):
        kperm[h * LANES:h * LANES + MLA_NOPE] = np.arange(h * dk, h * dk + MLA_NOPE)
        vperm[h * MLA_DV:(h + 1) * MLA_DV] = np.arange(h * dk + MLA_NOPE, (h + 1) * dk)
    return kperm, vperm


def _rope_tables(n_ctx, n_lat):
    rows = n_lat // ROPE_GRID_W
    row = np.repeat(np.arange(rows, dtype=np.float32), ROPE_GRID_W)
    col = np.tile(np.arange(ROPE_GRID_W, dtype=np.float32), rows)
    n_freq = MLA_ROPE // 4
    inv = (ROPE_THETA ** (-np.arange(n_freq, dtype=np.float32) / n_freq)).astype(np.float32)
    ang = np.concatenate([row[:, None] * inv, col[:, None] * inv], axis=-1)
    cos = np.concatenate([np.ones((n_ctx, 2 * n_freq), np.float32), np.cos(ang)], 0)
    sin = np.concatenate([np.zeros((n_ctx, 2 * n_freq), np.float32), np.sin(ang)], 0)
    n = n_ctx + n_lat
    half = MLA_ROPE // 2
    tab = np.zeros((3, n, LANES), np.float32)
    tab[0, :, :MLA_NOPE] = 1.0
    tab[0, :, MLA_NOPE:MLA_NOPE + half] = cos
    tab[0, :, MLA_NOPE + half:MLA_NOPE + MLA_ROPE] = cos
    tab[1, :, MLA_NOPE:MLA_NOPE + half] = -sin
    tab[2, :, MLA_NOPE + half:MLA_NOPE + MLA_ROPE] = sin
    return jnp.asarray(tab)


def _expand_matrix(src0, n_src, width, n_out):
    m = np.zeros((LANES, n_out), np.float32)
    for h in range(n_src):
        m[src0 + h, h * width:(h + 1) * width] = 1.0
    return m


def _ada_kernel(c_ref, w_ref, b_ref, o_ref):
    c = _silu(c_ref[...]).astype(BF16)
    o_ref[0] = _dot(c, w_ref[0].astype(BF16)) + b_ref[0]


def _ada(cc, w_ada, b_ada):
    nl, d, n3 = w_ada.shape
    r = cc.shape[0]
    tn = 1024
    return pl.pallas_call(
        _ada_kernel,
        grid=(nl, n3 // tn),
        in_specs=[pl.BlockSpec((r, d), lambda l, j: (0, 0)),
                  pl.BlockSpec((1, d, tn), lambda l, j: (l, 0, j)),
                  pl.BlockSpec((1, 1, tn), lambda l, j: (l, 0, j))],
        out_specs=pl.BlockSpec((1, r, tn), lambda l, j: (l, 0, j)),
        out_shape=jax.ShapeDtypeStruct((nl, r, n3), F32),
        compiler_params=_params("arbitrary", "arbitrary"),
        name="ada_mod",
    )(cc, w_ada, b_ada.reshape(nl, 1, n3))


def _inproj_kernel(x_ref, mod_ref, w_ref, wabt_ref, ogdn, omla, omlaz, ogla, osmall, okr, oabt):
    x = x_ref[0]
    h = x * lax.rsqrt(jnp.mean(x * x, axis=-1, keepdims=True) + EPS)
    h = h * (1.0 + mod_ref[0, 1:2, :]) + mod_ref[0, 0:1, :]
    hb = h.astype(BF16)
    pos = 0
    for ref, n in ((ogdn, _W_GDN), (omla, _W_MLA), (omlaz, _W_MLAZ), (ogla, _W_GLA),
                   (osmall, LANES), (okr, LANES)):
        ref[0] = _dot(hb, w_ref[:, pos:pos + n])
        pos += n
    oabt[0] = _dot_nt(wabt_ref[...], hb)


def _inproj(x_all, mod, w_p, wabt, nct):
    b, ta, d = x_all.shape
    nt = ta // ROWS
    widths = (_W_GDN, _W_MLA, _W_MLAZ, _W_GLA, LANES, LANES)
    row = lambda bi, j: (bi, j, 0)
    return pl.pallas_call(
        _inproj_kernel,
        grid=(b, nt),
        in_specs=[pl.BlockSpec((1, ROWS, d), row),
                  pl.BlockSpec((1, 3, d), lambda bi, j: (jnp.where(j < nct, 0, 1 + bi), 0, 0)),
                  pl.BlockSpec(w_p.shape, lambda bi, j: (0, 0)),
                  pl.BlockSpec(wabt.shape, lambda bi, j: (0, 0))],
        out_specs=[pl.BlockSpec((1, ROWS, n), row) for n in widths]
        + [pl.BlockSpec((1, 4 * GDN_HEADS, ROWS), lambda bi, j: (bi, 0, j))],
        out_shape=[jax.ShapeDtypeStruct((b, ta, n), F32) for n in widths]
        + [jax.ShapeDtypeStruct((b, 4 * GDN_HEADS, ta), F32)],
        compiler_params=_params("arbitrary", "arbitrary"),
        name="in_proj",
    )(x_all, mod, w_p, wabt)


def _scan_block_index(j, nct, nt, rev):
    if not rev:
        return j
    return jnp.where(j < nct, nct - 1 - j, nt - 1 - (j - nct))


def _head_lane_mask(n, width, h):
    lane = lax.broadcasted_iota(jnp.int32, (1, n), 1)
    return (lane >= h * width) & (lane < (h + 1) * width)


def _gdn_prep(d, blk, nct, nt, x_ref, prev_ref, next_ref, small_ref, abt_ref, convw_ref, prow_ref,
              pcol_ref, tri, ones, exp_ref, ob, xe_scr):
    dk = GDN_DK
    has_prev = jnp.logical_and(blk != 0, blk != nct)
    has_next = jnp.logical_and(blk != nct - 1, blk != nt - 1)
    xe_scr[d, 0:HALO, :] = jnp.where(has_prev, prev_ref[0], 0.0)
    xe_scr[d, HALO:HALO + ROWS, :] = x_ref[0, :, :GDN_QKV]
    xe_scr[d, HALO + ROWS:, :] = jnp.where(has_next, next_ref[0], 0.0)
    pad = (GDN_CONV - 1) // 2
    conv = jnp.zeros((ROWS, GDN_QKV), F32)
    for t in range(GDN_CONV):
        conv = conv + convw_ref[t:t + 1, :] * xe_scr[d, pl.ds(HALO - pad + t, ROWS), :]
    hqkv = _silu(conv)
    q = hqkv[:, :GDN_QK]
    k = hqkv[:, GDN_QK:2 * GDN_QK]
    v = hqkv[:, 2 * GDN_QK:]
    q = q * lax.rsqrt(_dot_sel2(q * q, ob) + EPS) * (dk ** -0.5)
    k = k * lax.rsqrt(_dot_sel2(k * k, ob) + EPS)
    sm = small_ref[0]
    g_all = -jnp.exp(prow_ref[0:1, :]) * _softplus(sm + prow_ref[1:2, :])
    beta_all = jax.nn.sigmoid(sm)
    gc_all = _sel_dot(tri, g_all)
    gl_all = _sel_dot(ones, g_all)
    g_t = -jnp.exp(pcol_ref[:, 0:1]) * _softplus(abt_ref[0] + pcol_ref[:, 1:2])
    gc_t = _dot_sel_nt(g_t, tri)
    gc_w = _dot_sel2(gc_all, exp_ref[d])
    gl_w = _dot_sel2(gl_all, exp_ref[d])
    beta_w = _dot_sel2(beta_all, exp_ref[2 + d])
    kb = k * beta_w
    return dict(q=q, kbf=k.astype(BF16), kb=kb, vb=v * beta_w, kbg=kb * jnp.exp(gc_w),
                qg=q * jnp.exp(gc_w), kdec=k * jnp.exp(gl_w - gc_w),
                gc_all=gc_all, gc_t=gc_t, gl_all=gl_all)


def _gdn_kernel(nct, xf_ref, pf_ref, nf_ref, smf_ref, abtf_ref, xb_ref, pb_ref, nb_ref, smb_ref,
                abtb_ref, convw_ref, prow_ref, pcol_ref, tri_ref, ones_ref, negm_ref, lvm_ref,
                exp_ref, ob_ref, of_ref, obk_ref, xe_scr, s_scr):
    j = pl.program_id(1)
    nt = pl.num_programs(1)
    nh, dk = GDN_HEADS, GDN_DK
    nchunk = ROWS // CHUNK
    nlev = len(LEVELS)

    @pl.when(j == 0)
    def _():
        s_scr[...] = jnp.zeros_like(s_scr)

    ob = ob_ref[...]
    ones = ones_ref[...]
    dir_refs = ((xf_ref, pf_ref, nf_ref, smf_ref, abtf_ref), (xb_ref, pb_ref, nb_ref, smb_ref, abtb_ref))
    prep = []
    for d in range(2):
        blk = _scan_block_index(j, nct, nt, d == 1)
        prep.append(_gdn_prep(d, blk, nct, nt, *dir_refs[d], convw_ref, prow_ref, pcol_ref,
                              tri_ref[d], ones, exp_ref, ob, xe_scr))
    ri = lax.broadcasted_iota(jnp.int32, (ROWS, ROWS), 0)
    ci = lax.broadcasted_iota(jnp.int32, (ROWS, ROWS), 1)
    eye = (ri == ci).astype(BF16)
    chains = [(d, h) for d in range(2) for h in range(nh)]
    nc = len(chains)

    low, a_intra = [], []
    for d, h in chains:
        p = prep[d]
        idx = d * nh + h
        hm = _head_lane_mask(GDN_QK, dk, h)
        decay = jnp.exp(p["gc_all"][:, idx:idx + 1] - p["gc_t"][idx:idx + 1, :] + negm_ref[d])
        kk = _dot_nt(jnp.where(hm, p["kb"], 0.0).astype(BF16), p["kbf"])
        qk = _dot_nt(jnp.where(hm, p["q"], 0.0).astype(BF16), p["kbf"])
        low.append((kk * decay).astype(BF16))
        a_intra.append((qk * decay).astype(BF16))

    t_inv = [eye - low[i] * lvm_ref[d, nlev - 1] for i, (d, h) in enumerate(chains)]
    for li in reversed(range(nlev - 1)):
        ys = [_dot(low[i] * lvm_ref[d, li], t_inv[i]).astype(BF16) for i, (d, h) in enumerate(chains)]
        t_inv = [t_inv[i] - _dot(t_inv[i], ys[i]).astype(BF16) for i in range(nc)]
    uw = []
    for i, (d, h) in enumerate(chains):
        sl = slice(h * dk, (h + 1) * dk)
        rhs = jnp.concatenate([prep[d]["vb"][:, sl], prep[d]["kbg"][:, sl]], axis=1)
        uw.append(_dot(t_inv[i], rhs.astype(BF16)))

    q2, ou, gb = [], [], []
    for i, (d, h) in enumerate(chains):
        sl = slice(h * dk, (h + 1) * dk)
        uwb = uw[i].astype(BF16)
        auw = _dot(a_intra[i], uwb)
        ou.append(auw[:, :GDN_DV])
        q2.append((prep[d]["qg"][:, sl] - auw[:, GDN_DV:]).astype(BF16))
        kd = prep[d]["kdec"][:, sl].astype(BF16)
        gb.append([_dot_tn(kd[c * CHUNK:(c + 1) * CHUNK], uwb[c * CHUNK:(c + 1) * CHUNK])
                   for c in range(nchunk)])
    states = [s_scr[i] for i in range(nc)]
    o_parts = [[None] * nchunk for _ in chains]
    for step in range(nchunk):
        for i, (d, h) in enumerate(chains):
            c = nchunk - 1 - step if d == 1 else step
            idx = d * nh + h
            r0 = c * CHUNK
            rs = slice(r0, r0 + CHUNK)
            sb = states[i].astype(BF16)
            o_parts[i][c] = _dot(q2[i][rs], sb) + ou[i][rs]
            states[i] = (states[i] * jnp.exp(prep[d]["gl_all"][r0:r0 + 1, idx:idx + 1])
                         - _dot(gb[i][c][:, GDN_DV:].astype(BF16), sb) + gb[i][c][:, :GDN_DV])
    for i in range(len(chains)):
        s_scr[i] = states[i]
    outs = [jnp.concatenate(o_parts[i], axis=0) for i in range(len(chains))]
    of_ref[0] = jnp.concatenate(outs[:nh], axis=1)
    obk_ref[0] = jnp.concatenate(outs[nh:], axis=1)


def _scan_row_specs(nct, nt, rev, width):
    blk_of = functools.partial(_scan_block_index, nct=nct, nt=nt, rev=rev)
    return pl.BlockSpec((1, ROWS, width), lambda bi, j: (bi, blk_of(j), 0))


def _gdn(nct, gdn_in, small, abt, convw, prow, pcol, exp_m, ob):
    b, ta, _ = gdn_in.shape
    nt = ta // ROWS
    cf, cb = _scan_consts(False), _scan_consts(True)
    stack = lambda name: jnp.stack([cf[name], cb[name]])
    hb = ROWS // HALO
    last_halo = ta // HALO - 1
    full2 = lambda bi, j: (0, 0)
    full3 = lambda bi, j: (0, 0, 0)

    def dir_specs(rev):
        blk_of = functools.partial(_scan_block_index, nct=nct, nt=nt, rev=rev)
        return [
            _scan_row_specs(nct, nt, rev, _W_GDN),
            pl.BlockSpec((1, HALO, GDN_QKV),
                         lambda bi, j: (bi, jnp.maximum(blk_of(j) * hb - 1, 0), 0)),
            pl.BlockSpec((1, HALO, GDN_QKV),
                         lambda bi, j: (bi, jnp.minimum((blk_of(j) + 1) * hb, last_halo), 0)),
            _scan_row_specs(nct, nt, rev, LANES),
            pl.BlockSpec((1, 4 * GDN_HEADS, ROWS), lambda bi, j: (bi, 0, blk_of(j))),
        ]

    in_specs = dir_specs(False) + dir_specs(True) + [
        pl.BlockSpec(convw.shape, full2),
        pl.BlockSpec(prow.shape, full2),
        pl.BlockSpec(pcol.shape, full2),
        pl.BlockSpec((2, ROWS, ROWS), full3),
        pl.BlockSpec((ROWS, ROWS), full2),
        pl.BlockSpec((2, ROWS, ROWS), full3),
        pl.BlockSpec((2, len(LEVELS), ROWS, ROWS), lambda bi, j: (0, 0, 0, 0)),
        pl.BlockSpec(exp_m.shape, full3),
        pl.BlockSpec(ob.shape, full2),
    ]
    dir_args = [gdn_in, gdn_in, gdn_in, small, abt]
    args = dir_args + dir_args + [convw, prow, pcol, stack("tri"), cf["ones"], stack("negm"),
                                  stack("lvm"), exp_m, ob]
    return pl.pallas_call(
        functools.partial(_gdn_kernel, nct),
        grid=(b, nt),
        in_specs=in_specs,
        out_specs=[_scan_row_specs(nct, nt, False, GDN_W), _scan_row_specs(nct, nt, True, GDN_W)],
        out_shape=[jax.ShapeDtypeStruct((b, ta, GDN_W), F32)] * 2,
        scratch_shapes=[pltpu.VMEM((2, ROWS + 2 * HALO, GDN_QKV), F32),
                        pltpu.VMEM((2 * GDN_HEADS, GDN_DK, GDN_DV), F32)],
        compiler_params=_params("arbitrary", "arbitrary"),
        name="gdn_scan",
    )(*args)


def _gla_kernel(xf_ref, smf_ref, xb_ref, smb_ref, wg_ref, bg_ref, stack_ref, lv_ref, eqk_ref,
                sbd_ref, of_ref, obk_ref, s_scr):
    j = pl.program_id(1)
    nh, dk, dv = GLA_HEADS, GLA_DK, GLA_DV
    nchunk = ROWS // CHUNK
    nlev = len(LEVELS)
    n = GLA_QK

    @pl.when(j == 0)
    def _():
        s_scr[...] = jnp.zeros_like(s_scr)

    lane_head = [_head_lane_mask(GLA_QK, dk, h) for h in range(nh)]
    out_head = [_head_lane_mask(GLA_W, dv, h) for h in range(nh)]
    eqk = eqk_ref[...]
    sbd = sbd_ref[...]
    x_refs, sm_refs, o_refs = (xf_ref, xb_ref), (smf_ref, smb_ref), (of_ref, obk_ref)
    dirs = range(2)
    xs = [x_refs[d][0] for d in dirs]
    q = [xs[d][:, :GLA_QK] * (dk ** -0.5) for d in dirs]
    k = [xs[d][:, GLA_QK:2 * GLA_QK] for d in dirs]
    v = [xs[d][:, 2 * GLA_QK:2 * GLA_QK + GLA_W] for d in dirs]
    vb = [v[d].astype(BF16) for d in dirs]
    gk = [_dot(sm_refs[d][0].astype(BF16), wg_ref[d]) + bg_ref[d] for d in dirs]
    la = [-_softplus(-gk[d]) * (1.0 / GLA_GATE_NORM) for d in dirs]
    la2 = [jnp.concatenate(_split2(la[d]), axis=1) for d in dirs]
    ys = [_dot(stack_ref[d], la2[d]) for d in dirs]
    cums = [ys[d][:, :n] + ys[d][:, n:] for d in dirs]
    piece = lambda d, i: cums[d][i * ROWS:(i + 1) * ROWS]
    bcum = [piece(d, 0) for d in dirs]
    blast = [piece(d, 1) for d in dirs]
    qg = [(q[d] * jnp.exp(bcum[d])).astype(BF16) for d in dirs]
    kdec = [(k[d] * jnp.exp(blast[d] - bcum[d])).astype(BF16) for d in dirs]

    lv4 = [jnp.concatenate([lv_ref[d]] * nh, axis=0) for d in dirs]
    acc = [jnp.zeros((nh * ROWS, ROWS), F32) for d in dirs]
    for li in range(nlev):
        ql = [q[d] * jnp.exp(piece(d, 2 + li)) for d in dirs]
        kl = [(k[d] * jnp.exp(piece(d, 2 + nlev + li))).astype(BF16) for d in dirs]
        qs = [jnp.concatenate([jnp.where(lane_head[h], ql[d], 0.0) for h in range(nh)],
                              axis=0).astype(BF16) for d in dirs]
        ps = [_dot_nt(qs[d], kl[d]) for d in dirs]
        acc = [jnp.where(lv4[d] == float(li), ps[d], acc[d]) for d in dirs]
    pv = [_dot(acc[d].astype(BF16), vb[d]) for d in dirs]
    o = [_dot_sel(q[d] * k[d], eqk) * v[d] for d in dirs]
    for h in range(nh):
        o = [o[d] + jnp.where(out_head[h], pv[d][h * ROWS:(h + 1) * ROWS], 0.0) for d in dirs]

    states = [s_scr[d] for d in dirs]
    o_parts = [[None] * nchunk for d in dirs]
    for step in range(nchunk):
        for d in dirs:
            c = nchunk - 1 - step if d == 1 else step
            rs = slice(c * CHUNK, (c + 1) * CHUNK)
            o_parts[d][c] = _dot_nt(qg[d][rs], states[d].astype(BF16))
            states[d] = (states[d] * jnp.exp(blast[d][c * CHUNK:c * CHUNK + 1, :])
                         + sbd * _dot_tn(vb[d][rs], kdec[d][rs]))
    for d in dirs:
        s_scr[d] = states[d]
        o_refs[d][0] = o[d] + jnp.concatenate(o_parts[d], axis=0)


def _gla(nct, gla_in, small, wg, bg, eqk, sbd):
    b, ta, _ = gla_in.shape
    nt = ta // ROWS
    cf, cb = _scan_consts(False), _scan_consts(True)
    stack = jnp.stack([cf["stack"], cb["stack"]])
    lv = jnp.stack([cf["lv"], cb["lv"]])
    full2 = lambda bi, j: (0, 0)
    full3 = lambda bi, j: (0, 0, 0)
    in_specs = [
        _scan_row_specs(nct, nt, False, _W_GLA), _scan_row_specs(nct, nt, False, LANES),
        _scan_row_specs(nct, nt, True, _W_GLA), _scan_row_specs(nct, nt, True, LANES),
        pl.BlockSpec(wg.shape, full3),
        pl.BlockSpec(bg.shape, full3),
        pl.BlockSpec(stack.shape, full3),
        pl.BlockSpec(lv.shape, full3),
        pl.BlockSpec(eqk.shape, full2),
        pl.BlockSpec(sbd.shape, full2),
    ]
    return pl.pallas_call(
        _gla_kernel,
        grid=(b, nt),
        in_specs=in_specs,
        out_specs=[_scan_row_specs(nct, nt, False, GLA_W), _scan_row_specs(nct, nt, True, GLA_W)],
        out_shape=[jax.ShapeDtypeStruct((b, ta, GLA_W), F32)] * 2,
        scratch_shapes=[pltpu.VMEM((2, GLA_W, GLA_QK), F32)],
        compiler_params=_params("arbitrary", "arbitrary"),
        name="gla_scan",
    )(gla_in, small, gla_in, small, wg, bg, stack, lv, eqk, sbd)


def _rope(x, tab_ref):
    half = MLA_ROPE // 2
    return (x * tab_ref[0] + pltpu.roll(x, LANES - half, 1) * tab_ref[1]
            + pltpu.roll(x, half, 1) * tab_ref[2])


def _mla_prep_kernel(x_ref, kr_ref, tab_ref, qw_ref, kvw_ref, wuq_ref, wuk_ref, wuv_ref,
                     q_ref, k_ref, v_ref):
    x = x_ref[0]
    cq = x[:, :MLA_Q_RANK]
    ckv = x[:, MLA_Q_RANK:]
    cq = cq * lax.rsqrt(jnp.mean(cq * cq, axis=-1, keepdims=True) + EPS) * qw_ref[...]
    ckv = ckv * lax.rsqrt(jnp.mean(ckv * ckv, axis=-1, keepdims=True) + EPS) * kvw_ref[...]
    cqb = cq.astype(BF16)
    ckvb = ckv.astype(BF16)
    qf = _dot(cqb, wuq_ref[...])
    kf = _dot(ckvb, wuk_ref[...])
    vf = _dot(ckvb, wuv_ref[...])
    kr = _rope(kr_ref[0], tab_ref)
    q_scale = MLA_SCALE * math.log2(math.e)
    for h in range(MLA_HEADS):
        sl = slice(h * LANES, (h + 1) * LANES)
        q_ref[0, h] = (_rope(qf[:, sl], tab_ref) * q_scale).astype(BF16)
        k_ref[0, h] = (kf[:, sl] + kr).astype(BF16)
    ones = jnp.ones((x.shape[0], MXU_N - LANES), BF16)
    for hp in range(MLA_HEADS // 2):
        v_ref[0, hp] = jnp.concatenate(
            [vf[:, hp * LANES:(hp + 1) * LANES].astype(BF16), ones], axis=1)


def _mla_prep(mla_in, kr128, tab, qw, kvw, wuq, wuk, wuv):
    b, ta, _ = mla_in.shape
    nt = ta // ROWS
    row = lambda bi, j: (bi, j, 0)
    full2 = lambda bi, j: (0, 0)
    hm = lambda bi, j: (bi, 0, j, 0)
    return pl.pallas_call(
        _mla_prep_kernel,
        grid=(b, nt),
        in_specs=[pl.BlockSpec((1, ROWS, _W_MLA), row),
                  pl.BlockSpec((1, ROWS, LANES), row),
                  pl.BlockSpec((3, ROWS, LANES), lambda bi, j: (0, j, 0)),
                  pl.BlockSpec(qw.shape, full2), pl.BlockSpec(kvw.shape, full2),
                  pl.BlockSpec(wuq.shape, full2), pl.BlockSpec(wuk.shape, full2),
                  pl.BlockSpec(wuv.shape, full2)],
        out_specs=[pl.BlockSpec((1, MLA_HEADS, ROWS, LANES), hm),
                   pl.BlockSpec((1, MLA_HEADS, ROWS, LANES), hm),
                   pl.BlockSpec((1, MLA_HEADS // 2, ROWS, MXU_N), hm)],
        out_shape=[jax.ShapeDtypeStruct((b, MLA_HEADS, ta, LANES), BF16),
                   jax.ShapeDtypeStruct((b, MLA_HEADS, ta, LANES), BF16),
                   jax.ShapeDtypeStruct((b, MLA_HEADS // 2, ta, MXU_N), BF16)],
        compiler_params=_params("arbitrary", "arbitrary"),
        name="mla_prep",
    )(mla_in, kr128, tab, qw, kvw, wuq, wuk, wuv)


def _attn_kernel(nct, off, n_ctx, q_ref, k_ref, v_ref, z_ref, o_ref):
    i = pl.program_id(2) + off

    def body(nk):
        parts = []
        for hh in range(ATTN_HEADS):
            s = _dot_nt(q_ref[0, hh], k_ref[0, hh, :nk, :])
            m = jnp.max(s, axis=-1, keepdims=True)
            p = jnp.exp2(s - m).astype(BF16)
            pv = _dot(p, v_ref[0, hh // 2, :nk, :])
            o = pv[:, (hh % 2) * MLA_DV:(hh % 2 + 1) * MLA_DV] / pv[:, LANES:LANES + 1]
            parts.append(o)
        o_ref[0] = jnp.concatenate(parts, axis=1) * _silu(z_ref[0])

    if off == 0:
        @pl.when(i < nct)
        def _():
            body(n_ctx)

        @pl.when(i >= nct)
        def _():
            body(k_ref.shape[2])
    else:
        body(k_ref.shape[2])


def _attention(q, k, v, z, nct, with_ctx):
    b, nh, ta, _ = q.shape
    nt = ta // ROWS
    off = 0 if with_ctx else nct
    nq = nt - off
    hg = ATTN_HEADS
    wo = hg * MLA_DV
    return pl.pallas_call(
        functools.partial(_attn_kernel, nct, off, nct * ROWS),
        grid=(b, nh // hg, nq),
        in_specs=[pl.BlockSpec((1, hg, ROWS, LANES), lambda bi, hp, i: (bi, hp, i + off, 0)),
                  pl.BlockSpec((1, hg, ta, LANES), lambda bi, hp, i: (bi, hp, 0, 0)),
                  pl.BlockSpec((1, hg // 2, ta, MXU_N), lambda bi, hp, i: (bi, hp, 0, 0)),
                  pl.BlockSpec((1, ROWS, wo), lambda bi, hp, i: (bi, i + off, hp))],
        out_specs=pl.BlockSpec((1, ROWS, wo), lambda bi, hp, i: (bi, i, hp)),
        out_shape=jax.ShapeDtypeStruct((b, nq * ROWS, MLA_W), F32),
        compiler_params=_params("arbitrary", "arbitrary", "arbitrary"),
        name="mla_attn",
    )(q, k, v, z)


def _outproj_kernel(final, gf_ref, gb_ref, gz_ref, gnw_ref, af_ref, ab_ref, az_ref, anw_ref,
                    ob_ref, m_ref, x_ref, mod_ref, w_ref, *rest):
    if final:
        fw_ref, o_ref = rest
    else:
        (o_ref,) = rest
    ob = ob_ref[...]

    def gated_head_norm(o, nw_ref, z):
        ms = _dot_sel2(o * o, ob) * (1.0 / GDN_DV)
        return (o * lax.rsqrt(ms + EPS) * nw_ref[...] * _silu(z)).astype(BF16)

    g = gated_head_norm(gf_ref[0] + gb_ref[0], gnw_ref, gz_ref[0])
    a = gated_head_norm(af_ref[0] + ab_ref[0], anw_ref, az_ref[0])
    y = _dot(g, w_ref[0:GDN_W, :])
    y = y + _dot(m_ref[0].astype(BF16), w_ref[GDN_W:GDN_W + MLA_W, :])
    y = y + _dot(a, w_ref[GDN_W + MLA_W:, :])
    xn = x_ref[0] + mod_ref[0, 2:3, :] * y
    if final:
        xn = xn * lax.rsqrt(jnp.mean(xn * xn, axis=-1, keepdims=True) + EPS) * fw_ref[...]
    o_ref[0] = xn


def _outproj(gdn_f, gdn_b, gdn_in, gnw, gla_f, gla_b, gla_in, anw, ob, mla_o, x_all, mod, w_out,
             nct, final_w):
    assert GDN_DV == GLA_DV and GDN_W == GLA_W
    b, ta, d = x_all.shape
    nt = ta // ROWS
    final = final_w is not None
    off = nct if final else 0
    moff = off - (ta - mla_o.shape[1]) // ROWS
    nq = nt - off
    row = lambda bi, j: (bi, j + off, 0)
    full2 = lambda bi, j: (0, 0)
    wide = pl.BlockSpec((1, ROWS, GDN_W), row)
    gz_blk = GDN_QKV // GDN_W
    az_blk = (2 * GLA_QK + GLA_W) // GLA_W
    in_specs = [wide, wide,
                pl.BlockSpec((1, ROWS, GDN_W), lambda bi, j: (bi, j + off, gz_blk)),
                pl.BlockSpec(gnw.shape, full2),
                wide, wide,
                pl.BlockSpec((1, ROWS, GLA_W), lambda bi, j: (bi, j + off, az_blk)),
                pl.BlockSpec(anw.shape, full2),
                pl.BlockSpec(ob.shape, full2),
                pl.BlockSpec((1, ROWS, MLA_W), lambda bi, j: (bi, j + moff, 0)),
                pl.BlockSpec((1, ROWS, d), row),
                pl.BlockSpec((1, 3, d), lambda bi, j: (jnp.where(j + off < nct, 0, 1 + bi), 0, 0)),
                pl.BlockSpec(w_out.shape, full2)]
    args = [gdn_f, gdn_b, gdn_in, gnw, gla_f, gla_b, gla_in, anw, ob, mla_o, x_all, mod, w_out]
    if final:
        in_specs.append(pl.BlockSpec(final_w.shape, full2))
        args.append(final_w)
    return pl.pallas_call(
        functools.partial(_outproj_kernel, final),
        grid=(b, nq),
        in_specs=in_specs,
        out_specs=pl.BlockSpec((1, ROWS, d), lambda bi, j: (bi, j, 0)),
        out_shape=jax.ShapeDtypeStruct((b, nq * ROWS, d), F32),
        compiler_params=_params("arbitrary", "arbitrary"),
        name="out_proj",
    )(*args)


def kernel(x, c, ctx, c_ctx, w_ada, b_ada, w_in, gdn_conv_w, gdn_a_log, gdn_dt_bias, gdn_norm_w,
           mla_q_norm_w, mla_w_uq, mla_kv_norm_w, mla_w_ukv, gla_w_gk, gla_b_gk, gla_norm_w,
           w_out, final_norm_w):
    b, seq, d = x.shape
    n_ctx = ctx.shape[1]
    depth = w_in.shape[0]
    assert n_ctx % ROWS == 0 and seq % ROWS == 0 and seq % ROPE_GRID_W == 0
    nct = n_ctx // ROWS
    nh = GDN_HEADS

    x_all = jnp.concatenate([ctx, x], axis=1)
    pad_rows = (-(1 + b)) % 8
    cc = jnp.concatenate([c_ctx[None, :], c, jnp.zeros((pad_rows, d), F32)], axis=0)
    mod_all = _ada(cc, w_ada, b_ada).reshape(depth, cc.shape[0], 3, d)

    perm = _inproj_perm()
    qperm = _mla_q_perm()
    kperm, vperm = _mla_kv_perm()
    tab = _rope_tables(n_ctx, seq)
    ob64 = _head_block_ones(GDN_W, GDN_DV)
    exp_m = jnp.asarray(np.stack(
        [_expand_matrix(_S_A + dd * nh, nh, GDN_DK, GDN_QK) for dd in range(2)]
        + [_expand_matrix(_S_B + dd * nh, nh, GDN_DK, GDN_QK) for dd in range(2)]), BF16)
    eqk_np = np.zeros((GLA_QK, GLA_W), np.float32)
    sbd_np = np.zeros((GLA_W, GLA_QK), np.float32)
    for h in range(GLA_HEADS):
        eqk_np[h * GLA_DK:(h + 1) * GLA_DK, h * GLA_DV:(h + 1) * GLA_DV] = 1.0
        sbd_np[h * GLA_DV:(h + 1) * GLA_DV, h * GLA_DK:(h + 1) * GLA_DK] = 1.0
    eqk = jnp.asarray(eqk_np, BF16)
    sbd = jnp.asarray(sbd_np)

    out = None
    for layer in range(depth):
        last = layer == depth - 1
        w_p = _take_cols(w_in[layer], perm).astype(BF16)
        wabt = w_in[layer][:, _O_A:_O_A + 4 * nh].T.astype(BF16)
        mod = mod_all[layer]
        gdn_in, mla_in, mla_z, gla_in, small, kr128, abt = _inproj(x_all, mod, w_p, wabt, nct)

        convw = jnp.concatenate(
            [gdn_conv_w[layer], jnp.zeros((8 - GDN_CONV, GDN_QKV), F32)], axis=0)
        a_flat = gdn_a_log[layer].reshape(-1)
        dt_flat = gdn_dt_bias[layer].reshape(-1)
        prow = jnp.zeros((8, LANES), F32)
        prow = prow.at[0, _S_A:_S_A + 2 * nh].set(a_flat).at[1, _S_A:_S_A + 2 * nh].set(dt_flat)
        pcol = jnp.zeros((4 * nh, LANES), F32)
        pcol = pcol.at[0:2 * nh, 0].set(a_flat).at[0:2 * nh, 1].set(dt_flat)
        gnw = jnp.tile(gdn_norm_w[layer], nh)[None, :]
        gdn_f, gdn_b = _gdn(nct, gdn_in, small, abt, convw, prow, pcol, exp_m, ob64)

        anw = jnp.tile(gla_norm_w[layer], GLA_HEADS)[None, :]
        wg = jnp.zeros((2, LANES, GLA_QK), F32)
        for dd in range(2):
            r0 = _S_GLOW + dd * GLA_GATE_RANK
            wg = wg.at[dd, r0:r0 + GLA_GATE_RANK, :].set(gla_w_gk[layer, dd])
        gla_f, gla_b = _gla(nct, gla_in, small, wg.astype(BF16), gla_b_gk[layer][:, None, :],
                            eqk, sbd)

        wuq = _take_cols(mla_w_uq[layer], qperm).astype(BF16)
        wuk = _take_cols(mla_w_ukv[layer], kperm).astype(BF16)
        wuv = jnp.take(mla_w_ukv[layer], jnp.asarray(vperm), axis=1).astype(BF16)
        qh, kh, vh = _mla_prep(mla_in, kr128, tab, mla_q_norm_w[layer][None, :],
                               mla_kv_norm_w[layer][None, :], wuq, wuk, wuv)
        mla_o = _attention(qh, kh, vh, mla_z, nct, with_ctx=not last)

        res = _outproj(gdn_f, gdn_b, gdn_in, gnw, gla_f, gla_b, gla_in, anw, ob64, mla_o, x_all,
                       mod, w_out[layer].astype(BF16), nct, final_norm_w[None, :] if last else None)
        if last:
            out = res
        else:
            x_all = res
    return out
```

```python
import functools
import math

import numpy as np
import jax
import jax.numpy as jnp
from jax import lax
from jax.experimental import pallas as pl
from jax.experimental.pallas import tpu as pltpu

F32 = jnp.float32
BF16 = jnp.bfloat16
EPS = 1e-6

GDN_HEADS, GDN_DK, GDN_DV, GDN_CONV = 4, 64, 64, 5
GDN_QK = GDN_HEADS * GDN_DK
GDN_W = GDN_HEADS * GDN_DV
GDN_QKV = 2 * GDN_QK + GDN_W
MLA_HEADS, MLA_Q_RANK, MLA_KV_RANK = 8, 384, 256
MLA_NOPE, MLA_ROPE, MLA_DV = 64, 32, 64
MLA_W = MLA_HEADS * MLA_DV
MLA_SCALE = (MLA_NOPE + MLA_ROPE) ** -0.5
ROPE_THETA = 10000.0
ROPE_GRID_W = 64
GLA_HEADS, GLA_DK, GLA_DV = 4, 32, 64
GLA_QK = GLA_HEADS * GLA_DK
GLA_W = GLA_HEADS * GLA_DV
GLA_GATE_RANK = 16
GLA_GATE_NORM = 16.0
CHUNK = 64
LEVELS = (32, 16, 8, 4, 2, 1)

LANES = 128
MXU_N = 256
ROWS = 256
HALO = 8
ATTN_HEADS = 8
VMEM_LIMIT = 56 * 1024 * 1024

_O_GDN_QKV, _O_GDN_Z, _O_A, _O_B = 0, 768, 1024, 1032
_O_CQ, _O_CKV, _O_KR, _O_MLA_Z = 1040, 1424, 1680, 1712
_O_GLA_Q, _O_GLA_K, _O_GLA_V, _O_GLA_Z, _O_GLOW = 2224, 2352, 2480, 2736, 2992
_S_A, _S_B, _S_GLOW = 0, 8, 16
_W_GDN, _W_MLA, _W_MLAZ, _W_GLA = 1024, 640, 512, 768
_N_IN_PAD = _W_GDN + _W_MLA + _W_MLAZ + _W_GLA + 2 * LANES


def _dot(a, b):
    return jnp.dot(a, b, preferred_element_type=F32)


def _dot_nt(a, b):
    return lax.dot_general(a, b, (((1,), (1,)), ((), ())), preferred_element_type=F32)


def _dot_tn(a, b):
    return lax.dot_general(a, b, (((0,), (0,)), ((), ())), preferred_element_type=F32)


def _bdot(a, b):
    return _dot(a.astype(BF16), b.astype(BF16))


def _split2(x):
    x1 = x.astype(BF16)
    return x1, (x - x1.astype(F32)).astype(BF16)


def _split3(x):
    x1 = x.astype(BF16)
    r1 = x - x1.astype(F32)
    x2 = r1.astype(BF16)
    x3 = (r1 - x2.astype(F32)).astype(BF16)
    return x1, x2, x3


def _sel_dot(m01, x):
    n = x.shape[1]
    y = _dot(m01, jnp.concatenate(_split3(x), axis=1))
    return y[:, :n] + y[:, n:2 * n] + y[:, 2 * n:]


def _dot_sel(x, m01):
    n = x.shape[0]
    y = _dot(jnp.concatenate(_split3(x), axis=0), m01)
    return y[:n] + y[n:2 * n] + y[2 * n:]


def _dot_sel2(x, m01):
    n = x.shape[0]
    y = _dot(jnp.concatenate(_split2(x), axis=0), m01)
    return y[:n] + y[n:]


def _dot_sel_nt(x, m01):
    n = x.shape[0]
    y = _dot_nt(jnp.concatenate(_split3(x), axis=0), m01)
    return y[:n] + y[n:2 * n] + y[2 * n:]


def _softplus(x):
    return jnp.maximum(x, 0.0) + jnp.log1p(jnp.exp(-jnp.abs(x)))


def _silu(x):
    return x * jax.nn.sigmoid(x)


def _params(*sem):
    return pltpu.CompilerParams(dimension_semantics=sem, vmem_limit_bytes=VMEM_LIMIT)


def _scan_consts(rev):
    t = np.arange(ROWS)
    ch = t // CHUNK
    p = (CHUNK - 1 - t % CHUNK) if rev else (t % CHUNK)
    same = ch[:, None] == ch[None, :]
    tri = same & (p[None, :] <= p[:, None])
    mq, mk = [], []
    lv = np.full((ROWS, ROWS), -1.0, np.float32)
    for li, s in enumerate(LEVELS):
        blk = p // s
        mq.append(same & (p[None, :] > (blk * s)[:, None]) & (p[None, :] <= p[:, None]))
        mk.append(same & (p[None, :] > p[:, None]) & (p[None, :] <= ((blk + 1) * s)[:, None]))
        pair = same & ((blk % 2) == 1)[:, None] & (blk[None, :] == (blk - 1)[:, None])
        lv[pair] = li
    stack = np.concatenate([tri, same] + mq + mk, axis=0)
    lvm = np.stack([lv == li for li in range(len(LEVELS))])
    negm = np.where(tri, 0.0, -np.inf).astype(np.float32)
    return dict(tri=jnp.asarray(tri, BF16), ones=jnp.asarray(same, BF16), negm=jnp.asarray(negm),
                lv=jnp.asarray(lv), lvm=jnp.asarray(lvm, BF16), stack=jnp.asarray(stack, BF16))


def _head_block_ones(n, width):
    i = np.arange(n)
    return jnp.asarray((i[:, None] // width) == (i[None, :] // width), BF16)


def _inproj_perm():
    perm = np.full((_N_IN_PAD,), -1, np.int64)

    def put(src, n, at):
        perm[at:at + n] = np.arange(src, src + n)

    put(_O_GDN_QKV, GDN_QKV, 0)
    put(_O_GDN_Z, GDN_W, GDN_QKV)
    pos = _W_GDN
    put(_O_CQ, MLA_Q_RANK, pos)
    put(_O_CKV, MLA_KV_RANK, pos + MLA_Q_RANK)
    pos += _W_MLA
    put(_O_MLA_Z, MLA_W, pos)
    pos += _W_MLAZ
    put(_O_GLA_Q, GLA_QK, pos)
    put(_O_GLA_K, GLA_QK, pos + GLA_QK)
    put(_O_GLA_V, GLA_W, pos + 2 * GLA_QK)
    put(_O_GLA_Z, GLA_W, pos + 2 * GLA_QK + GLA_W)
    pos += _W_GLA
    put(_O_A, 2 * GDN_HEADS, pos + _S_A)
    put(_O_B, 2 * GDN_HEADS, pos + _S_B)
    put(_O_GLOW, 2 * GLA_GATE_RANK, pos + _S_GLOW)
    pos += LANES
    put(_O_KR, MLA_ROPE, pos + MLA_NOPE)
    return perm


def _take_cols(w, perm):
    cols = jnp.take(w, jnp.asarray(np.maximum(perm, 0)), axis=1)
    return jnp.where(jnp.asarray(perm >= 0)[None, :], cols, 0.0)


def _mla_q_perm():
    perm = np.full((MLA_HEADS * LANES,), -1, np.int64)
    d = MLA_NOPE + MLA_ROPE
    for h in range(MLA_HEADS):
        perm[h * LANES:h * LANES + d] = np.arange(h * d, (h + 1) * d)
    return perm


def _mla_kv_perm():
    dk = MLA_NOPE + MLA_DV
    kperm = np.full((MLA_HEADS * LANES,), -1, np.int64)
    vperm = np.zeros((MLA_W,), np.int64)
    for h in range(MLA_HEADS):
        kperm[h * LANES:h * LANES + MLA_NOPE] = np.arange(h * dk, h * dk + MLA_NOPE)
        vperm[h * MLA_DV:(h + 1) * MLA_DV] = np.arange(h * dk + MLA_NOPE, (h + 1) * dk)
    return kperm, vperm


def _rope_tables(n_ctx, n_lat):
    rows = n_lat // ROPE_GRID_W
    row = np.repeat(np.arange(rows, dtype=np.float32), ROPE_GRID_W)
    col = np.tile(np.arange(ROPE_GRID_W, dtype=np.float32), rows)
    n_freq = MLA_ROPE // 4
    inv = (ROPE_THETA ** (-np.arange(n_freq, dtype=np.float32) / n_freq)).astype(np.float32)
    ang = np.concatenate([row[:, None] * inv, col[:, None] * inv], axis=-1)
    cos = np.concatenate([np.ones((n_ctx, 2 * n_freq), np.float32), np.cos(ang)], 0)
    sin = np.concatenate([np.zeros((n_ctx, 2 * n_freq), np.float32), np.sin(ang)], 0)
    n = n_ctx + n_lat
    half = MLA_ROPE // 2
    tab = np.zeros((3, n, LANES), np.float32)
    tab[0, :, :MLA_NOPE] = 1.0
    tab[0, :, MLA_NOPE:MLA_NOPE + half] = cos
    tab[0, :, MLA_NOPE + half:MLA_NOPE + MLA_ROPE] = cos
    tab[1, :, MLA_NOPE:MLA_NOPE + half] = -sin
    tab[2, :, MLA_NOPE + half:MLA_NOPE + MLA_ROPE] = sin
    return jnp.asarray(tab)


def _expand_matrix(src0, n_src, width, n_out):
    m = np.zeros((LANES, n_out), np.float32)
    for h in range(n_src):
        m[src0 + h, h * width:(h + 1) * width] = 1.0
    return m


def _ada_kernel(c_ref, w_ref, b_ref, o_ref):
    c = _silu(c_ref[...]).astype(BF16)
    o_ref[0] = _dot(c, w_ref[0].astype(BF16)) + b_ref[0]


def _ada(cc, w_ada, b_ada):
    nl, d, n3 = w_ada.shape
    r = cc.shape[0]
    tn = 1024
    return pl.pallas_call(
        _ada_kernel,
        grid=(nl, n3 // tn),
        in_specs=[pl.BlockSpec((r, d), lambda l, j: (0, 0)),
                  pl.BlockSpec((1, d, tn), lambda l, j: (l, 0, j)),
                  pl.BlockSpec((1, 1, tn), lambda l, j: (l, 0, j))],
        out_specs=pl.BlockSpec((1, r, tn), lambda l, j: (l, 0, j)),
        out_shape=jax.ShapeDtypeStruct((nl, r, n3), F32),
        compiler_params=_params("arbitrary", "arbitrary"),
        name="ada_mod",
    )(cc, w_ada, b_ada.reshape(nl, 1, n3))


def _token_block(nct, refs):
    if len(refs) == 1:
        return refs[0][0]
    return jnp.where(pl.program_id(1) < nct, refs[0][0], refs[1][0])


def _token_specs(arrays, nct, d):
    if len(arrays) == 1:
        return [pl.BlockSpec((1, ROWS, d), lambda bi, j: (bi, j, 0))]
    return [pl.BlockSpec((1, ROWS, d), lambda bi, j: (bi, jnp.minimum(j, nct - 1), 0)),
            pl.BlockSpec((1, ROWS, d), lambda bi, j: (bi, jnp.maximum(j - nct, 0), 0))]


def _rope(x, tab_ref):
    half = MLA_ROPE // 2
    return (x * tab_ref[0] + pltpu.roll(x, LANES - half, 1) * tab_ref[1]
            + pltpu.roll(x, half, 1) * tab_ref[2])


def _inproj_kernel(nct, nx, *refs):
    x_refs, refs = refs[:nx], refs[nx:]
    (mod_ref, w_ref, wabt_ref, tab_ref, qw_ref, kvw_ref, wuq_ref, wuk_ref, wuv_ref,
     ogdn, omlaz, ogla, osmall, oabt, q_ref, k_ref, v_ref) = refs
    x = _token_block(nct, x_refs)
    h = x * lax.rsqrt(jnp.mean(x * x, axis=-1, keepdims=True) + EPS)
    h = h * (1.0 + mod_ref[0, 1:2, :]) + mod_ref[0, 0:1, :]
    hb = h.astype(BF16)
    pos = 0
    proj = {}
    for name, ref, n in (("gdn", ogdn, _W_GDN), ("mla", None, _W_MLA), ("mlaz", omlaz, _W_MLAZ),
                         ("gla", ogla, _W_GLA), ("small", osmall, LANES), ("kr", None, LANES)):
        y = _dot(hb, w_ref[:, pos:pos + n])
        if ref is None:
            proj[name] = y
        else:
            ref[0] = y
        pos += n
    oabt[0] = _dot_nt(wabt_ref[...], hb)

    cq = proj["mla"][:, :MLA_Q_RANK]
    ckv = proj["mla"][:, MLA_Q_RANK:]
    cq = cq * lax.rsqrt(jnp.mean(cq * cq, axis=-1, keepdims=True) + EPS) * qw_ref[...]
    ckv = ckv * lax.rsqrt(jnp.mean(ckv * ckv, axis=-1, keepdims=True) + EPS) * kvw_ref[...]
    cqb = cq.astype(BF16)
    ckvb = ckv.astype(BF16)
    qf = _dot(cqb, wuq_ref[...])
    kf = _dot(ckvb, wuk_ref[...])
    vf = _dot(ckvb, wuv_ref[...])
    kr = _rope(proj["kr"], tab_ref)
    q_scale = MLA_SCALE * math.log2(math.e)
    for hd in range(MLA_HEADS):
        sl = slice(hd * LANES, (hd + 1) * LANES)
        q_ref[0, hd] = (_rope(qf[:, sl], tab_ref) * q_scale).astype(BF16)
        k_ref[0, hd] = (kf[:, sl] + kr).astype(BF16)
    ones = jnp.ones((x.shape[0], MXU_N - LANES), BF16)
    for hp in range(MLA_HEADS // 2):
        v_ref[0, hp] = jnp.concatenate(
            [vf[:, hp * LANES:(hp + 1) * LANES].astype(BF16), ones], axis=1)


def _inproj(xs, mod, w_p, wabt, tab, qw, kvw, wuq, wuk, wuv, nct):
    b, d = xs[0].shape[0], xs[0].shape[2]
    ta = sum(a.shape[1] for a in xs)
    nt = ta // ROWS
    widths = (_W_GDN, _W_MLAZ, _W_GLA, LANES)
    row = lambda bi, j: (bi, j, 0)
    full2 = lambda bi, j: (0, 0)
    hm = lambda bi, j: (bi, 0, j, 0)
    consts = (w_p, wabt)
    mla_consts = (qw, kvw, wuq, wuk, wuv)
    return pl.pallas_call(
        functools.partial(_inproj_kernel, nct, len(xs)),
        grid=(b, nt),
        in_specs=_token_specs(xs, nct, d)
        + [pl.BlockSpec((1, 3, d), lambda bi, j: (jnp.where(j < nct, 0, 1 + bi), 0, 0))]
        + [pl.BlockSpec(a.shape, full2) for a in consts]
        + [pl.BlockSpec((3, ROWS, LANES), lambda bi, j: (0, j, 0))]
        + [pl.BlockSpec(a.shape, full2) for a in mla_consts],
        out_specs=[pl.BlockSpec((1, ROWS, n), row) for n in widths]
        + [pl.BlockSpec((1, 4 * GDN_HEADS, ROWS), lambda bi, j: (bi, 0, j)),
           pl.BlockSpec((1, MLA_HEADS, ROWS, LANES), hm),
           pl.BlockSpec((1, MLA_HEADS, ROWS, LANES), hm),
           pl.BlockSpec((1, MLA_HEADS // 2, ROWS, MXU_N), hm)],
        out_shape=[jax.ShapeDtypeStruct((b, ta, n), F32) for n in widths]
        + [jax.ShapeDtypeStruct((b, 4 * GDN_HEADS, ta), F32),
           jax.ShapeDtypeStruct((b, MLA_HEADS, ta, LANES), BF16),
           jax.ShapeDtypeStruct((b, MLA_HEADS, ta, LANES), BF16),
           jax.ShapeDtypeStruct((b, MLA_HEADS // 2, ta, MXU_N), BF16)],
        compiler_params=_params("arbitrary", "arbitrary"),
        name="in_proj",
    )(*xs, mod, *consts, tab, *mla_consts)


def _scan_block_index(j, nct, nt, rev):
    if not rev:
        return j
    return jnp.where(j < nct, nct - 1 - j, nt - 1 - (j - nct))


def _head_lane_mask(n, width, h):
    lane = lax.broadcasted_iota(jnp.int32, (1, n), 1)
    return (lane >= h * width) & (lane < (h + 1) * width)


def _gdn_prep(d, blk, nct, nt, x_ref, prev_ref, next_ref, small_ref, abt_ref, convw_ref, prow_ref,
              pcol_ref, tri, ones, exp_ref, ob, xe_scr):
    dk = GDN_DK
    has_prev = jnp.logical_and(blk != 0, blk != nct)
    has_next = jnp.logical_and(blk != nct - 1, blk != nt - 1)
    xe_scr[d, 0:HALO, :] = jnp.where(has_prev, prev_ref[0], 0.0)
    xe_scr[d, HALO:HALO + ROWS, :] = x_ref[0, :, :GDN_QKV]
    xe_scr[d, HALO + ROWS:, :] = jnp.where(has_next, next_ref[0], 0.0)
    pad = (GDN_CONV - 1) // 2
    conv = jnp.zeros((ROWS, GDN_QKV), F32)
    for t in range(GDN_CONV):
        conv = conv + convw_ref[t:t + 1, :] * xe_scr[d, pl.ds(HALO - pad + t, ROWS), :]
    hqkv = _silu(conv)
    q = hqkv[:, :GDN_QK]
    k = hqkv[:, GDN_QK:2 * GDN_QK]
    v = hqkv[:, 2 * GDN_QK:]
    q = q * lax.rsqrt(_dot_sel2(q * q, ob) + EPS) * (dk ** -0.5)
    k = k * lax.rsqrt(_dot_sel2(k * k, ob) + EPS)
    sm = small_ref[0]
    g_all = -jnp.exp(prow_ref[0:1, :]) * _softplus(sm + prow_ref[1:2, :])
    beta_all = jax.nn.sigmoid(sm)
    gc_all = _sel_dot(tri, g_all)
    gl_all = _sel_dot(ones, g_all)
    g_t = -jnp.exp(pcol_ref[:, 0:1]) * _softplus(abt_ref[0] + pcol_ref[:, 1:2])
    gc_t = _dot_sel_nt(g_t, tri)
    gc_w = _dot_sel2(gc_all, exp_ref[d])
    gl_w = _dot_sel2(gl_all, exp_ref[d])
    beta_w = _dot_sel2(beta_all, exp_ref[2 + d])
    kb = k * beta_w
    return dict(q=q, kbf=k.astype(BF16), kb=kb, vb=v * beta_w, kbg=kb * jnp.exp(gc_w),
                qg=q * jnp.exp(gc_w), kdec=k * jnp.exp(gl_w - gc_w),
                gc_all=gc_all, gc_t=gc_t, gl_all=gl_all)


def _gdn_kernel(nct, xf_ref, pf_ref, nf_ref, smf_ref, abtf_ref, xb_ref, pb_ref, nb_ref, smb_ref,
                abtb_ref, convw_ref, prow_ref, pcol_ref, tri_ref, ones_ref, negm_ref, lvm_ref,
                exp_ref, ob_ref, of_ref, obk_ref, xe_scr, s_scr):
    j = pl.program_id(1)
    nt = pl.num_programs(1)
    nh, dk = GDN_HEADS, GDN_DK
    nchunk = ROWS // CHUNK
    nlev = len(LEVELS)

    @pl.when(j == 0)
    def _():
        s_scr[...] = jnp.zeros_like(s_scr)

    ob = ob_ref[...]
    ones = ones_ref[...]
    dir_refs = ((xf_ref, pf_ref, nf_ref, smf_ref, abtf_ref), (xb_ref, pb_ref, nb_ref, smb_ref, abtb_ref))
    prep = []
    for d in range(2):
        blk = _scan_block_index(j, nct, nt, d == 1)
        prep.append(_gdn_prep(d, blk, nct, nt, *dir_refs[d], convw_ref, prow_ref, pcol_ref,
                              tri_ref[d], ones, exp_ref, ob, xe_scr))
    ri = lax.broadcasted_iota(jnp.int32, (ROWS, ROWS), 0)
    ci = lax.broadcasted_iota(jnp.int32, (ROWS, ROWS), 1)
    eye = (ri == ci).astype(BF16)
    chains = [(d, h) for d in range(2) for h in range(nh)]
    nc = len(chains)

    low, a_intra = [], []
    for d, h in chains:
        p = prep[d]
        idx = d * nh + h
        hm = _head_lane_mask(GDN_QK, dk, h)
        decay = jnp.exp(p["gc_all"][:, idx:idx + 1] - p["gc_t"][idx:idx + 1, :] + negm_ref[d])
        kk = _dot_nt(jnp.where(hm, p["kb"], 0.0).astype(BF16), p["kbf"])
        qk = _dot_nt(jnp.where(hm, p["q"], 0.0).astype(BF16), p["kbf"])
        low.append((kk * decay).astype(BF16))
        a_intra.append((qk * decay).astype(BF16))

    t_inv = [eye - low[i] * lvm_ref[d, nlev - 1] for i, (d, h) in enumerate(chains)]
    for li in reversed(range(nlev - 1)):
        ys = [_dot(low[i] * lvm_ref[d, li], t_inv[i]).astype(BF16) for i, (d, h) in enumerate(chains)]
        t_inv = [t_inv[i] - _dot(t_inv[i], ys[i]).astype(BF16) for i in range(nc)]
    uw = []
    for i, (d, h) in enumerate(chains):
        sl = slice(h * dk, (h + 1) * dk)
        rhs = jnp.concatenate([prep[d]["vb"][:, sl], prep[d]["kbg"][:, sl]], axis=1)
        uw.append(_dot(t_inv[i], rhs.astype(BF16)))

    q2, ou, gb = [], [], []
    for i, (d, h) in enumerate(chains):
        sl = slice(h * dk, (h + 1) * dk)
        uwb = uw[i].astype(BF16)
        auw = _dot(a_intra[i], uwb)
        ou.append(auw[:, :GDN_DV])
        q2.append((prep[d]["qg"][:, sl] - auw[:, GDN_DV:]).astype(BF16))
        kd = prep[d]["kdec"][:, sl].astype(BF16)
        gb.append([_dot_tn(kd[c * CHUNK:(c + 1) * CHUNK], uwb[c * CHUNK:(c + 1) * CHUNK])
                   for c in range(nchunk)])
    states = [s_scr[i] for i in range(nc)]
    o_parts = [[None] * nchunk for _ in chains]
    for step in range(nchunk):
        for i, (d, h) in enumerate(chains):
            c = nchunk - 1 - step if d == 1 else step
            idx = d * nh + h
            r0 = c * CHUNK
            rs = slice(r0, r0 + CHUNK)
            sb = states[i].astype(BF16)
            o_parts[i][c] = _dot(q2[i][rs], sb) + ou[i][rs]
            states[i] = (states[i] * jnp.exp(prep[d]["gl_all"][r0:r0 + 1, idx:idx + 1])
                         - _dot(gb[i][c][:, GDN_DV:].astype(BF16), sb) + gb[i][c][:, :GDN_DV])
    for i in range(len(chains)):
        s_scr[i] = states[i]
    outs = [jnp.concatenate(o_parts[i], axis=0) for i in range(len(chains))]
    of_ref[0] = jnp.concatenate(outs[:nh], axis=1)
    obk_ref[0] = jnp.concatenate(outs[nh:], axis=1)


def _scan_row_specs(nct, nt, rev, width):
    blk_of = functools.partial(_scan_block_index, nct=nct, nt=nt, rev=rev)
    return pl.BlockSpec((1, ROWS, width), lambda bi, j: (bi, blk_of(j), 0))


def _gdn(nct, gdn_in, small, abt, convw, prow, pcol, exp_m, ob):
    b, ta, _ = gdn_in.shape
    nt = ta // ROWS
    cf, cb = _scan_consts(False), _scan_consts(True)
    stack = lambda name: jnp.stack([cf[name], cb[name]])
    hb = ROWS // HALO
    last_halo = ta // HALO - 1
    full2 = lambda bi, j: (0, 0)
    full3 = lambda bi, j: (0, 0, 0)

    def dir_specs(rev):
        blk_of = functools.partial(_scan_block_index, nct=nct, nt=nt, rev=rev)
        return [
            _scan_row_specs(nct, nt, rev, _W_GDN),
            pl.BlockSpec((1, HALO, GDN_QKV),
                         lambda bi, j: (bi, jnp.maximum(blk_of(j) * hb - 1, 0), 0)),
            pl.BlockSpec((1, HALO, GDN_QKV),
                         lambda bi, j: (bi, jnp.minimum((blk_of(j) + 1) * hb, last_halo), 0)),
            _scan_row_specs(nct, nt, rev, LANES),
            pl.BlockSpec((1, 4 * GDN_HEADS, ROWS), lambda bi, j: (bi, 0, blk_of(j))),
        ]

    in_specs = dir_specs(False) + dir_specs(True) + [
        pl.BlockSpec(convw.shape, full2),
        pl.BlockSpec(prow.shape, full2),
        pl.BlockSpec(pcol.shape, full2),
        pl.BlockSpec((2, ROWS, ROWS), full3),
        pl.BlockSpec((ROWS, ROWS), full2),
        pl.BlockSpec((2, ROWS, ROWS), full3),
        pl.BlockSpec((2, len(LEVELS), ROWS, ROWS), lambda bi, j: (0, 0, 0, 0)),
        pl.BlockSpec(exp_m.shape, full3),
        pl.BlockSpec(ob.shape, full2),
    ]
    dir_args = [gdn_in, gdn_in, gdn_in, small, abt]
    args = dir_args + dir_args + [convw, prow, pcol, stack("tri"), cf["ones"], stack("negm"),
                                  stack("lvm"), exp_m, ob]
    return pl.pallas_call(
        functools.partial(_gdn_kernel, nct),
        grid=(b, nt),
        in_specs=in_specs,
        out_specs=[_scan_row_specs(nct, nt, False, GDN_W), _scan_row_specs(nct, nt, True, GDN_W)],
        out_shape=[jax.ShapeDtypeStruct((b, ta, GDN_W), F32)] * 2,
        scratch_shapes=[pltpu.VMEM((2, ROWS + 2 * HALO, GDN_QKV), F32),
                        pltpu.VMEM((2 * GDN_HEADS, GDN_DK, GDN_DV), F32)],
        compiler_params=_params("arbitrary", "arbitrary"),
        name="gdn_scan",
    )(*args)


def _gla_kernel(xf_ref, smf_ref, xb_ref, smb_ref, wg_ref, bg_ref, stack_ref, lv_ref, eqk_ref,
                sbd_ref, of_ref, obk_ref, s_scr):
    j = pl.program_id(1)
    nh, dk, dv = GLA_HEADS, GLA_DK, GLA_DV
    nchunk = ROWS // CHUNK
    nlev = len(LEVELS)
    n = GLA_QK

    @pl.when(j == 0)
    def _():
        s_scr[...] = jnp.zeros_like(s_scr)

    lane_head = [_head_lane_mask(GLA_QK, dk, h) for h in range(nh)]
    out_head = [_head_lane_mask(GLA_W, dv, h) for h in range(nh)]
    eqk = eqk_ref[...]
    sbd = sbd_ref[...]
    x_refs, sm_refs, o_refs = (xf_ref, xb_ref), (smf_ref, smb_ref), (of_ref, obk_ref)
    dirs = range(2)
    xs = [x_refs[d][0] for d in dirs]
    q = [xs[d][:, :GLA_QK] * (dk ** -0.5) for d in dirs]
    k = [xs[d][:, GLA_QK:2 * GLA_QK] for d in dirs]
    v = [xs[d][:, 2 * GLA_QK:2 * GLA_QK + GLA_W] for d in dirs]
    vb = [v[d].astype(BF16) for d in dirs]
    gk = [_dot(sm_refs[d][0].astype(BF16), wg_ref[d]) + bg_ref[d] for d in dirs]
    la = [-_softplus(-gk[d]) * (1.0 / GLA_GATE_NORM) for d in dirs]
    la2 = [jnp.concatenate(_split2(la[d]), axis=1) for d in dirs]
    ys = [_dot(stack_ref[d], la2[d]) for d in dirs]
    cums = [ys[d][:, :n] + ys[d][:, n:] for d in dirs]
    piece = lambda d, i: cums[d][i * ROWS:(i + 1) * ROWS]
    bcum = [piece(d, 0) for d in dirs]
    blast = [piece(d, 1) for d in dirs]
    qg = [(q[d] * jnp.exp(bcum[d])).astype(BF16) for d in dirs]
    kdec = [(k[d] * jnp.exp(blast[d] - bcum[d])).astype(BF16) for d in dirs]

    lv4 = [jnp.concatenate([lv_ref[d]] * nh, axis=0) for d in dirs]
    acc = [jnp.zeros((nh * ROWS, ROWS), F32) for d in dirs]
    for li in range(nlev):
        ql = [q[d] * jnp.exp(piece(d, 2 + li)) for d in dirs]
        kl = [(k[d] * jnp.exp(piece(d, 2 + nlev + li))).astype(BF16) for d in dirs]
        qs = [jnp.concatenate([jnp.where(lane_head[h], ql[d], 0.0) for h in range(nh)],
                              axis=0).astype(BF16) for d in dirs]
        ps = [_dot_nt(qs[d], kl[d]) for d in dirs]
        acc = [jnp.where(lv4[d] == float(li), ps[d], acc[d]) for d in dirs]
    pv = [_dot(acc[d].astype(BF16), vb[d]) for d in dirs]
    o = [_dot_sel(q[d] * k[d], eqk) * v[d] for d in dirs]
    for h in range(nh):
        o = [o[d] + jnp.where(out_head[h], pv[d][h * ROWS:(h + 1) * ROWS], 0.0) for d in dirs]

    states = [s_scr[d] for d in dirs]
    o_parts = [[None] * nchunk for d in dirs]
    for step in range(nchunk):
        for d in dirs:
            c = nchunk - 1 - step if d == 1 else step
            rs = slice(c * CHUNK, (c + 1) * CHUNK)
            o_parts[d][c] = _dot_nt(qg[d][rs], states[d].astype(BF16))
            states[d] = (states[d] * jnp.exp(blast[d][c * CHUNK:c * CHUNK + 1, :])
                         + sbd * _dot_tn(vb[d][rs], kdec[d][rs]))
    for d in dirs:
        s_scr[d] = states[d]
        o_refs[d][0] = o[d] + jnp.concatenate(o_parts[d], axis=0)


def _gla(nct, gla_in, small, wg, bg, eqk, sbd):
    b, ta, _ = gla_in.shape
    nt = ta // ROWS
    cf, cb = _scan_consts(False), _scan_consts(True)
    stack = jnp.stack([cf["stack"], cb["stack"]])
    lv = jnp.stack([cf["lv"], cb["lv"]])
    full2 = lambda bi, j: (0, 0)
    full3 = lambda bi, j: (0, 0, 0)
    in_specs = [
        _scan_row_specs(nct, nt, False, _W_GLA), _scan_row_specs(nct, nt, False, LANES),
        _scan_row_specs(nct, nt, True, _W_GLA), _scan_row_specs(nct, nt, True, LANES),
        pl.BlockSpec(wg.shape, full3),
        pl.BlockSpec(bg.shape, full3),
        pl.BlockSpec(stack.shape, full3),
        pl.BlockSpec(lv.shape, full3),
        pl.BlockSpec(eqk.shape, full2),
        pl.BlockSpec(sbd.shape, full2),
    ]
    return pl.pallas_call(
        _gla_kernel,
        grid=(b, nt),
        in_specs=in_specs,
        out_specs=[_scan_row_specs(nct, nt, False, GLA_W), _scan_row_specs(nct, nt, True, GLA_W)],
        out_shape=[jax.ShapeDtypeStruct((b, ta, GLA_W), F32)] * 2,
        scratch_shapes=[pltpu.VMEM((2, GLA_W, GLA_QK), F32)],
        compiler_params=_params("arbitrary", "arbitrary"),
        name="gla_scan",
    )(gla_in, small, gla_in, small, wg, bg, stack, lv, eqk, sbd)


def _attn_kernel(nct, off, n_ctx, q_ref, k_ref, v_ref, z_ref, o_ref):
    i = pl.program_id(2) + off

    def body(nk):
        parts = []
        half = ROWS // 2

        def by_row_halves(dot, lhs, rhs):
            return jnp.concatenate([dot(lhs[:half], rhs), dot(lhs[half:], rhs)], axis=0)

        for hh in range(ATTN_HEADS):
            q = q_ref[0, hh]
            kk = k_ref[0, hh, :nk, :]
            s = by_row_halves(_dot_nt, q, kk) if hh == 0 else _dot_nt(q, kk)
            m = jnp.max(s, axis=-1, keepdims=True)
            p = jnp.exp2(s - m).astype(BF16)
            vv = v_ref[0, hh // 2, :nk, :]
            pv = by_row_halves(_dot, p, vv) if hh == ATTN_HEADS - 1 else _dot(p, vv)
            o = pv[:, (hh % 2) * MLA_DV:(hh % 2 + 1) * MLA_DV] / pv[:, LANES:LANES + 1]
            parts.append(o)
        o_ref[0] = (jnp.concatenate(parts, axis=1) * _silu(z_ref[0])).astype(BF16)

    if off == 0:
        @pl.when(i < nct)
        def _():
            body(n_ctx)

        @pl.when(i >= nct)
        def _():
            body(k_ref.shape[2])
    else:
        body(k_ref.shape[2])


def _attention(q, k, v, z, nct, with_ctx):
    b, nh, ta, _ = q.shape
    nt = ta // ROWS
    off = 0 if with_ctx else nct
    nq = nt - off
    hg = ATTN_HEADS
    wo = hg * MLA_DV
    return pl.pallas_call(
        functools.partial(_attn_kernel, nct, off, nct * ROWS),
        grid=(b, nh // hg, nq),
        in_specs=[pl.BlockSpec((1, hg, ROWS, LANES), lambda bi, hp, i: (bi, hp, i + off, 0)),
                  pl.BlockSpec((1, hg, ta, LANES), lambda bi, hp, i: (bi, hp, 0, 0)),
                  pl.BlockSpec((1, hg // 2, ta, MXU_N), lambda bi, hp, i: (bi, hp, 0, 0)),
                  pl.BlockSpec((1, ROWS, wo), lambda bi, hp, i: (bi, i + off, hp))],
        out_specs=pl.BlockSpec((1, ROWS, wo), lambda bi, hp, i: (bi, i, hp)),
        out_shape=jax.ShapeDtypeStruct((b, nq * ROWS, MLA_W), BF16),
        compiler_params=_params("arbitrary", "arbitrary", "arbitrary"),
        name="mla_attn",
    )(q, k, v, z)


def _outproj_kernel(final, nct, nx, gf_ref, gb_ref, gz_ref, gnw_ref, af_ref, ab_ref, az_ref, anw_ref,
                    ob_ref, m_ref, mod_ref, w_ref, *rest):
    x_refs, rest = rest[:nx], rest[nx:]
    if final:
        fw_ref, o_ref = rest
    else:
        (o_ref,) = rest
    ob = ob_ref[...]

    def gated_head_norm(o, nw_ref, z):
        ms = _dot_sel2(o * o, ob) * (1.0 / GDN_DV)
        return (o * lax.rsqrt(ms + EPS) * nw_ref[...] * _silu(z)).astype(BF16)

    g = gated_head_norm(gf_ref[0] + gb_ref[0], gnw_ref, gz_ref[0])
    a = gated_head_norm(af_ref[0] + ab_ref[0], anw_ref, az_ref[0])
    y = _dot(g, w_ref[0:GDN_W, :])
    y = y + _dot(m_ref[0], w_ref[GDN_W:GDN_W + MLA_W, :])
    y = y + _dot(a, w_ref[GDN_W + MLA_W:, :])
    xn = _token_block(nct, x_refs) + mod_ref[0, 2:3, :] * y
    if final:
        xn = xn * lax.rsqrt(jnp.mean(xn * xn, axis=-1, keepdims=True) + EPS) * fw_ref[...]
    o_ref[0] = xn


def _outproj(gdn_f, gdn_b, gdn_in, gnw, gla_f, gla_b, gla_in, anw, ob, mla_o, xs, mod, w_out,
             nct, final_w):
    assert GDN_DV == GLA_DV and GDN_W == GLA_W
    b, d = xs[0].shape[0], xs[0].shape[2]
    ta = sum(a.shape[1] for a in xs)
    nt = ta // ROWS
    final = final_w is not None
    assert not (final and len(xs) > 1)
    off = nct if final else 0
    moff = off - (ta - mla_o.shape[1]) // ROWS
    nq = nt - off
    row = lambda bi, j: (bi, j + off, 0)
    full2 = lambda bi, j: (0, 0)
    x_specs = [pl.BlockSpec((1, ROWS, d), row)] if len(xs) == 1 else _token_specs(xs, nct, d)
    wide = pl.BlockSpec((1, ROWS, GDN_W), row)
    gz_blk = GDN_QKV // GDN_W
    az_blk = (2 * GLA_QK + GLA_W) // GLA_W
    in_specs = [wide, wide,
                pl.BlockSpec((1, ROWS, GDN_W), lambda bi, j: (bi, j + off, gz_blk)),
                pl.BlockSpec(gnw.shape, full2),
                wide, wide,
                pl.BlockSpec((1, ROWS, GLA_W), lambda bi, j: (bi, j + off, az_blk)),
                pl.BlockSpec(anw.shape, full2),
                pl.BlockSpec(ob.shape, full2),
                pl.BlockSpec((1, ROWS, MLA_W), lambda bi, j: (bi, j + moff, 0)),
                pl.BlockSpec((1, 3, d), lambda bi, j: (jnp.where(j + off < nct, 0, 1 + bi), 0, 0)),
                pl.BlockSpec(w_out.shape, full2)] + x_specs
    args = [gdn_f, gdn_b, gdn_in, gnw, gla_f, gla_b, gla_in, anw, ob, mla_o, mod, w_out, *xs]
    if final:
        in_specs.append(pl.BlockSpec(final_w.shape, full2))
        args.append(final_w)
    return pl.pallas_call(
        functools.partial(_outproj_kernel, final, nct, len(xs)),
        grid=(b, nq),
        in_specs=in_specs,
        out_specs=pl.BlockSpec((1, ROWS, d), lambda bi, j: (bi, j, 0)),
        out_shape=jax.ShapeDtypeStruct((b, nq * ROWS, d), F32),
        compiler_params=_params("arbitrary", "arbitrary"),
        name="out_proj",
    )(*args)


def kernel(x, c, ctx, c_ctx, w_ada, b_ada, w_in, gdn_conv_w, gdn_a_log, gdn_dt_bias, gdn_norm_w,
           mla_q_norm_w, mla_w_uq, mla_kv_norm_w, mla_w_ukv, gla_w_gk, gla_b_gk, gla_norm_w,
           w_out, final_norm_w):
    b, seq, d = x.shape
    n_ctx = ctx.shape[1]
    depth = w_in.shape[0]
    assert n_ctx % ROWS == 0 and seq % ROWS == 0 and seq % ROPE_GRID_W == 0
    nct = n_ctx // ROWS
    nh = GDN_HEADS

    xs = (ctx, x)
    pad_rows = (-(1 + b)) % 8
    cc = jnp.concatenate([c_ctx[None, :], c, jnp.zeros((pad_rows, d), F32)], axis=0)
    mod_all = _ada(cc, w_ada, b_ada).reshape(depth, cc.shape[0], 3, d)

    perm = _inproj_perm()
    qperm = _mla_q_perm()
    kperm, vperm = _mla_kv_perm()
    tab = _rope_tables(n_ctx, seq)
    ob64 = _head_block_ones(GDN_W, GDN_DV)
    exp_m = jnp.asarray(np.stack(
        [_expand_matrix(_S_A + dd * nh, nh, GDN_DK, GDN_QK) for dd in range(2)]
        + [_expand_matrix(_S_B + dd * nh, nh, GDN_DK, GDN_QK) for dd in range(2)]), BF16)
    eqk_np = np.zeros((GLA_QK, GLA_W), np.float32)
    sbd_np = np.zeros((GLA_W, GLA_QK), np.float32)
    for h in range(GLA_HEADS):
        eqk_np[h * GLA_DK:(h + 1) * GLA_DK, h * GLA_DV:(h + 1) * GLA_DV] = 1.0
        sbd_np[h * GLA_DV:(h + 1) * GLA_DV, h * GLA_DK:(h + 1) * GLA_DK] = 1.0
    eqk = jnp.asarray(eqk_np, BF16)
    sbd = jnp.asarray(sbd_np)

    out = None
    for layer in range(depth):
        last = layer == depth - 1
        w_p = _take_cols(w_in[layer], perm).astype(BF16)
        wabt = w_in[layer][:, _O_A:_O_A + 4 * nh].T.astype(BF16)
        mod = mod_all[layer]
        wuq = _take_cols(mla_w_uq[layer], qperm).astype(BF16)
        wuk = _take_cols(mla_w_ukv[layer], kperm).astype(BF16)
        wuv = jnp.take(mla_w_ukv[layer], jnp.asarray(vperm), axis=1).astype(BF16)
        gdn_in, mla_z, gla_in, small, abt, qh, kh, vh = _inproj(
            xs, mod, w_p, wabt, tab, mla_q_norm_w[layer][None, :], mla_kv_norm_w[layer][None, :],
            wuq, wuk, wuv, nct)

        convw = jnp.concatenate(
            [gdn_conv_w[layer], jnp.zeros((8 - GDN_CONV, GDN_QKV), F32)], axis=0)
        a_flat = gdn_a_log[layer].reshape(-1)
        dt_flat = gdn_dt_bias[layer].reshape(-1)
        prow = jnp.zeros((8, LANES), F32)
        prow = prow.at[0, _S_A:_S_A + 2 * nh].set(a_flat).at[1, _S_A:_S_A + 2 * nh].set(dt_flat)
        pcol = jnp.zeros((4 * nh, LANES), F32)
        pcol = pcol.at[0:2 * nh, 0].set(a_flat).at[0:2 * nh, 1].set(dt_flat)
        gnw = jnp.tile(gdn_norm_w[layer], nh)[None, :]
        gdn_f, gdn_b = _gdn(nct, gdn_in, small, abt, convw, prow, pcol, exp_m, ob64)

        anw = jnp.tile(gla_norm_w[layer], GLA_HEADS)[None, :]
        wg = jnp.zeros((2, LANES, GLA_QK), F32)
        for dd in range(2):
            r0 = _S_GLOW + dd * GLA_GATE_RANK
            wg = wg.at[dd, r0:r0 + GLA_GATE_RANK, :].set(gla_w_gk[layer, dd])
        gla_f, gla_b = _gla(nct, gla_in, small, wg.astype(BF16), gla_b_gk[layer][:, None, :],
                            eqk, sbd)

        mla_o = _attention(qh, kh, vh, mla_z, nct, with_ctx=not last)

        if last and len(xs) > 1:
            xs = (jnp.concatenate(xs, axis=1),)
        res = _outproj(gdn_f, gdn_b, gdn_in, gnw, gla_f, gla_b, gla_in, anw, ob64, mla_o, xs,
                       mod, w_out[layer].astype(BF16), nct, final_norm_w[None, :] if last else None)
        if last:
            out = res
        else:
            xs = (res,)
    return out
```

```python
import functools
import math

import numpy as np
import jax
import jax.numpy as jnp
from jax import lax
from jax.experimental import pallas as pl
from jax.experimental.pallas import tpu as pltpu

F32 = jnp.float32
BF16 = jnp.bfloat16
EPS = 1e-6

GDN_HEADS, GDN_DK, GDN_DV, GDN_CONV = 4, 64, 64, 5
GDN_QK = GDN_HEADS * GDN_DK
GDN_W = GDN_HEADS * GDN_DV
GDN_QKV = 2 * GDN_QK + GDN_W
MLA_HEADS, MLA_Q_RANK, MLA_KV_RANK = 8, 384, 256
MLA_NOPE, MLA_ROPE, MLA_DV = 64, 32, 64
MLA_W = MLA_HEADS * MLA_DV
MLA_SCALE = (MLA_NOPE + MLA_ROPE) ** -0.5
ROPE_THETA = 10000.0
ROPE_GRID_W = 64
GLA_HEADS, GLA_DK, GLA_DV = 4, 32, 64
GLA_QK = GLA_HEADS * GLA_DK
GLA_W = GLA_HEADS * GLA_DV
GLA_GATE_RANK = 16
GLA_GATE_NORM = 16.0
CHUNK = 64
LEVELS = (32, 16, 8, 4, 2, 1)

LANES = 128
MXU_N = 256
ROWS = 256
HALO = 8
ATTN_HEADS = 8
VMEM_LIMIT = 56 * 1024 * 1024

_O_GDN_QKV, _O_GDN_Z, _O_A, _O_B = 0, 768, 1024, 1032
_O_CQ, _O_CKV, _O_KR, _O_MLA_Z = 1040, 1424, 1680, 1712
_O_GLA_Q, _O_GLA_K, _O_GLA_V, _O_GLA_Z, _O_GLOW = 2224, 2352, 2480, 2736, 2992
_S_A, _S_B, _S_GLOW = 0, 8, 16
_W_GDN, _W_MLA, _W_MLAZ, _W_GLA = 1024, 640, 512, 768
_N_IN_PAD = _W_GDN + _W_MLA + _W_MLAZ + _W_GLA + 2 * LANES


def _dot(a, b):
    return jnp.dot(a, b, preferred_element_type=F32)


def _dot_nt(a, b):
    return lax.dot_general(a, b, (((1,), (1,)), ((), ())), preferred_element_type=F32)


def _dot_tn(a, b):
    return lax.dot_general(a, b, (((0,), (0,)), ((), ())), preferred_element_type=F32)


def _bdot(a, b):
    return _dot(a.astype(BF16), b.astype(BF16))


def _split2(x):
    x1 = x.astype(BF16)
    return x1, (x - x1.astype(F32)).astype(BF16)


def _split3(x):
    x1 = x.astype(BF16)
    r1 = x - x1.astype(F32)
    x2 = r1.astype(BF16)
    x3 = (r1 - x2.astype(F32)).astype(BF16)
    return x1, x2, x3


def _sel_dot(m01, x):
    n = x.shape[1]
    y = _dot(m01, jnp.concatenate(_split3(x), axis=1))
    return y[:, :n] + y[:, n:2 * n] + y[:, 2 * n:]


def _dot_sel(x, m01):
    n = x.shape[0]
    y = _dot(jnp.concatenate(_split3(x), axis=0), m01)
    return y[:n] + y[n:2 * n] + y[2 * n:]


def _dot_sel2(x, m01):
    n = x.shape[0]
    y = _dot(jnp.concatenate(_split2(x), axis=0), m01)
    return y[:n] + y[n:]


def _dot_sel_nt(x, m01):
    n = x.shape[0]
    y = _dot_nt(jnp.concatenate(_split3(x), axis=0), m01)
    return y[:n] + y[n:2 * n] + y[2 * n:]


def _softplus(x):
    return jnp.maximum(x, 0.0) + jnp.log1p(jnp.exp(-jnp.abs(x)))


def _silu(x):
    return x * jax.nn.sigmoid(x)


def _params(*sem):
    return pltpu.CompilerParams(dimension_semantics=sem, vmem_limit_bytes=VMEM_LIMIT)


def _scan_consts(rev):
    t = np.arange(ROWS)
    ch = t // CHUNK
    p = (CHUNK - 1 - t % CHUNK) if rev else (t % CHUNK)
    same = ch[:, None] == ch[None, :]
    tri = same & (p[None, :] <= p[:, None])
    mq, mk = [], []
    lv = np.full((ROWS, ROWS), -1.0, np.float32)
    for li, s in enumerate(LEVELS):
        blk = p // s
        mq.append(same & (p[None, :] > (blk * s)[:, None]) & (p[None, :] <= p[:, None]))
        mk.append(same & (p[None, :] > p[:, None]) & (p[None, :] <= ((blk + 1) * s)[:, None]))
        pair = same & ((blk % 2) == 1)[:, None] & (blk[None, :] == (blk - 1)[:, None])
        lv[pair] = li
    stack = np.concatenate([tri, same] + mq + mk, axis=0)
    lvm = np.stack([lv == li for li in range(len(LEVELS))])
    negm = np.where(tri, 0.0, -np.inf).astype(np.float32)
    return dict(tri=jnp.asarray(tri, BF16), ones=jnp.asarray(same, BF16), negm=jnp.asarray(negm),
                lv=jnp.asarray(lv), lvm=jnp.asarray(lvm, BF16), stack=jnp.asarray(stack, BF16))


def _head_block_ones(n, width):
    i = np.arange(n)
    return jnp.asarray((i[:, None] // width) == (i[None, :] // width), BF16)


def _inproj_perm():
    perm = np.full((_N_IN_PAD,), -1, np.int64)

    def put(src, n, at):
        perm[at:at + n] = np.arange(src, src + n)

    put(_O_GDN_QKV, GDN_QKV, 0)
    put(_O_GDN_Z, GDN_W, GDN_QKV)
    pos = _W_GDN
    put(_O_CQ, MLA_Q_RANK, pos)
    put(_O_CKV, MLA_KV_RANK, pos + MLA_Q_RANK)
    pos += _W_MLA
    put(_O_MLA_Z, MLA_W, pos)
    pos += _W_MLAZ
    put(_O_GLA_Q, GLA_QK, pos)
    put(_O_GLA_K, GLA_QK, pos + GLA_QK)
    put(_O_GLA_V, GLA_W, pos + 2 * GLA_QK)
    put(_O_GLA_Z, GLA_W, pos + 2 * GLA_QK + GLA_W)
    pos += _W_GLA
    put(_O_A, 2 * GDN_HEADS, pos + _S_A)
    put(_O_B, 2 * GDN_HEADS, pos + _S_B)
    put(_O_GLOW, 2 * GLA_GATE_RANK, pos + _S_GLOW)
    pos += LANES
    put(_O_KR, MLA_ROPE, pos + MLA_NOPE)
    return perm


def _take_cols(w, perm):
    cols = jnp.take(w, jnp.asarray(np.maximum(perm, 0)), axis=1)
    return jnp.where(jnp.asarray(perm >= 0)[None, :], cols, 0.0)


def _mla_q_perm():
    perm = np.full((MLA_HEADS * LANES,), -1, np.int64)
    d = MLA_NOPE + MLA_ROPE
    for h in range(MLA_HEADS):
        perm[h * LANES:h * LANES + d] = np.arange(h * d, (h + 1) * d)
    return perm


def _mla_kv_perm():
    dk = MLA_NOPE + MLA_DV
    kperm = np.full((MLA_HEADS * LANES,), -1, np.int64)
    vperm = np.zeros((MLA_W,), np.int64)
    for h in range(MLA_HEADS):
        kperm[h * LANES:h * LANES + MLA_NOPE] = np.arange(h * dk, h * dk + MLA_NOPE)
        vperm[h * MLA_DV:(h + 1) * MLA_DV] = np.arange(h * dk + MLA_NOPE, (h + 1) * dk)
    return kperm, vperm


def _rope_tables(n_ctx, n_lat):
    rows = n_lat // ROPE_GRID_W
    row = np.repeat(np.arange(rows, dtype=np.float32), ROPE_GRID_W)
    col = np.tile(np.arange(ROPE_GRID_W, dtype=np.float32), rows)
    n_freq = MLA_ROPE // 4
    inv = (ROPE_THETA ** (-np.arange(n_freq, dtype=np.float32) / n_freq)).astype(np.float32)
    ang = np.concatenate([row[:, None] * inv, col[:, None] * inv], axis=-1)
    cos = np.concatenate([np.ones((n_ctx, 2 * n_freq), np.float32), np.cos(ang)], 0)
    sin = np.concatenate([np.zeros((n_ctx, 2 * n_freq), np.float32), np.sin(ang)], 0)
    n = n_ctx + n_lat
    half = MLA_ROPE // 2
    tab = np.zeros((3, n, LANES), np.float32)
    tab[0, :, :MLA_NOPE] = 1.0
    tab[0, :, MLA_NOPE:MLA_NOPE + half] = cos
    tab[0, :, MLA_NOPE + half:MLA_NOPE + MLA_ROPE] = cos
    tab[1, :, MLA_NOPE:MLA_NOPE + half] = -sin
    tab[2, :, MLA_NOPE + half:MLA_NOPE + MLA_ROPE] = sin
    return jnp.asarray(tab)


def _expand_matrix(src0, n_src, width, n_out):
    m = np.zeros((LANES, n_out), np.float32)
    for h in range(n_src):
        m[src0 + h, h * width:(h + 1) * width] = 1.0
    return m


def _ada_kernel(c_ref, w_ref, b_ref, o_ref):
    c = _silu(c_ref[...]).astype(BF16)
    o_ref[0] = _dot(c, w_ref[0].astype(BF16)) + b_ref[0]


def _ada(cc, w_ada, b_ada):
    nl, d, n3 = w_ada.shape
    r = cc.shape[0]
    tn = 1024
    return pl.pallas_call(
        _ada_kernel,
        grid=(nl, n3 // tn),
        in_specs=[pl.BlockSpec((r, d), lambda l, j: (0, 0)),
                  pl.BlockSpec((1, d, tn), lambda l, j: (l, 0, j)),
                  pl.BlockSpec((1, 1, tn), lambda l, j: (l, 0, j))],
        out_specs=pl.BlockSpec((1, r, tn), lambda l, j: (l, 0, j)),
        out_shape=jax.ShapeDtypeStruct((nl, r, n3), F32),
        compiler_params=_params("arbitrary", "arbitrary"),
        name="ada_mod",
    )(cc, w_ada, b_ada.reshape(nl, 1, n3))


def _token_block(nct, refs):
    if len(refs) == 1:
        return refs[0][0]
    return jnp.where(pl.program_id(1) < nct, refs[0][0], refs[1][0])


def _token_specs(arrays, nct, d):
    if len(arrays) == 1:
        return [pl.BlockSpec((1, ROWS, d), lambda bi, j: (bi, j, 0))]
    return [pl.BlockSpec((1, ROWS, d), lambda bi, j: (bi, jnp.minimum(j, nct - 1), 0)),
            pl.BlockSpec((1, ROWS, d), lambda bi, j: (bi, jnp.maximum(j - nct, 0), 0))]


def _rope(x, tab_ref):
    half = MLA_ROPE // 2
    return (x * tab_ref[0] + pltpu.roll(x, LANES - half, 1) * tab_ref[1]
            + pltpu.roll(x, half, 1) * tab_ref[2])


def _inproj_kernel(nct, nx, *refs):
    x_refs, refs = refs[:nx], refs[nx:]
    (mod_ref, w_ref, wabt_ref, tab_ref, qw_ref, kvw_ref, wuq_ref, wuk_ref, wuv_ref,
     ogdn, omlaz, ogla, osmall, oabt, q_ref, k_ref, v_ref) = refs
    x = _token_block(nct, x_refs)
    h = x * lax.rsqrt(jnp.mean(x * x, axis=-1, keepdims=True) + EPS)
    h = h * (1.0 + mod_ref[0, 1:2, :]) + mod_ref[0, 0:1, :]
    hb = h.astype(BF16)
    pos = 0
    proj = {}
    for name, ref, n in (("gdn", ogdn, _W_GDN), ("mla", None, _W_MLA), ("mlaz", omlaz, _W_MLAZ),
                         ("gla", ogla, _W_GLA), ("small", osmall, LANES), ("kr", None, LANES)):
        y = _dot(hb, w_ref[:, pos:pos + n])
        if ref is None:
            proj[name] = y
        else:
            ref[0] = y
        pos += n
    oabt[0] = _dot_nt(wabt_ref[...], hb)

    cq = proj["mla"][:, :MLA_Q_RANK]
    ckv = proj["mla"][:, MLA_Q_RANK:]
    cq = cq * lax.rsqrt(jnp.mean(cq * cq, axis=-1, keepdims=True) + EPS) * qw_ref[...]
    ckv = ckv * lax.rsqrt(jnp.mean(ckv * ckv, axis=-1, keepdims=True) + EPS) * kvw_ref[...]
    cqb = cq.astype(BF16)
    ckvb = ckv.astype(BF16)
    qf = _dot(cqb, wuq_ref[...])
    kf = _dot(ckvb, wuk_ref[...])
    vf = _dot(ckvb, wuv_ref[...])
    kr = _rope(proj["kr"], tab_ref)
    q_scale = MLA_SCALE * math.log2(math.e)
    for hd in range(MLA_HEADS):
        sl = slice(hd * LANES, (hd + 1) * LANES)
        q_ref[0, hd] = (_rope(qf[:, sl], tab_ref) * q_scale).astype(BF16)
        k_ref[0, hd] = (kf[:, sl] + kr).astype(BF16)
    ones = jnp.ones((x.shape[0], MXU_N - LANES), BF16)
    for hp in range(MLA_HEADS // 2):
        v_ref[0, hp] = jnp.concatenate(
            [vf[:, hp * LANES:(hp + 1) * LANES].astype(BF16), ones], axis=1)


def _inproj(xs, mod, w_p, wabt, tab, qw, kvw, wuq, wuk, wuv, nct):
    b, d = xs[0].shape[0], xs[0].shape[2]
    ta = sum(a.shape[1] for a in xs)
    nt = ta // ROWS
    widths = (_W_GDN, _W_MLAZ, _W_GLA, LANES)
    row = lambda bi, j: (bi, j, 0)
    full2 = lambda bi, j: (0, 0)
    hm = lambda bi, j: (bi, 0, j, 0)
    consts = (w_p, wabt)
    mla_consts = (qw, kvw, wuq, wuk, wuv)
    return pl.pallas_call(
        functools.partial(_inproj_kernel, nct, len(xs)),
        grid=(b, nt),
        in_specs=_token_specs(xs, nct, d)
        + [pl.BlockSpec((1, 3, d), lambda bi, j: (jnp.where(j < nct, 0, 1 + bi), 0, 0))]
        + [pl.BlockSpec(a.shape, full2) for a in consts]
        + [pl.BlockSpec((3, ROWS, LANES), lambda bi, j: (0, j, 0))]
        + [pl.BlockSpec(a.shape, full2) for a in mla_consts],
        out_specs=[pl.BlockSpec((1, ROWS, n), row) for n in widths]
        + [pl.BlockSpec((1, 4 * GDN_HEADS, ROWS), lambda bi, j: (bi, 0, j)),
           pl.BlockSpec((1, MLA_HEADS, ROWS, LANES), hm),
           pl.BlockSpec((1, MLA_HEADS, ROWS, LANES), hm),
           pl.BlockSpec((1, MLA_HEADS // 2, ROWS, MXU_N), hm)],
        out_shape=[jax.ShapeDtypeStruct((b, ta, n), F32) for n in widths]
        + [jax.ShapeDtypeStruct((b, 4 * GDN_HEADS, ta), F32),
           jax.ShapeDtypeStruct((b, MLA_HEADS, ta, LANES), BF16),
           jax.ShapeDtypeStruct((b, MLA_HEADS, ta, LANES), BF16),
           jax.ShapeDtypeStruct((b, MLA_HEADS // 2, ta, MXU_N), BF16)],
        compiler_params=_params("arbitrary", "arbitrary"),
        name="in_proj",
    )(*xs, mod, *consts, tab, *mla_consts)


def _scan_block_index(j, nct, nt, rev):
    if not rev:
        return j
    return jnp.where(j < nct, nct - 1 - j, nt - 1 - (j - nct))


def _head_lane_mask(n, width, h):
    lane = lax.broadcasted_iota(jnp.int32, (1, n), 1)
    return (lane >= h * width) & (lane < (h + 1) * width)


def _gdn_prep(d, blk, first, nct, nt, x_ref, prev_ref, next_ref, small_ref, abt_ref, convw_ref,
              prow_ref, pcol_ref, tri, ones, exp_ref, ob, xe_scr, qkv_scr):
    dk = GDN_DK

    @pl.when(first)
    def _():
        has_prev = jnp.logical_and(blk != 0, blk != nct)
        has_next = jnp.logical_and(blk != nct - 1, blk != nt - 1)
        xe_scr[d, 0:HALO, :] = jnp.where(has_prev, prev_ref[0], 0.0)
        xe_scr[d, HALO:HALO + ROWS, :] = x_ref[0, :, :GDN_QKV]
        xe_scr[d, HALO + ROWS:, :] = jnp.where(has_next, next_ref[0], 0.0)
        pad = (GDN_CONV - 1) // 2
        conv = jnp.zeros((ROWS, GDN_QKV), F32)
        for t in range(GDN_CONV):
            conv = conv + convw_ref[t:t + 1, :] * xe_scr[d, pl.ds(HALO - pad + t, ROWS), :]
        hqkv = _silu(conv)
        qn = hqkv[:, :GDN_QK]
        kn = hqkv[:, GDN_QK:2 * GDN_QK]
        qn = qn * lax.rsqrt(_dot_sel2(qn * qn, ob) + EPS) * (dk ** -0.5)
        kn = kn * lax.rsqrt(_dot_sel2(kn * kn, ob) + EPS)
        qkv_scr[blk] = jnp.concatenate([qn, kn, hqkv[:, 2 * GDN_QK:]], axis=1)

    q = qkv_scr[blk, :, :GDN_QK]
    k = qkv_scr[blk, :, GDN_QK:2 * GDN_QK]
    v = qkv_scr[blk, :, 2 * GDN_QK:]
    sm = small_ref[0]
    g_all = -jnp.exp(prow_ref[0:1, :]) * _softplus(sm + prow_ref[1:2, :])
    beta_all = jax.nn.sigmoid(sm)
    gc_all = _sel_dot(tri, g_all)
    gl_all = _sel_dot(ones, g_all)
    g_t = -jnp.exp(pcol_ref[:, 0:1]) * _softplus(abt_ref[0] + pcol_ref[:, 1:2])
    gc_t = _dot_sel_nt(g_t, tri)
    gc_w = _dot_sel2(gc_all, exp_ref[d])
    gl_w = _dot_sel2(gl_all, exp_ref[d])
    beta_w = _dot_sel2(beta_all, exp_ref[2 + d])
    kb = k * beta_w
    return dict(q=q, kbf=k.astype(BF16), kb=kb, vb=v * beta_w, kbg=kb * jnp.exp(gc_w),
                qg=q * jnp.exp(gc_w), kdec=k * jnp.exp(gl_w - gc_w),
                gc_all=gc_all, gc_t=gc_t, gl_all=gl_all)


def _gdn_stages(nct, xf_ref, pf_ref, nf_ref, smf_ref, abtf_ref, xb_ref, pb_ref, nb_ref, smb_ref,
                abtb_ref, convw_ref, prow_ref, pcol_ref, tri_ref, ones_ref, negm_ref, lvm_ref,
                exp_ref, ob_ref, of_ref, obk_ref, xe_scr, s_scr, qkv_scr):
    j = pl.program_id(1)
    nt = pl.num_programs(1)
    nh, dk = GDN_HEADS, GDN_DK
    nchunk = ROWS // CHUNK
    nlev = len(LEVELS)
    ob = ob_ref[...]
    ones = ones_ref[...]
    dir_refs = ((xf_ref, pf_ref, nf_ref, smf_ref, abtf_ref), (xb_ref, pb_ref, nb_ref, smb_ref, abtb_ref))
    prep = []
    bwd_step_of_j = jnp.where(j < nct, nct - 1 - j, nct + nt - 1 - j)
    blk_b = _scan_block_index(j, nct, nt, True)
    visits = ((j, bwd_step_of_j >= j), (blk_b, blk_b > j))
    for d in range(2):
        blk, first = visits[d]
        prep.append(_gdn_prep(d, blk, first, nct, nt, *dir_refs[d], convw_ref, prow_ref, pcol_ref,
                              tri_ref[d], ones, exp_ref, ob, xe_scr, qkv_scr))
    yield
    ri = lax.broadcasted_iota(jnp.int32, (ROWS, ROWS), 0)
    ci = lax.broadcasted_iota(jnp.int32, (ROWS, ROWS), 1)
    eye = (ri == ci).astype(BF16)
    chains = [(d, h) for d in range(2) for h in range(nh)]
    nc = len(chains)

    low, a_intra = [], []
    for d, h in chains:
        p = prep[d]
        idx = d * nh + h
        hm = _head_lane_mask(GDN_QK, dk, h)
        decay = jnp.exp(p["gc_all"][:, idx:idx + 1] - p["gc_t"][idx:idx + 1, :] + negm_ref[d])
        kk = _dot_nt(jnp.where(hm, p["kb"], 0.0).astype(BF16), p["kbf"])
        qk = _dot_nt(jnp.where(hm, p["q"], 0.0).astype(BF16), p["kbf"])
        low.append((kk * decay).astype(BF16))
        a_intra.append((qk * decay).astype(BF16))
        if h == nh - 1:
            yield

    t_inv = [eye - low[i] * lvm_ref[d, nlev - 1] for i, (d, h) in enumerate(chains)]
    for li in reversed(range(nlev - 1)):
        ys = [_dot(low[i] * lvm_ref[d, li], t_inv[i]).astype(BF16) for i, (d, h) in enumerate(chains)]
        t_inv = [t_inv[i] - _dot(t_inv[i], ys[i]).astype(BF16) for i in range(nc)]
        yield
    uw = []
    for i, (d, h) in enumerate(chains):
        sl = slice(h * dk, (h + 1) * dk)
        rhs = jnp.concatenate([prep[d]["vb"][:, sl], prep[d]["kbg"][:, sl]], axis=1)
        uw.append(_dot(t_inv[i], rhs.astype(BF16)))

    q2, ou, gb = [], [], []
    for i, (d, h) in enumerate(chains):
        sl = slice(h * dk, (h + 1) * dk)
        uwb = uw[i].astype(BF16)
        auw = _dot(a_intra[i], uwb)
        ou.append(auw[:, :GDN_DV])
        q2.append((prep[d]["qg"][:, sl] - auw[:, GDN_DV:]).astype(BF16))
        kd = prep[d]["kdec"][:, sl].astype(BF16)
        gb.append([_dot_tn(kd[c * CHUNK:(c + 1) * CHUNK], uwb[c * CHUNK:(c + 1) * CHUNK])
                   for c in range(nchunk)])
    yield
    states = [s_scr[i] for i in range(nc)]
    o_parts = [[None] * nchunk for _ in chains]
    for step in range(nchunk):
        for i, (d, h) in enumerate(chains):
            c = nchunk - 1 - step if d == 1 else step
            idx = d * nh + h
            r0 = c * CHUNK
            rs = slice(r0, r0 + CHUNK)
            sb = states[i].astype(BF16)
            o_parts[i][c] = _dot(q2[i][rs], sb) + ou[i][rs]
            states[i] = (states[i] * jnp.exp(prep[d]["gl_all"][r0:r0 + 1, idx:idx + 1])
                         - _dot(gb[i][c][:, GDN_DV:].astype(BF16), sb) + gb[i][c][:, :GDN_DV])
        yield
    for i in range(len(chains)):
        s_scr[i] = states[i]
    outs = [jnp.concatenate(o_parts[i], axis=0) for i in range(len(chains))]
    of_ref[0] = jnp.concatenate(outs[:nh], axis=1)
    obk_ref[0] = jnp.concatenate(outs[nh:], axis=1)


def _scan_row_specs(nct, nt, rev, width):
    blk_of = functools.partial(_scan_block_index, nct=nct, nt=nt, rev=rev)
    return pl.BlockSpec((1, ROWS, width), lambda bi, j: (bi, blk_of(j), 0))


def _gdn_specs(nct, nt, ta, gdn_in, small, abt, convw, prow, pcol, exp_m, ob):
    cf, cb = _scan_consts(False), _scan_consts(True)
    stack = lambda name: jnp.stack([cf[name], cb[name]])
    hb = ROWS // HALO
    last_halo = ta // HALO - 1
    full2 = lambda bi, j: (0, 0)
    full3 = lambda bi, j: (0, 0, 0)

    def dir_specs(rev):
        blk_of = functools.partial(_scan_block_index, nct=nct, nt=nt, rev=rev)
        return [
            _scan_row_specs(nct, nt, rev, _W_GDN),
            pl.BlockSpec((1, HALO, GDN_QKV),
                         lambda bi, j: (bi, jnp.maximum(blk_of(j) * hb - 1, 0), 0)),
            pl.BlockSpec((1, HALO, GDN_QKV),
                         lambda bi, j: (bi, jnp.minimum((blk_of(j) + 1) * hb, last_halo), 0)),
            _scan_row_specs(nct, nt, rev, LANES),
            pl.BlockSpec((1, 4 * GDN_HEADS, ROWS), lambda bi, j: (bi, 0, blk_of(j))),
        ]

    in_specs = dir_specs(False) + dir_specs(True) + [
        pl.BlockSpec(convw.shape, full2),
        pl.BlockSpec(prow.shape, full2),
        pl.BlockSpec(pcol.shape, full2),
        pl.BlockSpec((2, ROWS, ROWS), full3),
        pl.BlockSpec((ROWS, ROWS), full2),
        pl.BlockSpec((2, ROWS, ROWS), full3),
        pl.BlockSpec((2, len(LEVELS), ROWS, ROWS), lambda bi, j: (0, 0, 0, 0)),
        pl.BlockSpec(exp_m.shape, full3),
        pl.BlockSpec(ob.shape, full2),
    ]
    dir_args = [gdn_in, gdn_in, gdn_in, small, abt]
    args = dir_args + dir_args + [convw, prow, pcol, stack("tri"), cf["ones"], stack("negm"),
                                  stack("lvm"), exp_m, ob]
    out_specs = [_scan_row_specs(nct, nt, False, GDN_W), _scan_row_specs(nct, nt, True, GDN_W)]
    scratch = [pltpu.VMEM((2, ROWS + 2 * HALO, GDN_QKV), F32),
               pltpu.VMEM((2 * GDN_HEADS, GDN_DK, GDN_DV), F32),
               pltpu.VMEM((nt, ROWS, GDN_QKV), F32)]
    return in_specs, args, out_specs, scratch


def _gla_stages(xf_ref, smf_ref, xb_ref, smb_ref, wg_ref, bg_ref, stack_ref, lv_ref, eqk_ref,
                sbd_ref, of_ref, obk_ref, s_scr):
    nh, dk, dv = GLA_HEADS, GLA_DK, GLA_DV
    nchunk = ROWS // CHUNK
    nlev = len(LEVELS)
    n = GLA_QK
    lane_head = [_head_lane_mask(GLA_QK, dk, h) for h in range(nh)]
    out_head = [_head_lane_mask(GLA_W, dv, h) for h in range(nh)]
    eqk = eqk_ref[...]
    sbd = sbd_ref[...]
    x_refs, sm_refs, o_refs = (xf_ref, xb_ref), (smf_ref, smb_ref), (of_ref, obk_ref)
    dirs = range(2)
    xs = [x_refs[d][0] for d in dirs]
    q = [xs[d][:, :GLA_QK] * (dk ** -0.5) for d in dirs]
    k = [xs[d][:, GLA_QK:2 * GLA_QK] for d in dirs]
    v = [xs[d][:, 2 * GLA_QK:2 * GLA_QK + GLA_W] for d in dirs]
    vb = [v[d].astype(BF16) for d in dirs]
    gk = [_dot(sm_refs[d][0].astype(BF16), wg_ref[d]) + bg_ref[d] for d in dirs]
    la = [-_softplus(-gk[d]) * (1.0 / GLA_GATE_NORM) for d in dirs]
    la2 = [jnp.concatenate(_split2(la[d]), axis=1) for d in dirs]
    ys = [_dot(stack_ref[d], la2[d]) for d in dirs]
    cums = [ys[d][:, :n] + ys[d][:, n:] for d in dirs]
    piece = lambda d, i: cums[d][i * ROWS:(i + 1) * ROWS]
    bcum = [piece(d, 0) for d in dirs]
    blast = [piece(d, 1) for d in dirs]
    qg = [(q[d] * jnp.exp(bcum[d])).astype(BF16) for d in dirs]
    kdec = [(k[d] * jnp.exp(blast[d] - bcum[d])).astype(BF16) for d in dirs]
    yield

    lv4 = [jnp.concatenate([lv_ref[d]] * nh, axis=0) for d in dirs]
    acc = [jnp.zeros((nh * ROWS, ROWS), F32) for d in dirs]
    for li in range(nlev):
        ql = [q[d] * jnp.exp(piece(d, 2 + li)) for d in dirs]
        kl = [(k[d] * jnp.exp(piece(d, 2 + nlev + li))).astype(BF16) for d in dirs]
        qs = [jnp.concatenate([jnp.where(lane_head[h], ql[d], 0.0) for h in range(nh)],
                              axis=0).astype(BF16) for d in dirs]
        ps = [_dot_nt(qs[d], kl[d]) for d in dirs]
        acc = [jnp.where(lv4[d] == float(li), ps[d], acc[d]) for d in dirs]
        yield
    pv = [_dot(acc[d].astype(BF16), vb[d]) for d in dirs]
    o = [_dot_sel(q[d] * k[d], eqk) * v[d] for d in dirs]
    for h in range(nh):
        o = [o[d] + jnp.where(out_head[h], pv[d][h * ROWS:(h + 1) * ROWS], 0.0) for d in dirs]
    yield

    states = [s_scr[d] for d in dirs]
    o_parts = [[None] * nchunk for d in dirs]
    for step in range(nchunk):
        for d in dirs:
            c = nchunk - 1 - step if d == 1 else step
            rs = slice(c * CHUNK, (c + 1) * CHUNK)
            o_parts[d][c] = _dot_nt(qg[d][rs], states[d].astype(BF16))
            states[d] = (states[d] * jnp.exp(blast[d][c * CHUNK:c * CHUNK + 1, :])
                         + sbd * _dot_tn(vb[d][rs], kdec[d][rs]))
        yield
    for d in dirs:
        s_scr[d] = states[d]
        o_refs[d][0] = o[d] + jnp.concatenate(o_parts[d], axis=0)


def _gla_specs(nct, nt, gla_in, small, wg, bg, eqk, sbd):
    cf, cb = _scan_consts(False), _scan_consts(True)
    stack = jnp.stack([cf["stack"], cb["stack"]])
    lv = jnp.stack([cf["lv"], cb["lv"]])
    full2 = lambda bi, j: (0, 0)
    full3 = lambda bi, j: (0, 0, 0)
    in_specs = [
        _scan_row_specs(nct, nt, False, _W_GLA), _scan_row_specs(nct, nt, False, LANES),
        _scan_row_specs(nct, nt, True, _W_GLA), _scan_row_specs(nct, nt, True, LANES),
        pl.BlockSpec(wg.shape, full3),
        pl.BlockSpec(bg.shape, full3),
        pl.BlockSpec(stack.shape, full3),
        pl.BlockSpec(lv.shape, full3),
        pl.BlockSpec(eqk.shape, full2),
        pl.BlockSpec(sbd.shape, full2),
    ]
    args = [gla_in, small, gla_in, small, wg, bg, stack, lv, eqk, sbd]
    out_specs = [_scan_row_specs(nct, nt, False, GLA_W), _scan_row_specs(nct, nt, True, GLA_W)]
    scratch = [pltpu.VMEM((2, GLA_W, GLA_QK), F32)]
    return in_specs, args, out_specs, scratch


_N_GDN_IN, _N_GLA_IN = 19, 10


def _scan_kernel(nct, *refs):
    gdn_in, refs = refs[:_N_GDN_IN], refs[_N_GDN_IN:]
    gla_in, refs = refs[:_N_GLA_IN], refs[_N_GLA_IN:]
    gdn_out, gla_out, refs = refs[:2], refs[2:4], refs[4:]
    gdn_scr, gla_scr = refs[:3], refs[3:]

    @pl.when(pl.program_id(1) == 0)
    def _():
        gdn_scr[1][...] = jnp.zeros_like(gdn_scr[1])
        gla_scr[0][...] = jnp.zeros_like(gla_scr[0])

    stages = [_gdn_stages(nct, *gdn_in, *gdn_out, *gdn_scr),
              _gla_stages(*gla_in, *gla_out, *gla_scr)]
    while stages:
        for s in list(stages):
            if next(s, StopIteration) is StopIteration:
                stages.remove(s)


def _scans(nct, gdn_args, gla_args):
    gdn_in = gdn_args[0]
    b, ta, _ = gdn_in.shape
    nt = ta // ROWS
    g_in, g_args, g_out, g_scr = _gdn_specs(nct, nt, ta, *gdn_args)
    a_in, a_args, a_out, a_scr = _gla_specs(nct, nt, *gla_args)
    assert len(g_in) == _N_GDN_IN and len(a_in) == _N_GLA_IN
    return pl.pallas_call(
        functools.partial(_scan_kernel, nct),
        grid=(b, nt),
        in_specs=g_in + a_in,
        out_specs=g_out + a_out,
        out_shape=[jax.ShapeDtypeStruct((b, ta, GDN_W), F32)] * 2
        + [jax.ShapeDtypeStruct((b, ta, GLA_W), F32)] * 2,
        scratch_shapes=g_scr + a_scr,
        compiler_params=_params("arbitrary", "arbitrary"),
        name="gdn_gla_scan",
    )(*g_args, *a_args)


def _attn_kernel(nct, off, n_ctx, q_ref, k_ref, v_ref, z_ref, o_ref):
    i = pl.program_id(2) + off

    def body(nk):
        parts = []
        half = ROWS // 2

        def by_row_halves(dot, lhs, rhs):
            return jnp.concatenate([dot(lhs[:half], rhs), dot(lhs[half:], rhs)], axis=0)

        for hh in range(ATTN_HEADS):
            q = q_ref[0, hh]
            kk = k_ref[0, hh, :nk, :]
            s = by_row_halves(_dot_nt, q, kk) if hh == 0 else _dot_nt(q, kk)
            m = jnp.max(s, axis=-1, keepdims=True)
            p = jnp.exp2(s - m).astype(BF16)
            vv = v_ref[0, hh // 2, :nk, :]
            pv = by_row_halves(_dot, p, vv) if hh == ATTN_HEADS - 1 else _dot(p, vv)
            o = pv[:, (hh % 2) * MLA_DV:(hh % 2 + 1) * MLA_DV] / pv[:, LANES:LANES + 1]
            parts.append(o)
        o_ref[0] = (jnp.concatenate(parts, axis=1) * _silu(z_ref[0])).astype(BF16)

    if off == 0:
        @pl.when(i < nct)
        def _():
            body(n_ctx)

        @pl.when(i >= nct)
        def _():
            body(k_ref.shape[2])
    else:
        body(k_ref.shape[2])


def _attention(q, k, v, z, nct, with_ctx):
    b, nh, ta, _ = q.shape
    nt = ta // ROWS
    off = 0 if with_ctx else nct
    nq = nt - off
    hg = ATTN_HEADS
    wo = hg * MLA_DV
    return pl.pallas_call(
        functools.partial(_attn_kernel, nct, off, nct * ROWS),
        grid=(b, nh // hg, nq),
        in_specs=[pl.BlockSpec((1, hg, ROWS, LANES), lambda bi, hp, i: (bi, hp, i + off, 0)),
                  pl.BlockSpec((1, hg, ta, LANES), lambda bi, hp, i: (bi, hp, 0, 0)),
                  pl.BlockSpec((1, hg // 2, ta, MXU_N), lambda bi, hp, i: (bi, hp, 0, 0)),
                  pl.BlockSpec((1, ROWS, wo), lambda bi, hp, i: (bi, i + off, hp))],
        out_specs=pl.BlockSpec((1, ROWS, wo), lambda bi, hp, i: (bi, i, hp)),
        out_shape=jax.ShapeDtypeStruct((b, nq * ROWS, MLA_W), BF16),
        compiler_params=_params("arbitrary", "arbitrary", "arbitrary"),
        name="mla_attn",
    )(q, k, v, z)


def _outproj_kernel(final, nct, nx, gf_ref, gb_ref, gz_ref, gnw_ref, af_ref, ab_ref, az_ref, anw_ref,
                    ob_ref, m_ref, mod_ref, w_ref, *rest):
    x_refs, rest = rest[:nx], rest[nx:]
    if final:
        fw_ref, o_ref = rest
    else:
        (o_ref,) = rest
    ob = ob_ref[...]

    def gated_head_norm(o, nw_ref, z):
        ms = _dot_sel2(o * o, ob) * (1.0 / GDN_DV)
        return (o * lax.rsqrt(ms + EPS) * nw_ref[...] * _silu(z)).astype(BF16)

    g = gated_head_norm(gf_ref[0] + gb_ref[0], gnw_ref, gz_ref[0])
    a = gated_head_norm(af_ref[0] + ab_ref[0], anw_ref, az_ref[0])
    y = _dot(g, w_ref[0:GDN_W, :])
    y = y + _dot(m_ref[0], w_ref[GDN_W:GDN_W + MLA_W, :])
    y = y + _dot(a, w_ref[GDN_W + MLA_W:, :])
    xn = _token_block(nct, x_refs) + mod_ref[0, 2:3, :] * y
    if final:
        xn = xn * lax.rsqrt(jnp.mean(xn * xn, axis=-1, keepdims=True) + EPS) * fw_ref[...]
    o_ref[0] = xn


def _outproj(gdn_f, gdn_b, gdn_in, gnw, gla_f, gla_b, gla_in, anw, ob, mla_o, xs, mod, w_out,
             nct, final_w):
    assert GDN_DV == GLA_DV and GDN_W == GLA_W
    b, d = xs[0].shape[0], xs[0].shape[2]
    ta = sum(a.shape[1] for a in xs)
    nt = ta // ROWS
    final = final_w is not None
    assert not (final and len(xs) > 1)
    off = nct if final else 0
    moff = off - (ta - mla_o.shape[1]) // ROWS
    nq = nt - off
    row = lambda bi, j: (bi, j + off, 0)
    full2 = lambda bi, j: (0, 0)
    x_specs = [pl.BlockSpec((1, ROWS, d), row)] if len(xs) == 1 else _token_specs(xs, nct, d)
    wide = pl.BlockSpec((1, ROWS, GDN_W), row)
    gz_blk = GDN_QKV // GDN_W
    az_blk = (2 * GLA_QK + GLA_W) // GLA_W
    in_specs = [wide, wide,
                pl.BlockSpec((1, ROWS, GDN_W), lambda bi, j: (bi, j + off, gz_blk)),
                pl.BlockSpec(gnw.shape, full2),
                wide, wide,
                pl.BlockSpec((1, ROWS, GLA_W), lambda bi, j: (bi, j + off, az_blk)),
                pl.BlockSpec(anw.shape, full2),
                pl.BlockSpec(ob.shape, full2),
                pl.BlockSpec((1, ROWS, MLA_W), lambda bi, j: (bi, j + moff, 0)),
                pl.BlockSpec((1, 3, d), lambda bi, j: (jnp.where(j + off < nct, 0, 1 + bi), 0, 0)),
                pl.BlockSpec(w_out.shape, full2)] + x_specs
    args = [gdn_f, gdn_b, gdn_in, gnw, gla_f, gla_b, gla_in, anw, ob, mla_o, mod, w_out, *xs]
    if final:
        in_specs.append(pl.BlockSpec(final_w.shape, full2))
        args.append(final_w)
    return pl.pallas_call(
        functools.partial(_outproj_kernel, final, nct, len(xs)),
        grid=(b, nq),
        in_specs=in_specs,
        out_specs=pl.BlockSpec((1, ROWS, d), lambda bi, j: (bi, j, 0)),
        out_shape=jax.ShapeDtypeStruct((b, nq * ROWS, d), F32),
        compiler_params=_params("arbitrary", "arbitrary"),
        name="out_proj",
    )(*args)


def kernel(x, c, ctx, c_ctx, w_ada, b_ada, w_in, gdn_conv_w, gdn_a_log, gdn_dt_bias, gdn_norm_w,
           mla_q_norm_w, mla_w_uq, mla_kv_norm_w, mla_w_ukv, gla_w_gk, gla_b_gk, gla_norm_w,
           w_out, final_norm_w):
    b, seq, d = x.shape
    n_ctx = ctx.shape[1]
    depth = w_in.shape[0]
    assert n_ctx % ROWS == 0 and seq % ROWS == 0 and seq % ROPE_GRID_W == 0
    nct = n_ctx // ROWS
    nh = GDN_HEADS

    xs = (ctx, x)
    pad_rows = (-(1 + b)) % 8
    cc = jnp.concatenate([c_ctx[None, :], c, jnp.zeros((pad_rows, d), F32)], axis=0)
    mod_all = _ada(cc, w_ada, b_ada).reshape(depth, cc.shape[0], 3, d)

    perm = _inproj_perm()
    qperm = _mla_q_perm()
    kperm, vperm = _mla_kv_perm()
    tab = _rope_tables(n_ctx, seq)
    ob64 = _head_block_ones(GDN_W, GDN_DV)
    exp_m = jnp.asarray(np.stack(
        [_expand_matrix(_S_A + dd * nh, nh, GDN_DK, GDN_QK) for dd in range(2)]
        + [_expand_matrix(_S_B + dd * nh, nh, GDN_DK, GDN_QK) for dd in range(2)]), BF16)
    eqk_np = np.zeros((GLA_QK, GLA_W), np.float32)
    sbd_np = np.zeros((GLA_W, GLA_QK), np.float32)
    for h in range(GLA_HEADS):
        eqk_np[h * GLA_DK:(h + 1) * GLA_DK, h * GLA_DV:(h + 1) * GLA_DV] = 1.0
        sbd_np[h * GLA_DV:(h + 1) * GLA_DV, h * GLA_DK:(h + 1) * GLA_DK] = 1.0
    eqk = jnp.asarray(eqk_np, BF16)
    sbd = jnp.asarray(sbd_np)

    out = None
    for layer in range(depth):
        last = layer == depth - 1
        w_p = _take_cols(w_in[layer], perm).astype(BF16)
        wabt = w_in[layer][:, _O_A:_O_A + 4 * nh].T.astype(BF16)
        mod = mod_all[layer]
        wuq = _take_cols(mla_w_uq[layer], qperm).astype(BF16)
        wuk = _take_cols(mla_w_ukv[layer], kperm).astype(BF16)
        wuv = jnp.take(mla_w_ukv[layer], jnp.asarray(vperm), axis=1).astype(BF16)
        gdn_in, mla_z, gla_in, small, abt, qh, kh, vh = _inproj(
            xs, mod, w_p, wabt, tab, mla_q_norm_w[layer][None, :], mla_kv_norm_w[layer][None, :],
            wuq, wuk, wuv, nct)

        convw = jnp.concatenate(
            [gdn_conv_w[layer], jnp.zeros((8 - GDN_CONV, GDN_QKV), F32)], axis=0)
        a_flat = gdn_a_log[layer].reshape(-1)
        dt_flat = gdn_dt_bias[layer].reshape(-1)
        prow = jnp.zeros((8, LANES), F32)
        prow = prow.at[0, _S_A:_S_A + 2 * nh].set(a_flat).at[1, _S_A:_S_A + 2 * nh].set(dt_flat)
        pcol = jnp.zeros((4 * nh, LANES), F32)
        pcol = pcol.at[0:2 * nh, 0].set(a_flat).at[0:2 * nh, 1].set(dt_flat)
        gnw = jnp.tile(gdn_norm_w[layer], nh)[None, :]

        anw = jnp.tile(gla_norm_w[layer], GLA_HEADS)[None, :]
        wg = jnp.zeros((2, LANES, GLA_QK), F32)
        for dd in range(2):
            r0 = _S_GLOW + dd * GLA_GATE_RANK
            wg = wg.at[dd, r0:r0 + GLA_GATE_RANK, :].set(gla_w_gk[layer, dd])
        gdn_f, gdn_b, gla_f, gla_b = _scans(
            nct, (gdn_in, small, abt, convw, prow, pcol, exp_m, ob64),
            (gla_in, small, wg.astype(BF16), gla_b_gk[layer][:, None, :], eqk, sbd))

        mla_o = _attention(qh, kh, vh, mla_z, nct, with_ctx=not last)

        if last and len(xs) > 1:
            xs = (jnp.concatenate(xs, axis=1),)
        res = _outproj(gdn_f, gdn_b, gdn_in, gnw, gla_f, gla_b, gla_in, anw, ob64, mla_o, xs,
                       mod, w_out[layer].astype(BF16), nct, final_norm_w[None, :] if last else None)
        if last:
            out = res
        else:
            xs = (res,)
    return out
```

```python
import functools
import math

import numpy as np
import jax
import jax.numpy as jnp
from jax import lax
from jax.experimental import pallas as pl
from jax.experimental.pallas import tpu as pltpu

F32 = jnp.float32
BF16 = jnp.bfloat16
EPS = 1e-6

GDN_HEADS, GDN_DK, GDN_DV, GDN_CONV = 4, 64, 64, 5
GDN_QK = GDN_HEADS * GDN_DK
GDN_W = GDN_HEADS * GDN_DV
GDN_QKV = 2 * GDN_QK + GDN_W
MLA_HEADS, MLA_Q_RANK, MLA_KV_RANK = 8, 384, 256
MLA_NOPE, MLA_ROPE, MLA_DV = 64, 32, 64
MLA_W = MLA_HEADS * MLA_DV
MLA_SCALE = (MLA_NOPE + MLA_ROPE) ** -0.5
ROPE_THETA = 10000.0
ROPE_GRID_W = 64
GLA_HEADS, GLA_DK, GLA_DV = 4, 32, 64
GLA_QK = GLA_HEADS * GLA_DK
GLA_W = GLA_HEADS * GLA_DV
GLA_GATE_RANK = 16
GLA_GATE_NORM = 16.0
CHUNK = 64
LEVELS = (32, 16, 8, 4, 2, 1)

LANES = 128
MXU_N = 256
ROWS = 256
HALO = 8
ATTN_HEADS = 8
VMEM_LIMIT = 56 * 1024 * 1024

_O_GDN_QKV, _O_GDN_Z, _O_A, _O_B = 0, 768, 1024, 1032
_O_CQ, _O_CKV, _O_KR, _O_MLA_Z = 1040, 1424, 1680, 1712
_O_GLA_Q, _O_GLA_K, _O_GLA_V, _O_GLA_Z, _O_GLOW = 2224, 2352, 2480, 2736, 2992
_S_A, _S_B, _S_GLOW = 0, 8, 16
_W_GDN, _W_MLA, _W_MLAZ, _W_GLA = 1024, 640, 512, 768
_N_IN_PAD = _W_GDN + _W_MLA + _W_MLAZ + _W_GLA + 2 * LANES


def _dot(a, b):
    return jnp.dot(a, b, preferred_element_type=F32)


def _dot_nt(a, b):
    return lax.dot_general(a, b, (((1,), (1,)), ((), ())), preferred_element_type=F32)


def _dot_tn(a, b):
    return lax.dot_general(a, b, (((0,), (0,)), ((), ())), preferred_element_type=F32)


def _bdot(a, b):
    return _dot(a.astype(BF16), b.astype(BF16))


def _split2(x):
    x1 = x.astype(BF16)
    return x1, (x - x1.astype(F32)).astype(BF16)


def _split3(x):
    x1 = x.astype(BF16)
    r1 = x - x1.astype(F32)
    x2 = r1.astype(BF16)
    x3 = (r1 - x2.astype(F32)).astype(BF16)
    return x1, x2, x3


def _sel_dot(m01, x):
    n = x.shape[1]
    y = _dot(m01, jnp.concatenate(_split3(x), axis=1))
    return y[:, :n] + y[:, n:2 * n] + y[:, 2 * n:]


def _dot_sel(x, m01):
    n = x.shape[0]
    y = _dot(jnp.concatenate(_split3(x), axis=0), m01)
    return y[:n] + y[n:2 * n] + y[2 * n:]


def _dot_sel2(x, m01):
    n = x.shape[0]
    y = _dot(jnp.concatenate(_split2(x), axis=0), m01)
    return y[:n] + y[n:]


def _dot_sel_nt(x, m01):
    n = x.shape[0]
    y = _dot_nt(jnp.concatenate(_split3(x), axis=0), m01)
    return y[:n] + y[n:2 * n] + y[2 * n:]


def _softplus(x):
    return jnp.maximum(x, 0.0) + jnp.log1p(jnp.exp(-jnp.abs(x)))


def _silu(x):
    return x * jax.nn.sigmoid(x)


def _params(*sem):
    return pltpu.CompilerParams(dimension_semantics=sem, vmem_limit_bytes=VMEM_LIMIT)


def _scan_consts(rev):
    t = np.arange(ROWS)
    ch = t // CHUNK
    p = (CHUNK - 1 - t % CHUNK) if rev else (t % CHUNK)
    same = ch[:, None] == ch[None, :]
    tri = same & (p[None, :] <= p[:, None])
    mq, mk = [], []
    lv = np.full((ROWS, ROWS), -1.0, np.float32)
    for li, s in enumerate(LEVELS):
        blk = p // s
        mq.append(same & (p[None, :] > (blk * s)[:, None]) & (p[None, :] <= p[:, None]))
        mk.append(same & (p[None, :] > p[:, None]) & (p[None, :] <= ((blk + 1) * s)[:, None]))
        pair = same & ((blk % 2) == 1)[:, None] & (blk[None, :] == (blk - 1)[:, None])
        lv[pair] = li
    stack = np.concatenate([tri] + mq[:-1] + mk[:-1], axis=0)
    lvm = np.stack([lv == li for li in range(len(LEVELS))])
    negm = np.where(tri, 0.0, -np.inf).astype(np.float32)
    return dict(tri=jnp.asarray(tri, BF16), ones=jnp.asarray(same, BF16), negm=jnp.asarray(negm),
                lv=jnp.asarray(lv), lvm=jnp.asarray(lvm, BF16), stack=jnp.asarray(stack, BF16))


def _head_block_ones(n, width):
    i = np.arange(n)
    return jnp.asarray((i[:, None] // width) == (i[None, :] // width), BF16)


def _inproj_perm():
    perm = np.full((_N_IN_PAD,), -1, np.int64)

    def put(src, n, at):
        perm[at:at + n] = np.arange(src, src + n)

    put(_O_GDN_QKV, GDN_QKV, 0)
    put(_O_GDN_Z, GDN_W, GDN_QKV)
    pos = _W_GDN
    put(_O_CQ, MLA_Q_RANK, pos)
    put(_O_CKV, MLA_KV_RANK, pos + MLA_Q_RANK)
    pos += _W_MLA
    put(_O_MLA_Z, MLA_W, pos)
    pos += _W_MLAZ
    put(_O_GLA_Q, GLA_QK, pos)
    put(_O_GLA_K, GLA_QK, pos + GLA_QK)
    put(_O_GLA_V, GLA_W, pos + 2 * GLA_QK)
    put(_O_GLA_Z, GLA_W, pos + 2 * GLA_QK + GLA_W)
    pos += _W_GLA
    put(_O_A, 2 * GDN_HEADS, pos + _S_A)
    put(_O_B, 2 * GDN_HEADS, pos + _S_B)
    put(_O_GLOW, 2 * GLA_GATE_RANK, pos + _S_GLOW)
    pos += LANES
    put(_O_KR, MLA_ROPE, pos + MLA_NOPE)
    return perm


def _take_cols(w, perm):
    cols = jnp.take(w, jnp.asarray(np.maximum(perm, 0)), axis=1)
    return jnp.where(jnp.asarray(perm >= 0)[None, :], cols, 0.0)


def _mla_q_perm():
    perm = np.full((MLA_HEADS * LANES,), -1, np.int64)
    d = MLA_NOPE + MLA_ROPE
    for h in range(MLA_HEADS):
        perm[h * LANES:h * LANES + d] = np.arange(h * d, (h + 1) * d)
    return perm


def _mla_kv_perm():
    dk = MLA_NOPE + MLA_DV
    kperm = np.full((MLA_HEADS * LANES,), -1, np.int64)
    vperm = np.zeros((MLA_W,), np.int64)
    for h in range(MLA_HEADS):
        kperm[h * LANES:h * LANES + MLA_NOPE] = np.arange(h * dk, h * dk + MLA_NOPE)
        vperm[h * MLA_DV:(h + 1) * MLA_DV] = np.arange(h * dk + MLA_NOPE, (h + 1) * dk)
    return kperm, vperm


def _rope_tables(n_ctx, n_lat):
    rows = n_lat // ROPE_GRID_W
    row = np.repeat(np.arange(rows, dtype=np.float32), ROPE_GRID_W)
    col = np.tile(np.arange(ROPE_GRID_W, dtype=np.float32), rows)
    n_freq = MLA_ROPE // 4
    inv = (ROPE_THETA ** (-np.arange(n_freq, dtype=np.float32) / n_freq)).astype(np.float32)
    ang = np.concatenate([row[:, None] * inv, col[:, None] * inv], axis=-1)
    cos = np.concatenate([np.ones((n_ctx, 2 * n_freq), np.float32), np.cos(ang)], 0)
    sin = np.concatenate([np.zeros((n_ctx, 2 * n_freq), np.float32), np.sin(ang)], 0)
    n = n_ctx + n_lat
    half = MLA_ROPE // 2
    tab = np.zeros((3, n, LANES), np.float32)
    tab[0, :, :MLA_NOPE] = 1.0
    tab[0, :, MLA_NOPE:MLA_NOPE + half] = cos
    tab[0, :, MLA_NOPE + half:MLA_NOPE + MLA_ROPE] = cos
    tab[1, :, MLA_NOPE:MLA_NOPE + half] = -sin
    tab[2, :, MLA_NOPE + half:MLA_NOPE + MLA_ROPE] = sin
    return jnp.asarray(tab)


def _expand_matrix(src0, n_src, width, n_out):
    m = np.zeros((LANES, n_out), np.float32)
    for h in range(n_src):
        m[src0 + h, h * width:(h + 1) * width] = 1.0
    return m


def _ada_kernel(c_ref, w_ref, b_ref, o_ref):
    c = _silu(c_ref[...]).astype(BF16)
    o_ref[0] = _dot(c, w_ref[0].astype(BF16)) + b_ref[0]


def _ada(cc, w_ada, b_ada):
    nl, d, n3 = w_ada.shape
    r = cc.shape[0]
    tn = 1024
    return pl.pallas_call(
        _ada_kernel,
        grid=(nl, n3 // tn),
        in_specs=[pl.BlockSpec((r, d), lambda l, j: (0, 0)),
                  pl.BlockSpec((1, d, tn), lambda l, j: (l, 0, j)),
                  pl.BlockSpec((1, 1, tn), lambda l, j: (l, 0, j))],
        out_specs=pl.BlockSpec((1, r, tn), lambda l, j: (l, 0, j)),
        out_shape=jax.ShapeDtypeStruct((nl, r, n3), F32),
        compiler_params=_params("arbitrary", "arbitrary"),
        name="ada_mod",
    )(cc, w_ada, b_ada.reshape(nl, 1, n3))


def _token_block(nct, refs):
    if len(refs) == 1:
        return refs[0][0]
    return jnp.where(pl.program_id(1) < nct, refs[0][0], refs[1][0])


def _token_specs(arrays, nct, d):
    if len(arrays) == 1:
        return [pl.BlockSpec((1, ROWS, d), lambda bi, j: (bi, j, 0))]
    return [pl.BlockSpec((1, ROWS, d), lambda bi, j: (bi, jnp.minimum(j, nct - 1), 0)),
            pl.BlockSpec((1, ROWS, d), lambda bi, j: (bi, jnp.maximum(j - nct, 0), 0))]


def _halo_specs(arrays, nct, d, after):
    hb = ROWS // HALO
    offs = (0,) if len(arrays) == 1 else (0, nct)

    def spec(a, off):
        last = a.shape[1] // HALO - 1

        def index(bi, j):
            h = (j - off + 1) * hb if after else (j - off) * hb - 1
            return bi, jnp.clip(h, 0, last), 0
        return pl.BlockSpec((1, HALO, d), index)

    return [spec(a, off) for a, off in zip(arrays, offs)]


def _rope(x, tab_ref):
    half = MLA_ROPE // 2
    return (x * tab_ref[0] + pltpu.roll(x, LANES - half, 1) * tab_ref[1]
            + pltpu.roll(x, half, 1) * tab_ref[2])


def _inproj_kernel(nct, nx, *refs):
    x_refs, p_refs, n_refs, refs = refs[:nx], refs[nx:2 * nx], refs[2 * nx:3 * nx], refs[3 * nx:]
    (mod_ref, w_ref, wabt_ref, tab_ref, qw_ref, kvw_ref, wuq_ref, wuk_ref, wuv_ref, convw_ref,
     ob_ref, ogdn, omlaz, ogla, osmall, oabt, q_ref, k_ref, v_ref, xe_scr) = refs
    j = pl.program_id(1)
    nt = pl.num_programs(1)

    def modulated(x):
        h = x * lax.rsqrt(jnp.mean(x * x, axis=-1, keepdims=True) + EPS)
        return (h * (1.0 + mod_ref[0, 1:2, :]) + mod_ref[0, 0:1, :]).astype(BF16)

    hb = modulated(_token_block(nct, x_refs))
    pos = 0
    proj = {}
    for name, ref, n in (("gdn", None, _W_GDN), ("mla", None, _W_MLA), ("mlaz", omlaz, _W_MLAZ),
                         ("gla", ogla, _W_GLA), ("small", osmall, LANES), ("kr", None, LANES)):
        y = _dot(hb, w_ref[:, pos:pos + n])
        if ref is None:
            proj[name] = y
        else:
            ref[0] = y
        pos += n
    oabt[0] = _dot_nt(wabt_ref[...], hb)

    halo = jnp.concatenate([_token_block(nct, p_refs), _token_block(nct, n_refs)], axis=0)
    yh = _dot(modulated(halo), w_ref[:, 0:GDN_QKV])
    has_prev = jnp.logical_and(j != 0, j != nct)
    has_next = jnp.logical_and(j != nct - 1, j != nt - 1)
    xe_scr[0:HALO, :] = jnp.where(has_prev, yh[:HALO], 0.0)
    xe_scr[HALO:HALO + ROWS, :] = proj["gdn"][:, :GDN_QKV]
    xe_scr[HALO + ROWS:, :] = jnp.where(has_next, yh[HALO:], 0.0)
    pad = (GDN_CONV - 1) // 2
    conv = jnp.zeros((ROWS, GDN_QKV), F32)
    for t in range(GDN_CONV):
        conv = conv + convw_ref[t:t + 1, :] * xe_scr[pl.ds(HALO - pad + t, ROWS), :]
    hqkv = _silu(conv)
    ob = ob_ref[...]
    qn = hqkv[:, :GDN_QK]
    kn = hqkv[:, GDN_QK:2 * GDN_QK]
    qn = qn * lax.rsqrt(_dot_sel2(qn * qn, ob) + EPS) * (GDN_DK ** -0.5)
    kn = kn * lax.rsqrt(_dot_sel2(kn * kn, ob) + EPS)
    ogdn[0] = jnp.concatenate([qn, kn, hqkv[:, 2 * GDN_QK:], proj["gdn"][:, GDN_QKV:]], axis=1)

    cq = proj["mla"][:, :MLA_Q_RANK]
    ckv = proj["mla"][:, MLA_Q_RANK:]
    cq = cq * lax.rsqrt(jnp.mean(cq * cq, axis=-1, keepdims=True) + EPS) * qw_ref[...]
    ckv = ckv * lax.rsqrt(jnp.mean(ckv * ckv, axis=-1, keepdims=True) + EPS) * kvw_ref[...]
    cqb = cq.astype(BF16)
    ckvb = ckv.astype(BF16)
    qf = _dot(cqb, wuq_ref[...])
    kf = _dot(ckvb, wuk_ref[...])
    vf = _dot(ckvb, wuv_ref[...])
    kr = _rope(proj["kr"], tab_ref)
    q_scale = MLA_SCALE * math.log2(math.e)
    for hd in range(MLA_HEADS):
        sl = slice(hd * LANES, (hd + 1) * LANES)
        q_ref[0, hd] = (_rope(qf[:, sl], tab_ref) * q_scale).astype(BF16)
        k_ref[0, hd] = (kf[:, sl] + kr).astype(BF16)
    ones = jnp.ones((ROWS, MXU_N - LANES), BF16)
    for hp in range(MLA_HEADS // 2):
        v_ref[0, hp] = jnp.concatenate(
            [vf[:, hp * LANES:(hp + 1) * LANES].astype(BF16), ones], axis=1)


def _inproj(xs, mod, w_p, wabt, tab, qw, kvw, wuq, wuk, wuv, convw, ob, nct):
    b, d = xs[0].shape[0], xs[0].shape[2]
    ta = sum(a.shape[1] for a in xs)
    nt = ta // ROWS
    widths = (_W_GDN, _W_MLAZ, _W_GLA, LANES)
    row = lambda bi, j: (bi, j, 0)
    full2 = lambda bi, j: (0, 0)
    hm = lambda bi, j: (bi, 0, j, 0)
    consts = (w_p, wabt)
    mla_consts = (qw, kvw, wuq, wuk, wuv, convw, ob)
    return pl.pallas_call(
        functools.partial(_inproj_kernel, nct, len(xs)),
        grid=(b, nt),
        in_specs=_token_specs(xs, nct, d) + _halo_specs(xs, nct, d, False)
        + _halo_specs(xs, nct, d, True)
        + [pl.BlockSpec((1, 3, d), lambda bi, j: (jnp.where(j < nct, 0, 1 + bi), 0, 0))]
        + [pl.BlockSpec(a.shape, full2) for a in consts]
        + [pl.BlockSpec((3, ROWS, LANES), lambda bi, j: (0, j, 0))]
        + [pl.BlockSpec(a.shape, full2) for a in mla_consts],
        out_specs=[pl.BlockSpec((1, ROWS, n), row) for n in widths]
        + [pl.BlockSpec((1, 4 * GDN_HEADS, ROWS), lambda bi, j: (bi, 0, j)),
           pl.BlockSpec((1, MLA_HEADS, ROWS, LANES), hm),
           pl.BlockSpec((1, MLA_HEADS, ROWS, LANES), hm),
           pl.BlockSpec((1, MLA_HEADS // 2, ROWS, MXU_N), hm)],
        out_shape=[jax.ShapeDtypeStruct((b, ta, n), F32) for n in widths]
        + [jax.ShapeDtypeStruct((b, 4 * GDN_HEADS, ta), F32),
           jax.ShapeDtypeStruct((b, MLA_HEADS, ta, LANES), BF16),
           jax.ShapeDtypeStruct((b, MLA_HEADS, ta, LANES), BF16),
           jax.ShapeDtypeStruct((b, MLA_HEADS // 2, ta, MXU_N), BF16)],
        scratch_shapes=[pltpu.VMEM((ROWS + 2 * HALO, GDN_QKV), F32)],
        compiler_params=_params("arbitrary", "arbitrary"),
        name="in_proj",
    )(*xs, *xs, *xs, mod, *consts, tab, *mla_consts)


def _scan_block_index(j, nct, nt, rev):
    if not rev:
        return j
    return jnp.where(j < nct, nct - 1 - j, nt - 1 - (j - nct))


def _head_lane_mask(n, width, h):
    lane = lax.broadcasted_iota(jnp.int32, (1, n), 1)
    return (lane >= h * width) & (lane < (h + 1) * width)


def _gdn_prep(d, x_ref, small_ref, abt_ref, prow_ref, pcol_ref, tri, ones, exp_ref):
    q = x_ref[0, :, :GDN_QK]
    k = x_ref[0, :, GDN_QK:2 * GDN_QK]
    v = x_ref[0, :, 2 * GDN_QK:GDN_QKV]
    sm = small_ref[0]
    g_all = -jnp.exp(prow_ref[0:1, :]) * _softplus(sm + prow_ref[1:2, :])
    beta_all = jax.nn.sigmoid(sm)
    gc_all = _sel_dot(tri, g_all)
    gl_all = _sel_dot(ones, g_all)
    g_t = -jnp.exp(pcol_ref[:, 0:1]) * _softplus(abt_ref[0] + pcol_ref[:, 1:2])
    gc_t = _dot_sel_nt(g_t, tri)
    gc_w = _dot_sel2(gc_all, exp_ref[d])
    gl_w = _dot_sel2(gl_all, exp_ref[d])
    beta_w = _dot_sel2(beta_all, exp_ref[2 + d])
    kb = k * beta_w
    return dict(q=q, kbf=k.astype(BF16), kb=kb, vb=v * beta_w, kbg=kb * jnp.exp(gc_w),
                qg=q * jnp.exp(gc_w), kdec=k * jnp.exp(gl_w - gc_w),
                gc_all=gc_all, gc_t=gc_t, gl_all=gl_all)


def _gdn_stages(xf_ref, smf_ref, abtf_ref, xb_ref, smb_ref, abtb_ref, prow_ref, pcol_ref, tri_ref,
                ones_ref, negm_ref, lvm_ref, exp_ref, of_ref, obk_ref, s_scr):
    nh, dk = GDN_HEADS, GDN_DK
    nchunk = ROWS // CHUNK
    nlev = len(LEVELS)
    ones = ones_ref[...]
    dir_refs = ((xf_ref, smf_ref, abtf_ref), (xb_ref, smb_ref, abtb_ref))
    prep = []
    for d in range(2):
        prep.append(_gdn_prep(d, *dir_refs[d], prow_ref, pcol_ref, tri_ref[d], ones, exp_ref))
        yield
    ri = lax.broadcasted_iota(jnp.int32, (ROWS, ROWS), 0)
    ci = lax.broadcasted_iota(jnp.int32, (ROWS, ROWS), 1)
    eye = (ri == ci).astype(BF16)
    chains = [(d, h) for d in range(2) for h in range(nh)]
    nc = len(chains)

    low, a_intra = [], []
    for d, h in chains:
        p = prep[d]
        idx = d * nh + h
        hm = _head_lane_mask(GDN_QK, dk, h)
        decay = jnp.exp(p["gc_all"][:, idx:idx + 1] - p["gc_t"][idx:idx + 1, :] + negm_ref[d])
        both = jnp.concatenate([jnp.where(hm, p["kb"], 0.0), jnp.where(hm, p["q"], 0.0)], axis=0)
        kq = _dot_nt(both.astype(BF16), p["kbf"])
        low.append((kq[:ROWS] * decay).astype(BF16))
        a_intra.append((kq[ROWS:] * decay).astype(BF16))
        if h == nh - 1:
            yield

    t_inv = [eye - low[i] * lvm_ref[d, nlev - 1] for i, (d, h) in enumerate(chains)]
    for li in reversed(range(nlev - 1)):
        ys = [_dot(low[i] * lvm_ref[d, li], t_inv[i]).astype(BF16) for i, (d, h) in enumerate(chains)]
        t_inv = [t_inv[i] - _dot(t_inv[i], ys[i]).astype(BF16) for i in range(nc)]
        yield
    uw = []
    for i, (d, h) in enumerate(chains):
        sl = slice(h * dk, (h + 1) * dk)
        rhs = jnp.concatenate([prep[d]["vb"][:, sl], prep[d]["kbg"][:, sl]], axis=1)
        uw.append(_dot(t_inv[i], rhs.astype(BF16)))

    q2, ou, gb = [], [], []
    for i, (d, h) in enumerate(chains):
        sl = slice(h * dk, (h + 1) * dk)
        uwb = uw[i].astype(BF16)
        auw = _dot(a_intra[i], uwb)
        ou.append(auw[:, :GDN_DV])
        q2.append((prep[d]["qg"][:, sl] - auw[:, GDN_DV:]).astype(BF16))
        kd = prep[d]["kdec"][:, sl].astype(BF16)
        gb.append([_dot_tn(kd[c * CHUNK:(c + 1) * CHUNK], uwb[c * CHUNK:(c + 1) * CHUNK])
                   for c in range(nchunk)])
    yield
    states = [s_scr[i] for i in range(nc)]
    o_parts = [[None] * nchunk for _ in chains]
    for step in range(nchunk):
        for i, (d, h) in enumerate(chains):
            c = nchunk - 1 - step if d == 1 else step
            idx = d * nh + h
            r0 = c * CHUNK
            rs = slice(r0, r0 + CHUNK)
            sb = states[i].astype(BF16)
            o_parts[i][c] = _dot(q2[i][rs], sb) + ou[i][rs]
            states[i] = (states[i] * jnp.exp(prep[d]["gl_all"][r0:r0 + 1, idx:idx + 1])
                         - _dot(gb[i][c][:, GDN_DV:].astype(BF16), sb) + gb[i][c][:, :GDN_DV])
        yield
    for i in range(len(chains)):
        s_scr[i] = states[i]
    outs = [jnp.concatenate(o_parts[i], axis=0) for i in range(len(chains))]
    of_ref[0] = jnp.concatenate(outs[:nh], axis=1)
    obk_ref[0] = jnp.concatenate(outs[nh:], axis=1)


def _scan_row_specs(nct, nt, rev, width):
    blk_of = functools.partial(_scan_block_index, nct=nct, nt=nt, rev=rev)
    return pl.BlockSpec((1, ROWS, width), lambda bi, j: (bi, blk_of(j), 0))


def _gdn_specs(nct, nt, gdn_in, small, abt, prow, pcol, exp_m):
    cf, cb = _scan_consts(False), _scan_consts(True)
    stack = lambda name: jnp.stack([cf[name], cb[name]])
    full2 = lambda bi, j: (0, 0)
    full3 = lambda bi, j: (0, 0, 0)

    def dir_specs(rev):
        blk_of = functools.partial(_scan_block_index, nct=nct, nt=nt, rev=rev)
        return [
            _scan_row_specs(nct, nt, rev, GDN_QKV),
            _scan_row_specs(nct, nt, rev, LANES),
            pl.BlockSpec((1, 4 * GDN_HEADS, ROWS), lambda bi, j: (bi, 0, blk_of(j))),
        ]

    in_specs = dir_specs(False) + dir_specs(True) + [
        pl.BlockSpec(prow.shape, full2),
        pl.BlockSpec(pcol.shape, full2),
        pl.BlockSpec((2, ROWS, ROWS), full3),
        pl.BlockSpec((ROWS, ROWS), full2),
        pl.BlockSpec((2, ROWS, ROWS), full3),
        pl.BlockSpec((2, len(LEVELS), ROWS, ROWS), lambda bi, j: (0, 0, 0, 0)),
        pl.BlockSpec(exp_m.shape, full3),
    ]
    dir_args = [gdn_in, small, abt]
    args = dir_args + dir_args + [prow, pcol, stack("tri"), cf["ones"], stack("negm"),
                                  stack("lvm"), exp_m]
    out_specs = [_scan_row_specs(nct, nt, False, GDN_W), _scan_row_specs(nct, nt, True, GDN_W)]
    scratch = [pltpu.VMEM((2 * GDN_HEADS, GDN_DK, GDN_DV), F32)]
    return in_specs, args, out_specs, scratch


def _gla_stages(xf_ref, smf_ref, xb_ref, smb_ref, wg_ref, bg_ref, stack_ref, lv_ref, eqk_ref,
                sbd_ref, of_ref, obk_ref, s_scr):
    nh, dk, dv = GLA_HEADS, GLA_DK, GLA_DV
    nchunk = ROWS // CHUNK
    nlev = len(LEVELS)
    n = GLA_QK
    lane_head = [_head_lane_mask(GLA_QK, dk, h) for h in range(nh)]
    out_head = [_head_lane_mask(GLA_W, dv, h) for h in range(nh)]
    eqk = eqk_ref[...]
    sbd = sbd_ref[...]
    x_refs, sm_refs, o_refs = (xf_ref, xb_ref), (smf_ref, smb_ref), (of_ref, obk_ref)
    dirs = range(2)
    xs = [x_refs[d][0] for d in dirs]
    q = [xs[d][:, :GLA_QK] * (dk ** -0.5) for d in dirs]
    k = [xs[d][:, GLA_QK:2 * GLA_QK] for d in dirs]
    v = [xs[d][:, 2 * GLA_QK:2 * GLA_QK + GLA_W] for d in dirs]
    vb = [v[d].astype(BF16) for d in dirs]
    gk = [_dot(sm_refs[d][0].astype(BF16), wg_ref[d]) + bg_ref[d] for d in dirs]
    la = [-_softplus(-gk[d]) * (1.0 / GLA_GATE_NORM) for d in dirs]
    la2 = [jnp.concatenate(_split2(la[d]), axis=1) for d in dirs]
    ys = [_dot(stack_ref[d], la2[d]) for d in dirs]
    cums = [ys[d][:, :n] + ys[d][:, n:] for d in dirs]
    piece = lambda d, i: cums[d][i * ROWS:(i + 1) * ROWS]
    bcum = [piece(d, 0) for d in dirs]
    last_row = [0 if d == 1 else CHUNK - 1 for d in dirs]
    blast = [jnp.concatenate(
        [jnp.broadcast_to(bcum[d][c * CHUNK + last_row[d]:c * CHUNK + last_row[d] + 1, :], (CHUNK, n))
         for c in range(nchunk)], axis=0) for d in dirs]
    la_next = [pltpu.roll(la[d], 1 if d == 1 else ROWS - 1, 0) for d in dirs]
    qg = [(q[d] * jnp.exp(bcum[d])).astype(BF16) for d in dirs]
    kdec = [(k[d] * jnp.exp(blast[d] - bcum[d])).astype(BF16) for d in dirs]
    yield

    lv4 = [jnp.concatenate([lv_ref[d]] * nh, axis=0) for d in dirs]
    acc = [jnp.zeros((nh * ROWS, ROWS), F32) for d in dirs]
    for li in range(nlev):
        if li < nlev - 1:
            ql = [q[d] * jnp.exp(piece(d, 1 + li)) for d in dirs]
            kl = [(k[d] * jnp.exp(piece(d, nlev + li))).astype(BF16) for d in dirs]
        else:
            ql = q
            kl = [(k[d] * jnp.exp(la_next[d])).astype(BF16) for d in dirs]
        qs = [jnp.concatenate([jnp.where(lane_head[h], ql[d], 0.0) for h in range(nh)],
                              axis=0).astype(BF16) for d in dirs]
        ps = [_dot_nt(qs[d], kl[d]) for d in dirs]
        acc = [jnp.where(lv4[d] == float(li), ps[d], acc[d]) for d in dirs]
        yield
    pv = [_dot(acc[d].astype(BF16), vb[d]) for d in dirs]
    o = [_dot_sel(q[d] * k[d], eqk) * v[d] for d in dirs]
    for h in range(nh):
        o = [o[d] + jnp.where(out_head[h], pv[d][h * ROWS:(h + 1) * ROWS], 0.0) for d in dirs]
    yield

    states = [s_scr[d] for d in dirs]
    o_parts = [[None] * nchunk for d in dirs]
    for step in range(nchunk):
        for d in dirs:
            c = nchunk - 1 - step if d == 1 else step
            rs = slice(c * CHUNK, (c + 1) * CHUNK)
            o_parts[d][c] = _dot_nt(qg[d][rs], states[d].astype(BF16))
            states[d] = (states[d] * jnp.exp(blast[d][c * CHUNK:c * CHUNK + 1, :])
                         + sbd * _dot_tn(vb[d][rs], kdec[d][rs]))
        yield
    for d in dirs:
        s_scr[d] = states[d]
        o_refs[d][0] = o[d] + jnp.concatenate(o_parts[d], axis=0)


def _gla_specs(nct, nt, gla_in, small, wg, bg, eqk, sbd):
    cf, cb = _scan_consts(False), _scan_consts(True)
    stack = jnp.stack([cf["stack"], cb["stack"]])
    lv = jnp.stack([cf["lv"], cb["lv"]])
    full2 = lambda bi, j: (0, 0)
    full3 = lambda bi, j: (0, 0, 0)
    in_specs = [
        _scan_row_specs(nct, nt, False, _W_GLA), _scan_row_specs(nct, nt, False, LANES),
        _scan_row_specs(nct, nt, True, _W_GLA), _scan_row_specs(nct, nt, True, LANES),
        pl.BlockSpec(wg.shape, full3),
        pl.BlockSpec(bg.shape, full3),
        pl.BlockSpec(stack.shape, full3),
        pl.BlockSpec(lv.shape, full3),
        pl.BlockSpec(eqk.shape, full2),
        pl.BlockSpec(sbd.shape, full2),
    ]
    args = [gla_in, small, gla_in, small, wg, bg, stack, lv, eqk, sbd]
    out_specs = [_scan_row_specs(nct, nt, False, GLA_W), _scan_row_specs(nct, nt, True, GLA_W)]
    scratch = [pltpu.VMEM((2, GLA_W, GLA_QK), F32)]
    return in_specs, args, out_specs, scratch


_N_GDN_IN, _N_GLA_IN = 13, 10


def _scan_kernel(*refs):
    gdn_in, refs = refs[:_N_GDN_IN], refs[_N_GDN_IN:]
    gla_in, refs = refs[:_N_GLA_IN], refs[_N_GLA_IN:]
    gdn_out, gla_out, (gdn_state, gla_state) = refs[:2], refs[2:4], refs[4:]

    @pl.when(pl.program_id(1) == 0)
    def _():
        gdn_state[...] = jnp.zeros_like(gdn_state)
        gla_state[...] = jnp.zeros_like(gla_state)

    stages = [_gdn_stages(*gdn_in, *gdn_out, gdn_state),
              _gla_stages(*gla_in, *gla_out, gla_state)]
    while stages:
        for s in list(stages):
            if next(s, StopIteration) is StopIteration:
                stages.remove(s)


def _scans(nct, gdn_args, gla_args):
    gdn_in = gdn_args[0]
    b, ta, _ = gdn_in.shape
    nt = ta // ROWS
    g_in, g_args, g_out, g_scr = _gdn_specs(nct, nt, *gdn_args)
    a_in, a_args, a_out, a_scr = _gla_specs(nct, nt, *gla_args)
    assert len(g_in) == _N_GDN_IN and len(a_in) == _N_GLA_IN
    return pl.pallas_call(
        _scan_kernel,
        grid=(b, nt),
        in_specs=g_in + a_in,
        out_specs=g_out + a_out,
        out_shape=[jax.ShapeDtypeStruct((b, ta, GDN_W), F32)] * 2
        + [jax.ShapeDtypeStruct((b, ta, GLA_W), F32)] * 2,
        scratch_shapes=g_scr + a_scr,
        compiler_params=_params("arbitrary", "arbitrary"),
        name="gdn_gla_scan",
    )(*g_args, *a_args)


def _attn_kernel(nct, off, n_ctx, q_ref, k_ref, v_ref, z_ref, o_ref):
    i = pl.program_id(2) + off

    def body(nk):
        parts = []
        half = ROWS // 2

        def by_row_halves(dot, lhs, rhs):
            return jnp.concatenate([dot(lhs[:half], rhs), dot(lhs[half:], rhs)], axis=0)

        for hh in range(ATTN_HEADS):
            q = q_ref[0, hh]
            kk = k_ref[0, hh, :nk, :]
            s = by_row_halves(_dot_nt, q, kk) if hh == 0 else _dot_nt(q, kk)
            m = jnp.max(s, axis=-1, keepdims=True)
            p = jnp.exp2(s - m).astype(BF16)
            vv = v_ref[0, hh // 2, :nk, :]
            pv = by_row_halves(_dot, p, vv) if hh == ATTN_HEADS - 1 else _dot(p, vv)
            o = pv[:, (hh % 2) * MLA_DV:(hh % 2 + 1) * MLA_DV] / pv[:, LANES:LANES + 1]
            parts.append(o)
        o_ref[0] = (jnp.concatenate(parts, axis=1) * _silu(z_ref[0])).astype(BF16)

    if off == 0:
        @pl.when(i < nct)
        def _():
            body(n_ctx)

        @pl.when(i >= nct)
        def _():
            body(k_ref.shape[2])
    else:
        body(k_ref.shape[2])


def _attention(q, k, v, z, nct, with_ctx):
    b, nh, ta, _ = q.shape
    nt = ta // ROWS
    off = 0 if with_ctx else nct
    nq = nt - off
    hg = ATTN_HEADS
    wo = hg * MLA_DV
    return pl.pallas_call(
        functools.partial(_attn_kernel, nct, off, nct * ROWS),
        grid=(b, nh // hg, nq),
        in_specs=[pl.BlockSpec((1, hg, ROWS, LANES), lambda bi, hp, i: (bi, hp, i + off, 0)),
                  pl.BlockSpec((1, hg, ta, LANES), lambda bi, hp, i: (bi, hp, 0, 0)),
                  pl.BlockSpec((1, hg // 2, ta, MXU_N), lambda bi, hp, i: (bi, hp, 0, 0)),
                  pl.BlockSpec((1, ROWS, wo), lambda bi, hp, i: (bi, i + off, hp))],
        out_specs=pl.BlockSpec((1, ROWS, wo), lambda bi, hp, i: (bi, i, hp)),
        out_shape=jax.ShapeDtypeStruct((b, nq * ROWS, MLA_W), BF16),
        compiler_params=_params("arbitrary", "arbitrary", "arbitrary"),
        name="mla_attn",
    )(q, k, v, z)


def _outproj_kernel(final, nct, nx, gf_ref, gb_ref, gz_ref, gnw_ref, af_ref, ab_ref, az_ref, anw_ref,
                    ob_ref, m_ref, mod_ref, w_ref, *rest):
    x_refs, rest = rest[:nx], rest[nx:]
    if final:
        fw_ref, o_ref = rest
    else:
        (o_ref,) = rest
    ob = ob_ref[...]

    def gated_head_norm(o, nw_ref, z):
        ms = _dot_sel2(o * o, ob) * (1.0 / GDN_DV)
        return (o * lax.rsqrt(ms + EPS) * nw_ref[...] * _silu(z)).astype(BF16)

    g = gated_head_norm(gf_ref[0] + gb_ref[0], gnw_ref, gz_ref[0])
    a = gated_head_norm(af_ref[0] + ab_ref[0], anw_ref, az_ref[0])
    y = _dot(g, w_ref[0:GDN_W, :])
    y = y + _dot(m_ref[0], w_ref[GDN_W:GDN_W + MLA_W, :])
    y = y + _dot(a, w_ref[GDN_W + MLA_W:, :])
    xn = _token_block(nct, x_refs) + mod_ref[0, 2:3, :] * y
    if final:
        xn = xn * lax.rsqrt(jnp.mean(xn * xn, axis=-1, keepdims=True) + EPS) * fw_ref[...]
    o_ref[0] = xn


def _outproj(gdn_f, gdn_b, gdn_in, gnw, gla_f, gla_b, gla_in, anw, ob, mla_o, xs, mod, w_out,
             nct, final_w):
    assert GDN_DV == GLA_DV and GDN_W == GLA_W
    b, d = xs[0].shape[0], xs[0].shape[2]
    ta = sum(a.shape[1] for a in xs)
    nt = ta // ROWS
    final = final_w is not None
    assert not (final and len(xs) > 1)
    off = nct if final else 0
    moff = off - (ta - mla_o.shape[1]) // ROWS
    nq = nt - off
    row = lambda bi, j: (bi, j + off, 0)
    full2 = lambda bi, j: (0, 0)
    x_specs = [pl.BlockSpec((1, ROWS, d), row)] if len(xs) == 1 else _token_specs(xs, nct, d)
    wide = pl.BlockSpec((1, ROWS, GDN_W), row)
    gz_blk = GDN_QKV // GDN_W
    az_blk = (2 * GLA_QK + GLA_W) // GLA_W
    in_specs = [wide, wide,
                pl.BlockSpec((1, ROWS, GDN_W), lambda bi, j: (bi, j + off, gz_blk)),
                pl.BlockSpec(gnw.shape, full2),
                wide, wide,
                pl.BlockSpec((1, ROWS, GLA_W), lambda bi, j: (bi, j + off, az_blk)),
                pl.BlockSpec(anw.shape, full2),
                pl.BlockSpec(ob.shape, full2),
                pl.BlockSpec((1, ROWS, MLA_W), lambda bi, j: (bi, j + moff, 0)),
                pl.BlockSpec((1, 3, d), lambda bi, j: (jnp.where(j + off < nct, 0, 1 + bi), 0, 0)),
                pl.BlockSpec(w_out.shape, full2)] + x_specs
    args = [gdn_f, gdn_b, gdn_in, gnw, gla_f, gla_b, gla_in, anw, ob, mla_o, mod, w_out, *xs]
    if final:
        in_specs.append(pl.BlockSpec(final_w.shape, full2))
        args.append(final_w)
    return pl.pallas_call(
        functools.partial(_outproj_kernel, final, nct, len(xs)),
        grid=(b, nq),
        in_specs=in_specs,
        out_specs=pl.BlockSpec((1, ROWS, d), lambda bi, j: (bi, j, 0)),
        out_shape=jax.ShapeDtypeStruct((b, nq * ROWS, d), F32),
        compiler_params=_params("arbitrary", "arbitrary"),
        name="out_proj",
    )(*args)


def kernel(x, c, ctx, c_ctx, w_ada, b_ada, w_in, gdn_conv_w, gdn_a_log, gdn_dt_bias, gdn_norm_w,
           mla_q_norm_w, mla_w_uq, mla_kv_norm_w, mla_w_ukv, gla_w_gk, gla_b_gk, gla_norm_w,
           w_out, final_norm_w):
    b, seq, d = x.shape
    n_ctx = ctx.shape[1]
    depth = w_in.shape[0]
    assert n_ctx % ROWS == 0 and seq % ROWS == 0 and seq % ROPE_GRID_W == 0
    nct = n_ctx // ROWS
    nh = GDN_HEADS

    xs = (ctx, x)
    pad_rows = (-(1 + b)) % 8
    cc = jnp.concatenate([c_ctx[None, :], c, jnp.zeros((pad_rows, d), F32)], axis=0)
    mod_all = _ada(cc, w_ada, b_ada).reshape(depth, cc.shape[0], 3, d)

    perm = _inproj_perm()
    qperm = _mla_q_perm()
    kperm, vperm = _mla_kv_perm()
    tab = _rope_tables(n_ctx, seq)
    ob64 = _head_block_ones(GDN_W, GDN_DV)
    exp_m = jnp.asarray(np.stack(
        [_expand_matrix(_S_A + dd * nh, nh, GDN_DK, GDN_QK) for dd in range(2)]
        + [_expand_matrix(_S_B + dd * nh, nh, GDN_DK, GDN_QK) for dd in range(2)]), BF16)
    eqk_np = np.zeros((GLA_QK, GLA_W), np.float32)
    sbd_np = np.zeros((GLA_W, GLA_QK), np.float32)
    for h in range(GLA_HEADS):
        eqk_np[h * GLA_DK:(h + 1) * GLA_DK, h * GLA_DV:(h + 1) * GLA_DV] = 1.0
        sbd_np[h * GLA_DV:(h + 1) * GLA_DV, h * GLA_DK:(h + 1) * GLA_DK] = 1.0
    eqk = jnp.asarray(eqk_np, BF16)
    sbd = jnp.asarray(sbd_np)

    out = None
    for layer in range(depth):
        last = layer == depth - 1
        w_p = _take_cols(w_in[layer], perm).astype(BF16)
        wabt = w_in[layer][:, _O_A:_O_A + 4 * nh].T.astype(BF16)
        mod = mod_all[layer]
        wuq = _take_cols(mla_w_uq[layer], qperm).astype(BF16)
        wuk = _take_cols(mla_w_ukv[layer], kperm).astype(BF16)
        wuv = jnp.take(mla_w_ukv[layer], jnp.asarray(vperm), axis=1).astype(BF16)
        convw = jnp.concatenate(
            [gdn_conv_w[layer], jnp.zeros((8 - GDN_CONV, GDN_QKV), F32)], axis=0)
        gdn_in, mla_z, gla_in, small, abt, qh, kh, vh = _inproj(
            xs, mod, w_p, wabt, tab, mla_q_norm_w[layer][None, :], mla_kv_norm_w[layer][None, :],
            wuq, wuk, wuv, convw, ob64, nct)

        a_flat = gdn_a_log[layer].reshape(-1)
        dt_flat = gdn_dt_bias[layer].reshape(-1)
        prow = jnp.zeros((8, LANES), F32)
        prow = prow.at[0, _S_A:_S_A + 2 * nh].set(a_flat).at[1, _S_A:_S_A + 2 * nh].set(dt_flat)
        pcol = jnp.zeros((4 * nh, LANES), F32)
        pcol = pcol.at[0:2 * nh, 0].set(a_flat).at[0:2 * nh, 1].set(dt_flat)
        gnw = jnp.tile(gdn_norm_w[layer], nh)[None, :]

        anw = jnp.tile(gla_norm_w[layer], GLA_HEADS)[None, :]
        wg = jnp.zeros((2, LANES, GLA_QK), F32)
        for dd in range(2):
            r0 = _S_GLOW + dd * GLA_GATE_RANK
            wg = wg.at[dd, r0:r0 + GLA_GATE_RANK, :].set(gla_w_gk[layer, dd])
        gdn_f, gdn_b, gla_f, gla_b = _scans(
            nct, (gdn_in, small, abt, prow, pcol, exp_m),
            (gla_in, small, wg.astype(BF16), gla_b_gk[layer][:, None, :], eqk, sbd))

        mla_o = _attention(qh, kh, vh, mla_z, nct, with_ctx=not last)

        if last and len(xs) > 1:
            xs = (jnp.concatenate(xs, axis=1),)
        res = _outproj(gdn_f, gdn_b, gdn_in, gnw, gla_f, gla_b, gla_in, anw, ob64, mla_o, xs,
                       mod, w_out[layer].astype(BF16), nct, final_norm_w[None, :] if last else None)
        if last:
            out = res
        else:
            xs = (res,)
    return out
```

```python
import functools
import math

import numpy as np
import jax
import jax.numpy as jnp
from jax import lax
from jax.experimental import pallas as pl
from jax.experimental.pallas import tpu as pltpu

F32 = jnp.float32
BF16 = jnp.bfloat16
EPS = 1e-6

GDN_HEADS, GDN_DK, GDN_DV, GDN_CONV = 4, 64, 64, 5
GDN_QK = GDN_HEADS * GDN_DK
GDN_W = GDN_HEADS * GDN_DV
GDN_QKV = 2 * GDN_QK + GDN_W
MLA_HEADS, MLA_Q_RANK, MLA_KV_RANK = 8, 384, 256
MLA_NOPE, MLA_ROPE, MLA_DV = 64, 32, 64
MLA_W = MLA_HEADS * MLA_DV
MLA_SCALE = (MLA_NOPE + MLA_ROPE) ** -0.5
ROPE_THETA = 10000.0
ROPE_GRID_W = 64
GLA_HEADS, GLA_DK, GLA_DV = 4, 32, 64
GLA_QK = GLA_HEADS * GLA_DK
GLA_W = GLA_HEADS * GLA_DV
GLA_GATE_RANK = 16
GLA_GATE_NORM = 16.0
CHUNK = 64
LEVELS = (32, 16, 8, 4, 2, 1)

LANES = 128
MXU_N = 256
ROWS = 256
HALO = 8
ATTN_HEADS = 8
VMEM_LIMIT = 56 * 1024 * 1024

_O_GDN_QKV, _O_GDN_Z, _O_A, _O_B = 0, 768, 1024, 1032
_O_CQ, _O_CKV, _O_KR, _O_MLA_Z = 1040, 1424, 1680, 1712
_O_GLA_Q, _O_GLA_K, _O_GLA_V, _O_GLA_Z, _O_GLOW = 2224, 2352, 2480, 2736, 2992
_S_A, _S_B, _S_GLOW = 0, 8, 16
_W_GDN, _W_MLA, _W_MLAZ, _W_GLA = 1024, 640, 512, 768
_N_IN_PAD = _W_GDN + _W_MLA + _W_MLAZ + _W_GLA + 2 * LANES


def _dot(a, b):
    return jnp.dot(a, b, preferred_element_type=F32)


def _dot_nt(a, b):
    return lax.dot_general(a, b, (((1,), (1,)), ((), ())), preferred_element_type=F32)


def _dot_tn(a, b):
    return lax.dot_general(a, b, (((0,), (0,)), ((), ())), preferred_element_type=F32)


def _bdot(a, b):
    return _dot(a.astype(BF16), b.astype(BF16))


def _split2(x):
    x1 = x.astype(BF16)
    return x1, (x - x1.astype(F32)).astype(BF16)


def _split3(x):
    x1 = x.astype(BF16)
    r1 = x - x1.astype(F32)
    x2 = r1.astype(BF16)
    x3 = (r1 - x2.astype(F32)).astype(BF16)
    return x1, x2, x3


def _sel_dot(m01, x):
    n = x.shape[1]
    y = _dot(m01, jnp.concatenate(_split3(x), axis=1))
    return y[:, :n] + y[:, n:2 * n] + y[:, 2 * n:]


def _dot_sel(x, m01):
    n = x.shape[0]
    y = _dot(jnp.concatenate(_split3(x), axis=0), m01)
    return y[:n] + y[n:2 * n] + y[2 * n:]


def _dot_sel2(x, m01):
    n = x.shape[0]
    y = _dot(jnp.concatenate(_split2(x), axis=0), m01)
    return y[:n] + y[n:]


def _dot_sel_nt(x, m01):
    n = x.shape[0]
    y = _dot_nt(jnp.concatenate(_split3(x), axis=0), m01)
    return y[:n] + y[n:2 * n] + y[2 * n:]


def _softplus(x):
    return jnp.maximum(x, 0.0) + jnp.log1p(jnp.exp(-jnp.abs(x)))


def _silu(x):
    return x * jax.nn.sigmoid(x)


def _params(*sem):
    return pltpu.CompilerParams(dimension_semantics=sem, vmem_limit_bytes=VMEM_LIMIT)


def _scan_consts(rev):
    t = np.arange(ROWS)
    ch = t // CHUNK
    p = (CHUNK - 1 - t % CHUNK) if rev else (t % CHUNK)
    same = ch[:, None] == ch[None, :]
    tri = same & (p[None, :] <= p[:, None])
    mq, mk = [], []
    lv = np.full((ROWS, ROWS), -1.0, np.float32)
    for li, s in enumerate(LEVELS):
        blk = p // s
        mq.append(same & (p[None, :] > (blk * s)[:, None]) & (p[None, :] <= p[:, None]))
        mk.append(same & (p[None, :] > p[:, None]) & (p[None, :] <= ((blk + 1) * s)[:, None]))
        pair = same & ((blk % 2) == 1)[:, None] & (blk[None, :] == (blk - 1)[:, None])
        lv[pair] = li
    stack = np.concatenate([tri] + mq[:-1] + mk[:-1], axis=0)
    lvm = np.stack([sum((lv == li)[c * CHUNK:(c + 1) * CHUNK] for c in range(ROWS // CHUNK))
                    for li in range(len(LEVELS))])
    negm = np.where(tri, 0.0, -np.inf).astype(np.float32)
    return dict(tri=jnp.asarray(tri, BF16), ones=jnp.asarray(same, BF16), negm=jnp.asarray(negm),
                lv=jnp.asarray(lv), lvm=jnp.asarray(lvm, BF16), stack=jnp.asarray(stack, BF16))


def _head_block_ones(n, width):
    i = np.arange(n)
    return jnp.asarray((i[:, None] // width) == (i[None, :] // width), BF16)


def _inproj_perm():
    perm = np.full((_N_IN_PAD,), -1, np.int64)

    def put(src, n, at):
        perm[at:at + n] = np.arange(src, src + n)

    put(_O_GDN_QKV, GDN_QKV, 0)
    put(_O_GDN_Z, GDN_W, GDN_QKV)
    pos = _W_GDN
    put(_O_CQ, MLA_Q_RANK, pos)
    put(_O_CKV, MLA_KV_RANK, pos + MLA_Q_RANK)
    pos += _W_MLA
    put(_O_MLA_Z, MLA_W, pos)
    pos += _W_MLAZ
    put(_O_GLA_Q, GLA_QK, pos)
    put(_O_GLA_K, GLA_QK, pos + GLA_QK)
    put(_O_GLA_V, GLA_W, pos + 2 * GLA_QK)
    put(_O_GLA_Z, GLA_W, pos + 2 * GLA_QK + GLA_W)
    pos += _W_GLA
    put(_O_A, 2 * GDN_HEADS, pos + _S_A)
    put(_O_B, 2 * GDN_HEADS, pos + _S_B)
    put(_O_GLOW, 2 * GLA_GATE_RANK, pos + _S_GLOW)
    pos += LANES
    put(_O_KR, MLA_ROPE, pos + MLA_NOPE)
    return perm


def _take_cols(w, perm):
    cols = jnp.take(w, jnp.asarray(np.maximum(perm, 0)), axis=1)
    return jnp.where(jnp.asarray(perm >= 0)[None, :], cols, 0.0)


def _mla_q_perm():
    perm = np.full((MLA_HEADS * LANES,), -1, np.int64)
    d = MLA_NOPE + MLA_ROPE
    for h in range(MLA_HEADS):
        perm[h * LANES:h * LANES + d] = np.arange(h * d, (h + 1) * d)
    return perm


def _mla_kv_perm():
    dk = MLA_NOPE + MLA_DV
    kperm = np.full((MLA_HEADS * LANES,), -1, np.int64)
    vperm = np.zeros((MLA_W,), np.int64)
    for h in range(MLA_HEADS):
        kperm[h * LANES:h * LANES + MLA_NOPE] = np.arange(h * dk, h * dk + MLA_NOPE)
        vperm[h * MLA_DV:(h + 1) * MLA_DV] = np.arange(h * dk + MLA_NOPE, (h + 1) * dk)
    return kperm, vperm


def _rope_tables(n_ctx, n_lat):
    rows = n_lat // ROPE_GRID_W
    row = np.repeat(np.arange(rows, dtype=np.float32), ROPE_GRID_W)
    col = np.tile(np.arange(ROPE_GRID_W, dtype=np.float32), rows)
    n_freq = MLA_ROPE // 4
    inv = (ROPE_THETA ** (-np.arange(n_freq, dtype=np.float32) / n_freq)).astype(np.float32)
    ang = np.concatenate([row[:, None] * inv, col[:, None] * inv], axis=-1)
    cos = np.concatenate([np.ones((n_ctx, 2 * n_freq), np.float32), np.cos(ang)], 0)
    sin = np.concatenate([np.zeros((n_ctx, 2 * n_freq), np.float32), np.sin(ang)], 0)
    n = n_ctx + n_lat
    half = MLA_ROPE // 2
    tab = np.zeros((3, n, LANES), np.float32)
    tab[0, :, :MLA_NOPE] = 1.0
    tab[0, :, MLA_NOPE:MLA_NOPE + half] = cos
    tab[0, :, MLA_NOPE + half:MLA_NOPE + MLA_ROPE] = cos
    tab[1, :, MLA_NOPE:MLA_NOPE + half] = -sin
    tab[2, :, MLA_NOPE + half:MLA_NOPE + MLA_ROPE] = sin
    return jnp.asarray(tab)


def _expand_matrix(src0, n_src, width, n_out):
    m = np.zeros((LANES, n_out), np.float32)
    for h in range(n_src):
        m[src0 + h, h * width:(h + 1) * width] = 1.0
    return m


def _ada_kernel(c_ref, w_ref, b_ref, o_ref):
    c = _silu(c_ref[...]).astype(BF16)
    o_ref[0] = _dot(c, w_ref[0].astype(BF16)) + b_ref[0]


def _ada(cc, w_ada, b_ada):
    nl, d, n3 = w_ada.shape
    r = cc.shape[0]
    tn = 1024
    return pl.pallas_call(
        _ada_kernel,
        grid=(nl, n3 // tn),
        in_specs=[pl.BlockSpec((r, d), lambda l, j: (0, 0)),
                  pl.BlockSpec((1, d, tn), lambda l, j: (l, 0, j)),
                  pl.BlockSpec((1, 1, tn), lambda l, j: (l, 0, j))],
        out_specs=pl.BlockSpec((1, r, tn), lambda l, j: (l, 0, j)),
        out_shape=jax.ShapeDtypeStruct((nl, r, n3), F32),
        compiler_params=_params("arbitrary", "arbitrary"),
        name="ada_mod",
    )(cc, w_ada, b_ada.reshape(nl, 1, n3))


def _token_block(nct, refs):
    if len(refs) == 1:
        return refs[0][0]
    return jnp.where(pl.program_id(1) < nct, refs[0][0], refs[1][0])


def _token_specs(arrays, nct, d):
    if len(arrays) == 1:
        return [pl.BlockSpec((1, ROWS, d), lambda bi, j: (bi, j, 0))]
    return [pl.BlockSpec((1, ROWS, d), lambda bi, j: (bi, jnp.minimum(j, nct - 1), 0)),
            pl.BlockSpec((1, ROWS, d), lambda bi, j: (bi, jnp.maximum(j - nct, 0), 0))]


def _halo_specs(arrays, nct, d, after):
    hb = ROWS // HALO
    offs = (0,) if len(arrays) == 1 else (0, nct)

    def spec(a, off):
        last = a.shape[1] // HALO - 1

        def index(bi, j):
            h = (j - off + 1) * hb if after else (j - off) * hb - 1
            return bi, jnp.clip(h, 0, last), 0
        return pl.BlockSpec((1, HALO, d), index)

    return [spec(a, off) for a, off in zip(arrays, offs)]


def _rope(x, tab_ref):
    half = MLA_ROPE // 2
    return (x * tab_ref[0] + pltpu.roll(x, LANES - half, 1) * tab_ref[1]
            + pltpu.roll(x, half, 1) * tab_ref[2])


def _inproj_kernel(nct, nx, *refs):
    x_refs, p_refs, n_refs, refs = refs[:nx], refs[nx:2 * nx], refs[2 * nx:3 * nx], refs[3 * nx:]
    (mod_ref, w_ref, wabt_ref, tab_ref, qw_ref, kvw_ref, wuq_ref, wuk_ref, wuv_ref, convw_ref,
     ob_ref, ogdn, omlaz, ogla, osmall, oabt, q_ref, k_ref, v_ref, xe_scr) = refs
    j = pl.program_id(1)
    nt = pl.num_programs(1)

    def modulated(x):
        h = x * lax.rsqrt(jnp.mean(x * x, axis=-1, keepdims=True) + EPS)
        return (h * (1.0 + mod_ref[0, 1:2, :]) + mod_ref[0, 0:1, :]).astype(BF16)

    hb = modulated(_token_block(nct, x_refs))
    pos = 0
    proj = {}
    for name, ref, n in (("gdn", None, _W_GDN), ("mla", None, _W_MLA), ("mlaz", omlaz, _W_MLAZ),
                         ("gla", ogla, _W_GLA), ("small", osmall, LANES), ("kr", None, LANES)):
        y = _dot(hb, w_ref[:, pos:pos + n])
        if ref is None:
            proj[name] = y
        else:
            ref[0] = y
        pos += n
    oabt[0] = _dot_nt(wabt_ref[...], hb)

    halo = jnp.concatenate([_token_block(nct, p_refs), _token_block(nct, n_refs)], axis=0)
    yh = _dot(modulated(halo), w_ref[:, 0:GDN_QKV])
    has_prev = jnp.logical_and(j != 0, j != nct)
    has_next = jnp.logical_and(j != nct - 1, j != nt - 1)
    xe_scr[0:HALO, :] = jnp.where(has_prev, yh[:HALO], 0.0)
    xe_scr[HALO:HALO + ROWS, :] = proj["gdn"][:, :GDN_QKV]
    xe_scr[HALO + ROWS:, :] = jnp.where(has_next, yh[HALO:], 0.0)
    pad = (GDN_CONV - 1) // 2
    conv = jnp.zeros((ROWS, GDN_QKV), F32)
    for t in range(GDN_CONV):
        conv = conv + convw_ref[t:t + 1, :] * xe_scr[pl.ds(HALO - pad + t, ROWS), :]
    hqkv = _silu(conv)
    ob = ob_ref[...]
    qn = hqkv[:, :GDN_QK]
    kn = hqkv[:, GDN_QK:2 * GDN_QK]
    qn = qn * lax.rsqrt(_dot_sel2(qn * qn, ob) + EPS) * (GDN_DK ** -0.5)
    kn = kn * lax.rsqrt(_dot_sel2(kn * kn, ob) + EPS)
    ogdn[0] = jnp.concatenate([qn, kn, hqkv[:, 2 * GDN_QK:], proj["gdn"][:, GDN_QKV:]], axis=1)

    cq = proj["mla"][:, :MLA_Q_RANK]
    ckv = proj["mla"][:, MLA_Q_RANK:]
    cq = cq * lax.rsqrt(jnp.mean(cq * cq, axis=-1, keepdims=True) + EPS) * qw_ref[...]
    ckv = ckv * lax.rsqrt(jnp.mean(ckv * ckv, axis=-1, keepdims=True) + EPS) * kvw_ref[...]
    cqb = cq.astype(BF16)
    ckvb = ckv.astype(BF16)
    qf = _dot(cqb, wuq_ref[...])
    kf = _dot(ckvb, wuk_ref[...])
    vf = _dot(ckvb, wuv_ref[...])
    kr = _rope(proj["kr"], tab_ref)
    q_scale = MLA_SCALE * math.log2(math.e)
    for hd in range(MLA_HEADS):
        sl = slice(hd * LANES, (hd + 1) * LANES)
        q_ref[0, hd] = (_rope(qf[:, sl], tab_ref) * q_scale).astype(BF16)
        k_ref[0, hd] = (kf[:, sl] + kr).astype(BF16)
    ones = jnp.ones((ROWS, MXU_N - LANES), BF16)
    for hp in range(MLA_HEADS // 2):
        v_ref[0, hp] = jnp.concatenate(
            [vf[:, hp * LANES:(hp + 1) * LANES].astype(BF16), ones], axis=1)


def _inproj(xs, mod, w_p, wabt, tab, qw, kvw, wuq, wuk, wuv, convw, ob, nct):
    b, d = xs[0].shape[0], xs[0].shape[2]
    ta = sum(a.shape[1] for a in xs)
    nt = ta // ROWS
    widths = (_W_GDN, _W_MLAZ, _W_GLA, LANES)
    row = lambda bi, j: (bi, j, 0)
    full2 = lambda bi, j: (0, 0)
    hm = lambda bi, j: (bi, 0, j, 0)
    consts = (w_p, wabt)
    mla_consts = (qw, kvw, wuq, wuk, wuv, convw, ob)
    return pl.pallas_call(
        functools.partial(_inproj_kernel, nct, len(xs)),
        grid=(b, nt),
        in_specs=_token_specs(xs, nct, d) + _halo_specs(xs, nct, d, False)
        + _halo_specs(xs, nct, d, True)
        + [pl.BlockSpec((1, 3, d), lambda bi, j: (jnp.where(j < nct, 0, 1 + bi), 0, 0))]
        + [pl.BlockSpec(a.shape, full2) for a in consts]
        + [pl.BlockSpec((3, ROWS, LANES), lambda bi, j: (0, j, 0))]
        + [pl.BlockSpec(a.shape, full2) for a in mla_consts],
        out_specs=[pl.BlockSpec((1, ROWS, n), row) for n in widths]
        + [pl.BlockSpec((1, 4 * GDN_HEADS, ROWS), lambda bi, j: (bi, 0, j)),
           pl.BlockSpec((1, MLA_HEADS, ROWS, LANES), hm),
           pl.BlockSpec((1, MLA_HEADS, ROWS, LANES), hm),
           pl.BlockSpec((1, MLA_HEADS // 2, ROWS, MXU_N), hm)],
        out_shape=[jax.ShapeDtypeStruct((b, ta, n), F32) for n in widths]
        + [jax.ShapeDtypeStruct((b, 4 * GDN_HEADS, ta), F32),
           jax.ShapeDtypeStruct((b, MLA_HEADS, ta, LANES), BF16),
           jax.ShapeDtypeStruct((b, MLA_HEADS, ta, LANES), BF16),
           jax.ShapeDtypeStruct((b, MLA_HEADS // 2, ta, MXU_N), BF16)],
        scratch_shapes=[pltpu.VMEM((ROWS + 2 * HALO, GDN_QKV), F32)],
        compiler_params=_params("arbitrary", "arbitrary"),
        name="in_proj",
    )(*xs, *xs, *xs, mod, *consts, tab, *mla_consts)


def _scan_block_index(j, nct, nt, rev):
    if not rev:
        return j
    return jnp.where(j < nct, nct - 1 - j, nt - 1 - (j - nct))


def _head_lane_mask(n, width, h):
    lane = lax.broadcasted_iota(jnp.int32, (1, n), 1)
    return (lane >= h * width) & (lane < (h + 1) * width)


def _gdn_prep(d, x_ref, small_ref, abt_ref, prow_ref, pcol_ref, tri, ones, exp_ref):
    q = x_ref[0, :, :GDN_QK]
    k = x_ref[0, :, GDN_QK:2 * GDN_QK]
    v = x_ref[0, :, 2 * GDN_QK:GDN_QKV]
    sm = small_ref[0]
    g_all = -jnp.exp(prow_ref[0:1, :]) * _softplus(sm + prow_ref[1:2, :])
    beta_all = jax.nn.sigmoid(sm)
    gc_all = _sel_dot(tri, g_all)
    gl_all = _sel_dot(ones, g_all)
    g_t = -jnp.exp(pcol_ref[:, 0:1]) * _softplus(abt_ref[0] + pcol_ref[:, 1:2])
    gc_t = _dot_sel_nt(g_t, tri)
    gc_w = _dot_sel2(gc_all, exp_ref[d])
    gl_w = _dot_sel2(gl_all, exp_ref[d])
    beta_w = _dot_sel2(beta_all, exp_ref[2 + d])
    kb = k * beta_w
    return dict(q=q, kbf=k.astype(BF16), kb=kb, vb=v * beta_w, kbg=kb * jnp.exp(gc_w),
                qg=q * jnp.exp(gc_w), kdec=k * jnp.exp(gl_w - gc_w),
                gc_all=gc_all, gc_t=gc_t, gl_all=gl_all)


def _gdn_stages(xf_ref, smf_ref, abtf_ref, xb_ref, smb_ref, abtb_ref, prow_ref, pcol_ref, tri_ref,
                ones_ref, negm_ref, lvm_ref, exp_ref, of_ref, obk_ref, s_scr):
    nh, dk = GDN_HEADS, GDN_DK
    nchunk = ROWS // CHUNK
    nlev = len(LEVELS)
    ones = ones_ref[...]
    dir_refs = ((xf_ref, smf_ref, abtf_ref), (xb_ref, smb_ref, abtb_ref))
    prep = []
    for d in range(2):
        prep.append(_gdn_prep(d, *dir_refs[d], prow_ref, pcol_ref, tri_ref[d], ones, exp_ref))
        yield
    ri = lax.broadcasted_iota(jnp.int32, (ROWS, ROWS), 0)
    ci = lax.broadcasted_iota(jnp.int32, (ROWS, ROWS), 1)
    eye = (ri == ci).astype(BF16)
    chains = [(d, h) for d in range(2) for h in range(nh)]
    nc = len(chains)

    low, a_intra = [], []
    for d, h in chains:
        p = prep[d]
        idx = d * nh + h
        hm = _head_lane_mask(GDN_QK, dk, h)
        decay = jnp.exp(p["gc_all"][:, idx:idx + 1] - p["gc_t"][idx:idx + 1, :] + negm_ref[d])
        both = jnp.concatenate([jnp.where(hm, p["kb"], 0.0), jnp.where(hm, p["q"], 0.0)], axis=0)
        kq = _dot_nt(both.astype(BF16), p["kbf"])
        low.append((kq[:ROWS] * decay).astype(BF16))
        a_intra.append((kq[ROWS:] * decay).astype(BF16))
        if h == nh - 1:
            yield

    def compact(m):
        return functools.reduce(lambda a, b: a + b,
                                [m[c * CHUNK:(c + 1) * CHUNK] for c in range(nchunk)])

    def expand(mc):
        return jnp.concatenate([mc] * nchunk, axis=0) * ones

    low_c = [compact(m) for m in low]
    eye_c = compact(eye)
    t_c = [eye_c - low_c[i] * lvm_ref[d, nlev - 1] for i, (d, h) in enumerate(chains)]
    t_inv = [expand(t) for t in t_c]
    for li in reversed(range(nlev - 1)):
        ys = [expand(_dot(low_c[i] * lvm_ref[d, li], t_inv[i]).astype(BF16))
              for i, (d, h) in enumerate(chains)]
        t_c = [t_c[i] - _dot(t_c[i], ys[i]).astype(BF16) for i in range(nc)]
        t_inv = [expand(t) for t in t_c]
        yield
    uw = []
    for i, (d, h) in enumerate(chains):
        sl = slice(h * dk, (h + 1) * dk)
        rhs = jnp.concatenate([prep[d]["vb"][:, sl], prep[d]["kbg"][:, sl]], axis=1)
        uw.append(_dot(t_inv[i], rhs.astype(BF16)))

    q2, ou, gb = [], [], []
    for i, (d, h) in enumerate(chains):
        sl = slice(h * dk, (h + 1) * dk)
        uwb = uw[i].astype(BF16)
        auw = _dot(a_intra[i], uwb)
        ou.append(auw[:, :GDN_DV])
        q2.append((prep[d]["qg"][:, sl] - auw[:, GDN_DV:]).astype(BF16))
        kd = prep[d]["kdec"][:, sl].astype(BF16)
        gb.append([_dot_tn(kd[c * CHUNK:(c + 1) * CHUNK], uwb[c * CHUNK:(c + 1) * CHUNK])
                   for c in range(nchunk)])
    yield
    states = [s_scr[i] for i in range(nc)]
    o_parts = [[None] * nchunk for _ in chains]
    for step in range(nchunk):
        for i, (d, h) in enumerate(chains):
            c = nchunk - 1 - step if d == 1 else step
            idx = d * nh + h
            r0 = c * CHUNK
            rs = slice(r0, r0 + CHUNK)
            sb = states[i].astype(BF16)
            o_parts[i][c] = _dot(q2[i][rs], sb) + ou[i][rs]
            states[i] = (states[i] * jnp.exp(prep[d]["gl_all"][r0:r0 + 1, idx:idx + 1])
                         - _dot(gb[i][c][:, GDN_DV:].astype(BF16), sb) + gb[i][c][:, :GDN_DV])
        yield
    for i in range(len(chains)):
        s_scr[i] = states[i]
    outs = [jnp.concatenate(o_parts[i], axis=0) for i in range(len(chains))]
    of_ref[0] = jnp.concatenate(outs[:nh], axis=1)
    obk_ref[0] = jnp.concatenate(outs[nh:], axis=1)


def _scan_row_specs(nct, nt, rev, width):
    blk_of = functools.partial(_scan_block_index, nct=nct, nt=nt, rev=rev)
    return pl.BlockSpec((1, ROWS, width), lambda bi, j: (bi, blk_of(j), 0))


def _gdn_specs(nct, nt, gdn_in, small, abt, prow, pcol, exp_m):
    cf, cb = _scan_consts(False), _scan_consts(True)
    stack = lambda name: jnp.stack([cf[name], cb[name]])
    full2 = lambda bi, j: (0, 0)
    full3 = lambda bi, j: (0, 0, 0)

    def dir_specs(rev):
        blk_of = functools.partial(_scan_block_index, nct=nct, nt=nt, rev=rev)
        return [
            _scan_row_specs(nct, nt, rev, GDN_QKV),
            _scan_row_specs(nct, nt, rev, LANES),
            pl.BlockSpec((1, 4 * GDN_HEADS, ROWS), lambda bi, j: (bi, 0, blk_of(j))),
        ]

    in_specs = dir_specs(False) + dir_specs(True) + [
        pl.BlockSpec(prow.shape, full2),
        pl.BlockSpec(pcol.shape, full2),
        pl.BlockSpec((2, ROWS, ROWS), full3),
        pl.BlockSpec((ROWS, ROWS), full2),
        pl.BlockSpec((2, ROWS, ROWS), full3),
        pl.BlockSpec((2, len(LEVELS), CHUNK, ROWS), lambda bi, j: (0, 0, 0, 0)),
        pl.BlockSpec(exp_m.shape, full3),
    ]
    dir_args = [gdn_in, small, abt]
    args = dir_args + dir_args + [prow, pcol, stack("tri"), cf["ones"], stack("negm"),
                                  stack("lvm"), exp_m]
    out_specs = [_scan_row_specs(nct, nt, False, GDN_W), _scan_row_specs(nct, nt, True, GDN_W)]
    scratch = [pltpu.VMEM((2 * GDN_HEADS, GDN_DK, GDN_DV), F32)]
    return in_specs, args, out_specs, scratch


def _gla_stages(xf_ref, smf_ref, xb_ref, smb_ref, wg_ref, bg_ref, stack_ref, lv_ref, eqk_ref,
                sbd_ref, of_ref, obk_ref, s_scr):
    nh, dk, dv = GLA_HEADS, GLA_DK, GLA_DV
    nchunk = ROWS // CHUNK
    nlev = len(LEVELS)
    n = GLA_QK
    lane_head = [_head_lane_mask(GLA_QK, dk, h) for h in range(nh)]
    out_head = [_head_lane_mask(GLA_W, dv, h) for h in range(nh)]
    eqk = eqk_ref[...]
    sbd = sbd_ref[...]
    x_refs, sm_refs, o_refs = (xf_ref, xb_ref), (smf_ref, smb_ref), (of_ref, obk_ref)
    dirs = range(2)
    xs = [x_refs[d][0] for d in dirs]
    q = [xs[d][:, :GLA_QK] * (dk ** -0.5) for d in dirs]
    k = [xs[d][:, GLA_QK:2 * GLA_QK] for d in dirs]
    v = [xs[d][:, 2 * GLA_QK:2 * GLA_QK + GLA_W] for d in dirs]
    vb = [v[d].astype(BF16) for d in dirs]
    gk = [_dot(sm_refs[d][0].astype(BF16), wg_ref[d]) + bg_ref[d] for d in dirs]
    la = [-_softplus(-gk[d]) * (1.0 / GLA_GATE_NORM) for d in dirs]
    la2 = [jnp.concatenate(_split2(la[d]), axis=1) for d in dirs]
    ys = [_dot(stack_ref[d], la2[d]) for d in dirs]
    cums = [ys[d][:, :n] + ys[d][:, n:] for d in dirs]
    piece = lambda d, i: cums[d][i * ROWS:(i + 1) * ROWS]
    bcum = [piece(d, 0) for d in dirs]
    last_row = [0 if d == 1 else CHUNK - 1 for d in dirs]
    blast = [jnp.concatenate(
        [jnp.broadcast_to(bcum[d][c * CHUNK + last_row[d]:c * CHUNK + last_row[d] + 1, :], (CHUNK, n))
         for c in range(nchunk)], axis=0) for d in dirs]
    la_next = [pltpu.roll(la[d], 1 if d == 1 else ROWS - 1, 0) for d in dirs]
    qg = [(q[d] * jnp.exp(bcum[d])).astype(BF16) for d in dirs]
    kdec = [(k[d] * jnp.exp(blast[d] - bcum[d])).astype(BF16) for d in dirs]
    yield

    lv4 = [jnp.concatenate([lv_ref[d]] * nh, axis=0) for d in dirs]
    acc = [jnp.zeros((nh * ROWS, ROWS), F32) for d in dirs]
    for li in range(nlev):
        if li < nlev - 1:
            ql = [q[d] * jnp.exp(piece(d, 1 + li)) for d in dirs]
            kl = [(k[d] * jnp.exp(piece(d, nlev + li))).astype(BF16) for d in dirs]
        else:
            ql = q
            kl = [(k[d] * jnp.exp(la_next[d])).astype(BF16) for d in dirs]
        qs = [jnp.concatenate([jnp.where(lane_head[h], ql[d], 0.0) for h in range(nh)],
                              axis=0).astype(BF16) for d in dirs]
        ps = [_dot_nt(qs[d], kl[d]) for d in dirs]
        acc = [jnp.where(lv4[d] == float(li), ps[d], acc[d]) for d in dirs]
        yield
    pv = [_dot(acc[d].astype(BF16), vb[d]) for d in dirs]
    o = [_dot_sel(q[d] * k[d], eqk) * v[d] for d in dirs]
    for h in range(nh):
        o = [o[d] + jnp.where(out_head[h], pv[d][h * ROWS:(h + 1) * ROWS], 0.0) for d in dirs]
    yield

    states = [s_scr[d] for d in dirs]
    o_parts = [[None] * nchunk for d in dirs]
    for step in range(nchunk):
        for d in dirs:
            c = nchunk - 1 - step if d == 1 else step
            rs = slice(c * CHUNK, (c + 1) * CHUNK)
            o_parts[d][c] = _dot_nt(qg[d][rs], states[d].astype(BF16))
            states[d] = (states[d] * jnp.exp(blast[d][c * CHUNK:c * CHUNK + 1, :])
                         + sbd * _dot_tn(vb[d][rs], kdec[d][rs]))
        yield
    for d in dirs:
        s_scr[d] = states[d]
        o_refs[d][0] = o[d] + jnp.concatenate(o_parts[d], axis=0)


def _gla_specs(nct, nt, gla_in, small, wg, bg, eqk, sbd):
    cf, cb = _scan_consts(False), _scan_consts(True)
    stack = jnp.stack([cf["stack"], cb["stack"]])
    lv = jnp.stack([cf["lv"], cb["lv"]])
    full2 = lambda bi, j: (0, 0)
    full3 = lambda bi, j: (0, 0, 0)
    in_specs = [
        _scan_row_specs(nct, nt, False, _W_GLA), _scan_row_specs(nct, nt, False, LANES),
        _scan_row_specs(nct, nt, True, _W_GLA), _scan_row_specs(nct, nt, True, LANES),
        pl.BlockSpec(wg.shape, full3),
        pl.BlockSpec(bg.shape, full3),
        pl.BlockSpec(stack.shape, full3),
        pl.BlockSpec(lv.shape, full3),
        pl.BlockSpec(eqk.shape, full2),
        pl.BlockSpec(sbd.shape, full2),
    ]
    args = [gla_in, small, gla_in, small, wg, bg, stack, lv, eqk, sbd]
    out_specs = [_scan_row_specs(nct, nt, False, GLA_W), _scan_row_specs(nct, nt, True, GLA_W)]
    scratch = [pltpu.VMEM((2, GLA_W, GLA_QK), F32)]
    return in_specs, args, out_specs, scratch


_N_GDN_IN, _N_GLA_IN = 13, 10


def _scan_kernel(*refs):
    gdn_in, refs = refs[:_N_GDN_IN], refs[_N_GDN_IN:]
    gla_in, refs = refs[:_N_GLA_IN], refs[_N_GLA_IN:]
    gdn_out, gla_out, (gdn_state, gla_state) = refs[:2], refs[2:4], refs[4:]

    @pl.when(pl.program_id(1) == 0)
    def _():
        gdn_state[...] = jnp.zeros_like(gdn_state)
        gla_state[...] = jnp.zeros_like(gla_state)

    stages = [_gdn_stages(*gdn_in, *gdn_out, gdn_state),
              _gla_stages(*gla_in, *gla_out, gla_state)]
    while stages:
        for s in list(stages):
            if next(s, StopIteration) is StopIteration:
                stages.remove(s)


def _scans(nct, gdn_args, gla_args):
    gdn_in = gdn_args[0]
    b, ta, _ = gdn_in.shape
    nt = ta // ROWS
    g_in, g_args, g_out, g_scr = _gdn_specs(nct, nt, *gdn_args)
    a_in, a_args, a_out, a_scr = _gla_specs(nct, nt, *gla_args)
    assert len(g_in) == _N_GDN_IN and len(a_in) == _N_GLA_IN
    return pl.pallas_call(
        _scan_kernel,
        grid=(b, nt),
        in_specs=g_in + a_in,
        out_specs=g_out + a_out,
        out_shape=[jax.ShapeDtypeStruct((b, ta, GDN_W), F32)] * 2
        + [jax.ShapeDtypeStruct((b, ta, GLA_W), F32)] * 2,
        scratch_shapes=g_scr + a_scr,
        compiler_params=_params("arbitrary", "arbitrary"),
        name="gdn_gla_scan",
    )(*g_args, *a_args)


def _attn_kernel(nct, off, n_ctx, q_ref, k_ref, v_ref, z_ref, o_ref):
    i = pl.program_id(2) + off

    def body(nk):
        parts = []
        half = ROWS // 2

        def by_row_halves(dot, lhs, rhs):
            return jnp.concatenate([dot(lhs[:half], rhs), dot(lhs[half:], rhs)], axis=0)

        for hh in range(ATTN_HEADS):
            q = q_ref[0, hh]
            kk = k_ref[0, hh, :nk, :]
            s = by_row_halves(_dot_nt, q, kk) if hh == 0 else _dot_nt(q, kk)
            m = jnp.max(s, axis=-1, keepdims=True)
            p = jnp.exp2(s - m).astype(BF16)
            vv = v_ref[0, hh // 2, :nk, :]
            pv = by_row_halves(_dot, p, vv) if hh == ATTN_HEADS - 1 else _dot(p, vv)
            o = pv[:, (hh % 2) * MLA_DV:(hh % 2 + 1) * MLA_DV] / pv[:, LANES:LANES + 1]
            parts.append(o)
        o_ref[0] = (jnp.concatenate(parts, axis=1) * _silu(z_ref[0])).astype(BF16)

    if off == 0:
        @pl.when(i < nct)
        def _():
            body(n_ctx)

        @pl.when(i >= nct)
        def _():
            body(k_ref.shape[2])
    else:
        body(k_ref.shape[2])


def _attention(q, k, v, z, nct, with_ctx):
    b, nh, ta, _ = q.shape
    nt = ta // ROWS
    off = 0 if with_ctx else nct
    nq = nt - off
    hg = ATTN_HEADS
    wo = hg * MLA_DV
    return pl.pallas_call(
        functools.partial(_attn_kernel, nct, off, nct * ROWS),
        grid=(b, nh // hg, nq),
        in_specs=[pl.BlockSpec((1, hg, ROWS, LANES), lambda bi, hp, i: (bi, hp, i + off, 0)),
                  pl.BlockSpec((1, hg, ta, LANES), lambda bi, hp, i: (bi, hp, 0, 0)),
                  pl.BlockSpec((1, hg // 2, ta, MXU_N), lambda bi, hp, i: (bi, hp, 0, 0)),
                  pl.BlockSpec((1, ROWS, wo), lambda bi, hp, i: (bi, i + off, hp))],
        out_specs=pl.BlockSpec((1, ROWS, wo), lambda bi, hp, i: (bi, i, hp)),
        out_shape=jax.ShapeDtypeStruct((b, nq * ROWS, MLA_W), BF16),
        compiler_params=_params("arbitrary", "arbitrary", "arbitrary"),
        name="mla_attn",
    )(q, k, v, z)


def _outproj_kernel(final, nct, nx, gf_ref, gb_ref, gz_ref, gnw_ref, af_ref, ab_ref, az_ref, anw_ref,
                    ob_ref, m_ref, mod_ref, w_ref, *rest):
    x_refs, rest = rest[:nx], rest[nx:]
    if final:
        fw_ref, o_ref = rest
    else:
        (o_ref,) = rest
    ob = ob_ref[...]

    def gated_head_norm(o, nw_ref, z):
        ms = _dot_sel2(o * o, ob) * (1.0 / GDN_DV)
        return (o * lax.rsqrt(ms + EPS) * nw_ref[...] * _silu(z)).astype(BF16)

    g = gated_head_norm(gf_ref[0] + gb_ref[0], gnw_ref, gz_ref[0])
    a = gated_head_norm(af_ref[0] + ab_ref[0], anw_ref, az_ref[0])
    y = _dot(g, w_ref[0:GDN_W, :])
    y = y + _dot(m_ref[0], w_ref[GDN_W:GDN_W + MLA_W, :])
    y = y + _dot(a, w_ref[GDN_W + MLA_W:, :])
    xn = _token_block(nct, x_refs) + mod_ref[0, 2:3, :] * y
    if final:
        xn = xn * lax.rsqrt(jnp.mean(xn * xn, axis=-1, keepdims=True) + EPS) * fw_ref[...]
    o_ref[0] = xn


def _outproj(gdn_f, gdn_b, gdn_in, gnw, gla_f, gla_b, gla_in, anw, ob, mla_o, xs, mod, w_out,
             nct, final_w):
    assert GDN_DV == GLA_DV and GDN_W == GLA_W
    b, d = xs[0].shape[0], xs[0].shape[2]
    ta = sum(a.shape[1] for a in xs)
    nt = ta // ROWS
    final = final_w is not None
    assert not (final and len(xs) > 1)
    off = nct if final else 0
    moff = off - (ta - mla_o.shape[1]) // ROWS
    nq = nt - off
    row = lambda bi, j: (bi, j + off, 0)
    full2 = lambda bi, j: (0, 0)
    x_specs = [pl.BlockSpec((1, ROWS, d), row)] if len(xs) == 1 else _token_specs(xs, nct, d)
    wide = pl.BlockSpec((1, ROWS, GDN_W), row)
    gz_blk = GDN_QKV // GDN_W
    az_blk = (2 * GLA_QK + GLA_W) // GLA_W
    in_specs = [wide, wide,
                pl.BlockSpec((1, ROWS, GDN_W), lambda bi, j: (bi, j + off, gz_blk)),
                pl.BlockSpec(gnw.shape, full2),
                wide, wide,
                pl.BlockSpec((1, ROWS, GLA_W), lambda bi, j: (bi, j + off, az_blk)),
                pl.BlockSpec(anw.shape, full2),
                pl.BlockSpec(ob.shape, full2),
                pl.BlockSpec((1, ROWS, MLA_W), lambda bi, j: (bi, j + moff, 0)),
                pl.BlockSpec((1, 3, d), lambda bi, j: (jnp.where(j + off < nct, 0, 1 + bi), 0, 0)),
                pl.BlockSpec(w_out.shape, full2)] + x_specs
    args = [gdn_f, gdn_b, gdn_in, gnw, gla_f, gla_b, gla_in, anw, ob, mla_o, mod, w_out, *xs]
    if final:
        in_specs.append(pl.BlockSpec(final_w.shape, full2))
        args.append(final_w)
    return pl.pallas_call(
        functools.partial(_outproj_kernel, final, nct, len(xs)),
        grid=(b, nq),
        in_specs=in_specs,
        out_specs=pl.BlockSpec((1, ROWS, d), lambda bi, j: (bi, j, 0)),
        out_shape=jax.ShapeDtypeStruct((b, nq * ROWS, d), F32),
        compiler_params=_params("arbitrary", "arbitrary"),
        name="out_proj",
    )(*args)


def kernel(x, c, ctx, c_ctx, w_ada, b_ada, w_in, gdn_conv_w, gdn_a_log, gdn_dt_bias, gdn_norm_w,
           mla_q_norm_w, mla_w_uq, mla_kv_norm_w, mla_w_ukv, gla_w_gk, gla_b_gk, gla_norm_w,
           w_out, final_norm_w):
    b, seq, d = x.shape
    n_ctx = ctx.shape[1]
    depth = w_in.shape[0]
    assert n_ctx % ROWS == 0 and seq % ROWS == 0 and seq % ROPE_GRID_W == 0
    nct = n_ctx // ROWS
    nh = GDN_HEADS

    xs = (ctx, x)
    pad_rows = (-(1 + b)) % 8
    cc = jnp.concatenate([c_ctx[None, :], c, jnp.zeros((pad_rows, d), F32)], axis=0)
    mod_all = _ada(cc, w_ada, b_ada).reshape(depth, cc.shape[0], 3, d)

    perm = _inproj_perm()
    qperm = _mla_q_perm()
    kperm, vperm = _mla_kv_perm()
    tab = _rope_tables(n_ctx, seq)
    ob64 = _head_block_ones(GDN_W, GDN_DV)
    exp_m = jnp.asarray(np.stack(
        [_expand_matrix(_S_A + dd * nh, nh, GDN_DK, GDN_QK) for dd in range(2)]
        + [_expand_matrix(_S_B + dd * nh, nh, GDN_DK, GDN_QK) for dd in range(2)]), BF16)
    eqk_np = np.zeros((GLA_QK, GLA_W), np.float32)
    sbd_np = np.zeros((GLA_W, GLA_QK), np.float32)
    for h in range(GLA_HEADS):
        eqk_np[h * GLA_DK:(h + 1) * GLA_DK, h * GLA_DV:(h + 1) * GLA_DV] = 1.0
        sbd_np[h * GLA_DV:(h + 1) * GLA_DV, h * GLA_DK:(h + 1) * GLA_DK] = 1.0
    eqk = jnp.asarray(eqk_np, BF16)
    sbd = jnp.asarray(sbd_np)

    out = None
    for layer in range(depth):
        last = layer == depth - 1
        w_p = _take_cols(w_in[layer], perm).astype(BF16)
        wabt = w_in[layer][:, _O_A:_O_A + 4 * nh].T.astype(BF16)
        mod = mod_all[layer]
        wuq = _take_cols(mla_w_uq[layer], qperm).astype(BF16)
        wuk = _take_cols(mla_w_ukv[layer], kperm).astype(BF16)
        wuv = jnp.take(mla_w_ukv[layer], jnp.asarray(vperm), axis=1).astype(BF16)
        convw = jnp.concatenate(
            [gdn_conv_w[layer], jnp.zeros((8 - GDN_CONV, GDN_QKV), F32)], axis=0)
        gdn_in, mla_z, gla_in, small, abt, qh, kh, vh = _inproj(
            xs, mod, w_p, wabt, tab, mla_q_norm_w[layer][None, :], mla_kv_norm_w[layer][None, :],
            wuq, wuk, wuv, convw, ob64, nct)

        a_flat = gdn_a_log[layer].reshape(-1)
        dt_flat = gdn_dt_bias[layer].reshape(-1)
        prow = jnp.zeros((8, LANES), F32)
        prow = prow.at[0, _S_A:_S_A + 2 * nh].set(a_flat).at[1, _S_A:_S_A + 2 * nh].set(dt_flat)
        pcol = jnp.zeros((4 * nh, LANES), F32)
        pcol = pcol.at[0:2 * nh, 0].set(a_flat).at[0:2 * nh, 1].set(dt_flat)
        gnw = jnp.tile(gdn_norm_w[layer], nh)[None, :]

        anw = jnp.tile(gla_norm_w[layer], GLA_HEADS)[None, :]
        wg = jnp.zeros((2, LANES, GLA_QK), F32)
        for dd in range(2):
            r0 = _S_GLOW + dd * GLA_GATE_RANK
            wg = wg.at[dd, r0:r0 + GLA_GATE_RANK, :].set(gla_w_gk[layer, dd])
        gdn_f, gdn_b, gla_f, gla_b = _scans(
            nct, (gdn_in, small, abt, prow, pcol, exp_m),
            (gla_in, small, wg.astype(BF16), gla_b_gk[layer][:, None, :], eqk, sbd))

        mla_o = _attention(qh, kh, vh, mla_z, nct, with_ctx=not last)

        if last and len(xs) > 1:
            xs = (jnp.concatenate(xs, axis=1),)
        res = _outproj(gdn_f, gdn_b, gdn_in, gnw, gla_f, gla_b, gla_in, anw, ob64, mla_o, xs,
                       mod, w_out[layer].astype(BF16), nct, final_norm_w[None, :] if last else None)
        if last:
            out = res
        else:
            xs = (res,)
    return out
```

```python
import functools
import math

import numpy as np
import jax
import jax.numpy as jnp
from jax import lax
from jax.experimental import pallas as pl
from jax.experimental.pallas import tpu as pltpu

F32 = jnp.float32
BF16 = jnp.bfloat16
EPS = 1e-6

GDN_HEADS, GDN_DK, GDN_DV, GDN_CONV = 4, 64, 64, 5
GDN_QK = GDN_HEADS * GDN_DK
GDN_W = GDN_HEADS * GDN_DV
GDN_QKV = 2 * GDN_QK + GDN_W
MLA_HEADS, MLA_Q_RANK, MLA_KV_RANK = 8, 384, 256
MLA_NOPE, MLA_ROPE, MLA_DV = 64, 32, 64
MLA_W = MLA_HEADS * MLA_DV
MLA_SCALE = (MLA_NOPE + MLA_ROPE) ** -0.5
ROPE_THETA = 10000.0
ROPE_GRID_W = 64
GLA_HEADS, GLA_DK, GLA_DV = 4, 32, 64
GLA_QK = GLA_HEADS * GLA_DK
GLA_W = GLA_HEADS * GLA_DV
GLA_GATE_RANK = 16
GLA_GATE_NORM = 16.0
CHUNK = 64
LEVELS = (32, 16, 8, 4, 2, 1)

LANES = 128
MXU_N = 256
ROWS = 256
HALO = 8
ATTN_HEADS = 8
VMEM_LIMIT = 56 * 1024 * 1024

_O_GDN_QKV, _O_GDN_Z, _O_A, _O_B = 0, 768, 1024, 1032
_O_CQ, _O_CKV, _O_KR, _O_MLA_Z = 1040, 1424, 1680, 1712
_O_GLA_Q, _O_GLA_K, _O_GLA_V, _O_GLA_Z, _O_GLOW = 2224, 2352, 2480, 2736, 2992
_S_A, _S_B, _S_GLOW = 0, 8, 16
_W_GDN, _W_MLA, _W_MLAZ, _W_GLA = 1024, 640, 512, 768
_N_IN_PAD = _W_GDN + _W_MLA + _W_MLAZ + _W_GLA + 2 * LANES


def _dot(a, b):
    return jnp.dot(a, b, preferred_element_type=F32)


def _dot_nt(a, b):
    return lax.dot_general(a, b, (((1,), (1,)), ((), ())), preferred_element_type=F32)


def _dot_tn(a, b):
    return lax.dot_general(a, b, (((0,), (0,)), ((), ())), preferred_element_type=F32)


def _bdot(a, b):
    return _dot(a.astype(BF16), b.astype(BF16))


def _split2(x):
    x1 = x.astype(BF16)
    return x1, (x - x1.astype(F32)).astype(BF16)


def _split3(x):
    x1 = x.astype(BF16)
    r1 = x - x1.astype(F32)
    x2 = r1.astype(BF16)
    x3 = (r1 - x2.astype(F32)).astype(BF16)
    return x1, x2, x3


def _sel_dot(m01, x):
    n = x.shape[1]
    y = _dot(m01, jnp.concatenate(_split3(x), axis=1))
    return y[:, :n] + y[:, n:2 * n] + y[:, 2 * n:]


def _dot_sel(x, m01):
    n = x.shape[0]
    y = _dot(jnp.concatenate(_split3(x), axis=0), m01)
    return y[:n] + y[n:2 * n] + y[2 * n:]


def _dot_sel2(x, m01):
    n = x.shape[0]
    y = _dot(jnp.concatenate(_split2(x), axis=0), m01)
    return y[:n] + y[n:]


def _dot_sel_nt(x, m01):
    n = x.shape[0]
    y = _dot_nt(jnp.concatenate(_split3(x), axis=0), m01)
    return y[:n] + y[n:2 * n] + y[2 * n:]


def _softplus(x):
    return jnp.maximum(x, 0.0) + jnp.log1p(jnp.exp(-jnp.abs(x)))


def _silu(x):
    return x * jax.nn.sigmoid(x)


def _params(*sem):
    return pltpu.CompilerParams(dimension_semantics=sem, vmem_limit_bytes=VMEM_LIMIT)


def _scan_consts(rev):
    t = np.arange(ROWS)
    ch = t // CHUNK
    p = (CHUNK - 1 - t % CHUNK) if rev else (t % CHUNK)
    same = ch[:, None] == ch[None, :]
    tri = same & (p[None, :] <= p[:, None])
    mq, mk = [], []
    lv = np.full((ROWS, ROWS), -1.0, np.float32)
    for li, s in enumerate(LEVELS):
        blk = p // s
        mq.append(same & (p[None, :] > (blk * s)[:, None]) & (p[None, :] <= p[:, None]))
        mk.append(same & (p[None, :] > p[:, None]) & (p[None, :] <= ((blk + 1) * s)[:, None]))
        pair = same & ((blk % 2) == 1)[:, None] & (blk[None, :] == (blk - 1)[:, None])
        lv[pair] = li
    stack = np.concatenate([tri] + mq[:-1] + mk[:-1], axis=0)
    lvm = np.stack([sum((lv == li)[c * CHUNK:(c + 1) * CHUNK] for c in range(ROWS // CHUNK))
                    for li in range(len(LEVELS))])
    negm = np.where(tri, 0.0, -np.inf).astype(np.float32)
    lvc = np.concatenate([lv[c * CHUNK:(c + 1) * CHUNK, c * CHUNK:(c + 1) * CHUNK]
                          for c in range(ROWS // CHUNK)], axis=1)
    return dict(tri=jnp.asarray(tri, BF16), ones=jnp.asarray(same, BF16), negm=jnp.asarray(negm),
                lvc=jnp.asarray(lvc), lvm=jnp.asarray(lvm, BF16), stack=jnp.asarray(stack, BF16))


def _head_block_ones(n, width):
    i = np.arange(n)
    return jnp.asarray((i[:, None] // width) == (i[None, :] // width), BF16)


def _inproj_perm():
    perm = np.full((_N_IN_PAD,), -1, np.int64)

    def put(src, n, at):
        perm[at:at + n] = np.arange(src, src + n)

    put(_O_GDN_QKV, GDN_QKV, 0)
    put(_O_GDN_Z, GDN_W, GDN_QKV)
    pos = _W_GDN
    put(_O_CQ, MLA_Q_RANK, pos)
    put(_O_CKV, MLA_KV_RANK, pos + MLA_Q_RANK)
    pos += _W_MLA
    put(_O_MLA_Z, MLA_W, pos)
    pos += _W_MLAZ
    put(_O_GLA_Q, GLA_QK, pos)
    put(_O_GLA_K, GLA_QK, pos + GLA_QK)
    put(_O_GLA_V, GLA_W, pos + 2 * GLA_QK)
    put(_O_GLA_Z, GLA_W, pos + 2 * GLA_QK + GLA_W)
    pos += _W_GLA
    put(_O_A, 2 * GDN_HEADS, pos + _S_A)
    put(_O_B, 2 * GDN_HEADS, pos + _S_B)
    put(_O_GLOW, 2 * GLA_GATE_RANK, pos + _S_GLOW)
    pos += LANES
    put(_O_KR, MLA_ROPE, pos + MLA_NOPE)
    return perm


def _take_cols(w, perm):
    cols = jnp.take(w, jnp.asarray(np.maximum(perm, 0)), axis=1)
    return jnp.where(jnp.asarray(perm >= 0)[None, :], cols, 0.0)


def _mla_q_perm():
    perm = np.full((MLA_HEADS * LANES,), -1, np.int64)
    d = MLA_NOPE + MLA_ROPE
    for h in range(MLA_HEADS):
        perm[h * LANES:h * LANES + d] = np.arange(h * d, (h + 1) * d)
    return perm


def _mla_kv_perm():
    dk = MLA_NOPE + MLA_DV
    kperm = np.full((MLA_HEADS * LANES,), -1, np.int64)
    vperm = np.zeros((MLA_W,), np.int64)
    for h in range(MLA_HEADS):
        kperm[h * LANES:h * LANES + MLA_NOPE] = np.arange(h * dk, h * dk + MLA_NOPE)
        vperm[h * MLA_DV:(h + 1) * MLA_DV] = np.arange(h * dk + MLA_NOPE, (h + 1) * dk)
    return kperm, vperm


def _rope_tables(n_ctx, n_lat):
    rows = n_lat // ROPE_GRID_W
    row = np.repeat(np.arange(rows, dtype=np.float32), ROPE_GRID_W)
    col = np.tile(np.arange(ROPE_GRID_W, dtype=np.float32), rows)
    n_freq = MLA_ROPE // 4
    inv = (ROPE_THETA ** (-np.arange(n_freq, dtype=np.float32) / n_freq)).astype(np.float32)
    ang = np.concatenate([row[:, None] * inv, col[:, None] * inv], axis=-1)
    cos = np.concatenate([np.ones((n_ctx, 2 * n_freq), np.float32), np.cos(ang)], 0)
    sin = np.concatenate([np.zeros((n_ctx, 2 * n_freq), np.float32), np.sin(ang)], 0)
    n = n_ctx + n_lat
    half = MLA_ROPE // 2
    tab = np.zeros((3, n, LANES), np.float32)
    tab[0, :, :MLA_NOPE] = 1.0
    tab[0, :, MLA_NOPE:MLA_NOPE + half] = cos
    tab[0, :, MLA_NOPE + half:MLA_NOPE + MLA_ROPE] = cos
    tab[1, :, MLA_NOPE:MLA_NOPE + half] = -sin
    tab[2, :, MLA_NOPE + half:MLA_NOPE + MLA_ROPE] = sin
    return jnp.asarray(tab)


def _expand_matrix(src0, n_src, width, n_out):
    m = np.zeros((LANES, n_out), np.float32)
    for h in range(n_src):
        m[src0 + h, h * width:(h + 1) * width] = 1.0
    return m


def _ada_kernel(c_ref, w_ref, b_ref, o_ref):
    c = _silu(c_ref[...]).astype(BF16)
    o_ref[0] = _dot(c, w_ref[0].astype(BF16)) + b_ref[0]


def _ada(cc, w_ada, b_ada):
    nl, d, n3 = w_ada.shape
    r = cc.shape[0]
    tn = 1024
    return pl.pallas_call(
        _ada_kernel,
        grid=(nl, n3 // tn),
        in_specs=[pl.BlockSpec((r, d), lambda l, j: (0, 0)),
                  pl.BlockSpec((1, d, tn), lambda l, j: (l, 0, j)),
                  pl.BlockSpec((1, 1, tn), lambda l, j: (l, 0, j))],
        out_specs=pl.BlockSpec((1, r, tn), lambda l, j: (l, 0, j)),
        out_shape=jax.ShapeDtypeStruct((nl, r, n3), F32),
        compiler_params=_params("arbitrary", "arbitrary"),
        name="ada_mod",
    )(cc, w_ada, b_ada.reshape(nl, 1, n3))


def _token_block(nct, refs):
    if len(refs) == 1:
        return refs[0][0]
    return jnp.where(pl.program_id(1) < nct, refs[0][0], refs[1][0])


def _token_specs(arrays, nct, d):
    if len(arrays) == 1:
        return [pl.BlockSpec((1, ROWS, d), lambda bi, j: (bi, j, 0))]
    return [pl.BlockSpec((1, ROWS, d), lambda bi, j: (bi, jnp.minimum(j, nct - 1), 0)),
            pl.BlockSpec((1, ROWS, d), lambda bi, j: (bi, jnp.maximum(j - nct, 0), 0))]


def _halo_specs(arrays, nct, d, after):
    hb = ROWS // HALO
    offs = (0,) if len(arrays) == 1 else (0, nct)

    def spec(a, off):
        last = a.shape[1] // HALO - 1

        def index(bi, j):
            h = (j - off + 1) * hb if after else (j - off) * hb - 1
            return bi, jnp.clip(h, 0, last), 0
        return pl.BlockSpec((1, HALO, d), index)

    return [spec(a, off) for a, off in zip(arrays, offs)]


def _rope(x, tab_ref):
    half = MLA_ROPE // 2
    return (x * tab_ref[0] + pltpu.roll(x, LANES - half, 1) * tab_ref[1]
            + pltpu.roll(x, half, 1) * tab_ref[2])


def _inproj_kernel(nct, nx, *refs):
    x_refs, p_refs, n_refs, refs = refs[:nx], refs[nx:2 * nx], refs[2 * nx:3 * nx], refs[3 * nx:]
    (mod_ref, w_ref, wabt_ref, tab_ref, qw_ref, kvw_ref, wuq_ref, wuk_ref, wuv_ref, convw_ref,
     ob_ref, ogdn, omlaz, ogla, osmall, oabt, q_ref, k_ref, v_ref, xe_scr) = refs
    j = pl.program_id(1)
    nt = pl.num_programs(1)

    def modulated(x):
        h = x * lax.rsqrt(jnp.mean(x * x, axis=-1, keepdims=True) + EPS)
        return (h * (1.0 + mod_ref[0, 1:2, :]) + mod_ref[0, 0:1, :]).astype(BF16)

    hb = modulated(_token_block(nct, x_refs))
    pos = 0
    proj = {}
    for name, ref, n in (("gdn", None, _W_GDN), ("mla", None, _W_MLA), ("mlaz", omlaz, _W_MLAZ),
                         ("gla", ogla, _W_GLA), ("small", osmall, LANES), ("kr", None, LANES)):
        y = _dot(hb, w_ref[:, pos:pos + n])
        if ref is None:
            proj[name] = y
        else:
            ref[0] = y
        pos += n
    oabt[0] = _dot_nt(wabt_ref[...], hb)

    halo = jnp.concatenate([_token_block(nct, p_refs), _token_block(nct, n_refs)], axis=0)
    yh = _dot(modulated(halo), w_ref[:, 0:GDN_QKV])
    has_prev = jnp.logical_and(j != 0, j != nct)
    has_next = jnp.logical_and(j != nct - 1, j != nt - 1)
    xe_scr[0:HALO, :] = jnp.where(has_prev, yh[:HALO], 0.0)
    xe_scr[HALO:HALO + ROWS, :] = proj["gdn"][:, :GDN_QKV]
    xe_scr[HALO + ROWS:, :] = jnp.where(has_next, yh[HALO:], 0.0)
    pad = (GDN_CONV - 1) // 2
    conv = jnp.zeros((ROWS, GDN_QKV), F32)
    for t in range(GDN_CONV):
        conv = conv + convw_ref[t:t + 1, :] * xe_scr[pl.ds(HALO - pad + t, ROWS), :]
    hqkv = _silu(conv)
    ob = ob_ref[...]
    qn = hqkv[:, :GDN_QK]
    kn = hqkv[:, GDN_QK:2 * GDN_QK]
    qn = qn * lax.rsqrt(_dot_sel2(qn * qn, ob) + EPS) * (GDN_DK ** -0.5)
    kn = kn * lax.rsqrt(_dot_sel2(kn * kn, ob) + EPS)
    ogdn[0] = jnp.concatenate([qn, kn, hqkv[:, 2 * GDN_QK:], proj["gdn"][:, GDN_QKV:]], axis=1)

    cq = proj["mla"][:, :MLA_Q_RANK]
    ckv = proj["mla"][:, MLA_Q_RANK:]
    cq = cq * lax.rsqrt(jnp.mean(cq * cq, axis=-1, keepdims=True) + EPS) * qw_ref[...]
    ckv = ckv * lax.rsqrt(jnp.mean(ckv * ckv, axis=-1, keepdims=True) + EPS) * kvw_ref[...]
    cqb = cq.astype(BF16)
    ckvb = ckv.astype(BF16)
    qf = _dot(cqb, wuq_ref[...])
    kf = _dot(ckvb, wuk_ref[...])
    vf = _dot(ckvb, wuv_ref[...])
    kr = _rope(proj["kr"], tab_ref)
    q_scale = MLA_SCALE * math.log2(math.e)
    for hd in range(MLA_HEADS):
        sl = slice(hd * LANES, (hd + 1) * LANES)
        q_ref[0, hd] = (_rope(qf[:, sl], tab_ref) * q_scale).astype(BF16)
        k_ref[0, hd] = (kf[:, sl] + kr).astype(BF16)
    ones = jnp.ones((ROWS, MXU_N - LANES), BF16)
    for hp in range(MLA_HEADS // 2):
        v_ref[0, hp] = jnp.concatenate(
            [vf[:, hp * LANES:(hp + 1) * LANES].astype(BF16), ones], axis=1)


def _inproj(xs, mod, w_p, wabt, tab, qw, kvw, wuq, wuk, wuv, convw, ob, nct):
    b, d = xs[0].shape[0], xs[0].shape[2]
    ta = sum(a.shape[1] for a in xs)
    nt = ta // ROWS
    widths = (_W_GDN, _W_MLAZ, _W_GLA, LANES)
    row = lambda bi, j: (bi, j, 0)
    full2 = lambda bi, j: (0, 0)
    hm = lambda bi, j: (bi, 0, j, 0)
    consts = (w_p, wabt)
    mla_consts = (qw, kvw, wuq, wuk, wuv, convw, ob)
    return pl.pallas_call(
        functools.partial(_inproj_kernel, nct, len(xs)),
        grid=(b, nt),
        in_specs=_token_specs(xs, nct, d) + _halo_specs(xs, nct, d, False)
        + _halo_specs(xs, nct, d, True)
        + [pl.BlockSpec((1, 3, d), lambda bi, j: (jnp.where(j < nct, 0, 1 + bi), 0, 0))]
        + [pl.BlockSpec(a.shape, full2) for a in consts]
        + [pl.BlockSpec((3, ROWS, LANES), lambda bi, j: (0, j, 0))]
        + [pl.BlockSpec(a.shape, full2) for a in mla_consts],
        out_specs=[pl.BlockSpec((1, ROWS, n), row) for n in widths]
        + [pl.BlockSpec((1, 4 * GDN_HEADS, ROWS), lambda bi, j: (bi, 0, j)),
           pl.BlockSpec((1, MLA_HEADS, ROWS, LANES), hm),
           pl.BlockSpec((1, MLA_HEADS, ROWS, LANES), hm),
           pl.BlockSpec((1, MLA_HEADS // 2, ROWS, MXU_N), hm)],
        out_shape=[jax.ShapeDtypeStruct((b, ta, n), F32) for n in widths]
        + [jax.ShapeDtypeStruct((b, 4 * GDN_HEADS, ta), F32),
           jax.ShapeDtypeStruct((b, MLA_HEADS, ta, LANES), BF16),
           jax.ShapeDtypeStruct((b, MLA_HEADS, ta, LANES), BF16),
           jax.ShapeDtypeStruct((b, MLA_HEADS // 2, ta, MXU_N), BF16)],
        scratch_shapes=[pltpu.VMEM((ROWS + 2 * HALO, GDN_QKV), F32)],
        compiler_params=_params("arbitrary", "arbitrary"),
        name="in_proj",
    )(*xs, *xs, *xs, mod, *consts, tab, *mla_consts)


def _scan_block_index(j, nct, nt, rev):
    if not rev:
        return j
    return jnp.where(j < nct, nct - 1 - j, nt - 1 - (j - nct))


def _head_lane_mask(n, width, h):
    lane = lax.broadcasted_iota(jnp.int32, (1, n), 1)
    return (lane >= h * width) & (lane < (h + 1) * width)


def _gdn_prep(d, x_ref, small_ref, abt_ref, prow_ref, pcol_ref, tri, ones, exp_ref):
    q = x_ref[0, :, :GDN_QK]
    k = x_ref[0, :, GDN_QK:2 * GDN_QK]
    v = x_ref[0, :, 2 * GDN_QK:GDN_QKV]
    sm = small_ref[0]
    g_all = -jnp.exp(prow_ref[0:1, :]) * _softplus(sm + prow_ref[1:2, :])
    beta_all = jax.nn.sigmoid(sm)
    gc_all = _sel_dot(tri, g_all)
    gl_all = _sel_dot(ones, g_all)
    g_t = -jnp.exp(pcol_ref[:, 0:1]) * _softplus(abt_ref[0] + pcol_ref[:, 1:2])
    gc_t = _dot_sel_nt(g_t, tri)
    gc_w = _dot_sel2(gc_all, exp_ref[d])
    gl_w = _dot_sel2(gl_all, exp_ref[d])
    beta_w = _dot_sel2(beta_all, exp_ref[2 + d])
    kb = k * beta_w
    return dict(q=q, kbf=k.astype(BF16), kb=kb, vb=v * beta_w, kbg=kb * jnp.exp(gc_w),
                qg=q * jnp.exp(gc_w), kdec=k * jnp.exp(gl_w - gc_w),
                gc_all=gc_all, gc_t=gc_t, gl_all=gl_all)


def _gdn_stages(xf_ref, smf_ref, abtf_ref, xb_ref, smb_ref, abtb_ref, prow_ref, pcol_ref, tri_ref,
                ones_ref, negm_ref, lvm_ref, exp_ref, of_ref, obk_ref, s_scr):
    nh, dk = GDN_HEADS, GDN_DK
    nchunk = ROWS // CHUNK
    nlev = len(LEVELS)
    ones = ones_ref[...]
    dir_refs = ((xf_ref, smf_ref, abtf_ref), (xb_ref, smb_ref, abtb_ref))
    prep = []
    for d in range(2):
        prep.append(_gdn_prep(d, *dir_refs[d], prow_ref, pcol_ref, tri_ref[d], ones, exp_ref))
        yield
    ri = lax.broadcasted_iota(jnp.int32, (ROWS, ROWS), 0)
    ci = lax.broadcasted_iota(jnp.int32, (ROWS, ROWS), 1)
    eye = (ri == ci).astype(BF16)
    chains = [(d, h) for d in range(2) for h in range(nh)]
    nc = len(chains)

    low, a_intra = [], []
    for d, h in chains:
        p = prep[d]
        idx = d * nh + h
        hm = _head_lane_mask(GDN_QK, dk, h)
        decay = jnp.exp(p["gc_all"][:, idx:idx + 1] - p["gc_t"][idx:idx + 1, :] + negm_ref[d])
        both = jnp.concatenate([jnp.where(hm, p["kb"], 0.0), jnp.where(hm, p["q"], 0.0)], axis=0)
        kq = _dot_nt(both.astype(BF16), p["kbf"])
        low.append((kq[:ROWS] * decay).astype(BF16))
        a_intra.append((kq[ROWS:] * decay).astype(BF16))
        if h == nh - 1:
            yield

    def compact(m):
        return functools.reduce(lambda a, b: a + b,
                                [m[c * CHUNK:(c + 1) * CHUNK] for c in range(nchunk)])

    def expand(mc):
        return jnp.concatenate([mc] * nchunk, axis=0) * ones

    low_c = [compact(m) for m in low]
    eye_c = compact(eye)
    t_c = [eye_c - low_c[i] * lvm_ref[d, nlev - 1] for i, (d, h) in enumerate(chains)]
    t_inv = [expand(t) for t in t_c]
    for li in reversed(range(nlev - 1)):
        ys = [expand(_dot(low_c[i] * lvm_ref[d, li], t_inv[i]).astype(BF16))
              for i, (d, h) in enumerate(chains)]
        t_c = [t_c[i] - _dot(t_c[i], ys[i]).astype(BF16) for i in range(nc)]
        t_inv = [expand(t) for t in t_c]
        yield
    uw = []
    for i, (d, h) in enumerate(chains):
        sl = slice(h * dk, (h + 1) * dk)
        rhs = jnp.concatenate([prep[d]["vb"][:, sl], prep[d]["kbg"][:, sl]], axis=1)
        uw.append(_dot(t_inv[i], rhs.astype(BF16)))

    q2, ou, gb = [], [], []
    for i, (d, h) in enumerate(chains):
        sl = slice(h * dk, (h + 1) * dk)
        uwb = uw[i].astype(BF16)
        auw = _dot(a_intra[i], uwb)
        ou.append(auw[:, :GDN_DV])
        q2.append((prep[d]["qg"][:, sl] - auw[:, GDN_DV:]).astype(BF16))
        kd = prep[d]["kdec"][:, sl].astype(BF16)
        gb.append([_dot_tn(kd[c * CHUNK:(c + 1) * CHUNK], uwb[c * CHUNK:(c + 1) * CHUNK])
                   for c in range(nchunk)])
    yield
    states = [s_scr[i] for i in range(nc)]
    o_parts = [[None] * nchunk for _ in chains]
    for step in range(nchunk):
        for i, (d, h) in enumerate(chains):
            c = nchunk - 1 - step if d == 1 else step
            idx = d * nh + h
            r0 = c * CHUNK
            rs = slice(r0, r0 + CHUNK)
            sb = states[i].astype(BF16)
            o_parts[i][c] = _dot(q2[i][rs], sb) + ou[i][rs]
            states[i] = (states[i] * jnp.exp(prep[d]["gl_all"][r0:r0 + 1, idx:idx + 1])
                         - _dot(gb[i][c][:, GDN_DV:].astype(BF16), sb) + gb[i][c][:, :GDN_DV])
        yield
    for i in range(len(chains)):
        s_scr[i] = states[i]
    outs = [jnp.concatenate(o_parts[i], axis=0) for i in range(len(chains))]
    of_ref[0] = jnp.concatenate(outs[:nh], axis=1)
    obk_ref[0] = jnp.concatenate(outs[nh:], axis=1)


def _scan_row_specs(nct, nt, rev, width):
    blk_of = functools.partial(_scan_block_index, nct=nct, nt=nt, rev=rev)
    return pl.BlockSpec((1, ROWS, width), lambda bi, j: (bi, blk_of(j), 0))


def _gdn_specs(nct, nt, gdn_in, small, abt, prow, pcol, exp_m):
    cf, cb = _scan_consts(False), _scan_consts(True)
    stack = lambda name: jnp.stack([cf[name], cb[name]])
    full2 = lambda bi, j: (0, 0)
    full3 = lambda bi, j: (0, 0, 0)

    def dir_specs(rev):
        blk_of = functools.partial(_scan_block_index, nct=nct, nt=nt, rev=rev)
        return [
            _scan_row_specs(nct, nt, rev, GDN_QKV),
            _scan_row_specs(nct, nt, rev, LANES),
            pl.BlockSpec((1, 4 * GDN_HEADS, ROWS), lambda bi, j: (bi, 0, blk_of(j))),
        ]

    in_specs = dir_specs(False) + dir_specs(True) + [
        pl.BlockSpec(prow.shape, full2),
        pl.BlockSpec(pcol.shape, full2),
        pl.BlockSpec((2, ROWS, ROWS), full3),
        pl.BlockSpec((ROWS, ROWS), full2),
        pl.BlockSpec((2, ROWS, ROWS), full3),
        pl.BlockSpec((2, len(LEVELS), CHUNK, ROWS), lambda bi, j: (0, 0, 0, 0)),
        pl.BlockSpec(exp_m.shape, full3),
    ]
    dir_args = [gdn_in, small, abt]
    args = dir_args + dir_args + [prow, pcol, stack("tri"), cf["ones"], stack("negm"),
                                  stack("lvm"), exp_m]
    out_specs = [_scan_row_specs(nct, nt, False, GDN_W), _scan_row_specs(nct, nt, True, GDN_W)]
    scratch = [pltpu.VMEM((2 * GDN_HEADS, GDN_DK, GDN_DV), F32)]
    return in_specs, args, out_specs, scratch


def _gla_stages(xf_ref, smf_ref, xb_ref, smb_ref, wg_ref, bg_ref, stack_ref, lvc_ref, kmask_ref,
                same_ref, eqk_ref, sbd_ref, of_ref, obk_ref, s_scr):
    nh, dk, dv = GLA_HEADS, GLA_DK, GLA_DV
    nchunk = ROWS // CHUNK
    nlev = len(LEVELS)
    n = GLA_QK
    out_head = [_head_lane_mask(GLA_W, dv, h) for h in range(nh)]
    eqk = eqk_ref[...]
    sbd = sbd_ref[...]
    x_refs, sm_refs, o_refs = (xf_ref, xb_ref), (smf_ref, smb_ref), (of_ref, obk_ref)
    dirs = range(2)
    xs = [x_refs[d][0] for d in dirs]
    q = [xs[d][:, :GLA_QK] * (dk ** -0.5) for d in dirs]
    k = [xs[d][:, GLA_QK:2 * GLA_QK] for d in dirs]
    v = [xs[d][:, 2 * GLA_QK:2 * GLA_QK + GLA_W] for d in dirs]
    vb = [v[d].astype(BF16) for d in dirs]
    gk = [_dot(sm_refs[d][0].astype(BF16), wg_ref[d]) + bg_ref[d] for d in dirs]
    la = [-_softplus(-gk[d]) * (1.0 / GLA_GATE_NORM) for d in dirs]
    la2 = [jnp.concatenate(_split2(la[d]), axis=1) for d in dirs]
    ys = [_dot(stack_ref[d], la2[d]) for d in dirs]
    cums = [ys[d][:, :n] + ys[d][:, n:] for d in dirs]
    piece = lambda d, i: cums[d][i * ROWS:(i + 1) * ROWS]
    bcum = [piece(d, 0) for d in dirs]
    last_row = [0 if d == 1 else CHUNK - 1 for d in dirs]
    blast = [jnp.concatenate(
        [jnp.broadcast_to(bcum[d][c * CHUNK + last_row[d]:c * CHUNK + last_row[d] + 1, :], (CHUNK, n))
         for c in range(nchunk)], axis=0) for d in dirs]
    la_next = [pltpu.roll(la[d], 1 if d == 1 else ROWS - 1, 0) for d in dirs]
    qg = [(q[d] * jnp.exp(bcum[d])).astype(BF16) for d in dirs]
    kdec = [(k[d] * jnp.exp(blast[d] - bcum[d])).astype(BF16) for d in dirs]
    yield

    lane4 = lax.broadcasted_iota(jnp.int32, (1, nchunk * n), 1) % n
    lane_head4 = [(lane4 >= h * dk) & (lane4 < (h + 1) * dk) for h in range(nh)]
    kmask = kmask_ref[...]
    same = same_ref[...]
    lv4 = [jnp.concatenate([lvc_ref[d]] * nh, axis=0) for d in dirs]
    acc = [jnp.zeros((nh * CHUNK, ROWS), F32) for d in dirs]
    for li in range(nlev):
        if li < nlev - 1:
            ql = [q[d] * jnp.exp(piece(d, 1 + li)) for d in dirs]
            kl = [(k[d] * jnp.exp(piece(d, nlev + li))).astype(BF16) for d in dirs]
        else:
            ql = q
            kl = [(k[d] * jnp.exp(la_next[d])).astype(BF16) for d in dirs]
        qc = [jnp.concatenate([ql[d][c * CHUNK:(c + 1) * CHUNK] for c in range(nchunk)], axis=1)
              for d in dirs]
        qs = [jnp.concatenate([jnp.where(lane_head4[h], qc[d], 0.0) for h in range(nh)],
                              axis=0).astype(BF16) for d in dirs]
        kt = [jnp.concatenate([kl[d]] * nchunk, axis=1) * kmask for d in dirs]
        ps = [_dot_nt(qs[d], kt[d]) for d in dirs]
        acc = [jnp.where(lv4[d] == float(li), ps[d], acc[d]) for d in dirs]
        yield
    accb = [acc[d].astype(BF16) for d in dirs]
    acc_bd = [jnp.concatenate(
        [jnp.concatenate([accb[d][h * CHUNK:(h + 1) * CHUNK]] * nchunk, axis=0) * same
         for h in range(nh)], axis=0) for d in dirs]
    pv = [_dot(acc_bd[d], vb[d]) for d in dirs]
    o = [_dot_sel(q[d] * k[d], eqk) * v[d] for d in dirs]
    for h in range(nh):
        o = [o[d] + jnp.where(out_head[h], pv[d][h * ROWS:(h + 1) * ROWS], 0.0) for d in dirs]
    yield

    states = [s_scr[d] for d in dirs]
    o_parts = [[None] * nchunk for d in dirs]
    for step in range(nchunk):
        for d in dirs:
            c = nchunk - 1 - step if d == 1 else step
            rs = slice(c * CHUNK, (c + 1) * CHUNK)
            o_parts[d][c] = _dot_nt(qg[d][rs], states[d].astype(BF16))
            states[d] = (states[d] * jnp.exp(blast[d][c * CHUNK:c * CHUNK + 1, :])
                         + sbd * _dot_tn(vb[d][rs], kdec[d][rs]))
        yield
    for d in dirs:
        s_scr[d] = states[d]
        o_refs[d][0] = o[d] + jnp.concatenate(o_parts[d], axis=0)


def _gla_specs(nct, nt, gla_in, small, wg, bg, eqk, sbd):
    cf, cb = _scan_consts(False), _scan_consts(True)
    stack = jnp.stack([cf["stack"], cb["stack"]])
    lvc = jnp.stack([cf["lvc"], cb["lvc"]])
    nchunk = ROWS // CHUNK
    kmask = jnp.asarray(np.repeat(np.repeat(np.eye(nchunk), CHUNK, axis=0), GLA_QK, axis=1), BF16)
    full2 = lambda bi, j: (0, 0)
    full3 = lambda bi, j: (0, 0, 0)
    in_specs = [
        _scan_row_specs(nct, nt, False, _W_GLA), _scan_row_specs(nct, nt, False, LANES),
        _scan_row_specs(nct, nt, True, _W_GLA), _scan_row_specs(nct, nt, True, LANES),
        pl.BlockSpec(wg.shape, full3),
        pl.BlockSpec(bg.shape, full3),
        pl.BlockSpec(stack.shape, full3),
        pl.BlockSpec(lvc.shape, full3),
        pl.BlockSpec(kmask.shape, full2),
        pl.BlockSpec((ROWS, ROWS), full2),
        pl.BlockSpec(eqk.shape, full2),
        pl.BlockSpec(sbd.shape, full2),
    ]
    args = [gla_in, small, gla_in, small, wg, bg, stack, lvc, kmask, cf["ones"], eqk, sbd]
    out_specs = [_scan_row_specs(nct, nt, False, GLA_W), _scan_row_specs(nct, nt, True, GLA_W)]
    scratch = [pltpu.VMEM((2, GLA_W, GLA_QK), F32)]
    return in_specs, args, out_specs, scratch


_N_GDN_IN, _N_GLA_IN = 13, 12


def _scan_kernel(*refs):
    gdn_in, refs = refs[:_N_GDN_IN], refs[_N_GDN_IN:]
    gla_in, refs = refs[:_N_GLA_IN], refs[_N_GLA_IN:]
    gdn_out, gla_out, (gdn_state, gla_state) = refs[:2], refs[2:4], refs[4:]

    @pl.when(pl.program_id(1) == 0)
    def _():
        gdn_state[...] = jnp.zeros_like(gdn_state)
        gla_state[...] = jnp.zeros_like(gla_state)

    stages = [_gdn_stages(*gdn_in, *gdn_out, gdn_state),
              _gla_stages(*gla_in, *gla_out, gla_state)]
    while stages:
        for s in list(stages):
            if next(s, StopIteration) is StopIteration:
                stages.remove(s)


def _scans(nct, gdn_args, gla_args):
    gdn_in = gdn_args[0]
    b, ta, _ = gdn_in.shape
    nt = ta // ROWS
    g_in, g_args, g_out, g_scr = _gdn_specs(nct, nt, *gdn_args)
    a_in, a_args, a_out, a_scr = _gla_specs(nct, nt, *gla_args)
    assert len(g_in) == _N_GDN_IN and len(a_in) == _N_GLA_IN
    return pl.pallas_call(
        _scan_kernel,
        grid=(b, nt),
        in_specs=g_in + a_in,
        out_specs=g_out + a_out,
        out_shape=[jax.ShapeDtypeStruct((b, ta, GDN_W), F32)] * 2
        + [jax.ShapeDtypeStruct((b, ta, GLA_W), F32)] * 2,
        scratch_shapes=g_scr + a_scr,
        compiler_params=_params("arbitrary", "arbitrary"),
        name="gdn_gla_scan",
    )(*g_args, *a_args)


def _attn_kernel(nct, off, n_ctx, q_ref, k_ref, v_ref, z_ref, o_ref):
    i = pl.program_id(2) + off

    def body(nk):
        parts = []
        half = ROWS // 2

        def by_row_halves(dot, lhs, rhs):
            return jnp.concatenate([dot(lhs[:half], rhs), dot(lhs[half:], rhs)], axis=0)

        for hh in range(ATTN_HEADS):
            q = q_ref[0, hh]
            kk = k_ref[0, hh, :nk, :]
            s = by_row_halves(_dot_nt, q, kk) if hh == 0 else _dot_nt(q, kk)
            m = jnp.max(s, axis=-1, keepdims=True)
            p = jnp.exp2(s - m).astype(BF16)
            vv = v_ref[0, hh // 2, :nk, :]
            pv = by_row_halves(_dot, p, vv) if hh == ATTN_HEADS - 1 else _dot(p, vv)
            o = pv[:, (hh % 2) * MLA_DV:(hh % 2 + 1) * MLA_DV] / pv[:, LANES:LANES + 1]
            parts.append(o)
        o_ref[0] = (jnp.concatenate(parts, axis=1) * _silu(z_ref[0])).astype(BF16)

    if off == 0:
        @pl.when(i < nct)
        def _():
            body(n_ctx)

        @pl.when(i >= nct)
        def _():
            body(k_ref.shape[2])
    else:
        body(k_ref.shape[2])


def _attention(q, k, v, z, nct, with_ctx):
    b, nh, ta, _ = q.shape
    nt = ta // ROWS
    off = 0 if with_ctx else nct
    nq = nt - off
    hg = ATTN_HEADS
    wo = hg * MLA_DV
    return pl.pallas_call(
        functools.partial(_attn_kernel, nct, off, nct * ROWS),
        grid=(b, nh // hg, nq),
        in_specs=[pl.BlockSpec((1, hg, ROWS, LANES), lambda bi, hp, i: (bi, hp, i + off, 0)),
                  pl.BlockSpec((1, hg, ta, LANES), lambda bi, hp, i: (bi, hp, 0, 0)),
                  pl.BlockSpec((1, hg // 2, ta, MXU_N), lambda bi, hp, i: (bi, hp, 0, 0)),
                  pl.BlockSpec((1, ROWS, wo), lambda bi, hp, i: (bi, i + off, hp))],
        out_specs=pl.BlockSpec((1, ROWS, wo), lambda bi, hp, i: (bi, i, hp)),
        out_shape=jax.ShapeDtypeStruct((b, nq * ROWS, MLA_W), BF16),
        compiler_params=_params("arbitrary", "arbitrary", "arbitrary"),
        name="mla_attn",
    )(q, k, v, z)


def _outproj_kernel(final, nct, nx, gf_ref, gb_ref, gz_ref, gnw_ref, af_ref, ab_ref, az_ref, anw_ref,
                    ob_ref, m_ref, mod_ref, w_ref, *rest):
    x_refs, rest = rest[:nx], rest[nx:]
    if final:
        fw_ref, o_ref = rest
    else:
        (o_ref,) = rest
    ob = ob_ref[...]

    def gated_head_norm(o, nw_ref, z):
        ms = _dot_sel2(o * o, ob) * (1.0 / GDN_DV)
        return (o * lax.rsqrt(ms + EPS) * nw_ref[...] * _silu(z)).astype(BF16)

    g = gated_head_norm(gf_ref[0] + gb_ref[0], gnw_ref, gz_ref[0])
    a = gated_head_norm(af_ref[0] + ab_ref[0], anw_ref, az_ref[0])
    y = _dot(g, w_ref[0:GDN_W, :])
    y = y + _dot(m_ref[0], w_ref[GDN_W:GDN_W + MLA_W, :])
    y = y + _dot(a, w_ref[GDN_W + MLA_W:, :])
    xn = _token_block(nct, x_refs) + mod_ref[0, 2:3, :] * y
    if final:
        xn = xn * lax.rsqrt(jnp.mean(xn * xn, axis=-1, keepdims=True) + EPS) * fw_ref[...]
    o_ref[0] = xn


def _outproj(gdn_f, gdn_b, gdn_in, gnw, gla_f, gla_b, gla_in, anw, ob, mla_o, xs, mod, w_out,
             nct, final_w):
    assert GDN_DV == GLA_DV and GDN_W == GLA_W
    b, d = xs[0].shape[0], xs[0].shape[2]
    ta = sum(a.shape[1] for a in xs)
    nt = ta // ROWS
    final = final_w is not None
    assert not (final and len(xs) > 1)
    off = nct if final else 0
    moff = off - (ta - mla_o.shape[1]) // ROWS
    nq = nt - off
    row = lambda bi, j: (bi, j + off, 0)
    full2 = lambda bi, j: (0, 0)
    x_specs = [pl.BlockSpec((1, ROWS, d), row)] if len(xs) == 1 else _token_specs(xs, nct, d)
    wide = pl.BlockSpec((1, ROWS, GDN_W), row)
    gz_blk = GDN_QKV // GDN_W
    az_blk = (2 * GLA_QK + GLA_W) // GLA_W
    in_specs = [wide, wide,
                pl.BlockSpec((1, ROWS, GDN_W), lambda bi, j: (bi, j + off, gz_blk)),
                pl.BlockSpec(gnw.shape, full2),
                wide, wide,
                pl.BlockSpec((1, ROWS, GLA_W), lambda bi, j: (bi, j + off, az_blk)),
                pl.BlockSpec(anw.shape, full2),
                pl.BlockSpec(ob.shape, full2),
                pl.BlockSpec((1, ROWS, MLA_W), lambda bi, j: (bi, j + moff, 0)),
                pl.BlockSpec((1, 3, d), lambda bi, j: (jnp.where(j + off < nct, 0, 1 + bi), 0, 0)),
                pl.BlockSpec(w_out.shape, full2)] + x_specs
    args = [gdn_f, gdn_b, gdn_in, gnw, gla_f, gla_b, gla_in, anw, ob, mla_o, mod, w_out, *xs]
    if final:
        in_specs.append(pl.BlockSpec(final_w.shape, full2))
        args.append(final_w)
    return pl.pallas_call(
        functools.partial(_outproj_kernel, final, nct, len(xs)),
        grid=(b, nq),
        in_specs=in_specs,
        out_specs=pl.BlockSpec((1, ROWS, d), lambda bi, j: (bi, j, 0)),
        out_shape=jax.ShapeDtypeStruct((b, nq * ROWS, d), F32),
        compiler_params=_params("arbitrary", "arbitrary"),
        name="out_proj",
    )(*args)


def kernel(x, c, ctx, c_ctx, w_ada, b_ada, w_in, gdn_conv_w, gdn_a_log, gdn_dt_bias, gdn_norm_w,
           mla_q_norm_w, mla_w_uq, mla_kv_norm_w, mla_w_ukv, gla_w_gk, gla_b_gk, gla_norm_w,
           w_out, final_norm_w):
    b, seq, d = x.shape
    n_ctx = ctx.shape[1]
    depth = w_in.shape[0]
    assert n_ctx % ROWS == 0 and seq % ROWS == 0 and seq % ROPE_GRID_W == 0
    nct = n_ctx // ROWS
    nh = GDN_HEADS

    xs = (ctx, x)
    pad_rows = (-(1 + b)) % 8
    cc = jnp.concatenate([c_ctx[None, :], c, jnp.zeros((pad_rows, d), F32)], axis=0)
    mod_all = _ada(cc, w_ada, b_ada).reshape(depth, cc.shape[0], 3, d)

    perm = _inproj_perm()
    qperm = _mla_q_perm()
    kperm, vperm = _mla_kv_perm()
    tab = _rope_tables(n_ctx, seq)
    ob64 = _head_block_ones(GDN_W, GDN_DV)
    exp_m = jnp.asarray(np.stack(
        [_expand_matrix(_S_A + dd * nh, nh, GDN_DK, GDN_QK) for dd in range(2)]
        + [_expand_matrix(_S_B + dd * nh, nh, GDN_DK, GDN_QK) for dd in range(2)]), BF16)
    eqk_np = np.zeros((GLA_QK, GLA_W), np.float32)
    sbd_np = np.zeros((GLA_W, GLA_QK), np.float32)
    for h in range(GLA_HEADS):
        eqk_np[h * GLA_DK:(h + 1) * GLA_DK, h * GLA_DV:(h + 1) * GLA_DV] = 1.0
        sbd_np[h * GLA_DV:(h + 1) * GLA_DV, h * GLA_DK:(h + 1) * GLA_DK] = 1.0
    eqk = jnp.asarray(eqk_np, BF16)
    sbd = jnp.asarray(sbd_np)

    out = None
    for layer in range(depth):
        last = layer == depth - 1
        w_p = _take_cols(w_in[layer], perm).astype(BF16)
        wabt = w_in[layer][:, _O_A:_O_A + 4 * nh].T.astype(BF16)
        mod = mod_all[layer]
        wuq = _take_cols(mla_w_uq[layer], qperm).astype(BF16)
        wuk = _take_cols(mla_w_ukv[layer], kperm).astype(BF16)
        wuv = jnp.take(mla_w_ukv[layer], jnp.asarray(vperm), axis=1).astype(BF16)
        convw = jnp.concatenate(
            [gdn_conv_w[layer], jnp.zeros((8 - GDN_CONV, GDN_QKV), F32)], axis=0)
        gdn_in, mla_z, gla_in, small, abt, qh, kh, vh = _inproj(
            xs, mod, w_p, wabt, tab, mla_q_norm_w[layer][None, :], mla_kv_norm_w[layer][None, :],
            wuq, wuk, wuv, convw, ob64, nct)

        a_flat = gdn_a_log[layer].reshape(-1)
        dt_flat = gdn_dt_bias[layer].reshape(-1)
        prow = jnp.zeros((8, LANES), F32)
        prow = prow.at[0, _S_A:_S_A + 2 * nh].set(a_flat).at[1, _S_A:_S_A + 2 * nh].set(dt_flat)
        pcol = jnp.zeros((4 * nh, LANES), F32)
        pcol = pcol.at[0:2 * nh, 0].set(a_flat).at[0:2 * nh, 1].set(dt_flat)
        gnw = jnp.tile(gdn_norm_w[layer], nh)[None, :]

        anw = jnp.tile(gla_norm_w[layer], GLA_HEADS)[None, :]
        wg = jnp.zeros((2, LANES, GLA_QK), F32)
        for dd in range(2):
            r0 = _S_GLOW + dd * GLA_GATE_RANK
            wg = wg.at[dd, r0:r0 + GLA_GATE_RANK, :].set(gla_w_gk[layer, dd])
        gdn_f, gdn_b, gla_f, gla_b = _scans(
            nct, (gdn_in, small, abt, prow, pcol, exp_m),
            (gla_in, small, wg.astype(BF16), gla_b_gk[layer][:, None, :], eqk, sbd))

        mla_o = _attention(qh, kh, vh, mla_z, nct, with_ctx=not last)

        if last and len(xs) > 1:
            xs = (jnp.concatenate(xs, axis=1),)
        res = _outproj(gdn_f, gdn_b, gdn_in, gnw, gla_f, gla_b, gla_in, anw, ob64, mla_o, xs,
                       mod, w_out[layer].astype(BF16), nct, final_norm_w[None, :] if last else None)
        if last:
            out = res
        else:
            xs = (res,)
    return out
```

```python
import functools
import math

import numpy as np
import jax
import jax.numpy as jnp
from jax import lax
from jax.experimental import pallas as pl
from jax.experimental.pallas import tpu as pltpu

F32 = jnp.float32
BF16 = jnp.bfloat16
EPS = 1e-6

GDN_HEADS, GDN_DK, GDN_DV, GDN_CONV = 4, 64, 64, 5
GDN_QK = GDN_HEADS * GDN_DK
GDN_W = GDN_HEADS * GDN_DV
GDN_QKV = 2 * GDN_QK + GDN_W
MLA_HEADS, MLA_Q_RANK, MLA_KV_RANK = 8, 384, 256
MLA_NOPE, MLA_ROPE, MLA_DV = 64, 32, 64
MLA_W = MLA_HEADS * MLA_DV
MLA_SCALE = (MLA_NOPE + MLA_ROPE) ** -0.5
ROPE_THETA = 10000.0
ROPE_GRID_W = 64
GLA_HEADS, GLA_DK, GLA_DV = 4, 32, 64
GLA_QK = GLA_HEADS * GLA_DK
GLA_W = GLA_HEADS * GLA_DV
GLA_GATE_RANK = 16
GLA_GATE_NORM = 16.0
CHUNK = 64
LEVELS = (32, 16, 8, 4, 2, 1)

LANES = 128
MXU_N = 256
ROWS = 256
HALO = 8
ATTN_HEADS = 8
VMEM_LIMIT = 56 * 1024 * 1024

_O_GDN_QKV, _O_GDN_Z, _O_A, _O_B = 0, 768, 1024, 1032
_O_CQ, _O_CKV, _O_KR, _O_MLA_Z = 1040, 1424, 1680, 1712
_O_GLA_Q, _O_GLA_K, _O_GLA_V, _O_GLA_Z, _O_GLOW = 2224, 2352, 2480, 2736, 2992
_S_A, _S_B, _S_GLOW = 0, 8, 16
_W_GDN, _W_MLA, _W_MLAZ, _W_GLA = 1024, 640, 512, 768
_N_IN_PAD = _W_GDN + _W_MLA + _W_MLAZ + _W_GLA + 2 * LANES


def _dot(a, b):
    return jnp.dot(a, b, preferred_element_type=F32)


def _dot_nt(a, b):
    return lax.dot_general(a, b, (((1,), (1,)), ((), ())), preferred_element_type=F32)


def _dot_tn(a, b):
    return lax.dot_general(a, b, (((0,), (0,)), ((), ())), preferred_element_type=F32)


def _split2(x):
    x1 = x.astype(BF16)
    return x1, (x - x1.astype(F32)).astype(BF16)


def _split3(x):
    x1 = x.astype(BF16)
    r1 = x - x1.astype(F32)
    x2 = r1.astype(BF16)
    x3 = (r1 - x2.astype(F32)).astype(BF16)
    return x1, x2, x3


def _sel_dot(m01, x):
    n = x.shape[1]
    y = _dot(m01, jnp.concatenate(_split3(x), axis=1))
    return y[:, :n] + y[:, n:2 * n] + y[:, 2 * n:]


def _dot_sel(x, m01):
    n = x.shape[0]
    y = _dot(jnp.concatenate(_split3(x), axis=0), m01)
    return y[:n] + y[n:2 * n] + y[2 * n:]


def _dot_sel2(x, m01):
    n = x.shape[0]
    y = _dot(jnp.concatenate(_split2(x), axis=0), m01)
    return y[:n] + y[n:]


def _dot_sel_nt(x, m01):
    n = x.shape[0]
    y = _dot_nt(jnp.concatenate(_split3(x), axis=0), m01)
    return y[:n] + y[n:2 * n] + y[2 * n:]


def _softplus(x):
    return jnp.maximum(x, 0.0) + jnp.log1p(jnp.exp(-jnp.abs(x)))


def _silu(x):
    return x * jax.nn.sigmoid(x)


def _params(*sem):
    return pltpu.CompilerParams(dimension_semantics=sem, vmem_limit_bytes=VMEM_LIMIT)


def _scan_consts(rev):
    t = np.arange(ROWS)
    ch = t // CHUNK
    p = (CHUNK - 1 - t % CHUNK) if rev else (t % CHUNK)
    same = ch[:, None] == ch[None, :]
    tri = same & (p[None, :] <= p[:, None])
    mq, mk = [], []
    lv = np.full((ROWS, ROWS), -1.0, np.float32)
    for li, s in enumerate(LEVELS):
        blk = p // s
        mq.append(same & (p[None, :] > (blk * s)[:, None]) & (p[None, :] <= p[:, None]))
        mk.append(same & (p[None, :] > p[:, None]) & (p[None, :] <= ((blk + 1) * s)[:, None]))
        pair = same & ((blk % 2) == 1)[:, None] & (blk[None, :] == (blk - 1)[:, None])
        lv[pair] = li
    stack = np.concatenate([tri] + mq[:-1] + mk[:-1], axis=0)
    lvm = np.stack([sum((lv == li)[c * CHUNK:(c + 1) * CHUNK] for c in range(ROWS // CHUNK))
                    for li in range(len(LEVELS))])
    def blocks_side_by_side(m):
        return np.concatenate([m[c * CHUNK:(c + 1) * CHUNK, c * CHUNK:(c + 1) * CHUNK]
                               for c in range(ROWS // CHUNK)], axis=1)

    negm = blocks_side_by_side(np.where(tri, 0.0, -np.inf).astype(np.float32))
    lvc = blocks_side_by_side(lv)
    return dict(tri=jnp.asarray(tri, BF16), ones=jnp.asarray(same, BF16), negm=jnp.asarray(negm),
                lvc=jnp.asarray(lvc), lvm=jnp.asarray(lvm, BF16), stack=jnp.asarray(stack, BF16))


def _chunk_lane_mask():
    nchunk = ROWS // CHUNK
    return jnp.asarray(np.repeat(np.repeat(np.eye(nchunk), CHUNK, axis=0), LANES, axis=1), BF16)


def _head_block_ones(n, width):
    i = np.arange(n)
    return jnp.asarray((i[:, None] // width) == (i[None, :] // width), BF16)


def _inproj_perm():
    perm = np.full((_N_IN_PAD,), -1, np.int64)

    def put(src, n, at):
        perm[at:at + n] = np.arange(src, src + n)

    put(_O_GDN_QKV, GDN_QKV, 0)
    put(_O_GDN_Z, GDN_W, GDN_QKV)
    pos = _W_GDN
    put(_O_CQ, MLA_Q_RANK, pos)
    put(_O_CKV, MLA_KV_RANK, pos + MLA_Q_RANK)
    pos += _W_MLA
    put(_O_MLA_Z, MLA_W, pos)
    pos += _W_MLAZ
    put(_O_GLA_Q, GLA_QK, pos)
    put(_O_GLA_K, GLA_QK, pos + GLA_QK)
    put(_O_GLA_V, GLA_W, pos + 2 * GLA_QK)
    put(_O_GLA_Z, GLA_W, pos + 2 * GLA_QK + GLA_W)
    pos += _W_GLA
    put(_O_A, 2 * GDN_HEADS, pos + _S_A)
    put(_O_B, 2 * GDN_HEADS, pos + _S_B)
    put(_O_GLOW, 2 * GLA_GATE_RANK, pos + _S_GLOW)
    pos += LANES
    put(_O_KR, MLA_ROPE, pos + MLA_NOPE)
    return perm


def _take_cols(w, perm):
    cols = jnp.take(w, jnp.asarray(np.maximum(perm, 0)), axis=1)
    return jnp.where(jnp.asarray(perm >= 0)[None, :], cols, 0.0)


def _mla_q_perm():
    perm = np.full((MLA_HEADS * LANES,), -1, np.int64)
    d = MLA_NOPE + MLA_ROPE
    for h in range(MLA_HEADS):
        perm[h * LANES:h * LANES + d] = np.arange(h * d, (h + 1) * d)
    return perm


def _mla_kv_perm():
    dk = MLA_NOPE + MLA_DV
    kperm = np.full((MLA_HEADS * LANES,), -1, np.int64)
    vperm = np.zeros((MLA_W,), np.int64)
    for h in range(MLA_HEADS):
        kperm[h * LANES:h * LANES + MLA_NOPE] = np.arange(h * dk, h * dk + MLA_NOPE)
        vperm[h * MLA_DV:(h + 1) * MLA_DV] = np.arange(h * dk + MLA_NOPE, (h + 1) * dk)
    return kperm, vperm


def _rope_tables(n_ctx, n_lat):
    rows = n_lat // ROPE_GRID_W
    row = np.repeat(np.arange(rows, dtype=np.float32), ROPE_GRID_W)
    col = np.tile(np.arange(ROPE_GRID_W, dtype=np.float32), rows)
    n_freq = MLA_ROPE // 4
    inv = (ROPE_THETA ** (-np.arange(n_freq, dtype=np.float32) / n_freq)).astype(np.float32)
    ang = np.concatenate([row[:, None] * inv, col[:, None] * inv], axis=-1)
    cos = np.concatenate([np.ones((n_ctx, 2 * n_freq), np.float32), np.cos(ang)], 0)
    sin = np.concatenate([np.zeros((n_ctx, 2 * n_freq), np.float32), np.sin(ang)], 0)
    n = n_ctx + n_lat
    half = MLA_ROPE // 2
    tab = np.zeros((3, n, LANES), np.float32)
    tab[0, :, :MLA_NOPE] = 1.0
    tab[0, :, MLA_NOPE:MLA_NOPE + half] = cos
    tab[0, :, MLA_NOPE + half:MLA_NOPE + MLA_ROPE] = cos
    tab[1, :, MLA_NOPE:MLA_NOPE + half] = -sin
    tab[2, :, MLA_NOPE + half:MLA_NOPE + MLA_ROPE] = sin
    return jnp.asarray(tab)


def _expand_matrix(src0, n_src, width, n_out):
    m = np.zeros((LANES, n_out), np.float32)
    for h in range(n_src):
        m[src0 + h, h * width:(h + 1) * width] = 1.0
    return m


def _ada_kernel(c_ref, w_ref, b_ref, o_ref):
    c = _silu(c_ref[...]).astype(BF16)
    o_ref[0] = _dot(c, w_ref[0].astype(BF16)) + b_ref[0]


def _ada(cc, w_ada, b_ada):
    nl, d, n3 = w_ada.shape
    r = cc.shape[0]
    tn = 1024
    return pl.pallas_call(
        _ada_kernel,
        grid=(nl, n3 // tn),
        in_specs=[pl.BlockSpec((r, d), lambda l, j: (0, 0)),
                  pl.BlockSpec((1, d, tn), lambda l, j: (l, 0, j)),
                  pl.BlockSpec((1, 1, tn), lambda l, j: (l, 0, j))],
        out_specs=pl.BlockSpec((1, r, tn), lambda l, j: (l, 0, j)),
        out_shape=jax.ShapeDtypeStruct((nl, r, n3), F32),
        compiler_params=_params("arbitrary", "arbitrary"),
        name="ada_mod",
    )(cc, w_ada, b_ada.reshape(nl, 1, n3))


def _token_block(nct, refs):
    if len(refs) == 1:
        return refs[0][0]
    return jnp.where(pl.program_id(1) < nct, refs[0][0], refs[1][0])


def _token_specs(arrays, nct, d):
    if len(arrays) == 1:
        return [pl.BlockSpec((1, ROWS, d), lambda bi, j: (bi, j, 0))]
    return [pl.BlockSpec((1, ROWS, d), lambda bi, j: (bi, jnp.minimum(j, nct - 1), 0)),
            pl.BlockSpec((1, ROWS, d), lambda bi, j: (bi, jnp.maximum(j - nct, 0), 0))]


def _halo_specs(arrays, nct, d, after):
    hb = ROWS // HALO
    offs = (0,) if len(arrays) == 1 else (0, nct)

    def spec(a, off):
        last = a.shape[1] // HALO - 1

        def index(bi, j):
            h = (j - off + 1) * hb if after else (j - off) * hb - 1
            return bi, jnp.clip(h, 0, last), 0
        return pl.BlockSpec((1, HALO, d), index)

    return [spec(a, off) for a, off in zip(arrays, offs)]


def _rope(x, tab_ref):
    half = MLA_ROPE // 2
    return (x * tab_ref[0] + pltpu.roll(x, LANES - half, 1) * tab_ref[1]
            + pltpu.roll(x, half, 1) * tab_ref[2])


def _inproj_kernel(nct, nx, *refs):
    x_refs, p_refs, n_refs, refs = refs[:nx], refs[nx:2 * nx], refs[2 * nx:3 * nx], refs[3 * nx:]
    (mod_ref, w_ref, wabt_ref, tab_ref, qw_ref, kvw_ref, wuq_ref, wuk_ref, wuv_ref, convw_ref,
     ob_ref, ogdn, omlaz, ogla, osmall, oabt, q_ref, k_ref, v_ref, xe_scr) = refs
    j = pl.program_id(1)
    nt = pl.num_programs(1)

    def modulated(x):
        h = x * lax.rsqrt(jnp.mean(x * x, axis=-1, keepdims=True) + EPS)
        return (h * (1.0 + mod_ref[0, 1:2, :]) + mod_ref[0, 0:1, :]).astype(BF16)

    hb = modulated(_token_block(nct, x_refs))
    pos = 0
    proj = {}
    for name, ref, n in (("gdn", None, _W_GDN), ("mla", None, _W_MLA), ("mlaz", omlaz, _W_MLAZ),
                         ("gla", ogla, _W_GLA), ("small", osmall, LANES), ("kr", None, LANES)):
        y = _dot(hb, w_ref[:, pos:pos + n])
        if ref is None:
            proj[name] = y
        else:
            ref[0] = y
        pos += n
    oabt[0] = _dot_nt(wabt_ref[...], hb)

    halo = jnp.concatenate([_token_block(nct, p_refs), _token_block(nct, n_refs)], axis=0)
    yh = _dot(modulated(halo), w_ref[:, 0:GDN_QKV])
    has_prev = jnp.logical_and(j != 0, j != nct)
    has_next = jnp.logical_and(j != nct - 1, j != nt - 1)
    xe_scr[0:HALO, :] = jnp.where(has_prev, yh[:HALO], 0.0)
    xe_scr[HALO:HALO + ROWS, :] = proj["gdn"][:, :GDN_QKV]
    xe_scr[HALO + ROWS:, :] = jnp.where(has_next, yh[HALO:], 0.0)
    pad = (GDN_CONV - 1) // 2
    conv = jnp.zeros((ROWS, GDN_QKV), F32)
    for t in range(GDN_CONV):
        conv = conv + convw_ref[t:t + 1, :] * xe_scr[pl.ds(HALO - pad + t, ROWS), :]
    hqkv = _silu(conv)
    ob = ob_ref[...]
    qn = hqkv[:, :GDN_QK]
    kn = hqkv[:, GDN_QK:2 * GDN_QK]
    qn = qn * lax.rsqrt(_dot_sel2(qn * qn, ob) + EPS) * (GDN_DK ** -0.5)
    kn = kn * lax.rsqrt(_dot_sel2(kn * kn, ob) + EPS)
    ogdn[0] = jnp.concatenate([qn, kn, hqkv[:, 2 * GDN_QK:], proj["gdn"][:, GDN_QKV:]], axis=1)

    cq = proj["mla"][:, :MLA_Q_RANK]
    ckv = proj["mla"][:, MLA_Q_RANK:]
    cq = cq * lax.rsqrt(jnp.mean(cq * cq, axis=-1, keepdims=True) + EPS) * qw_ref[...]
    ckv = ckv * lax.rsqrt(jnp.mean(ckv * ckv, axis=-1, keepdims=True) + EPS) * kvw_ref[...]
    cqb = cq.astype(BF16)
    ckvb = ckv.astype(BF16)
    qf = _dot(cqb, wuq_ref[...])
    kf = _dot(ckvb, wuk_ref[...])
    vf = _dot(ckvb, wuv_ref[...])
    kr = _rope(proj["kr"], tab_ref)
    q_scale = MLA_SCALE * math.log2(math.e)
    for hd in range(MLA_HEADS):
        sl = slice(hd * LANES, (hd + 1) * LANES)
        q_ref[0, hd] = (_rope(qf[:, sl], tab_ref) * q_scale).astype(BF16)
        k_ref[0, hd] = (kf[:, sl] + kr).astype(BF16)
    ones = jnp.ones((ROWS, MXU_N - LANES), BF16)
    for hp in range(MLA_HEADS // 2):
        v_ref[0, hp] = jnp.concatenate(
            [vf[:, hp * LANES:(hp + 1) * LANES].astype(BF16), ones], axis=1)


def _inproj(xs, mod, w_p, wabt, tab, qw, kvw, wuq, wuk, wuv, convw, ob, nct):
    b, d = xs[0].shape[0], xs[0].shape[2]
    ta = sum(a.shape[1] for a in xs)
    nt = ta // ROWS
    widths = (_W_GDN, _W_MLAZ, _W_GLA, LANES)
    row = lambda bi, j: (bi, j, 0)
    full2 = lambda bi, j: (0, 0)
    hm = lambda bi, j: (bi, 0, j, 0)
    consts = (w_p, wabt)
    mla_consts = (qw, kvw, wuq, wuk, wuv, convw, ob)
    return pl.pallas_call(
        functools.partial(_inproj_kernel, nct, len(xs)),
        grid=(b, nt),
        in_specs=_token_specs(xs, nct, d) + _halo_specs(xs, nct, d, False)
        + _halo_specs(xs, nct, d, True)
        + [pl.BlockSpec((1, 3, d), lambda bi, j: (jnp.where(j < nct, 0, 1 + bi), 0, 0))]
        + [pl.BlockSpec(a.shape, full2) for a in consts]
        + [pl.BlockSpec((3, ROWS, LANES), lambda bi, j: (0, j, 0))]
        + [pl.BlockSpec(a.shape, full2) for a in mla_consts],
        out_specs=[pl.BlockSpec((1, ROWS, n), row) for n in widths]
        + [pl.BlockSpec((1, 4 * GDN_HEADS, ROWS), lambda bi, j: (bi, 0, j)),
           pl.BlockSpec((1, MLA_HEADS, ROWS, LANES), hm),
           pl.BlockSpec((1, MLA_HEADS, ROWS, LANES), hm),
           pl.BlockSpec((1, MLA_HEADS // 2, ROWS, MXU_N), hm)],
        out_shape=[jax.ShapeDtypeStruct((b, ta, n), F32) for n in widths]
        + [jax.ShapeDtypeStruct((b, 4 * GDN_HEADS, ta), F32),
           jax.ShapeDtypeStruct((b, MLA_HEADS, ta, LANES), BF16),
           jax.ShapeDtypeStruct((b, MLA_HEADS, ta, LANES), BF16),
           jax.ShapeDtypeStruct((b, MLA_HEADS // 2, ta, MXU_N), BF16)],
        scratch_shapes=[pltpu.VMEM((ROWS + 2 * HALO, GDN_QKV), F32)],
        compiler_params=_params("arbitrary", "arbitrary"),
        name="in_proj",
    )(*xs, *xs, *xs, mod, *consts, tab, *mla_consts)


def _scan_block_index(j, nct, nt, rev):
    if not rev:
        return j
    return jnp.where(j < nct, nct - 1 - j, nt - 1 - (j - nct))


def _head_lane_mask(n, width, h):
    lane = lax.broadcasted_iota(jnp.int32, (1, n), 1)
    return (lane >= h * width) & (lane < (h + 1) * width)


def _gdn_prep(d, x_ref, small_ref, abt_ref, prow_ref, pcol_ref, tri, ones, exp_ref):
    q = x_ref[0, :, :GDN_QK]
    k = x_ref[0, :, GDN_QK:2 * GDN_QK]
    v = x_ref[0, :, 2 * GDN_QK:GDN_QKV]
    sm = small_ref[0]
    g_all = -jnp.exp(prow_ref[0:1, :]) * _softplus(sm + prow_ref[1:2, :])
    beta_all = jax.nn.sigmoid(sm)
    gc_all = _sel_dot(tri, g_all)
    gl_all = _sel_dot(ones, g_all)
    g_t = -jnp.exp(pcol_ref[:, 0:1]) * _softplus(abt_ref[0] + pcol_ref[:, 1:2])
    gc_t = _dot_sel_nt(g_t, tri)
    gc_w = _dot_sel2(gc_all, exp_ref[d])
    gl_w = _dot_sel2(gl_all, exp_ref[d])
    beta_w = _dot_sel2(beta_all, exp_ref[2 + d])
    kb = k * beta_w
    return dict(q=q, kbf=k.astype(BF16), kb=kb, vb=v * beta_w, kbg=kb * jnp.exp(gc_w),
                qg=q * jnp.exp(gc_w), kdec=k * jnp.exp(gl_w - gc_w),
                gc_all=gc_all, gc_t=gc_t, gl_all=gl_all)


def _gdn_stages(xf_ref, smf_ref, abtf_ref, xb_ref, smb_ref, abtb_ref, prow_ref, pcol_ref, tri_ref,
                ones_ref, negm_ref, lvm_ref, exp_ref, of_ref, obk_ref, s_scr):
    nh, dk = GDN_HEADS, GDN_DK
    nchunk = ROWS // CHUNK
    nlev = len(LEVELS)
    ones = ones_ref[...]
    dir_refs = ((xf_ref, smf_ref, abtf_ref), (xb_ref, smb_ref, abtb_ref))
    prep = []
    for d in range(2):
        prep.append(_gdn_prep(d, *dir_refs[d], prow_ref, pcol_ref, tri_ref[d], ones, exp_ref))
        yield
    ri = lax.broadcasted_iota(jnp.int32, (ROWS, ROWS), 0)
    ci = lax.broadcasted_iota(jnp.int32, (ROWS, ROWS), 1)
    eye = (ri == ci).astype(BF16)
    chains = [(d, h) for d in range(2) for h in range(nh)]
    nc = len(chains)

    def compact(m):
        return functools.reduce(lambda a, b: a + b,
                                [m[c * CHUNK:(c + 1) * CHUNK] for c in range(nchunk)])

    def expand(mc):
        return jnp.concatenate([mc] * nchunk, axis=0) * ones

    def side_by_side(x):
        return jnp.concatenate([x[c * CHUNK:(c + 1) * CHUNK] for c in range(nchunk)], axis=1)

    low_c, a_intra = [], []
    for d, h in chains:
        p = prep[d]
        idx = d * nh + h
        sl = slice(h * dk, (h + 1) * dk)
        gc_col = side_by_side(jnp.broadcast_to(p["gc_all"][:, idx:idx + 1], (ROWS, CHUNK)))
        decay = jnp.exp(gc_col - p["gc_t"][idx:idx + 1, :] + negm_ref[d])
        lhs = jnp.concatenate([side_by_side(p["kb"][:, sl]), side_by_side(p["q"][:, sl])], axis=0)
        kt = jnp.concatenate([p["kbf"][:, sl]] * nchunk, axis=1) * ones
        kq = _dot_nt(lhs.astype(BF16), kt)
        low_c.append((kq[:CHUNK] * decay).astype(BF16))
        a_intra.append(expand((kq[CHUNK:] * decay).astype(BF16)))
        if h == nh - 1:
            yield

    eye_c = compact(eye)
    t_c = [eye_c - low_c[i] * lvm_ref[d, nlev - 1] for i, (d, h) in enumerate(chains)]
    t_inv = [expand(t) for t in t_c]
    for li in reversed(range(nlev - 1)):
        ys = [expand(_dot(low_c[i] * lvm_ref[d, li], t_inv[i]).astype(BF16))
              for i, (d, h) in enumerate(chains)]
        t_c = [t_c[i] - _dot(t_c[i], ys[i]).astype(BF16) for i in range(nc)]
        t_inv = [expand(t) for t in t_c]
        yield
    uw = []
    for i, (d, h) in enumerate(chains):
        sl = slice(h * dk, (h + 1) * dk)
        rhs = jnp.concatenate([prep[d]["vb"][:, sl], prep[d]["kbg"][:, sl]], axis=1)
        uw.append(_dot(t_inv[i], rhs.astype(BF16)))

    q2, ou, gb = [], [], []
    for i, (d, h) in enumerate(chains):
        sl = slice(h * dk, (h + 1) * dk)
        uwb = uw[i].astype(BF16)
        auw = _dot(a_intra[i], uwb)
        ou.append(auw[:, :GDN_DV])
        q2.append((prep[d]["qg"][:, sl] - auw[:, GDN_DV:]).astype(BF16))
        kd = prep[d]["kdec"][:, sl].astype(BF16)
        gb.append([_dot_tn(kd[c * CHUNK:(c + 1) * CHUNK], uwb[c * CHUNK:(c + 1) * CHUNK])
                   for c in range(nchunk)])
    yield
    states = [s_scr[i] for i in range(nc)]
    o_parts = [[None] * nchunk for _ in chains]
    for step in range(nchunk):
        for i, (d, h) in enumerate(chains):
            c = nchunk - 1 - step if d == 1 else step
            idx = d * nh + h
            r0 = c * CHUNK
            rs = slice(r0, r0 + CHUNK)
            sb = states[i].astype(BF16)
            o_parts[i][c] = _dot(q2[i][rs], sb) + ou[i][rs]
            states[i] = (states[i] * jnp.exp(prep[d]["gl_all"][r0:r0 + 1, idx:idx + 1])
                         - _dot(gb[i][c][:, GDN_DV:].astype(BF16), sb) + gb[i][c][:, :GDN_DV])
        yield
    for i in range(len(chains)):
        s_scr[i] = states[i]
    outs = [jnp.concatenate(o_parts[i], axis=0) for i in range(len(chains))]
    of_ref[0] = jnp.concatenate(outs[:nh], axis=1)
    obk_ref[0] = jnp.concatenate(outs[nh:], axis=1)


def _scan_row_specs(nct, nt, rev, width):
    blk_of = functools.partial(_scan_block_index, nct=nct, nt=nt, rev=rev)
    return pl.BlockSpec((1, ROWS, width), lambda bi, j: (bi, blk_of(j), 0))


def _gdn_specs(nct, nt, gdn_in, small, abt, prow, pcol, exp_m):
    cf, cb = _scan_consts(False), _scan_consts(True)
    stack = lambda name: jnp.stack([cf[name], cb[name]])
    full2 = lambda bi, j: (0, 0)
    full3 = lambda bi, j: (0, 0, 0)

    def dir_specs(rev):
        blk_of = functools.partial(_scan_block_index, nct=nct, nt=nt, rev=rev)
        return [
            _scan_row_specs(nct, nt, rev, GDN_QKV),
            _scan_row_specs(nct, nt, rev, LANES),
            pl.BlockSpec((1, 4 * GDN_HEADS, ROWS), lambda bi, j: (bi, 0, blk_of(j))),
        ]

    in_specs = dir_specs(False) + dir_specs(True) + [
        pl.BlockSpec(prow.shape, full2),
        pl.BlockSpec(pcol.shape, full2),
        pl.BlockSpec((2, ROWS, ROWS), full3),
        pl.BlockSpec((ROWS, ROWS), full2),
        pl.BlockSpec((2, CHUNK, ROWS), full3),
        pl.BlockSpec((2, len(LEVELS), CHUNK, ROWS), lambda bi, j: (0, 0, 0, 0)),
        pl.BlockSpec(exp_m.shape, full3),
    ]
    dir_args = [gdn_in, small, abt]
    args = dir_args + dir_args + [prow, pcol, stack("tri"), cf["ones"], stack("negm"),
                                  stack("lvm"), exp_m]
    out_specs = [_scan_row_specs(nct, nt, False, GDN_W), _scan_row_specs(nct, nt, True, GDN_W)]
    scratch = [pltpu.VMEM((2 * GDN_HEADS, GDN_DK, GDN_DV), F32)]
    return in_specs, args, out_specs, scratch


def _gla_stages(xf_ref, smf_ref, xb_ref, smb_ref, wg_ref, bg_ref, stack_ref, lvc_ref, kmask_ref,
                same_ref, eqk_ref, sbd_ref, of_ref, obk_ref, s_scr):
    nh, dk, dv = GLA_HEADS, GLA_DK, GLA_DV
    nchunk = ROWS // CHUNK
    nlev = len(LEVELS)
    n = GLA_QK
    out_head = [_head_lane_mask(GLA_W, dv, h) for h in range(nh)]
    eqk = eqk_ref[...]
    sbd = sbd_ref[...]
    x_refs, sm_refs, o_refs = (xf_ref, xb_ref), (smf_ref, smb_ref), (of_ref, obk_ref)
    dirs = range(2)
    xs = [x_refs[d][0] for d in dirs]
    q = [xs[d][:, :GLA_QK] * (dk ** -0.5) for d in dirs]
    k = [xs[d][:, GLA_QK:2 * GLA_QK] for d in dirs]
    v = [xs[d][:, 2 * GLA_QK:2 * GLA_QK + GLA_W] for d in dirs]
    vb = [v[d].astype(BF16) for d in dirs]
    gk = [_dot(sm_refs[d][0].astype(BF16), wg_ref[d]) + bg_ref[d] for d in dirs]
    la = [-_softplus(-gk[d]) * (1.0 / GLA_GATE_NORM) for d in dirs]
    la2 = [jnp.concatenate(_split2(la[d]), axis=1) for d in dirs]
    ys = [_dot(stack_ref[d], la2[d]) for d in dirs]
    cums = [ys[d][:, :n] + ys[d][:, n:] for d in dirs]
    piece = lambda d, i: cums[d][i * ROWS:(i + 1) * ROWS]
    bcum = [piece(d, 0) for d in dirs]
    last_row = [0 if d == 1 else CHUNK - 1 for d in dirs]
    blast = [jnp.concatenate(
        [jnp.broadcast_to(bcum[d][c * CHUNK + last_row[d]:c * CHUNK + last_row[d] + 1, :], (CHUNK, n))
         for c in range(nchunk)], axis=0) for d in dirs]
    la_next = [pltpu.roll(la[d], 1 if d == 1 else ROWS - 1, 0) for d in dirs]
    qg = [(q[d] * jnp.exp(bcum[d])).astype(BF16) for d in dirs]
    kdec = [(k[d] * jnp.exp(blast[d] - bcum[d])).astype(BF16) for d in dirs]
    yield

    lane4 = lax.broadcasted_iota(jnp.int32, (1, nchunk * n), 1) % n
    lane_head4 = [(lane4 >= h * dk) & (lane4 < (h + 1) * dk) for h in range(nh)]
    kmask = kmask_ref[...]
    same = same_ref[...]
    lv4 = [jnp.concatenate([lvc_ref[d]] * nh, axis=0) for d in dirs]
    acc = [jnp.zeros((nh * CHUNK, ROWS), F32) for d in dirs]
    for li in range(nlev):
        if li < nlev - 1:
            ql = [q[d] * jnp.exp(piece(d, 1 + li)) for d in dirs]
            kl = [(k[d] * jnp.exp(piece(d, nlev + li))).astype(BF16) for d in dirs]
        else:
            ql = q
            kl = [(k[d] * jnp.exp(la_next[d])).astype(BF16) for d in dirs]
        qc = [jnp.concatenate([ql[d][c * CHUNK:(c + 1) * CHUNK] for c in range(nchunk)], axis=1)
              for d in dirs]
        qs = [jnp.concatenate([jnp.where(lane_head4[h], qc[d], 0.0) for h in range(nh)],
                              axis=0).astype(BF16) for d in dirs]
        kt = [jnp.concatenate([kl[d]] * nchunk, axis=1) * kmask for d in dirs]
        ps = [_dot_nt(qs[d], kt[d]) for d in dirs]
        acc = [jnp.where(lv4[d] == float(li), ps[d], acc[d]) for d in dirs]
        yield
    accb = [acc[d].astype(BF16) for d in dirs]
    acc_bd = [jnp.concatenate(
        [jnp.concatenate([accb[d][h * CHUNK:(h + 1) * CHUNK]] * nchunk, axis=0) * same
         for h in range(nh)], axis=0) for d in dirs]
    pv = [_dot(acc_bd[d], vb[d]) for d in dirs]
    o = [_dot_sel(q[d] * k[d], eqk) * v[d] for d in dirs]
    for h in range(nh):
        o = [o[d] + jnp.where(out_head[h], pv[d][h * ROWS:(h + 1) * ROWS], 0.0) for d in dirs]
    yield

    states = [s_scr[d] for d in dirs]
    o_parts = [[None] * nchunk for d in dirs]
    for step in range(nchunk):
        for d in dirs:
            c = nchunk - 1 - step if d == 1 else step
            rs = slice(c * CHUNK, (c + 1) * CHUNK)
            o_parts[d][c] = _dot_nt(qg[d][rs], states[d].astype(BF16))
            states[d] = (states[d] * jnp.exp(blast[d][c * CHUNK:c * CHUNK + 1, :])
                         + sbd * _dot_tn(vb[d][rs], kdec[d][rs]))
        yield
    for d in dirs:
        s_scr[d] = states[d]
        o_refs[d][0] = o[d] + jnp.concatenate(o_parts[d], axis=0)


def _gla_specs(nct, nt, gla_in, small, wg, bg, eqk, sbd):
    cf, cb = _scan_consts(False), _scan_consts(True)
    stack = jnp.stack([cf["stack"], cb["stack"]])
    lvc = jnp.stack([cf["lvc"], cb["lvc"]])
    assert GLA_QK == LANES
    kmask = _chunk_lane_mask()
    full2 = lambda bi, j: (0, 0)
    full3 = lambda bi, j: (0, 0, 0)
    in_specs = [
        _scan_row_specs(nct, nt, False, _W_GLA), _scan_row_specs(nct, nt, False, LANES),
        _scan_row_specs(nct, nt, True, _W_GLA), _scan_row_specs(nct, nt, True, LANES),
        pl.BlockSpec(wg.shape, full3),
        pl.BlockSpec(bg.shape, full3),
        pl.BlockSpec(stack.shape, full3),
        pl.BlockSpec(lvc.shape, full3),
        pl.BlockSpec(kmask.shape, full2),
        pl.BlockSpec((ROWS, ROWS), full2),
        pl.BlockSpec(eqk.shape, full2),
        pl.BlockSpec(sbd.shape, full2),
    ]
    args = [gla_in, small, gla_in, small, wg, bg, stack, lvc, kmask, cf["ones"], eqk, sbd]
    out_specs = [_scan_row_specs(nct, nt, False, GLA_W), _scan_row_specs(nct, nt, True, GLA_W)]
    scratch = [pltpu.VMEM((2, GLA_W, GLA_QK), F32)]
    return in_specs, args, out_specs, scratch


_N_GDN_IN, _N_GLA_IN = 13, 12


def _scan_kernel(*refs):
    gdn_in, refs = refs[:_N_GDN_IN], refs[_N_GDN_IN:]
    gla_in, refs = refs[:_N_GLA_IN], refs[_N_GLA_IN:]
    gdn_out, gla_out, (gdn_state, gla_state) = refs[:2], refs[2:4], refs[4:]

    @pl.when(pl.program_id(1) == 0)
    def _():
        gdn_state[...] = jnp.zeros_like(gdn_state)
        gla_state[...] = jnp.zeros_like(gla_state)

    stages = [_gdn_stages(*gdn_in, *gdn_out, gdn_state),
              _gla_stages(*gla_in, *gla_out, gla_state)]
    while stages:
        for s in list(stages):
            if next(s, StopIteration) is StopIteration:
                stages.remove(s)


def _scans(nct, gdn_args, gla_args):
    gdn_in = gdn_args[0]
    b, ta, _ = gdn_in.shape
    nt = ta // ROWS
    g_in, g_args, g_out, g_scr = _gdn_specs(nct, nt, *gdn_args)
    a_in, a_args, a_out, a_scr = _gla_specs(nct, nt, *gla_args)
    assert len(g_in) == _N_GDN_IN and len(a_in) == _N_GLA_IN
    return pl.pallas_call(
        _scan_kernel,
        grid=(b, nt),
        in_specs=g_in + a_in,
        out_specs=g_out + a_out,
        out_shape=[jax.ShapeDtypeStruct((b, ta, GDN_W), F32)] * 2
        + [jax.ShapeDtypeStruct((b, ta, GLA_W), F32)] * 2,
        scratch_shapes=g_scr + a_scr,
        compiler_params=_params("arbitrary", "arbitrary"),
        name="gdn_gla_scan",
    )(*g_args, *a_args)


def _attn_kernel(nct, off, n_ctx, q_ref, k_ref, v_ref, z_ref, o_ref):
    i = pl.program_id(2) + off

    def body(nk):
        parts = []
        half = ROWS // 2

        def by_row_halves(dot, lhs, rhs):
            return jnp.concatenate([dot(lhs[:half], rhs), dot(lhs[half:], rhs)], axis=0)

        for hh in range(ATTN_HEADS):
            q = q_ref[0, hh]
            kk = k_ref[0, hh, :nk, :]
            s = by_row_halves(_dot_nt, q, kk) if hh == 0 else _dot_nt(q, kk)
            m = jnp.max(s, axis=-1, keepdims=True)
            p = jnp.exp2(s - m).astype(BF16)
            vv = v_ref[0, hh // 2, :nk, :]
            pv = by_row_halves(_dot, p, vv) if hh == ATTN_HEADS - 1 else _dot(p, vv)
            o = pv[:, (hh % 2) * MLA_DV:(hh % 2 + 1) * MLA_DV] / pv[:, LANES:LANES + 1]
            parts.append(o)
        o_ref[0] = (jnp.concatenate(parts, axis=1) * _silu(z_ref[0])).astype(BF16)

    if off == 0:
        @pl.when(i < nct)
        def _():
            body(n_ctx)

        @pl.when(i >= nct)
        def _():
            body(k_ref.shape[2])
    else:
        body(k_ref.shape[2])


def _attention(q, k, v, z, nct, with_ctx):
    b, nh, ta, _ = q.shape
    nt = ta // ROWS
    off = 0 if with_ctx else nct
    nq = nt - off
    hg = ATTN_HEADS
    wo = hg * MLA_DV
    return pl.pallas_call(
        functools.partial(_attn_kernel, nct, off, nct * ROWS),
        grid=(b, nh // hg, nq),
        in_specs=[pl.BlockSpec((1, hg, ROWS, LANES), lambda bi, hp, i: (bi, hp, i + off, 0)),
                  pl.BlockSpec((1, hg, ta, LANES), lambda bi, hp, i: (bi, hp, 0, 0)),
                  pl.BlockSpec((1, hg // 2, ta, MXU_N), lambda bi, hp, i: (bi, hp, 0, 0)),
                  pl.BlockSpec((1, ROWS, wo), lambda bi, hp, i: (bi, i + off, hp))],
        out_specs=pl.BlockSpec((1, ROWS, wo), lambda bi, hp, i: (bi, i, hp)),
        out_shape=jax.ShapeDtypeStruct((b, nq * ROWS, MLA_W), BF16),
        compiler_params=_params("arbitrary", "arbitrary", "arbitrary"),
        name="mla_attn",
    )(q, k, v, z)


def _outproj_kernel(final, nct, nx, gf_ref, gb_ref, gz_ref, gnw_ref, af_ref, ab_ref, az_ref, anw_ref,
                    ob_ref, m_ref, mod_ref, w_ref, *rest):
    x_refs, rest = rest[:nx], rest[nx:]
    if final:
        fw_ref, o_ref = rest
    else:
        (o_ref,) = rest
    ob = ob_ref[...]

    def gated_head_norm(o, nw_ref, z):
        ms = _dot_sel2(o * o, ob) * (1.0 / GDN_DV)
        return (o * lax.rsqrt(ms + EPS) * nw_ref[...] * _silu(z)).astype(BF16)

    g = gated_head_norm(gf_ref[0] + gb_ref[0], gnw_ref, gz_ref[0])
    a = gated_head_norm(af_ref[0] + ab_ref[0], anw_ref, az_ref[0])
    y = _dot(g, w_ref[0:GDN_W, :])
    y = y + _dot(m_ref[0], w_ref[GDN_W:GDN_W + MLA_W, :])
    y = y + _dot(a, w_ref[GDN_W + MLA_W:, :])
    xn = _token_block(nct, x_refs) + mod_ref[0, 2:3, :] * y
    if final:
        xn = xn * lax.rsqrt(jnp.mean(xn * xn, axis=-1, keepdims=True) + EPS) * fw_ref[...]
    o_ref[0] = xn


def _outproj(gdn_f, gdn_b, gdn_in, gnw, gla_f, gla_b, gla_in, anw, ob, mla_o, xs, mod, w_out,
             nct, final_w):
    assert GDN_DV == GLA_DV and GDN_W == GLA_W
    b, d = xs[0].shape[0], xs[0].shape[2]
    ta = sum(a.shape[1] for a in xs)
    nt = ta // ROWS
    final = final_w is not None
    assert not (final and len(xs) > 1)
    off = nct if final else 0
    moff = off - (ta - mla_o.shape[1]) // ROWS
    nq = nt - off
    row = lambda bi, j: (bi, j + off, 0)
    full2 = lambda bi, j: (0, 0)
    x_specs = [pl.BlockSpec((1, ROWS, d), row)] if len(xs) == 1 else _token_specs(xs, nct, d)
    wide = pl.BlockSpec((1, ROWS, GDN_W), row)
    gz_blk = GDN_QKV // GDN_W
    az_blk = (2 * GLA_QK + GLA_W) // GLA_W
    in_specs = [wide, wide,
                pl.BlockSpec((1, ROWS, GDN_W), lambda bi, j: (bi, j + off, gz_blk)),
                pl.BlockSpec(gnw.shape, full2),
                wide, wide,
                pl.BlockSpec((1, ROWS, GLA_W), lambda bi, j: (bi, j + off, az_blk)),
                pl.BlockSpec(anw.shape, full2),
                pl.BlockSpec(ob.shape, full2),
                pl.BlockSpec((1, ROWS, MLA_W), lambda bi, j: (bi, j + moff, 0)),
                pl.BlockSpec((1, 3, d), lambda bi, j: (jnp.where(j + off < nct, 0, 1 + bi), 0, 0)),
                pl.BlockSpec(w_out.shape, full2)] + x_specs
    args = [gdn_f, gdn_b, gdn_in, gnw, gla_f, gla_b, gla_in, anw, ob, mla_o, mod, w_out, *xs]
    if final:
        in_specs.append(pl.BlockSpec(final_w.shape, full2))
        args.append(final_w)
    return pl.pallas_call(
        functools.partial(_outproj_kernel, final, nct, len(xs)),
        grid=(b, nq),
        in_specs=in_specs,
        out_specs=pl.BlockSpec((1, ROWS, d), lambda bi, j: (bi, j, 0)),
        out_shape=jax.ShapeDtypeStruct((b, nq * ROWS, d), F32),
        compiler_params=_params("arbitrary", "arbitrary"),
        name="out_proj",
    )(*args)


def kernel(x, c, ctx, c_ctx, w_ada, b_ada, w_in, gdn_conv_w, gdn_a_log, gdn_dt_bias, gdn_norm_w,
           mla_q_norm_w, mla_w_uq, mla_kv_norm_w, mla_w_ukv, gla_w_gk, gla_b_gk, gla_norm_w,
           w_out, final_norm_w):
    b, seq, d = x.shape
    n_ctx = ctx.shape[1]
    depth = w_in.shape[0]
    assert n_ctx % ROWS == 0 and seq % ROWS == 0 and seq % ROPE_GRID_W == 0
    nct = n_ctx // ROWS
    nh = GDN_HEADS

    xs = (ctx, x)
    pad_rows = (-(1 + b)) % 8
    cc = jnp.concatenate([c_ctx[None, :], c, jnp.zeros((pad_rows, d), F32)], axis=0)
    mod_all = _ada(cc, w_ada, b_ada).reshape(depth, cc.shape[0], 3, d)

    perm = _inproj_perm()
    qperm = _mla_q_perm()
    kperm, vperm = _mla_kv_perm()
    tab = _rope_tables(n_ctx, seq)
    ob64 = _head_block_ones(GDN_W, GDN_DV)
    exp_m = jnp.asarray(np.stack(
        [_expand_matrix(_S_A + dd * nh, nh, GDN_DK, GDN_QK) for dd in range(2)]
        + [_expand_matrix(_S_B + dd * nh, nh, GDN_DK, GDN_QK) for dd in range(2)]), BF16)
    eqk_np = np.zeros((GLA_QK, GLA_W), np.float32)
    sbd_np = np.zeros((GLA_W, GLA_QK), np.float32)
    for h in range(GLA_HEADS):
        eqk_np[h * GLA_DK:(h + 1) * GLA_DK, h * GLA_DV:(h + 1) * GLA_DV] = 1.0
        sbd_np[h * GLA_DV:(h + 1) * GLA_DV, h * GLA_DK:(h + 1) * GLA_DK] = 1.0
    eqk = jnp.asarray(eqk_np, BF16)
    sbd = jnp.asarray(sbd_np)

    out = None
    for layer in range(depth):
        last = layer == depth - 1
        w_p = _take_cols(w_in[layer], perm).astype(BF16)
        wabt = w_in[layer][:, _O_A:_O_A + 4 * nh].T.astype(BF16)
        mod = mod_all[layer]
        wuq = _take_cols(mla_w_uq[layer], qperm).astype(BF16)
        wuk = _take_cols(mla_w_ukv[layer], kperm).astype(BF16)
        wuv = jnp.take(mla_w_ukv[layer], jnp.asarray(vperm), axis=1).astype(BF16)
        convw = jnp.concatenate(
            [gdn_conv_w[layer], jnp.zeros((8 - GDN_CONV, GDN_QKV), F32)], axis=0)
        gdn_in, mla_z, gla_in, small, abt, qh, kh, vh = _inproj(
            xs, mod, w_p, wabt, tab, mla_q_norm_w[layer][None, :], mla_kv_norm_w[layer][None, :],
            wuq, wuk, wuv, convw, ob64, nct)

        a_flat = gdn_a_log[layer].reshape(-1)
        dt_flat = gdn_dt_bias[layer].reshape(-1)
        prow = jnp.zeros((8, LANES), F32)
        prow = prow.at[0, _S_A:_S_A + 2 * nh].set(a_flat).at[1, _S_A:_S_A + 2 * nh].set(dt_flat)
        pcol = jnp.zeros((4 * nh, LANES), F32)
        pcol = pcol.at[0:2 * nh, 0].set(a_flat).at[0:2 * nh, 1].set(dt_flat)
        gnw = jnp.tile(gdn_norm_w[layer], nh)[None, :]

        anw = jnp.tile(gla_norm_w[layer], GLA_HEADS)[None, :]
        wg = jnp.zeros((2, LANES, GLA_QK), F32)
        for dd in range(2):
            r0 = _S_GLOW + dd * GLA_GATE_RANK
            wg = wg.at[dd, r0:r0 + GLA_GATE_RANK, :].set(gla_w_gk[layer, dd])
        gdn_f, gdn_b, gla_f, gla_b = _scans(
            nct, (gdn_in, small, abt, prow, pcol, exp_m),
            (gla_in, small, wg.astype(BF16), gla_b_gk[layer][:, None, :], eqk, sbd))

        mla_o = _attention(qh, kh, vh, mla_z, nct, with_ctx=not last)

        if last and len(xs) > 1:
            xs = (jnp.concatenate(xs, axis=1),)
        res = _outproj(gdn_f, gdn_b, gdn_in, gnw, gla_f, gla_b, gla_in, anw, ob64, mla_o, xs,
                       mod, w_out[layer].astype(BF16), nct, final_norm_w[None, :] if last else None)
        if last:
            out = res
        else:
            xs = (res,)
    return out
```

```python
import functools
import math

import numpy as np
import jax
import jax.numpy as jnp
from jax import lax
from jax.experimental import pallas as pl
from jax.experimental.pallas import tpu as pltpu

F32 = jnp.float32
BF16 = jnp.bfloat16
EPS = 1e-6

GDN_HEADS, GDN_DK, GDN_DV, GDN_CONV = 4, 64, 64, 5
GDN_QK = GDN_HEADS * GDN_DK
GDN_W = GDN_HEADS * GDN_DV
GDN_QKV = 2 * GDN_QK + GDN_W
MLA_HEADS, MLA_Q_RANK, MLA_KV_RANK = 8, 384, 256
MLA_NOPE, MLA_ROPE, MLA_DV = 64, 32, 64
MLA_W = MLA_HEADS * MLA_DV
MLA_SCALE = (MLA_NOPE + MLA_ROPE) ** -0.5
ROPE_THETA = 10000.0
ROPE_GRID_W = 64
GLA_HEADS, GLA_DK, GLA_DV = 4, 32, 64
GLA_QK = GLA_HEADS * GLA_DK
GLA_W = GLA_HEADS * GLA_DV
GLA_GATE_RANK = 16
GLA_GATE_NORM = 16.0
CHUNK = 64
LEVELS = (32, 16, 8, 4, 2, 1)

LANES = 128
MXU_N = 256
ROWS = 256
HALO = 8
ATTN_HEADS = 8
VMEM_LIMIT = 56 * 1024 * 1024

_O_GDN_QKV, _O_GDN_Z, _O_A, _O_B = 0, 768, 1024, 1032
_O_CQ, _O_CKV, _O_KR, _O_MLA_Z = 1040, 1424, 1680, 1712
_O_GLA_Q, _O_GLA_K, _O_GLA_V, _O_GLA_Z, _O_GLOW = 2224, 2352, 2480, 2736, 2992
_S_A, _S_B, _S_GLOW = 0, 8, 16
_W_GDN, _W_MLA, _W_MLAZ, _W_GLA = 1024, 640, 512, 768
_N_IN_PAD = _W_GDN + _W_MLA + _W_MLAZ + _W_GLA + 2 * LANES


def _dot(a, b):
    return jnp.dot(a, b, preferred_element_type=F32)


def _dot_nt(a, b):
    return lax.dot_general(a, b, (((1,), (1,)), ((), ())), preferred_element_type=F32)


def _dot_tn(a, b):
    return lax.dot_general(a, b, (((0,), (0,)), ((), ())), preferred_element_type=F32)


def _split2(x):
    x1 = x.astype(BF16)
    return x1, (x - x1.astype(F32)).astype(BF16)


def _split3(x):
    x1 = x.astype(BF16)
    r1 = x - x1.astype(F32)
    x2 = r1.astype(BF16)
    x3 = (r1 - x2.astype(F32)).astype(BF16)
    return x1, x2, x3


def _sel_dot(m01, x):
    n = x.shape[1]
    y = _dot(m01, jnp.concatenate(_split3(x), axis=1))
    return y[:, :n] + y[:, n:2 * n] + y[:, 2 * n:]


def _dot_sel2(x, m01):
    n = x.shape[0]
    y = _dot(jnp.concatenate(_split2(x), axis=0), m01)
    return y[:n] + y[n:]


def _dot_sel_nt(x, m01):
    n = x.shape[0]
    y = _dot_nt(jnp.concatenate(_split3(x), axis=0), m01)
    return y[:n] + y[n:2 * n] + y[2 * n:]


def _softplus(x):
    return jnp.maximum(x, 0.0) + jnp.log1p(jnp.exp(-jnp.abs(x)))


def _silu(x):
    return x * jax.nn.sigmoid(x)


def _params(*sem):
    return pltpu.CompilerParams(dimension_semantics=sem, vmem_limit_bytes=VMEM_LIMIT)


def _scan_consts(rev):
    t = np.arange(ROWS)
    ch = t // CHUNK
    p = (CHUNK - 1 - t % CHUNK) if rev else (t % CHUNK)
    same = ch[:, None] == ch[None, :]
    tri = same & (p[None, :] <= p[:, None])
    mq, mk = [], []
    lv = np.full((ROWS, ROWS), -1.0, np.float32)
    for li, s in enumerate(LEVELS):
        blk = p // s
        mq.append(same & (p[None, :] > (blk * s)[:, None]) & (p[None, :] <= p[:, None]))
        mk.append(same & (p[None, :] > p[:, None]) & (p[None, :] <= ((blk + 1) * s)[:, None]))
        pair = same & ((blk % 2) == 1)[:, None] & (blk[None, :] == (blk - 1)[:, None])
        lv[pair] = li
    stack = np.concatenate([tri] + mq[:-1] + mk[:-1], axis=0)
    lvm = np.stack([sum((lv == li)[c * CHUNK:(c + 1) * CHUNK] for c in range(ROWS // CHUNK))
                    for li in range(len(LEVELS))])
    def blocks_side_by_side(m):
        return np.concatenate([m[c * CHUNK:(c + 1) * CHUNK, c * CHUNK:(c + 1) * CHUNK]
                               for c in range(ROWS // CHUNK)], axis=1)

    negm = blocks_side_by_side(np.where(tri, 0.0, -np.inf).astype(np.float32))
    lvc = blocks_side_by_side(lv)
    return dict(tri=jnp.asarray(tri, BF16), ones=jnp.asarray(same, BF16), negm=jnp.asarray(negm),
                lvc=jnp.asarray(lvc), lvm=jnp.asarray(lvm, BF16), stack=jnp.asarray(stack, BF16))


def _chunk_lane_mask():
    nchunk = ROWS // CHUNK
    return jnp.asarray(np.repeat(np.repeat(np.eye(nchunk), CHUNK, axis=0), LANES, axis=1), BF16)


def _head_block_ones(n, width):
    i = np.arange(n)
    return jnp.asarray((i[:, None] // width) == (i[None, :] // width), BF16)


def _inproj_perm():
    perm = np.full((_N_IN_PAD,), -1, np.int64)

    def put(src, n, at):
        perm[at:at + n] = np.arange(src, src + n)

    put(_O_GDN_QKV, GDN_QKV, 0)
    put(_O_GDN_Z, GDN_W, GDN_QKV)
    pos = _W_GDN
    put(_O_CQ, MLA_Q_RANK, pos)
    put(_O_CKV, MLA_KV_RANK, pos + MLA_Q_RANK)
    pos += _W_MLA
    put(_O_MLA_Z, MLA_W, pos)
    pos += _W_MLAZ
    put(_O_GLA_Q, GLA_QK, pos)
    put(_O_GLA_K, GLA_QK, pos + GLA_QK)
    put(_O_GLA_V, GLA_W, pos + 2 * GLA_QK)
    put(_O_GLA_Z, GLA_W, pos + 2 * GLA_QK + GLA_W)
    pos += _W_GLA
    put(_O_A, 2 * GDN_HEADS, pos + _S_A)
    put(_O_B, 2 * GDN_HEADS, pos + _S_B)
    put(_O_GLOW, 2 * GLA_GATE_RANK, pos + _S_GLOW)
    pos += LANES
    put(_O_KR, MLA_ROPE, pos + MLA_NOPE)
    return perm


def _take_cols(w, perm):
    cols = jnp.take(w, jnp.asarray(np.maximum(perm, 0)), axis=1)
    return jnp.where(jnp.asarray(perm >= 0)[None, :], cols, 0.0)


def _mla_q_perm():
    perm = np.full((MLA_HEADS * LANES,), -1, np.int64)
    d = MLA_NOPE + MLA_ROPE
    for h in range(MLA_HEADS):
        perm[h * LANES:h * LANES + d] = np.arange(h * d, (h + 1) * d)
    return perm


def _mla_kv_perm():
    dk = MLA_NOPE + MLA_DV
    kperm = np.full((MLA_HEADS * LANES,), -1, np.int64)
    vperm = np.zeros((MLA_W,), np.int64)
    for h in range(MLA_HEADS):
        kperm[h * LANES:h * LANES + MLA_NOPE] = np.arange(h * dk, h * dk + MLA_NOPE)
        vperm[h * MLA_DV:(h + 1) * MLA_DV] = np.arange(h * dk + MLA_NOPE, (h + 1) * dk)
    return kperm, vperm


def _rope_tables(n_ctx, n_lat):
    rows = n_lat // ROPE_GRID_W
    row = np.repeat(np.arange(rows, dtype=np.float32), ROPE_GRID_W)
    col = np.tile(np.arange(ROPE_GRID_W, dtype=np.float32), rows)
    n_freq = MLA_ROPE // 4
    inv = (ROPE_THETA ** (-np.arange(n_freq, dtype=np.float32) / n_freq)).astype(np.float32)
    ang = np.concatenate([row[:, None] * inv, col[:, None] * inv], axis=-1)
    cos = np.concatenate([np.ones((n_ctx, 2 * n_freq), np.float32), np.cos(ang)], 0)
    sin = np.concatenate([np.zeros((n_ctx, 2 * n_freq), np.float32), np.sin(ang)], 0)
    n = n_ctx + n_lat
    half = MLA_ROPE // 2
    tab = np.zeros((3, n, LANES), np.float32)
    tab[0, :, :MLA_NOPE] = 1.0
    tab[0, :, MLA_NOPE:MLA_NOPE + half] = cos
    tab[0, :, MLA_NOPE + half:MLA_NOPE + MLA_ROPE] = cos
    tab[1, :, MLA_NOPE:MLA_NOPE + half] = -sin
    tab[2, :, MLA_NOPE + half:MLA_NOPE + MLA_ROPE] = sin
    return jnp.asarray(tab)


def _expand_matrix(src0, n_src, width, n_out):
    m = np.zeros((LANES, n_out), np.float32)
    for h in range(n_src):
        m[src0 + h, h * width:(h + 1) * width] = 1.0
    return m


def _ada_kernel(c_ref, w_ref, b_ref, o_ref):
    c = _silu(c_ref[...]).astype(BF16)
    o_ref[0] = _dot(c, w_ref[0].astype(BF16)) + b_ref[0]


def _ada(cc, w_ada, b_ada):
    nl, d, n3 = w_ada.shape
    r = cc.shape[0]
    tn = 1024
    return pl.pallas_call(
        _ada_kernel,
        grid=(nl, n3 // tn),
        in_specs=[pl.BlockSpec((r, d), lambda l, j: (0, 0)),
                  pl.BlockSpec((1, d, tn), lambda l, j: (l, 0, j)),
                  pl.BlockSpec((1, 1, tn), lambda l, j: (l, 0, j))],
        out_specs=pl.BlockSpec((1, r, tn), lambda l, j: (l, 0, j)),
        out_shape=jax.ShapeDtypeStruct((nl, r, n3), F32),
        compiler_params=_params("arbitrary", "arbitrary"),
        name="ada_mod",
    )(cc, w_ada, b_ada.reshape(nl, 1, n3))


def _token_block(nct, refs):
    if len(refs) == 1:
        return refs[0][0]
    return jnp.where(pl.program_id(1) < nct, refs[0][0], refs[1][0])


def _token_specs(arrays, nct, d):
    if len(arrays) == 1:
        return [pl.BlockSpec((1, ROWS, d), lambda bi, j: (bi, j, 0))]
    return [pl.BlockSpec((1, ROWS, d), lambda bi, j: (bi, jnp.minimum(j, nct - 1), 0)),
            pl.BlockSpec((1, ROWS, d), lambda bi, j: (bi, jnp.maximum(j - nct, 0), 0))]


def _halo_specs(arrays, nct, d, after):
    hb = ROWS // HALO
    offs = (0,) if len(arrays) == 1 else (0, nct)

    def spec(a, off):
        last = a.shape[1] // HALO - 1

        def index(bi, j):
            h = (j - off + 1) * hb if after else (j - off) * hb - 1
            return bi, jnp.clip(h, 0, last), 0
        return pl.BlockSpec((1, HALO, d), index)

    return [spec(a, off) for a, off in zip(arrays, offs)]


def _rope(x, tab_ref):
    half = MLA_ROPE // 2
    return (x * tab_ref[0] + pltpu.roll(x, LANES - half, 1) * tab_ref[1]
            + pltpu.roll(x, half, 1) * tab_ref[2])


def _inproj_kernel(nct, nx, *refs):
    x_refs, p_refs, n_refs, refs = refs[:nx], refs[nx:2 * nx], refs[2 * nx:3 * nx], refs[3 * nx:]
    (mod_ref, w_ref, wabt_ref, tab_ref, qw_ref, kvw_ref, wuq_ref, wuk_ref, wuv_ref, convw_ref,
     ob_ref, ogdn, omlaz, ogla, osmall, oabt, q_ref, k_ref, v_ref, xe_scr) = refs
    j = pl.program_id(1)
    nt = pl.num_programs(1)

    def modulated(x):
        h = x * lax.rsqrt(jnp.mean(x * x, axis=-1, keepdims=True) + EPS)
        return (h * (1.0 + mod_ref[0, 1:2, :]) + mod_ref[0, 0:1, :]).astype(BF16)

    hb = modulated(_token_block(nct, x_refs))
    pos = 0
    proj = {}
    for name, ref, n in (("gdn", None, _W_GDN), ("mla", None, _W_MLA), ("mlaz", omlaz, _W_MLAZ),
                         ("gla", ogla, _W_GLA), ("small", osmall, LANES), ("kr", None, LANES)):
        y = _dot(hb, w_ref[:, pos:pos + n])
        if ref is None:
            proj[name] = y
        else:
            ref[0] = y
        pos += n
    oabt[0] = _dot_nt(wabt_ref[...], hb)

    halo = jnp.concatenate([_token_block(nct, p_refs), _token_block(nct, n_refs)], axis=0)
    yh = _dot(modulated(halo), w_ref[:, 0:GDN_QKV])
    has_prev = jnp.logical_and(j != 0, j != nct)
    has_next = jnp.logical_and(j != nct - 1, j != nt - 1)
    xe_scr[0:HALO, :] = jnp.where(has_prev, yh[:HALO], 0.0)
    xe_scr[HALO:HALO + ROWS, :] = proj["gdn"][:, :GDN_QKV]
    xe_scr[HALO + ROWS:, :] = jnp.where(has_next, yh[HALO:], 0.0)
    pad = (GDN_CONV - 1) // 2
    conv = jnp.zeros((ROWS, GDN_QKV), F32)
    for t in range(GDN_CONV):
        conv = conv + convw_ref[t:t + 1, :] * xe_scr[pl.ds(HALO - pad + t, ROWS), :]
    hqkv = _silu(conv)
    ob = ob_ref[...]
    qn = hqkv[:, :GDN_QK]
    kn = hqkv[:, GDN_QK:2 * GDN_QK]
    qn = qn * lax.rsqrt(_dot_sel2(qn * qn, ob) + EPS) * (GDN_DK ** -0.5)
    kn = kn * lax.rsqrt(_dot_sel2(kn * kn, ob) + EPS)
    ogdn[0] = jnp.concatenate([qn, kn, hqkv[:, 2 * GDN_QK:], proj["gdn"][:, GDN_QKV:]], axis=1)

    cq = proj["mla"][:, :MLA_Q_RANK]
    ckv = proj["mla"][:, MLA_Q_RANK:]
    cq = cq * lax.rsqrt(jnp.mean(cq * cq, axis=-1, keepdims=True) + EPS) * qw_ref[...]
    ckv = ckv * lax.rsqrt(jnp.mean(ckv * ckv, axis=-1, keepdims=True) + EPS) * kvw_ref[...]
    cqb = cq.astype(BF16)
    ckvb = ckv.astype(BF16)
    qf = _dot(cqb, wuq_ref[...])
    kf = _dot(ckvb, wuk_ref[...])
    vf = _dot(ckvb, wuv_ref[...])
    kr = _rope(proj["kr"], tab_ref)
    q_scale = MLA_SCALE * math.log2(math.e)
    for hd in range(MLA_HEADS):
        sl = slice(hd * LANES, (hd + 1) * LANES)
        q_ref[0, hd] = (_rope(qf[:, sl], tab_ref) * q_scale).astype(BF16)
        k_ref[0, hd] = (kf[:, sl] + kr).astype(BF16)
    ones = jnp.ones((ROWS, MXU_N - LANES), BF16)
    for hp in range(MLA_HEADS // 2):
        v_ref[0, hp] = jnp.concatenate(
            [vf[:, hp * LANES:(hp + 1) * LANES].astype(BF16), ones], axis=1)


def _inproj(xs, mod, w_p, wabt, tab, qw, kvw, wuq, wuk, wuv, convw, ob, nct):
    b, d = xs[0].shape[0], xs[0].shape[2]
    ta = sum(a.shape[1] for a in xs)
    nt = ta // ROWS
    widths = (_W_GDN, _W_MLAZ, _W_GLA, LANES)
    row = lambda bi, j: (bi, j, 0)
    full2 = lambda bi, j: (0, 0)
    hm = lambda bi, j: (bi, 0, j, 0)
    consts = (w_p, wabt)
    mla_consts = (qw, kvw, wuq, wuk, wuv, convw, ob)
    return pl.pallas_call(
        functools.partial(_inproj_kernel, nct, len(xs)),
        grid=(b, nt),
        in_specs=_token_specs(xs, nct, d) + _halo_specs(xs, nct, d, False)
        + _halo_specs(xs, nct, d, True)
        + [pl.BlockSpec((1, 3, d), lambda bi, j: (jnp.where(j < nct, 0, 1 + bi), 0, 0))]
        + [pl.BlockSpec(a.shape, full2) for a in consts]
        + [pl.BlockSpec((3, ROWS, LANES), lambda bi, j: (0, j, 0))]
        + [pl.BlockSpec(a.shape, full2) for a in mla_consts],
        out_specs=[pl.BlockSpec((1, ROWS, n), row) for n in widths]
        + [pl.BlockSpec((1, 4 * GDN_HEADS, ROWS), lambda bi, j: (bi, 0, j)),
           pl.BlockSpec((1, MLA_HEADS, ROWS, LANES), hm),
           pl.BlockSpec((1, MLA_HEADS, ROWS, LANES), hm),
           pl.BlockSpec((1, MLA_HEADS // 2, ROWS, MXU_N), hm)],
        out_shape=[jax.ShapeDtypeStruct((b, ta, n), F32) for n in widths]
        + [jax.ShapeDtypeStruct((b, 4 * GDN_HEADS, ta), F32),
           jax.ShapeDtypeStruct((b, MLA_HEADS, ta, LANES), BF16),
           jax.ShapeDtypeStruct((b, MLA_HEADS, ta, LANES), BF16),
           jax.ShapeDtypeStruct((b, MLA_HEADS // 2, ta, MXU_N), BF16)],
        scratch_shapes=[pltpu.VMEM((ROWS + 2 * HALO, GDN_QKV), F32)],
        compiler_params=_params("arbitrary", "arbitrary"),
        name="in_proj",
    )(*xs, *xs, *xs, mod, *consts, tab, *mla_consts)


def _scan_block_index(j, nct, nt, rev):
    if not rev:
        return j
    return jnp.where(j < nct, nct - 1 - j, nt - 1 - (j - nct))


def _head_lane_mask(n, width, h):
    lane = lax.broadcasted_iota(jnp.int32, (1, n), 1)
    return (lane >= h * width) & (lane < (h + 1) * width)


def _gdn_prep(d, x_ref, small_ref, abt_ref, prow_ref, pcol_ref, tri, exp_ref):
    q = x_ref[0, :, :GDN_QK]
    k = x_ref[0, :, GDN_QK:2 * GDN_QK]
    v = x_ref[0, :, 2 * GDN_QK:GDN_QKV]
    sm = small_ref[0]
    g_all = -jnp.exp(prow_ref[0:1, :]) * _softplus(sm + prow_ref[1:2, :])
    beta_all = jax.nn.sigmoid(sm)
    gc_all = _sel_dot(tri, g_all)
    g_t = -jnp.exp(pcol_ref[:, 0:1]) * _softplus(abt_ref[0] + pcol_ref[:, 1:2])
    gc_t = _dot_sel_nt(g_t, tri)
    gc_w = _dot_sel2(gc_all, exp_ref[d])
    beta_w = _dot_sel2(beta_all, exp_ref[2 + d])
    last = 0 if d == 1 else CHUNK - 1

    def chunk_last(x):
        return jnp.concatenate(
            [jnp.broadcast_to(x[c * CHUNK + last:c * CHUNK + last + 1, :], (CHUNK, x.shape[1]))
             for c in range(ROWS // CHUNK)], axis=0)

    gl_all = chunk_last(gc_all)
    gl_w = chunk_last(gc_w)
    kb = k * beta_w
    return dict(q=q, kbf=k.astype(BF16), kb=kb, vb=v * beta_w, kbg=kb * jnp.exp(gc_w),
                qg=q * jnp.exp(gc_w), kdec=k * jnp.exp(gl_w - gc_w),
                gc_all=gc_all, gc_t=gc_t, gl_all=gl_all)


def _gdn_stages(xf_ref, smf_ref, abtf_ref, xb_ref, smb_ref, abtb_ref, prow_ref, pcol_ref, tri_ref,
                ones_ref, negm_ref, lvm_ref, exp_ref, of_ref, obk_ref, s_scr):
    nh, dk = GDN_HEADS, GDN_DK
    nchunk = ROWS // CHUNK
    nlev = len(LEVELS)
    ones = ones_ref[...]
    dir_refs = ((xf_ref, smf_ref, abtf_ref), (xb_ref, smb_ref, abtb_ref))
    prep = []
    for d in range(2):
        prep.append(_gdn_prep(d, *dir_refs[d], prow_ref, pcol_ref, tri_ref[d], exp_ref))
        yield
    ri = lax.broadcasted_iota(jnp.int32, (ROWS, ROWS), 0)
    ci = lax.broadcasted_iota(jnp.int32, (ROWS, ROWS), 1)
    eye = (ri == ci).astype(BF16)
    chains = [(d, h) for d in range(2) for h in range(nh)]
    nc = len(chains)

    def compact(m):
        return functools.reduce(lambda a, b: a + b,
                                [m[c * CHUNK:(c + 1) * CHUNK] for c in range(nchunk)])

    def expand(mc):
        return jnp.concatenate([mc] * nchunk, axis=0) * ones

    def side_by_side(x):
        return jnp.concatenate([x[c * CHUNK:(c + 1) * CHUNK] for c in range(nchunk)], axis=1)

    low_c, a_intra = [], []
    for d, h in chains:
        p = prep[d]
        idx = d * nh + h
        sl = slice(h * dk, (h + 1) * dk)
        gc_col = side_by_side(jnp.broadcast_to(p["gc_all"][:, idx:idx + 1], (ROWS, CHUNK)))
        decay = jnp.exp(gc_col - p["gc_t"][idx:idx + 1, :] + negm_ref[d])
        lhs = jnp.concatenate([side_by_side(p["kb"][:, sl]), side_by_side(p["q"][:, sl])], axis=0)
        kt = jnp.concatenate([p["kbf"][:, sl]] * nchunk, axis=1) * ones
        kq = _dot_nt(lhs.astype(BF16), kt)
        low_c.append((kq[:CHUNK] * decay).astype(BF16))
        a_intra.append(expand((kq[CHUNK:] * decay).astype(BF16)))
        if h == nh - 1:
            yield

    eye_c = compact(eye)
    t_c = [eye_c - low_c[i] * lvm_ref[d, nlev - 1] for i, (d, h) in enumerate(chains)]
    t_inv = [expand(t) for t in t_c]
    for li in reversed(range(nlev - 1)):
        ys = [expand(_dot(low_c[i] * lvm_ref[d, li], t_inv[i]).astype(BF16))
              for i, (d, h) in enumerate(chains)]
        t_c = [t_c[i] - _dot(t_c[i], ys[i]).astype(BF16) for i in range(nc)]
        t_inv = [expand(t) for t in t_c]
        yield
    uw = []
    for i, (d, h) in enumerate(chains):
        sl = slice(h * dk, (h + 1) * dk)
        rhs = jnp.concatenate([prep[d]["vb"][:, sl], prep[d]["kbg"][:, sl]], axis=1)
        uw.append(_dot(t_inv[i], rhs.astype(BF16)))

    q2, ou, gb = [], [], []
    for i, (d, h) in enumerate(chains):
        sl = slice(h * dk, (h + 1) * dk)
        uwb = uw[i].astype(BF16)
        auw = _dot(a_intra[i], uwb)
        ou.append(auw[:, :GDN_DV])
        q2.append((prep[d]["qg"][:, sl] - auw[:, GDN_DV:]).astype(BF16))
        kd = prep[d]["kdec"][:, sl].astype(BF16)
        gb.append([_dot_tn(kd[c * CHUNK:(c + 1) * CHUNK], uwb[c * CHUNK:(c + 1) * CHUNK])
                   for c in range(nchunk)])
    yield
    states = [s_scr[i] for i in range(nc)]
    o_parts = [[None] * nchunk for _ in chains]
    for step in range(nchunk):
        for i, (d, h) in enumerate(chains):
            c = nchunk - 1 - step if d == 1 else step
            idx = d * nh + h
            r0 = c * CHUNK
            rs = slice(r0, r0 + CHUNK)
            sb = states[i].astype(BF16)
            o_parts[i][c] = _dot(q2[i][rs], sb) + ou[i][rs]
            states[i] = (states[i] * jnp.exp(prep[d]["gl_all"][r0:r0 + 1, idx:idx + 1])
                         - _dot(gb[i][c][:, GDN_DV:].astype(BF16), sb) + gb[i][c][:, :GDN_DV])
        yield
    for i in range(len(chains)):
        s_scr[i] = states[i]
    outs = [jnp.concatenate(o_parts[i], axis=0) for i in range(len(chains))]
    of_ref[0] = jnp.concatenate(outs[:nh], axis=1)
    obk_ref[0] = jnp.concatenate(outs[nh:], axis=1)


def _scan_row_specs(nct, nt, rev, width):
    blk_of = functools.partial(_scan_block_index, nct=nct, nt=nt, rev=rev)
    return pl.BlockSpec((1, ROWS, width), lambda bi, j: (bi, blk_of(j), 0))


def _gdn_specs(nct, nt, gdn_in, small, abt, prow, pcol, exp_m):
    cf, cb = _scan_consts(False), _scan_consts(True)
    stack = lambda name: jnp.stack([cf[name], cb[name]])
    full2 = lambda bi, j: (0, 0)
    full3 = lambda bi, j: (0, 0, 0)

    def dir_specs(rev):
        blk_of = functools.partial(_scan_block_index, nct=nct, nt=nt, rev=rev)
        return [
            _scan_row_specs(nct, nt, rev, GDN_QKV),
            _scan_row_specs(nct, nt, rev, LANES),
            pl.BlockSpec((1, 4 * GDN_HEADS, ROWS), lambda bi, j: (bi, 0, blk_of(j))),
        ]

    in_specs = dir_specs(False) + dir_specs(True) + [
        pl.BlockSpec(prow.shape, full2),
        pl.BlockSpec(pcol.shape, full2),
        pl.BlockSpec((2, ROWS, ROWS), full3),
        pl.BlockSpec((ROWS, ROWS), full2),
        pl.BlockSpec((2, CHUNK, ROWS), full3),
        pl.BlockSpec((2, len(LEVELS), CHUNK, ROWS), lambda bi, j: (0, 0, 0, 0)),
        pl.BlockSpec(exp_m.shape, full3),
    ]
    dir_args = [gdn_in, small, abt]
    args = dir_args + dir_args + [prow, pcol, stack("tri"), cf["ones"], stack("negm"),
                                  stack("lvm"), exp_m]
    out_specs = [_scan_row_specs(nct, nt, False, GDN_W), _scan_row_specs(nct, nt, True, GDN_W)]
    scratch = [pltpu.VMEM((2 * GDN_HEADS, GDN_DK, GDN_DV), F32)]
    return in_specs, args, out_specs, scratch


def _gla_stages(xf_ref, smf_ref, xb_ref, smb_ref, wg_ref, bg_ref, stack_ref, lvc_ref, kmask_ref,
                same_ref, eqk_ref, sbd_ref, of_ref, obk_ref, s_scr):
    nh, dk, dv = GLA_HEADS, GLA_DK, GLA_DV
    nchunk = ROWS // CHUNK
    nlev = len(LEVELS)
    n = GLA_QK
    out_head = [_head_lane_mask(GLA_W, dv, h) for h in range(nh)]
    eqk = eqk_ref[...]
    sbd = sbd_ref[...]
    x_refs, sm_refs, o_refs = (xf_ref, xb_ref), (smf_ref, smb_ref), (of_ref, obk_ref)
    dirs = range(2)
    xs = [x_refs[d][0] for d in dirs]
    q = [xs[d][:, :GLA_QK] * (dk ** -0.5) for d in dirs]
    k = [xs[d][:, GLA_QK:2 * GLA_QK] for d in dirs]
    v = [xs[d][:, 2 * GLA_QK:2 * GLA_QK + GLA_W] for d in dirs]
    vb = [v[d].astype(BF16) for d in dirs]
    gk = [_dot(sm_refs[d][0].astype(BF16), wg_ref[d]) + bg_ref[d] for d in dirs]
    la = [-_softplus(-gk[d]) * (1.0 / GLA_GATE_NORM) for d in dirs]
    la2 = [jnp.concatenate(_split2(la[d]), axis=1) for d in dirs]
    ys = [_dot(stack_ref[d], la2[d]) for d in dirs]
    cums = [ys[d][:, :n] + ys[d][:, n:] for d in dirs]
    piece = lambda d, i: cums[d][i * ROWS:(i + 1) * ROWS]
    bcum = [piece(d, 0) for d in dirs]
    last_row = [0 if d == 1 else CHUNK - 1 for d in dirs]
    blast = [jnp.concatenate(
        [jnp.broadcast_to(bcum[d][c * CHUNK + last_row[d]:c * CHUNK + last_row[d] + 1, :], (CHUNK, n))
         for c in range(nchunk)], axis=0) for d in dirs]
    la_next = [pltpu.roll(la[d], 1 if d == 1 else ROWS - 1, 0) for d in dirs]
    qg = [(q[d] * jnp.exp(bcum[d])).astype(BF16) for d in dirs]
    kdec = [(k[d] * jnp.exp(blast[d] - bcum[d])).astype(BF16) for d in dirs]
    yield

    lane4 = lax.broadcasted_iota(jnp.int32, (1, nchunk * n), 1) % n
    lane_head4 = [(lane4 >= h * dk) & (lane4 < (h + 1) * dk) for h in range(nh)]
    kmask = kmask_ref[...]
    same = same_ref[...]
    lv4 = [jnp.concatenate([lvc_ref[d]] * nh, axis=0) for d in dirs]
    acc = [jnp.zeros((nh * CHUNK, ROWS), F32) for d in dirs]
    for li in range(nlev):
        if li < nlev - 1:
            ql = [q[d] * jnp.exp(piece(d, 1 + li)) for d in dirs]
            kl = [(k[d] * jnp.exp(piece(d, nlev + li))).astype(BF16) for d in dirs]
        else:
            ql = q
            kl = [(k[d] * jnp.exp(la_next[d])).astype(BF16) for d in dirs]
        qc = [jnp.concatenate([ql[d][c * CHUNK:(c + 1) * CHUNK] for c in range(nchunk)], axis=1)
              for d in dirs]
        qs = [jnp.concatenate([jnp.where(lane_head4[h], qc[d], 0.0) for h in range(nh)],
                              axis=0).astype(BF16) for d in dirs]
        kt = [jnp.concatenate([kl[d]] * nchunk, axis=1) * kmask for d in dirs]
        ps = [_dot_nt(qs[d], kt[d]) for d in dirs]
        acc = [jnp.where(lv4[d] == float(li), ps[d], acc[d]) for d in dirs]
        yield
    accb = [acc[d].astype(BF16) for d in dirs]
    acc_bd = [jnp.concatenate(
        [jnp.concatenate([accb[d][h * CHUNK:(h + 1) * CHUNK]] * nchunk, axis=0) * same
         for h in range(nh)], axis=0) for d in dirs]
    pv = [_dot(acc_bd[d], vb[d]) for d in dirs]
    o = [_dot_sel2(q[d] * k[d], eqk) * v[d] for d in dirs]
    for h in range(nh):
        o = [o[d] + jnp.where(out_head[h], pv[d][h * ROWS:(h + 1) * ROWS], 0.0) for d in dirs]
    yield

    states = [s_scr[d] for d in dirs]
    o_parts = [[None] * nchunk for d in dirs]
    for step in range(nchunk):
        for d in dirs:
            c = nchunk - 1 - step if d == 1 else step
            rs = slice(c * CHUNK, (c + 1) * CHUNK)
            o_parts[d][c] = _dot_nt(qg[d][rs], states[d].astype(BF16))
            states[d] = (states[d] * jnp.exp(blast[d][c * CHUNK:c * CHUNK + 1, :])
                         + sbd * _dot_tn(vb[d][rs], kdec[d][rs]))
        yield
    for d in dirs:
        s_scr[d] = states[d]
        o_refs[d][0] = o[d] + jnp.concatenate(o_parts[d], axis=0)


def _gla_specs(nct, nt, gla_in, small, wg, bg, eqk, sbd):
    cf, cb = _scan_consts(False), _scan_consts(True)
    stack = jnp.stack([cf["stack"], cb["stack"]])
    lvc = jnp.stack([cf["lvc"], cb["lvc"]])
    assert GLA_QK == LANES
    kmask = _chunk_lane_mask()
    full2 = lambda bi, j: (0, 0)
    full3 = lambda bi, j: (0, 0, 0)
    in_specs = [
        _scan_row_specs(nct, nt, False, _W_GLA), _scan_row_specs(nct, nt, False, LANES),
        _scan_row_specs(nct, nt, True, _W_GLA), _scan_row_specs(nct, nt, True, LANES),
        pl.BlockSpec(wg.shape, full3),
        pl.BlockSpec(bg.shape, full3),
        pl.BlockSpec(stack.shape, full3),
        pl.BlockSpec(lvc.shape, full3),
        pl.BlockSpec(kmask.shape, full2),
        pl.BlockSpec((ROWS, ROWS), full2),
        pl.BlockSpec(eqk.shape, full2),
        pl.BlockSpec(sbd.shape, full2),
    ]
    args = [gla_in, small, gla_in, small, wg, bg, stack, lvc, kmask, cf["ones"], eqk, sbd]
    out_specs = [_scan_row_specs(nct, nt, False, GLA_W), _scan_row_specs(nct, nt, True, GLA_W)]
    scratch = [pltpu.VMEM((2, GLA_W, GLA_QK), F32)]
    return in_specs, args, out_specs, scratch


_N_GDN_IN, _N_GLA_IN = 13, 12


def _scan_kernel(*refs):
    gdn_in, refs = refs[:_N_GDN_IN], refs[_N_GDN_IN:]
    gla_in, refs = refs[:_N_GLA_IN], refs[_N_GLA_IN:]
    gdn_out, gla_out, (gdn_state, gla_state) = refs[:2], refs[2:4], refs[4:]

    @pl.when(pl.program_id(1) == 0)
    def _():
        gdn_state[...] = jnp.zeros_like(gdn_state)
        gla_state[...] = jnp.zeros_like(gla_state)

    stages = [_gdn_stages(*gdn_in, *gdn_out, gdn_state),
              _gla_stages(*gla_in, *gla_out, gla_state)]
    while stages:
        for s in list(stages):
            if next(s, StopIteration) is StopIteration:
                stages.remove(s)


def _scans(nct, gdn_args, gla_args):
    gdn_in = gdn_args[0]
    b, ta, _ = gdn_in.shape
    nt = ta // ROWS
    g_in, g_args, g_out, g_scr = _gdn_specs(nct, nt, *gdn_args)
    a_in, a_args, a_out, a_scr = _gla_specs(nct, nt, *gla_args)
    assert len(g_in) == _N_GDN_IN and len(a_in) == _N_GLA_IN
    return pl.pallas_call(
        _scan_kernel,
        grid=(b, nt),
        in_specs=g_in + a_in,
        out_specs=g_out + a_out,
        out_shape=[jax.ShapeDtypeStruct((b, ta, GDN_W), F32)] * 2
        + [jax.ShapeDtypeStruct((b, ta, GLA_W), F32)] * 2,
        scratch_shapes=g_scr + a_scr,
        compiler_params=_params("arbitrary", "arbitrary"),
        name="gdn_gla_scan",
    )(*g_args, *a_args)


def _attn_kernel(nct, off, n_ctx, q_ref, k_ref, v_ref, z_ref, o_ref):
    i = pl.program_id(2) + off

    def body(nk):
        parts = []
        half = ROWS // 2

        def by_row_halves(dot, lhs, rhs):
            return jnp.concatenate([dot(lhs[:half], rhs), dot(lhs[half:], rhs)], axis=0)

        for hh in range(ATTN_HEADS):
            q = q_ref[0, hh]
            kk = k_ref[0, hh, :nk, :]
            s = by_row_halves(_dot_nt, q, kk) if hh == 0 else _dot_nt(q, kk)
            m = jnp.max(s, axis=-1, keepdims=True)
            p = jnp.exp2(s - m).astype(BF16)
            vv = v_ref[0, hh // 2, :nk, :]
            pv = by_row_halves(_dot, p, vv) if hh == ATTN_HEADS - 1 else _dot(p, vv)
            o = pv[:, (hh % 2) * MLA_DV:(hh % 2 + 1) * MLA_DV] / pv[:, LANES:LANES + 1]
            parts.append(o)
        o_ref[0] = (jnp.concatenate(parts, axis=1) * _silu(z_ref[0])).astype(BF16)

    if off == 0:
        @pl.when(i < nct)
        def _():
            body(n_ctx)

        @pl.when(i >= nct)
        def _():
            body(k_ref.shape[2])
    else:
        body(k_ref.shape[2])


def _attention(q, k, v, z, nct, with_ctx):
    b, nh, ta, _ = q.shape
    nt = ta // ROWS
    off = 0 if with_ctx else nct
    nq = nt - off
    hg = ATTN_HEADS
    wo = hg * MLA_DV
    return pl.pallas_call(
        functools.partial(_attn_kernel, nct, off, nct * ROWS),
        grid=(b, nh // hg, nq),
        in_specs=[pl.BlockSpec((1, hg, ROWS, LANES), lambda bi, hp, i: (bi, hp, i + off, 0)),
                  pl.BlockSpec((1, hg, ta, LANES), lambda bi, hp, i: (bi, hp, 0, 0)),
                  pl.BlockSpec((1, hg // 2, ta, MXU_N), lambda bi, hp, i: (bi, hp, 0, 0)),
                  pl.BlockSpec((1, ROWS, wo), lambda bi, hp, i: (bi, i + off, hp))],
        out_specs=pl.BlockSpec((1, ROWS, wo), lambda bi, hp, i: (bi, i, hp)),
        out_shape=jax.ShapeDtypeStruct((b, nq * ROWS, MLA_W), BF16),
        compiler_params=_params("arbitrary", "arbitrary", "arbitrary"),
        name="mla_attn",
    )(q, k, v, z)


def _outproj_kernel(final, nct, nx, gf_ref, gb_ref, gz_ref, gnw_ref, af_ref, ab_ref, az_ref, anw_ref,
                    ob_ref, m_ref, mod_ref, w_ref, *rest):
    x_refs, rest = rest[:nx], rest[nx:]
    if final:
        fw_ref, o_ref = rest
    else:
        (o_ref,) = rest
    ob = ob_ref[...]

    def gated_head_norm(o, nw_ref, z):
        ms = _dot_sel2(o * o, ob) * (1.0 / GDN_DV)
        return (o * lax.rsqrt(ms + EPS) * nw_ref[...] * _silu(z)).astype(BF16)

    g = gated_head_norm(gf_ref[0] + gb_ref[0], gnw_ref, gz_ref[0])
    a = gated_head_norm(af_ref[0] + ab_ref[0], anw_ref, az_ref[0])
    y = _dot(g, w_ref[0:GDN_W, :])
    y = y + _dot(m_ref[0], w_ref[GDN_W:GDN_W + MLA_W, :])
    y = y + _dot(a, w_ref[GDN_W + MLA_W:, :])
    xn = _token_block(nct, x_refs) + mod_ref[0, 2:3, :] * y
    if final:
        xn = xn * lax.rsqrt(jnp.mean(xn * xn, axis=-1, keepdims=True) + EPS) * fw_ref[...]
    o_ref[0] = xn


def _outproj(gdn_f, gdn_b, gdn_in, gnw, gla_f, gla_b, gla_in, anw, ob, mla_o, xs, mod, w_out,
             nct, final_w):
    assert GDN_DV == GLA_DV and GDN_W == GLA_W
    b, d = xs[0].shape[0], xs[0].shape[2]
    ta = sum(a.shape[1] for a in xs)
    nt = ta // ROWS
    final = final_w is not None
    assert not (final and len(xs) > 1)
    off = nct if final else 0
    moff = off - (ta - mla_o.shape[1]) // ROWS
    nq = nt - off
    row = lambda bi, j: (bi, j + off, 0)
    full2 = lambda bi, j: (0, 0)
    x_specs = [pl.BlockSpec((1, ROWS, d), row)] if len(xs) == 1 else _token_specs(xs, nct, d)
    wide = pl.BlockSpec((1, ROWS, GDN_W), row)
    gz_blk = GDN_QKV // GDN_W
    az_blk = (2 * GLA_QK + GLA_W) // GLA_W
    in_specs = [wide, wide,
                pl.BlockSpec((1, ROWS, GDN_W), lambda bi, j: (bi, j + off, gz_blk)),
                pl.BlockSpec(gnw.shape, full2),
                wide, wide,
                pl.BlockSpec((1, ROWS, GLA_W), lambda bi, j: (bi, j + off, az_blk)),
                pl.BlockSpec(anw.shape, full2),
                pl.BlockSpec(ob.shape, full2),
                pl.BlockSpec((1, ROWS, MLA_W), lambda bi, j: (bi, j + moff, 0)),
                pl.BlockSpec((1, 3, d), lambda bi, j: (jnp.where(j + off < nct, 0, 1 + bi), 0, 0)),
                pl.BlockSpec(w_out.shape, full2)] + x_specs
    args = [gdn_f, gdn_b, gdn_in, gnw, gla_f, gla_b, gla_in, anw, ob, mla_o, mod, w_out, *xs]
    if final:
        in_specs.append(pl.BlockSpec(final_w.shape, full2))
        args.append(final_w)
    return pl.pallas_call(
        functools.partial(_outproj_kernel, final, nct, len(xs)),
        grid=(b, nq),
        in_specs=in_specs,
        out_specs=pl.BlockSpec((1, ROWS, d), lambda bi, j: (bi, j, 0)),
        out_shape=jax.ShapeDtypeStruct((b, nq * ROWS, d), F32),
        compiler_params=_params("arbitrary", "arbitrary"),
        name="out_proj",
    )(*args)


def kernel(x, c, ctx, c_ctx, w_ada, b_ada, w_in, gdn_conv_w, gdn_a_log, gdn_dt_bias, gdn_norm_w,
           mla_q_norm_w, mla_w_uq, mla_kv_norm_w, mla_w_ukv, gla_w_gk, gla_b_gk, gla_norm_w,
           w_out, final_norm_w):
    b, seq, d = x.shape
    n_ctx = ctx.shape[1]
    depth = w_in.shape[0]
    assert n_ctx % ROWS == 0 and seq % ROWS == 0 and seq % ROPE_GRID_W == 0
    nct = n_ctx // ROWS
    nh = GDN_HEADS

    xs = (ctx, x)
    pad_rows = (-(1 + b)) % 8
    cc = jnp.concatenate([c_ctx[None, :], c, jnp.zeros((pad_rows, d), F32)], axis=0)
    mod_all = _ada(cc, w_ada, b_ada).reshape(depth, cc.shape[0], 3, d)

    perm = _inproj_perm()
    qperm = _mla_q_perm()
    kperm, vperm = _mla_kv_perm()
    tab = _rope_tables(n_ctx, seq)
    ob64 = _head_block_ones(GDN_W, GDN_DV)
    exp_m = jnp.asarray(np.stack(
        [_expand_matrix(_S_A + dd * nh, nh, GDN_DK, GDN_QK) for dd in range(2)]
        + [_expand_matrix(_S_B + dd * nh, nh, GDN_DK, GDN_QK) for dd in range(2)]), BF16)
    eqk_np = np.zeros((GLA_QK, GLA_W), np.float32)
    sbd_np = np.zeros((GLA_W, GLA_QK), np.float32)
    for h in range(GLA_HEADS):
        eqk_np[h * GLA_DK:(h + 1) * GLA_DK, h * GLA_DV:(h + 1) * GLA_DV] = 1.0
        sbd_np[h * GLA_DV:(h + 1) * GLA_DV, h * GLA_DK:(h + 1) * GLA_DK] = 1.0
    eqk = jnp.asarray(eqk_np, BF16)
    sbd = jnp.asarray(sbd_np)

    out = None
    for layer in range(depth):
        last = layer == depth - 1
        w_p = _take_cols(w_in[layer], perm).astype(BF16)
        wabt = w_in[layer][:, _O_A:_O_A + 4 * nh].T.astype(BF16)
        mod = mod_all[layer]
        wuq = _take_cols(mla_w_uq[layer], qperm).astype(BF16)
        wuk = _take_cols(mla_w_ukv[layer], kperm).astype(BF16)
        wuv = jnp.take(mla_w_ukv[layer], jnp.asarray(vperm), axis=1).astype(BF16)
        convw = jnp.concatenate(
            [gdn_conv_w[layer], jnp.zeros((8 - GDN_CONV, GDN_QKV), F32)], axis=0)
        gdn_in, mla_z, gla_in, small, abt, qh, kh, vh = _inproj(
            xs, mod, w_p, wabt, tab, mla_q_norm_w[layer][None, :], mla_kv_norm_w[layer][None, :],
            wuq, wuk, wuv, convw, ob64, nct)

        a_flat = gdn_a_log[layer].reshape(-1)
        dt_flat = gdn_dt_bias[layer].reshape(-1)
        prow = jnp.zeros((8, LANES), F32)
        prow = prow.at[0, _S_A:_S_A + 2 * nh].set(a_flat).at[1, _S_A:_S_A + 2 * nh].set(dt_flat)
        pcol = jnp.zeros((4 * nh, LANES), F32)
        pcol = pcol.at[0:2 * nh, 0].set(a_flat).at[0:2 * nh, 1].set(dt_flat)
        gnw = jnp.tile(gdn_norm_w[layer], nh)[None, :]

        anw = jnp.tile(gla_norm_w[layer], GLA_HEADS)[None, :]
        wg = jnp.zeros((2, LANES, GLA_QK), F32)
        for dd in range(2):
            r0 = _S_GLOW + dd * GLA_GATE_RANK
            wg = wg.at[dd, r0:r0 + GLA_GATE_RANK, :].set(gla_w_gk[layer, dd])
        gdn_f, gdn_b, gla_f, gla_b = _scans(
            nct, (gdn_in, small, abt, prow, pcol, exp_m),
            (gla_in, small, wg.astype(BF16), gla_b_gk[layer][:, None, :], eqk, sbd))

        mla_o = _attention(qh, kh, vh, mla_z, nct, with_ctx=not last)

        if last and len(xs) > 1:
            xs = (jnp.concatenate(xs, axis=1),)
        res = _outproj(gdn_f, gdn_b, gdn_in, gnw, gla_f, gla_b, gla_in, anw, ob64, mla_o, xs,
                       mod, w_out[layer].astype(BF16), nct, final_norm_w[None, :] if last else None)
        if last:
            out = res
        else:
            xs = (res,)
    return out
```

```python
import functools
import math

import numpy as np
import jax
import jax.numpy as jnp
from jax import lax
from jax.experimental import pallas as pl
from jax.experimental.pallas import tpu as pltpu

F32 = jnp.float32
BF16 = jnp.bfloat16
EPS = 1e-6

GDN_HEADS, GDN_DK, GDN_DV, GDN_CONV = 4, 64, 64, 5
GDN_QK = GDN_HEADS * GDN_DK
GDN_W = GDN_HEADS * GDN_DV
GDN_QKV = 2 * GDN_QK + GDN_W
MLA_HEADS, MLA_Q_RANK, MLA_KV_RANK = 8, 384, 256
MLA_NOPE, MLA_ROPE, MLA_DV = 64, 32, 64
MLA_W = MLA_HEADS * MLA_DV
MLA_SCALE = (MLA_NOPE + MLA_ROPE) ** -0.5
ROPE_THETA = 10000.0
ROPE_GRID_W = 64
GLA_HEADS, GLA_DK, GLA_DV = 4, 32, 64
GLA_QK = GLA_HEADS * GLA_DK
GLA_W = GLA_HEADS * GLA_DV
GLA_GATE_RANK = 16
GLA_GATE_NORM = 16.0
CHUNK = 64
LEVELS = (32, 16, 8, 4, 2, 1)

LANES = 128
MXU_N = 256
ROWS = 256
HALO = 8
ATTN_HEADS = 8
VMEM_LIMIT = 56 * 1024 * 1024

_O_GDN_QKV, _O_GDN_Z, _O_A, _O_B = 0, 768, 1024, 1032
_O_CQ, _O_CKV, _O_KR, _O_MLA_Z = 1040, 1424, 1680, 1712
_O_GLA_Q, _O_GLA_K, _O_GLA_V, _O_GLA_Z, _O_GLOW = 2224, 2352, 2480, 2736, 2992
_S_A, _S_B, _S_GLOW = 0, 8, 16
_W_GDN, _W_MLA, _W_MLAZ, _W_GLA = 1024, 640, 512, 768
_N_IN_PAD = _W_GDN + _W_MLA + _W_MLAZ + _W_GLA + 2 * LANES


def _dot(a, b):
    return jnp.dot(a, b, preferred_element_type=F32)


def _dot_nt(a, b):
    return lax.dot_general(a, b, (((1,), (1,)), ((), ())), preferred_element_type=F32)


def _dot_tn(a, b):
    return lax.dot_general(a, b, (((0,), (0,)), ((), ())), preferred_element_type=F32)


def _split2(x):
    x1 = x.astype(BF16)
    return x1, (x - x1.astype(F32)).astype(BF16)


def _split3(x):
    x1 = x.astype(BF16)
    r1 = x - x1.astype(F32)
    x2 = r1.astype(BF16)
    x3 = (r1 - x2.astype(F32)).astype(BF16)
    return x1, x2, x3


def _sel_dot(m01, x):
    n = x.shape[1]
    y = _dot(m01, jnp.concatenate(_split3(x), axis=1))
    return y[:, :n] + y[:, n:2 * n] + y[:, 2 * n:]


def _dot_sel2(x, m01):
    n = x.shape[0]
    y = _dot(jnp.concatenate(_split2(x), axis=0), m01)
    return y[:n] + y[n:]


def _dot_sel_nt(x, m01):
    n = x.shape[0]
    y = _dot_nt(jnp.concatenate(_split3(x), axis=0), m01)
    return y[:n] + y[n:2 * n] + y[2 * n:]


def _softplus(x):
    return jnp.maximum(x, 0.0) + jnp.log1p(jnp.exp(-jnp.abs(x)))


def _silu(x):
    return x * jax.nn.sigmoid(x)


def _params(*sem):
    return pltpu.CompilerParams(dimension_semantics=sem, vmem_limit_bytes=VMEM_LIMIT)


def _scan_consts(rev):
    t = np.arange(ROWS)
    ch = t // CHUNK
    p = (CHUNK - 1 - t % CHUNK) if rev else (t % CHUNK)
    same = ch[:, None] == ch[None, :]
    tri = same & (p[None, :] <= p[:, None])
    mq, mk = [], []
    lv = np.full((ROWS, ROWS), -1.0, np.float32)
    for li, s in enumerate(LEVELS):
        blk = p // s
        mq.append(same & (p[None, :] > (blk * s)[:, None]) & (p[None, :] <= p[:, None]))
        mk.append(same & (p[None, :] > p[:, None]) & (p[None, :] <= ((blk + 1) * s)[:, None]))
        pair = same & ((blk % 2) == 1)[:, None] & (blk[None, :] == (blk - 1)[:, None])
        lv[pair] = li
    stack = np.concatenate([tri] + mq[:-1] + mk[:-1], axis=0)
    lvm = np.stack([sum((lv == li)[c * CHUNK:(c + 1) * CHUNK] for c in range(ROWS // CHUNK))
                    for li in range(len(LEVELS))])
    def blocks_side_by_side(m):
        return np.concatenate([m[c * CHUNK:(c + 1) * CHUNK, c * CHUNK:(c + 1) * CHUNK]
                               for c in range(ROWS // CHUNK)], axis=1)

    negm = blocks_side_by_side(np.where(tri, 0.0, -np.inf).astype(np.float32))
    lvc = blocks_side_by_side(lv)
    return dict(tri=jnp.asarray(tri, BF16), ones=jnp.asarray(same, BF16), negm=jnp.asarray(negm),
                lvc=jnp.asarray(lvc), lvm=jnp.asarray(lvm, BF16), stack=jnp.asarray(stack, BF16))


def _chunk_lane_mask():
    nchunk = ROWS // CHUNK
    return jnp.asarray(np.repeat(np.repeat(np.eye(nchunk), CHUNK, axis=0), LANES, axis=1), BF16)


def _head_block_ones(n, width):
    i = np.arange(n)
    return jnp.asarray((i[:, None] // width) == (i[None, :] // width), BF16)


def _inproj_perm():
    perm = np.full((_N_IN_PAD,), -1, np.int64)

    def put(src, n, at):
        perm[at:at + n] = np.arange(src, src + n)

    put(_O_GDN_QKV, GDN_QKV, 0)
    put(_O_GDN_Z, GDN_W, GDN_QKV)
    pos = _W_GDN
    put(_O_CQ, MLA_Q_RANK, pos)
    put(_O_CKV, MLA_KV_RANK, pos + MLA_Q_RANK)
    pos += _W_MLA
    put(_O_MLA_Z, MLA_W, pos)
    pos += _W_MLAZ
    put(_O_GLA_Q, GLA_QK, pos)
    put(_O_GLA_K, GLA_QK, pos + GLA_QK)
    put(_O_GLA_V, GLA_W, pos + 2 * GLA_QK)
    put(_O_GLA_Z, GLA_W, pos + 2 * GLA_QK + GLA_W)
    pos += _W_GLA
    put(_O_A, 2 * GDN_HEADS, pos + _S_A)
    put(_O_B, 2 * GDN_HEADS, pos + _S_B)
    put(_O_GLOW, 2 * GLA_GATE_RANK, pos + _S_GLOW)
    pos += LANES
    put(_O_KR, MLA_ROPE, pos + MLA_NOPE)
    return perm


def _take_cols(w, perm):
    cols = jnp.take(w, jnp.asarray(np.maximum(perm, 0)), axis=1)
    return jnp.where(jnp.asarray(perm >= 0)[None, :], cols, 0.0)


def _mla_q_perm():
    perm = np.full((MLA_HEADS * LANES,), -1, np.int64)
    d = MLA_NOPE + MLA_ROPE
    for h in range(MLA_HEADS):
        perm[h * LANES:h * LANES + d] = np.arange(h * d, (h + 1) * d)
    return perm


def _mla_kv_perm():
    dk = MLA_NOPE + MLA_DV
    kperm = np.full((MLA_HEADS * LANES,), -1, np.int64)
    vperm = np.zeros((MLA_W,), np.int64)
    for h in range(MLA_HEADS):
        kperm[h * LANES:h * LANES + MLA_NOPE] = np.arange(h * dk, h * dk + MLA_NOPE)
        vperm[h * MLA_DV:(h + 1) * MLA_DV] = np.arange(h * dk + MLA_NOPE, (h + 1) * dk)
    return kperm, vperm


def _rope_tables(n_ctx, n_lat):
    rows = n_lat // ROPE_GRID_W
    row = np.repeat(np.arange(rows, dtype=np.float32), ROPE_GRID_W)
    col = np.tile(np.arange(ROPE_GRID_W, dtype=np.float32), rows)
    n_freq = MLA_ROPE // 4
    inv = (ROPE_THETA ** (-np.arange(n_freq, dtype=np.float32) / n_freq)).astype(np.float32)
    ang = np.concatenate([row[:, None] * inv, col[:, None] * inv], axis=-1)
    cos = np.concatenate([np.ones((n_ctx, 2 * n_freq), np.float32), np.cos(ang)], 0)
    sin = np.concatenate([np.zeros((n_ctx, 2 * n_freq), np.float32), np.sin(ang)], 0)
    n = n_ctx + n_lat
    half = MLA_ROPE // 2
    tab = np.zeros((3, n, LANES), np.float32)
    tab[0, :, :MLA_NOPE] = 1.0
    tab[0, :, MLA_NOPE:MLA_NOPE + half] = cos
    tab[0, :, MLA_NOPE + half:MLA_NOPE + MLA_ROPE] = cos
    tab[1, :, MLA_NOPE:MLA_NOPE + half] = -sin
    tab[2, :, MLA_NOPE + half:MLA_NOPE + MLA_ROPE] = sin
    return jnp.asarray(tab)


def _expand_matrix(src0, n_src, width, n_out):
    m = np.zeros((LANES, n_out), np.float32)
    for h in range(n_src):
        m[src0 + h, h * width:(h + 1) * width] = 1.0
    return m


def _ada_kernel(c_ref, w_ref, b_ref, o_ref):
    c = _silu(c_ref[...]).astype(BF16)
    o_ref[0] = _dot(c, w_ref[0].astype(BF16)) + b_ref[0]


def _ada(cc, w_ada, b_ada):
    nl, d, n3 = w_ada.shape
    r = cc.shape[0]
    tn = 1024
    return pl.pallas_call(
        _ada_kernel,
        grid=(nl, n3 // tn),
        in_specs=[pl.BlockSpec((r, d), lambda l, j: (0, 0)),
                  pl.BlockSpec((1, d, tn), lambda l, j: (l, 0, j)),
                  pl.BlockSpec((1, 1, tn), lambda l, j: (l, 0, j))],
        out_specs=pl.BlockSpec((1, r, tn), lambda l, j: (l, 0, j)),
        out_shape=jax.ShapeDtypeStruct((nl, r, n3), F32),
        compiler_params=_params("arbitrary", "arbitrary"),
        name="ada_mod",
    )(cc, w_ada, b_ada.reshape(nl, 1, n3))


def _token_block(nct, refs):
    if len(refs) == 1:
        return refs[0][0]
    return jnp.where(pl.program_id(1) < nct, refs[0][0], refs[1][0])


def _token_specs(arrays, nct, d):
    if len(arrays) == 1:
        return [pl.BlockSpec((1, ROWS, d), lambda bi, j: (bi, j, 0))]
    return [pl.BlockSpec((1, ROWS, d), lambda bi, j: (bi, jnp.minimum(j, nct - 1), 0)),
            pl.BlockSpec((1, ROWS, d), lambda bi, j: (bi, jnp.maximum(j - nct, 0), 0))]


def _halo_specs(arrays, nct, d, after):
    hb = ROWS // HALO
    offs = (0,) if len(arrays) == 1 else (0, nct)

    def spec(a, off):
        last = a.shape[1] // HALO - 1

        def index(bi, j):
            h = (j - off + 1) * hb if after else (j - off) * hb - 1
            return bi, jnp.clip(h, 0, last), 0
        return pl.BlockSpec((1, HALO, d), index)

    return [spec(a, off) for a, off in zip(arrays, offs)]


def _rope(x, tab_ref):
    half = MLA_ROPE // 2
    return (x * tab_ref[0] + pltpu.roll(x, LANES - half, 1) * tab_ref[1]
            + pltpu.roll(x, half, 1) * tab_ref[2])


def _inproj_kernel(nct, nx, *refs):
    x_refs, p_refs, n_refs, refs = refs[:nx], refs[nx:2 * nx], refs[2 * nx:3 * nx], refs[3 * nx:]
    (mod_ref, w_ref, wabt_ref, tab_ref, qw_ref, kvw_ref, wuq_ref, wuk_ref, wuv_ref, convw_ref,
     ob_ref, ogdn, omlaz, ogla, osmall, oabt, q_ref, k_ref, v_ref, xe_scr) = refs
    j = pl.program_id(1)
    nt = pl.num_programs(1)

    def modulated(x):
        h = x * lax.rsqrt(jnp.mean(x * x, axis=-1, keepdims=True) + EPS)
        return (h * (1.0 + mod_ref[0, 1:2, :]) + mod_ref[0, 0:1, :]).astype(BF16)

    hb = modulated(_token_block(nct, x_refs))
    halo = jnp.concatenate([_token_block(nct, p_refs), _token_block(nct, n_refs)], axis=0)
    ygdn = _dot(jnp.concatenate([hb, modulated(halo)], axis=0), w_ref[:, 0:_W_GDN])
    proj = {"gdn": ygdn[:ROWS]}
    yh = ygdn[ROWS:, :GDN_QKV]
    pos = _W_GDN
    for name, ref, n in (("mla", None, _W_MLA), ("mlaz", omlaz, _W_MLAZ), ("gla", ogla, _W_GLA),
                         ("small", osmall, LANES), ("kr", None, LANES)):
        y = _dot(hb, w_ref[:, pos:pos + n])
        if ref is None:
            proj[name] = y
        else:
            ref[0] = y
        pos += n
    oabt[0] = _dot_nt(wabt_ref[...], hb)

    has_prev = jnp.logical_and(j != 0, j != nct)
    has_next = jnp.logical_and(j != nct - 1, j != nt - 1)
    xe_scr[0:HALO, :] = jnp.where(has_prev, yh[:HALO], 0.0)
    xe_scr[HALO:HALO + ROWS, :] = proj["gdn"][:, :GDN_QKV]
    xe_scr[HALO + ROWS:, :] = jnp.where(has_next, yh[HALO:], 0.0)
    pad = (GDN_CONV - 1) // 2
    conv = jnp.zeros((ROWS, GDN_QKV), F32)
    for t in range(GDN_CONV):
        conv = conv + convw_ref[t:t + 1, :] * xe_scr[pl.ds(HALO - pad + t, ROWS), :]
    hqkv = _silu(conv)
    ob = ob_ref[...]
    qn = hqkv[:, :GDN_QK]
    kn = hqkv[:, GDN_QK:2 * GDN_QK]
    qn = qn * lax.rsqrt(_dot_sel2(qn * qn, ob) + EPS) * (GDN_DK ** -0.5)
    kn = kn * lax.rsqrt(_dot_sel2(kn * kn, ob) + EPS)
    ogdn[0] = jnp.concatenate([qn, kn, hqkv[:, 2 * GDN_QK:], proj["gdn"][:, GDN_QKV:]], axis=1)

    cq = proj["mla"][:, :MLA_Q_RANK]
    ckv = proj["mla"][:, MLA_Q_RANK:]
    cq = cq * lax.rsqrt(jnp.mean(cq * cq, axis=-1, keepdims=True) + EPS) * qw_ref[...]
    ckv = ckv * lax.rsqrt(jnp.mean(ckv * ckv, axis=-1, keepdims=True) + EPS) * kvw_ref[...]
    cqb = cq.astype(BF16)
    ckvb = ckv.astype(BF16)
    qf = _dot(cqb, wuq_ref[...])
    kf = _dot(ckvb, wuk_ref[...])
    vf = _dot(ckvb, wuv_ref[...])
    kr = _rope(proj["kr"], tab_ref)
    q_scale = MLA_SCALE * math.log2(math.e)
    for hd in range(MLA_HEADS):
        sl = slice(hd * LANES, (hd + 1) * LANES)
        q_ref[0, hd] = (_rope(qf[:, sl], tab_ref) * q_scale).astype(BF16)
        k_ref[0, hd] = (kf[:, sl] + kr).astype(BF16)
    ones = jnp.ones((ROWS, MXU_N - LANES), BF16)
    for hp in range(MLA_HEADS // 2):
        v_ref[0, hp] = jnp.concatenate(
            [vf[:, hp * LANES:(hp + 1) * LANES].astype(BF16), ones], axis=1)


def _inproj(xs, mod, w_p, wabt, tab, qw, kvw, wuq, wuk, wuv, convw, ob, nct):
    b, d = xs[0].shape[0], xs[0].shape[2]
    ta = sum(a.shape[1] for a in xs)
    nt = ta // ROWS
    widths = (_W_GDN, _W_MLAZ, _W_GLA, LANES)
    row = lambda bi, j: (bi, j, 0)
    full2 = lambda bi, j: (0, 0)
    hm = lambda bi, j: (bi, 0, j, 0)
    consts = (w_p, wabt)
    mla_consts = (qw, kvw, wuq, wuk, wuv, convw, ob)
    return pl.pallas_call(
        functools.partial(_inproj_kernel, nct, len(xs)),
        grid=(b, nt),
        in_specs=_token_specs(xs, nct, d) + _halo_specs(xs, nct, d, False)
        + _halo_specs(xs, nct, d, True)
        + [pl.BlockSpec((1, 3, d), lambda bi, j: (jnp.where(j < nct, 0, 1 + bi), 0, 0))]
        + [pl.BlockSpec(a.shape, full2) for a in consts]
        + [pl.BlockSpec((3, ROWS, LANES), lambda bi, j: (0, j, 0))]
        + [pl.BlockSpec(a.shape, full2) for a in mla_consts],
        out_specs=[pl.BlockSpec((1, ROWS, n), row) for n in widths]
        + [pl.BlockSpec((1, 4 * GDN_HEADS, ROWS), lambda bi, j: (bi, 0, j)),
           pl.BlockSpec((1, MLA_HEADS, ROWS, LANES), hm),
           pl.BlockSpec((1, MLA_HEADS, ROWS, LANES), hm),
           pl.BlockSpec((1, MLA_HEADS // 2, ROWS, MXU_N), hm)],
        out_shape=[jax.ShapeDtypeStruct((b, ta, n), F32) for n in widths]
        + [jax.ShapeDtypeStruct((b, 4 * GDN_HEADS, ta), F32),
           jax.ShapeDtypeStruct((b, MLA_HEADS, ta, LANES), BF16),
           jax.ShapeDtypeStruct((b, MLA_HEADS, ta, LANES), BF16),
           jax.ShapeDtypeStruct((b, MLA_HEADS // 2, ta, MXU_N), BF16)],
        scratch_shapes=[pltpu.VMEM((ROWS + 2 * HALO, GDN_QKV), F32)],
        compiler_params=_params("arbitrary", "arbitrary"),
        name="in_proj",
    )(*xs, *xs, *xs, mod, *consts, tab, *mla_consts)


def _scan_block_index(j, nct, nt, rev):
    if not rev:
        return j
    return jnp.where(j < nct, nct - 1 - j, nt - 1 - (j - nct))


def _head_lane_mask(n, width, h):
    lane = lax.broadcasted_iota(jnp.int32, (1, n), 1)
    return (lane >= h * width) & (lane < (h + 1) * width)


def _gdn_prep(d, x_ref, small_ref, abt_ref, prow_ref, pcol_ref, tri, exp_ref):
    q = x_ref[0, :, :GDN_QK]
    k = x_ref[0, :, GDN_QK:2 * GDN_QK]
    v = x_ref[0, :, 2 * GDN_QK:GDN_QKV]
    sm = small_ref[0]
    g_all = -jnp.exp(prow_ref[0:1, :]) * _softplus(sm + prow_ref[1:2, :])
    beta_all = jax.nn.sigmoid(sm)
    gc_all = _sel_dot(tri, g_all)
    g_t = -jnp.exp(pcol_ref[:, 0:1]) * _softplus(abt_ref[0] + pcol_ref[:, 1:2])
    gc_t = _dot_sel_nt(g_t, tri)
    gc_w = _dot_sel2(gc_all, exp_ref[d])
    beta_w = _dot_sel2(beta_all, exp_ref[2 + d])
    last = 0 if d == 1 else CHUNK - 1

    def chunk_last(x):
        return jnp.concatenate(
            [jnp.broadcast_to(x[c * CHUNK + last:c * CHUNK + last + 1, :], (CHUNK, x.shape[1]))
             for c in range(ROWS // CHUNK)], axis=0)

    gl_all = chunk_last(gc_all)
    gl_w = chunk_last(gc_w)
    kb = k * beta_w
    return dict(q=q, kbf=k.astype(BF16), kb=kb, vb=v * beta_w, kbg=kb * jnp.exp(gc_w),
                qg=q * jnp.exp(gc_w), kdec=k * jnp.exp(gl_w - gc_w),
                gc_all=gc_all, gc_t=gc_t, gl_all=gl_all)


def _gdn_stages(xf_ref, smf_ref, abtf_ref, xb_ref, smb_ref, abtb_ref, prow_ref, pcol_ref, tri_ref,
                ones_ref, negm_ref, lvm_ref, exp_ref, of_ref, obk_ref, s_scr):
    nh, dk = GDN_HEADS, GDN_DK
    nchunk = ROWS // CHUNK
    nlev = len(LEVELS)
    ones = ones_ref[...]
    dir_refs = ((xf_ref, smf_ref, abtf_ref), (xb_ref, smb_ref, abtb_ref))
    prep = []
    for d in range(2):
        prep.append(_gdn_prep(d, *dir_refs[d], prow_ref, pcol_ref, tri_ref[d], exp_ref))
        yield
    ri = lax.broadcasted_iota(jnp.int32, (ROWS, ROWS), 0)
    ci = lax.broadcasted_iota(jnp.int32, (ROWS, ROWS), 1)
    eye = (ri == ci).astype(BF16)
    chains = [(d, h) for d in range(2) for h in range(nh)]
    nc = len(chains)

    def compact(m):
        return functools.reduce(lambda a, b: a + b,
                                [m[c * CHUNK:(c + 1) * CHUNK] for c in range(nchunk)])

    def expand(mc):
        return jnp.concatenate([mc] * nchunk, axis=0) * ones

    def side_by_side(x):
        return jnp.concatenate([x[c * CHUNK:(c + 1) * CHUNK] for c in range(nchunk)], axis=1)

    low_c, a_intra = [], []
    for d, h in chains:
        p = prep[d]
        idx = d * nh + h
        sl = slice(h * dk, (h + 1) * dk)
        gc_col = side_by_side(jnp.broadcast_to(p["gc_all"][:, idx:idx + 1], (ROWS, CHUNK)))
        decay = jnp.exp(gc_col - p["gc_t"][idx:idx + 1, :] + negm_ref[d])
        lhs = jnp.concatenate([side_by_side(p["kb"][:, sl]), side_by_side(p["q"][:, sl])], axis=0)
        kt = jnp.concatenate([p["kbf"][:, sl]] * nchunk, axis=1) * ones
        kq = _dot_nt(lhs.astype(BF16), kt)
        low_c.append((kq[:CHUNK] * decay).astype(BF16))
        a_intra.append(expand((kq[CHUNK:] * decay).astype(BF16)))
        if h == nh - 1:
            yield

    eye_c = compact(eye)
    t_c = [eye_c - low_c[i] * lvm_ref[d, nlev - 1] for i, (d, h) in enumerate(chains)]
    t_inv = [expand(t) for t in t_c]
    for li in reversed(range(nlev - 1)):
        ys = [expand(_dot(low_c[i] * lvm_ref[d, li], t_inv[i]).astype(BF16))
              for i, (d, h) in enumerate(chains)]
        t_c = [t_c[i] - _dot(t_c[i], ys[i]).astype(BF16) for i in range(nc)]
        t_inv = [expand(t) for t in t_c]
        yield
    uw = []
    for i, (d, h) in enumerate(chains):
        sl = slice(h * dk, (h + 1) * dk)
        rhs = jnp.concatenate([prep[d]["vb"][:, sl], prep[d]["kbg"][:, sl]], axis=1)
        uw.append(_dot(t_inv[i], rhs.astype(BF16)))

    q2, ou, gb = [], [], []
    for i, (d, h) in enumerate(chains):
        sl = slice(h * dk, (h + 1) * dk)
        uwb = uw[i].astype(BF16)
        auw = _dot(a_intra[i], uwb)
        ou.append(auw[:, :GDN_DV])
        q2.append((prep[d]["qg"][:, sl] - auw[:, GDN_DV:]).astype(BF16))
        kd = prep[d]["kdec"][:, sl].astype(BF16)
        gb.append([_dot_tn(kd[c * CHUNK:(c + 1) * CHUNK], uwb[c * CHUNK:(c + 1) * CHUNK])
                   for c in range(nchunk)])
    yield
    states = [s_scr[i] for i in range(nc)]
    o_parts = [[None] * nchunk for _ in chains]
    for step in range(nchunk):
        for i, (d, h) in enumerate(chains):
            c = nchunk - 1 - step if d == 1 else step
            idx = d * nh + h
            r0 = c * CHUNK
            rs = slice(r0, r0 + CHUNK)
            sb = states[i].astype(BF16)
            o_parts[i][c] = _dot(q2[i][rs], sb) + ou[i][rs]
            states[i] = (states[i] * jnp.exp(prep[d]["gl_all"][r0:r0 + 1, idx:idx + 1])
                         - _dot(gb[i][c][:, GDN_DV:].astype(BF16), sb) + gb[i][c][:, :GDN_DV])
        yield
    for i in range(len(chains)):
        s_scr[i] = states[i]
    outs = [jnp.concatenate(o_parts[i], axis=0) for i in range(len(chains))]
    of_ref[0] = jnp.concatenate(outs[:nh], axis=1)
    obk_ref[0] = jnp.concatenate(outs[nh:], axis=1)


def _scan_row_specs(nct, nt, rev, width):
    blk_of = functools.partial(_scan_block_index, nct=nct, nt=nt, rev=rev)
    return pl.BlockSpec((1, ROWS, width), lambda bi, j: (bi, blk_of(j), 0))


def _gdn_specs(nct, nt, gdn_in, small, abt, prow, pcol, exp_m):
    cf, cb = _scan_consts(False), _scan_consts(True)
    stack = lambda name: jnp.stack([cf[name], cb[name]])
    full2 = lambda bi, j: (0, 0)
    full3 = lambda bi, j: (0, 0, 0)

    def dir_specs(rev):
        blk_of = functools.partial(_scan_block_index, nct=nct, nt=nt, rev=rev)
        return [
            _scan_row_specs(nct, nt, rev, GDN_QKV),
            _scan_row_specs(nct, nt, rev, LANES),
            pl.BlockSpec((1, 4 * GDN_HEADS, ROWS), lambda bi, j: (bi, 0, blk_of(j))),
        ]

    in_specs = dir_specs(False) + dir_specs(True) + [
        pl.BlockSpec(prow.shape, full2),
        pl.BlockSpec(pcol.shape, full2),
        pl.BlockSpec((2, ROWS, ROWS), full3),
        pl.BlockSpec((ROWS, ROWS), full2),
        pl.BlockSpec((2, CHUNK, ROWS), full3),
        pl.BlockSpec((2, len(LEVELS), CHUNK, ROWS), lambda bi, j: (0, 0, 0, 0)),
        pl.BlockSpec(exp_m.shape, full3),
    ]
    dir_args = [gdn_in, small, abt]
    args = dir_args + dir_args + [prow, pcol, stack("tri"), cf["ones"], stack("negm"),
                                  stack("lvm"), exp_m]
    out_specs = [_scan_row_specs(nct, nt, False, GDN_W), _scan_row_specs(nct, nt, True, GDN_W)]
    scratch = [pltpu.VMEM((2 * GDN_HEADS, GDN_DK, GDN_DV), F32)]
    return in_specs, args, out_specs, scratch


def _gla_stages(xf_ref, smf_ref, xb_ref, smb_ref, wg_ref, bg_ref, stack_ref, lvc_ref, kmask_ref,
                same_ref, eqk_ref, sbd_ref, of_ref, obk_ref, s_scr):
    nh, dk, dv = GLA_HEADS, GLA_DK, GLA_DV
    nchunk = ROWS // CHUNK
    nlev = len(LEVELS)
    n = GLA_QK
    out_head = [_head_lane_mask(GLA_W, dv, h) for h in range(nh)]
    eqk = eqk_ref[...]
    sbd = sbd_ref[...]
    x_refs, sm_refs, o_refs = (xf_ref, xb_ref), (smf_ref, smb_ref), (of_ref, obk_ref)
    dirs = range(2)
    xs = [x_refs[d][0] for d in dirs]
    q = [xs[d][:, :GLA_QK] * (dk ** -0.5) for d in dirs]
    k = [xs[d][:, GLA_QK:2 * GLA_QK] for d in dirs]
    v = [xs[d][:, 2 * GLA_QK:2 * GLA_QK + GLA_W] for d in dirs]
    vb = [v[d].astype(BF16) for d in dirs]
    gk = [_dot(sm_refs[d][0].astype(BF16), wg_ref[d]) + bg_ref[d] for d in dirs]
    la = [-_softplus(-gk[d]) * (1.0 / GLA_GATE_NORM) for d in dirs]
    la2 = [jnp.concatenate(_split2(la[d]), axis=1) for d in dirs]
    ys = [_dot(stack_ref[d], la2[d]) for d in dirs]
    cums = [ys[d][:, :n] + ys[d][:, n:] for d in dirs]
    piece = lambda d, i: cums[d][i * ROWS:(i + 1) * ROWS]
    bcum = [piece(d, 0) for d in dirs]
    last_row = [0 if d == 1 else CHUNK - 1 for d in dirs]
    blast = [jnp.concatenate(
        [jnp.broadcast_to(bcum[d][c * CHUNK + last_row[d]:c * CHUNK + last_row[d] + 1, :], (CHUNK, n))
         for c in range(nchunk)], axis=0) for d in dirs]
    la_next = [pltpu.roll(la[d], 1 if d == 1 else ROWS - 1, 0) for d in dirs]
    qg = [(q[d] * jnp.exp(bcum[d])).astype(BF16) for d in dirs]
    kdec = [(k[d] * jnp.exp(blast[d] - bcum[d])).astype(BF16) for d in dirs]
    yield

    lane4 = lax.broadcasted_iota(jnp.int32, (1, nchunk * n), 1) % n
    lane_head4 = [(lane4 >= h * dk) & (lane4 < (h + 1) * dk) for h in range(nh)]
    kmask = kmask_ref[...]
    same = same_ref[...]
    lv4 = [jnp.concatenate([lvc_ref[d]] * nh, axis=0) for d in dirs]
    acc = [jnp.zeros((nh * CHUNK, ROWS), F32) for d in dirs]
    for li in range(nlev):
        if li < nlev - 1:
            ql = [q[d] * jnp.exp(piece(d, 1 + li)) for d in dirs]
            kl = [(k[d] * jnp.exp(piece(d, nlev + li))).astype(BF16) for d in dirs]
        else:
            ql = q
            kl = [(k[d] * jnp.exp(la_next[d])).astype(BF16) for d in dirs]
        qc = [jnp.concatenate([ql[d][c * CHUNK:(c + 1) * CHUNK] for c in range(nchunk)], axis=1)
              for d in dirs]
        qs = [jnp.concatenate([jnp.where(lane_head4[h], qc[d], 0.0) for h in range(nh)],
                              axis=0).astype(BF16) for d in dirs]
        kt = [jnp.concatenate([kl[d]] * nchunk, axis=1) * kmask for d in dirs]
        ps = [_dot_nt(qs[d], kt[d]) for d in dirs]
        acc = [jnp.where(lv4[d] == float(li), ps[d], acc[d]) for d in dirs]
        yield
    accb = [acc[d].astype(BF16) for d in dirs]
    acc_bd = [jnp.concatenate(
        [jnp.concatenate([accb[d][h * CHUNK:(h + 1) * CHUNK]] * nchunk, axis=0) * same
         for h in range(nh)], axis=0) for d in dirs]
    pv = [_dot(acc_bd[d], vb[d]) for d in dirs]
    o = [_dot_sel2(q[d] * k[d], eqk) * v[d] for d in dirs]
    for h in range(nh):
        o = [o[d] + jnp.where(out_head[h], pv[d][h * ROWS:(h + 1) * ROWS], 0.0) for d in dirs]
    yield

    states = [s_scr[d] for d in dirs]
    o_parts = [[None] * nchunk for d in dirs]
    for step in range(nchunk):
        for d in dirs:
            c = nchunk - 1 - step if d == 1 else step
            rs = slice(c * CHUNK, (c + 1) * CHUNK)
            o_parts[d][c] = _dot_nt(qg[d][rs], states[d].astype(BF16))
            states[d] = (states[d] * jnp.exp(blast[d][c * CHUNK:c * CHUNK + 1, :])
                         + sbd * _dot_tn(vb[d][rs], kdec[d][rs]))
        yield
    for d in dirs:
        s_scr[d] = states[d]
        o_refs[d][0] = o[d] + jnp.concatenate(o_parts[d], axis=0)


def _gla_specs(nct, nt, gla_in, small, wg, bg, eqk, sbd):
    cf, cb = _scan_consts(False), _scan_consts(True)
    stack = jnp.stack([cf["stack"], cb["stack"]])
    lvc = jnp.stack([cf["lvc"], cb["lvc"]])
    assert GLA_QK == LANES
    kmask = _chunk_lane_mask()
    full2 = lambda bi, j: (0, 0)
    full3 = lambda bi, j: (0, 0, 0)
    in_specs = [
        _scan_row_specs(nct, nt, False, _W_GLA), _scan_row_specs(nct, nt, False, LANES),
        _scan_row_specs(nct, nt, True, _W_GLA), _scan_row_specs(nct, nt, True, LANES),
        pl.BlockSpec(wg.shape, full3),
        pl.BlockSpec(bg.shape, full3),
        pl.BlockSpec(stack.shape, full3),
        pl.BlockSpec(lvc.shape, full3),
        pl.BlockSpec(kmask.shape, full2),
        pl.BlockSpec((ROWS, ROWS), full2),
        pl.BlockSpec(eqk.shape, full2),
        pl.BlockSpec(sbd.shape, full2),
    ]
    args = [gla_in, small, gla_in, small, wg, bg, stack, lvc, kmask, cf["ones"], eqk, sbd]
    out_specs = [_scan_row_specs(nct, nt, False, GLA_W), _scan_row_specs(nct, nt, True, GLA_W)]
    scratch = [pltpu.VMEM((2, GLA_W, GLA_QK), F32)]
    return in_specs, args, out_specs, scratch


_N_GDN_IN, _N_GLA_IN = 13, 12


def _scan_kernel(*refs):
    gdn_in, refs = refs[:_N_GDN_IN], refs[_N_GDN_IN:]
    gla_in, refs = refs[:_N_GLA_IN], refs[_N_GLA_IN:]
    gdn_out, gla_out, (gdn_state, gla_state) = refs[:2], refs[2:4], refs[4:]

    @pl.when(pl.program_id(1) == 0)
    def _():
        gdn_state[...] = jnp.zeros_like(gdn_state)
        gla_state[...] = jnp.zeros_like(gla_state)

    stages = [_gdn_stages(*gdn_in, *gdn_out, gdn_state),
              _gla_stages(*gla_in, *gla_out, gla_state)]
    while stages:
        for s in list(stages):
            if next(s, StopIteration) is StopIteration:
                stages.remove(s)


def _scans(nct, gdn_args, gla_args):
    gdn_in = gdn_args[0]
    b, ta, _ = gdn_in.shape
    nt = ta // ROWS
    g_in, g_args, g_out, g_scr = _gdn_specs(nct, nt, *gdn_args)
    a_in, a_args, a_out, a_scr = _gla_specs(nct, nt, *gla_args)
    assert len(g_in) == _N_GDN_IN and len(a_in) == _N_GLA_IN
    return pl.pallas_call(
        _scan_kernel,
        grid=(b, nt),
        in_specs=g_in + a_in,
        out_specs=g_out + a_out,
        out_shape=[jax.ShapeDtypeStruct((b, ta, GDN_W), F32)] * 2
        + [jax.ShapeDtypeStruct((b, ta, GLA_W), F32)] * 2,
        scratch_shapes=g_scr + a_scr,
        compiler_params=_params("arbitrary", "arbitrary"),
        name="gdn_gla_scan",
    )(*g_args, *a_args)


def _attn_kernel(nct, off, n_ctx, q_ref, k_ref, v_ref, z_ref, o_ref):
    i = pl.program_id(2) + off

    def body(nk):
        parts = []
        half = ROWS // 2

        def by_row_halves(dot, lhs, rhs):
            return jnp.concatenate([dot(lhs[:half], rhs), dot(lhs[half:], rhs)], axis=0)

        for hh in range(ATTN_HEADS):
            q = q_ref[0, hh]
            kk = k_ref[0, hh, :nk, :]
            s = by_row_halves(_dot_nt, q, kk) if hh == 0 else _dot_nt(q, kk)
            m = jnp.max(s, axis=-1, keepdims=True)
            p = jnp.exp2(s - m).astype(BF16)
            vv = v_ref[0, hh // 2, :nk, :]
            pv = by_row_halves(_dot, p, vv) if hh == ATTN_HEADS - 1 else _dot(p, vv)
            o = pv[:, (hh % 2) * MLA_DV:(hh % 2 + 1) * MLA_DV] / pv[:, LANES:LANES + 1]
            parts.append(o)
        o_ref[0] = (jnp.concatenate(parts, axis=1) * _silu(z_ref[0])).astype(BF16)

    if off == 0:
        @pl.when(i < nct)
        def _():
            body(n_ctx)

        @pl.when(i >= nct)
        def _():
            body(k_ref.shape[2])
    else:
        body(k_ref.shape[2])


def _attention(q, k, v, z, nct, with_ctx):
    b, nh, ta, _ = q.shape
    nt = ta // ROWS
    off = 0 if with_ctx else nct
    nq = nt - off
    hg = ATTN_HEADS
    wo = hg * MLA_DV
    return pl.pallas_call(
        functools.partial(_attn_kernel, nct, off, nct * ROWS),
        grid=(b, nh // hg, nq),
        in_specs=[pl.BlockSpec((1, hg, ROWS, LANES), lambda bi, hp, i: (bi, hp, i + off, 0)),
                  pl.BlockSpec((1, hg, ta, LANES), lambda bi, hp, i: (bi, hp, 0, 0)),
                  pl.BlockSpec((1, hg // 2, ta, MXU_N), lambda bi, hp, i: (bi, hp, 0, 0)),
                  pl.BlockSpec((1, ROWS, wo), lambda bi, hp, i: (bi, i + off, hp))],
        out_specs=pl.BlockSpec((1, ROWS, wo), lambda bi, hp, i: (bi, i, hp)),
        out_shape=jax.ShapeDtypeStruct((b, nq * ROWS, MLA_W), BF16),
        compiler_params=_params("arbitrary", "arbitrary", "arbitrary"),
        name="mla_attn",
    )(q, k, v, z)


def _outproj_kernel(final, nct, nx, gf_ref, gb_ref, gz_ref, gnw_ref, af_ref, ab_ref, az_ref, anw_ref,
                    ob_ref, m_ref, mod_ref, w_ref, *rest):
    x_refs, rest = rest[:nx], rest[nx:]
    if final:
        fw_ref, o_ref = rest
    else:
        (o_ref,) = rest
    ob = ob_ref[...]

    def gated_head_norm(o, nw_ref, z):
        ms = _dot_sel2(o * o, ob) * (1.0 / GDN_DV)
        return (o * lax.rsqrt(ms + EPS) * nw_ref[...] * _silu(z)).astype(BF16)

    g = gated_head_norm(gf_ref[0] + gb_ref[0], gnw_ref, gz_ref[0])
    a = gated_head_norm(af_ref[0] + ab_ref[0], anw_ref, az_ref[0])
    y = _dot(g, w_ref[0:GDN_W, :])
    y = y + _dot(m_ref[0], w_ref[GDN_W:GDN_W + MLA_W, :])
    y = y + _dot(a, w_ref[GDN_W + MLA_W:, :])
    xn = _token_block(nct, x_refs) + mod_ref[0, 2:3, :] * y
    if final:
        xn = xn * lax.rsqrt(jnp.mean(xn * xn, axis=-1, keepdims=True) + EPS) * fw_ref[...]
    o_ref[0] = xn


def _outproj(gdn_f, gdn_b, gdn_in, gnw, gla_f, gla_b, gla_in, anw, ob, mla_o, xs, mod, w_out,
             nct, final_w):
    assert GDN_DV == GLA_DV and GDN_W == GLA_W
    b, d = xs[0].shape[0], xs[0].shape[2]
    ta = sum(a.shape[1] for a in xs)
    nt = ta // ROWS
    final = final_w is not None
    assert not (final and len(xs) > 1)
    off = nct if final else 0
    moff = off - (ta - mla_o.shape[1]) // ROWS
    nq = nt - off
    row = lambda bi, j: (bi, j + off, 0)
    full2 = lambda bi, j: (0, 0)
    x_specs = [pl.BlockSpec((1, ROWS, d), row)] if len(xs) == 1 else _token_specs(xs, nct, d)
    wide = pl.BlockSpec((1, ROWS, GDN_W), row)
    gz_blk = GDN_QKV // GDN_W
    az_blk = (2 * GLA_QK + GLA_W) // GLA_W
    in_specs = [wide, wide,
                pl.BlockSpec((1, ROWS, GDN_W), lambda bi, j: (bi, j + off, gz_blk)),
                pl.BlockSpec(gnw.shape, full2),
                wide, wide,
                pl.BlockSpec((1, ROWS, GLA_W), lambda bi, j: (bi, j + off, az_blk)),
                pl.BlockSpec(anw.shape, full2),
                pl.BlockSpec(ob.shape, full2),
                pl.BlockSpec((1, ROWS, MLA_W), lambda bi, j: (bi, j + moff, 0)),
                pl.BlockSpec((1, 3, d), lambda bi, j: (jnp.where(j + off < nct, 0, 1 + bi), 0, 0)),
                pl.BlockSpec(w_out.shape, full2)] + x_specs
    args = [gdn_f, gdn_b, gdn_in, gnw, gla_f, gla_b, gla_in, anw, ob, mla_o, mod, w_out, *xs]
    if final:
        in_specs.append(pl.BlockSpec(final_w.shape, full2))
        args.append(final_w)
    return pl.pallas_call(
        functools.partial(_outproj_kernel, final, nct, len(xs)),
        grid=(b, nq),
        in_specs=in_specs,
        out_specs=pl.BlockSpec((1, ROWS, d), lambda bi, j: (bi, j, 0)),
        out_shape=jax.ShapeDtypeStruct((b, nq * ROWS, d), F32),
        compiler_params=_params("arbitrary", "arbitrary"),
        name="out_proj",
    )(*args)


def kernel(x, c, ctx, c_ctx, w_ada, b_ada, w_in, gdn_conv_w, gdn_a_log, gdn_dt_bias, gdn_norm_w,
           mla_q_norm_w, mla_w_uq, mla_kv_norm_w, mla_w_ukv, gla_w_gk, gla_b_gk, gla_norm_w,
           w_out, final_norm_w):
    b, seq, d = x.shape
    n_ctx = ctx.shape[1]
    depth = w_in.shape[0]
    assert n_ctx % ROWS == 0 and seq % ROWS == 0 and seq % ROPE_GRID_W == 0
    nct = n_ctx // ROWS
    nh = GDN_HEADS

    xs = (ctx, x)
    pad_rows = (-(1 + b)) % 8
    cc = jnp.concatenate([c_ctx[None, :], c, jnp.zeros((pad_rows, d), F32)], axis=0)
    mod_all = _ada(cc, w_ada, b_ada).reshape(depth, cc.shape[0], 3, d)

    perm = _inproj_perm()
    qperm = _mla_q_perm()
    kperm, vperm = _mla_kv_perm()
    tab = _rope_tables(n_ctx, seq)
    ob64 = _head_block_ones(GDN_W, GDN_DV)
    exp_m = jnp.asarray(np.stack(
        [_expand_matrix(_S_A + dd * nh, nh, GDN_DK, GDN_QK) for dd in range(2)]
        + [_expand_matrix(_S_B + dd * nh, nh, GDN_DK, GDN_QK) for dd in range(2)]), BF16)
    eqk_np = np.zeros((GLA_QK, GLA_W), np.float32)
    sbd_np = np.zeros((GLA_W, GLA_QK), np.float32)
    for h in range(GLA_HEADS):
        eqk_np[h * GLA_DK:(h + 1) * GLA_DK, h * GLA_DV:(h + 1) * GLA_DV] = 1.0
        sbd_np[h * GLA_DV:(h + 1) * GLA_DV, h * GLA_DK:(h + 1) * GLA_DK] = 1.0
    eqk = jnp.asarray(eqk_np, BF16)
    sbd = jnp.asarray(sbd_np)

    out = None
    for layer in range(depth):
        last = layer == depth - 1
        w_p = _take_cols(w_in[layer], perm).astype(BF16)
        wabt = w_in[layer][:, _O_A:_O_A + 4 * nh].T.astype(BF16)
        mod = mod_all[layer]
        wuq = _take_cols(mla_w_uq[layer], qperm).astype(BF16)
        wuk = _take_cols(mla_w_ukv[layer], kperm).astype(BF16)
        wuv = jnp.take(mla_w_ukv[layer], jnp.asarray(vperm), axis=1).astype(BF16)
        convw = jnp.concatenate(
            [gdn_conv_w[layer], jnp.zeros((8 - GDN_CONV, GDN_QKV), F32)], axis=0)
        gdn_in, mla_z, gla_in, small, abt, qh, kh, vh = _inproj(
            xs, mod, w_p, wabt, tab, mla_q_norm_w[layer][None, :], mla_kv_norm_w[layer][None, :],
            wuq, wuk, wuv, convw, ob64, nct)

        a_flat = gdn_a_log[layer].reshape(-1)
        dt_flat = gdn_dt_bias[layer].reshape(-1)
        prow = jnp.zeros((8, LANES), F32)
        prow = prow.at[0, _S_A:_S_A + 2 * nh].set(a_flat).at[1, _S_A:_S_A + 2 * nh].set(dt_flat)
        pcol = jnp.zeros((4 * nh, LANES), F32)
        pcol = pcol.at[0:2 * nh, 0].set(a_flat).at[0:2 * nh, 1].set(dt_flat)
        gnw = jnp.tile(gdn_norm_w[layer], nh)[None, :]

        anw = jnp.tile(gla_norm_w[layer], GLA_HEADS)[None, :]
        wg = jnp.zeros((2, LANES, GLA_QK), F32)
        for dd in range(2):
            r0 = _S_GLOW + dd * GLA_GATE_RANK
            wg = wg.at[dd, r0:r0 + GLA_GATE_RANK, :].set(gla_w_gk[layer, dd])
        gdn_f, gdn_b, gla_f, gla_b = _scans(
            nct, (gdn_in, small, abt, prow, pcol, exp_m),
            (gla_in, small, wg.astype(BF16), gla_b_gk[layer][:, None, :], eqk, sbd))

        mla_o = _attention(qh, kh, vh, mla_z, nct, with_ctx=not last)

        if last and len(xs) > 1:
            xs = (jnp.concatenate(xs, axis=1),)
        res = _outproj(gdn_f, gdn_b, gdn_in, gnw, gla_f, gla_b, gla_in, anw, ob64, mla_o, xs,
                       mod, w_out[layer].astype(BF16), nct, final_norm_w[None, :] if last else None)
        if last:
            out = res
        else:
            xs = (res,)
    return out
```

```python
import functools
import math

import numpy as np
import jax
import jax.numpy as jnp
from jax import lax
from jax.experimental import pallas as pl
from jax.experimental.pallas import tpu as pltpu

F32 = jnp.float32
BF16 = jnp.bfloat16
EPS = 1e-6

GDN_HEADS, GDN_DK, GDN_DV, GDN_CONV = 4, 64, 64, 5
GDN_QK = GDN_HEADS * GDN_DK
GDN_W = GDN_HEADS * GDN_DV
GDN_QKV = 2 * GDN_QK + GDN_W
MLA_HEADS, MLA_Q_RANK, MLA_KV_RANK = 8, 384, 256
MLA_NOPE, MLA_ROPE, MLA_DV = 64, 32, 64
MLA_W = MLA_HEADS * MLA_DV
MLA_SCALE = (MLA_NOPE + MLA_ROPE) ** -0.5
ROPE_THETA = 10000.0
ROPE_GRID_W = 64
GLA_HEADS, GLA_DK, GLA_DV = 4, 32, 64
GLA_QK = GLA_HEADS * GLA_DK
GLA_W = GLA_HEADS * GLA_DV
GLA_GATE_RANK = 16
GLA_GATE_NORM = 16.0
CHUNK = 64
LEVELS = (32, 16, 8, 4, 2, 1)

LANES = 128
MXU_N = 256
ROWS = 256
HALO = 8
ATTN_HEADS = 8
SCAN_BATCH = 2
VMEM_LIMIT = 56 * 1024 * 1024

_O_GDN_QKV, _O_GDN_Z, _O_A, _O_B = 0, 768, 1024, 1032
_O_CQ, _O_CKV, _O_KR, _O_MLA_Z = 1040, 1424, 1680, 1712
_O_GLA_Q, _O_GLA_K, _O_GLA_V, _O_GLA_Z, _O_GLOW = 2224, 2352, 2480, 2736, 2992
_S_A, _S_B, _S_GLOW = 0, 8, 16
_W_GDN, _W_MLA, _W_MLAZ, _W_GLA = 1024, 640, 512, 768
_N_IN_PAD = _W_GDN + _W_MLA + _W_MLAZ + _W_GLA + 2 * LANES


def _dot(a, b):
    return jnp.dot(a, b, preferred_element_type=F32)


def _dot_nt(a, b):
    return lax.dot_general(a, b, (((1,), (1,)), ((), ())), preferred_element_type=F32)


def _dot_tn(a, b):
    return lax.dot_general(a, b, (((0,), (0,)), ((), ())), preferred_element_type=F32)


def _split2(x):
    x1 = x.astype(BF16)
    return x1, (x - x1.astype(F32)).astype(BF16)


def _split3(x):
    x1 = x.astype(BF16)
    r1 = x - x1.astype(F32)
    x2 = r1.astype(BF16)
    x3 = (r1 - x2.astype(F32)).astype(BF16)
    return x1, x2, x3


def _sel_dot(m01, x):
    n = x.shape[1]
    y = _dot(m01, jnp.concatenate(_split3(x), axis=1))
    return y[:, :n] + y[:, n:2 * n] + y[:, 2 * n:]


def _dot_sel2(x, m01):
    n = x.shape[0]
    y = _dot(jnp.concatenate(_split2(x), axis=0), m01)
    return y[:n] + y[n:]


def _dot_sel_nt(x, m01):
    n = x.shape[0]
    y = _dot_nt(jnp.concatenate(_split3(x), axis=0), m01)
    return y[:n] + y[n:2 * n] + y[2 * n:]


def _softplus(x):
    return jnp.maximum(x, 0.0) + jnp.log1p(jnp.exp(-jnp.abs(x)))


def _silu(x):
    return x * jax.nn.sigmoid(x)


def _params(*sem):
    return pltpu.CompilerParams(dimension_semantics=sem, vmem_limit_bytes=VMEM_LIMIT)


def _scan_consts(rev):
    t = np.arange(ROWS)
    ch = t // CHUNK
    p = (CHUNK - 1 - t % CHUNK) if rev else (t % CHUNK)
    same = ch[:, None] == ch[None, :]
    tri = same & (p[None, :] <= p[:, None])
    mq, mk = [], []
    lv = np.full((ROWS, ROWS), -1.0, np.float32)
    for li, s in enumerate(LEVELS):
        blk = p // s
        mq.append(same & (p[None, :] > (blk * s)[:, None]) & (p[None, :] <= p[:, None]))
        mk.append(same & (p[None, :] > p[:, None]) & (p[None, :] <= ((blk + 1) * s)[:, None]))
        pair = same & ((blk % 2) == 1)[:, None] & (blk[None, :] == (blk - 1)[:, None])
        lv[pair] = li
    stack = np.concatenate([tri] + mq[:-1] + mk[:-1], axis=0)
    lvm = np.stack([sum((lv == li)[c * CHUNK:(c + 1) * CHUNK] for c in range(ROWS // CHUNK))
                    for li in range(len(LEVELS))])
    def blocks_side_by_side(m):
        return np.concatenate([m[c * CHUNK:(c + 1) * CHUNK, c * CHUNK:(c + 1) * CHUNK]
                               for c in range(ROWS // CHUNK)], axis=1)

    negm = blocks_side_by_side(np.where(tri, 0.0, -np.inf).astype(np.float32))
    lvc = blocks_side_by_side(lv)
    return dict(tri=jnp.asarray(tri, BF16), ones=jnp.asarray(same, BF16), negm=jnp.asarray(negm),
                lvc=jnp.asarray(lvc), lvm=jnp.asarray(lvm, BF16), stack=jnp.asarray(stack, BF16))


def _chunk_lane_mask():
    nchunk = ROWS // CHUNK
    return jnp.asarray(np.repeat(np.repeat(np.eye(nchunk), CHUNK, axis=0), LANES, axis=1), BF16)


def _head_block_ones(n, width):
    i = np.arange(n)
    return jnp.asarray((i[:, None] // width) == (i[None, :] // width), BF16)


def _inproj_perm():
    perm = np.full((_N_IN_PAD,), -1, np.int64)

    def put(src, n, at):
        perm[at:at + n] = np.arange(src, src + n)

    put(_O_GDN_QKV, GDN_QKV, 0)
    put(_O_GDN_Z, GDN_W, GDN_QKV)
    pos = _W_GDN
    put(_O_CQ, MLA_Q_RANK, pos)
    put(_O_CKV, MLA_KV_RANK, pos + MLA_Q_RANK)
    pos += _W_MLA
    put(_O_MLA_Z, MLA_W, pos)
    pos += _W_MLAZ
    put(_O_GLA_Q, GLA_QK, pos)
    put(_O_GLA_K, GLA_QK, pos + GLA_QK)
    put(_O_GLA_V, GLA_W, pos + 2 * GLA_QK)
    put(_O_GLA_Z, GLA_W, pos + 2 * GLA_QK + GLA_W)
    pos += _W_GLA
    put(_O_A, 2 * GDN_HEADS, pos + _S_A)
    put(_O_B, 2 * GDN_HEADS, pos + _S_B)
    put(_O_GLOW, 2 * GLA_GATE_RANK, pos + _S_GLOW)
    pos += LANES
    put(_O_KR, MLA_ROPE, pos + MLA_NOPE)
    return perm


def _take_cols(w, perm):
    cols = jnp.take(w, jnp.asarray(np.maximum(perm, 0)), axis=1)
    return jnp.where(jnp.asarray(perm >= 0)[None, :], cols, 0.0)


def _mla_q_perm():
    perm = np.full((MLA_HEADS * LANES,), -1, np.int64)
    d = MLA_NOPE + MLA_ROPE
    for h in range(MLA_HEADS):
        perm[h * LANES:h * LANES + d] = np.arange(h * d, (h + 1) * d)
    return perm


def _mla_kv_perm():
    dk = MLA_NOPE + MLA_DV
    kperm = np.full((MLA_HEADS * LANES,), -1, np.int64)
    vperm = np.zeros((MLA_W,), np.int64)
    for h in range(MLA_HEADS):
        kperm[h * LANES:h * LANES + MLA_NOPE] = np.arange(h * dk, h * dk + MLA_NOPE)
        vperm[h * MLA_DV:(h + 1) * MLA_DV] = np.arange(h * dk + MLA_NOPE, (h + 1) * dk)
    return kperm, vperm


def _rope_tables(n_ctx, n_lat):
    rows = n_lat // ROPE_GRID_W
    row = np.repeat(np.arange(rows, dtype=np.float32), ROPE_GRID_W)
    col = np.tile(np.arange(ROPE_GRID_W, dtype=np.float32), rows)
    n_freq = MLA_ROPE // 4
    inv = (ROPE_THETA ** (-np.arange(n_freq, dtype=np.float32) / n_freq)).astype(np.float32)
    ang = np.concatenate([row[:, None] * inv, col[:, None] * inv], axis=-1)
    cos = np.concatenate([np.ones((n_ctx, 2 * n_freq), np.float32), np.cos(ang)], 0)
    sin = np.concatenate([np.zeros((n_ctx, 2 * n_freq), np.float32), np.sin(ang)], 0)
    n = n_ctx + n_lat
    half = MLA_ROPE // 2
    tab = np.zeros((3, n, LANES), np.float32)
    tab[0, :, :MLA_NOPE] = 1.0
    tab[0, :, MLA_NOPE:MLA_NOPE + half] = cos
    tab[0, :, MLA_NOPE + half:MLA_NOPE + MLA_ROPE] = cos
    tab[1, :, MLA_NOPE:MLA_NOPE + half] = -sin
    tab[2, :, MLA_NOPE + half:MLA_NOPE + MLA_ROPE] = sin
    return jnp.asarray(tab)


def _expand_matrix(src0, n_src, width, n_out):
    m = np.zeros((LANES, n_out), np.float32)
    for h in range(n_src):
        m[src0 + h, h * width:(h + 1) * width] = 1.0
    return m


def _ada_kernel(c_ref, w_ref, b_ref, o_ref):
    c = _silu(c_ref[...]).astype(BF16)
    o_ref[0] = _dot(c, w_ref[0].astype(BF16)) + b_ref[0]


def _ada(cc, w_ada, b_ada):
    nl, d, n3 = w_ada.shape
    r = cc.shape[0]
    tn = 1024
    return pl.pallas_call(
        _ada_kernel,
        grid=(nl, n3 // tn),
        in_specs=[pl.BlockSpec((r, d), lambda l, j: (0, 0)),
                  pl.BlockSpec((1, d, tn), lambda l, j: (l, 0, j)),
                  pl.BlockSpec((1, 1, tn), lambda l, j: (l, 0, j))],
        out_specs=pl.BlockSpec((1, r, tn), lambda l, j: (l, 0, j)),
        out_shape=jax.ShapeDtypeStruct((nl, r, n3), F32),
        compiler_params=_params("arbitrary", "arbitrary"),
        name="ada_mod",
    )(cc, w_ada, b_ada.reshape(nl, 1, n3))


def _token_block(nct, refs):
    if len(refs) == 1:
        return refs[0][0]
    return jnp.where(pl.program_id(1) < nct, refs[0][0], refs[1][0])


def _token_specs(arrays, nct, d):
    if len(arrays) == 1:
        return [pl.BlockSpec((1, ROWS, d), lambda bi, j: (bi, j, 0))]
    return [pl.BlockSpec((1, ROWS, d), lambda bi, j: (bi, jnp.minimum(j, nct - 1), 0)),
            pl.BlockSpec((1, ROWS, d), lambda bi, j: (bi, jnp.maximum(j - nct, 0), 0))]


def _halo_specs(arrays, nct, d, after):
    hb = ROWS // HALO
    offs = (0,) if len(arrays) == 1 else (0, nct)

    def spec(a, off):
        last = a.shape[1] // HALO - 1

        def index(bi, j):
            h = (j - off + 1) * hb if after else (j - off) * hb - 1
            return bi, jnp.clip(h, 0, last), 0
        return pl.BlockSpec((1, HALO, d), index)

    return [spec(a, off) for a, off in zip(arrays, offs)]


def _rope(x, tab_ref):
    half = MLA_ROPE // 2
    return (x * tab_ref[0] + pltpu.roll(x, LANES - half, 1) * tab_ref[1]
            + pltpu.roll(x, half, 1) * tab_ref[2])


def _inproj_kernel(nct, nx, *refs):
    x_refs, p_refs, n_refs, refs = refs[:nx], refs[nx:2 * nx], refs[2 * nx:3 * nx], refs[3 * nx:]
    (mod_ref, w_ref, wabt_ref, tab_ref, qw_ref, kvw_ref, wuq_ref, wuk_ref, wuv_ref, convw_ref,
     ob_ref, ogdn, omlaz, ogla, osmall, oabt, q_ref, k_ref, v_ref, xe_scr) = refs
    j = pl.program_id(1)
    nt = pl.num_programs(1)

    def modulated(x):
        h = x * lax.rsqrt(jnp.mean(x * x, axis=-1, keepdims=True) + EPS)
        return (h * (1.0 + mod_ref[0, 1:2, :]) + mod_ref[0, 0:1, :]).astype(BF16)

    hb = modulated(_token_block(nct, x_refs))
    halo = jnp.concatenate([_token_block(nct, p_refs), _token_block(nct, n_refs)], axis=0)
    ygdn = _dot(jnp.concatenate([hb, modulated(halo)], axis=0), w_ref[:, 0:_W_GDN])
    proj = {"gdn": ygdn[:ROWS]}
    yh = ygdn[ROWS:, :GDN_QKV]
    pos = _W_GDN
    for name, ref, n in (("mla", None, _W_MLA), ("mlaz", omlaz, _W_MLAZ), ("gla", ogla, _W_GLA),
                         ("small", osmall, LANES), ("kr", None, LANES)):
        y = _dot(hb, w_ref[:, pos:pos + n])
        if ref is None:
            proj[name] = y
        else:
            ref[0] = y
        pos += n
    oabt[0] = _dot_nt(wabt_ref[...], hb)

    has_prev = jnp.logical_and(j != 0, j != nct)
    has_next = jnp.logical_and(j != nct - 1, j != nt - 1)
    xe_scr[0:HALO, :] = jnp.where(has_prev, yh[:HALO], 0.0)
    xe_scr[HALO:HALO + ROWS, :] = proj["gdn"][:, :GDN_QKV]
    xe_scr[HALO + ROWS:, :] = jnp.where(has_next, yh[HALO:], 0.0)
    pad = (GDN_CONV - 1) // 2
    conv = jnp.zeros((ROWS, GDN_QKV), F32)
    for t in range(GDN_CONV):
        conv = conv + convw_ref[t:t + 1, :] * xe_scr[pl.ds(HALO - pad + t, ROWS), :]
    hqkv = _silu(conv)
    ob = ob_ref[...]
    qn = hqkv[:, :GDN_QK]
    kn = hqkv[:, GDN_QK:2 * GDN_QK]
    qn = qn * lax.rsqrt(_dot_sel2(qn * qn, ob) + EPS) * (GDN_DK ** -0.5)
    kn = kn * lax.rsqrt(_dot_sel2(kn * kn, ob) + EPS)
    ogdn[0] = jnp.concatenate([qn, kn, hqkv[:, 2 * GDN_QK:], proj["gdn"][:, GDN_QKV:]], axis=1)

    cq = proj["mla"][:, :MLA_Q_RANK]
    ckv = proj["mla"][:, MLA_Q_RANK:]
    cq = cq * lax.rsqrt(jnp.mean(cq * cq, axis=-1, keepdims=True) + EPS) * qw_ref[...]
    ckv = ckv * lax.rsqrt(jnp.mean(ckv * ckv, axis=-1, keepdims=True) + EPS) * kvw_ref[...]
    cqb = cq.astype(BF16)
    ckvb = ckv.astype(BF16)
    qf = _dot(cqb, wuq_ref[...])
    kf = _dot(ckvb, wuk_ref[...])
    vf = _dot(ckvb, wuv_ref[...])
    kr = _rope(proj["kr"], tab_ref)
    q_scale = MLA_SCALE * math.log2(math.e)
    for hd in range(MLA_HEADS):
        sl = slice(hd * LANES, (hd + 1) * LANES)
        q_ref[0, hd] = (_rope(qf[:, sl], tab_ref) * q_scale).astype(BF16)
        k_ref[0, hd] = (kf[:, sl] + kr).astype(BF16)
    ones = jnp.ones((ROWS, MXU_N - LANES), BF16)
    for hp in range(MLA_HEADS // 2):
        v_ref[0, hp] = jnp.concatenate(
            [vf[:, hp * LANES:(hp + 1) * LANES].astype(BF16), ones], axis=1)


def _inproj(xs, mod, w_p, wabt, tab, qw, kvw, wuq, wuk, wuv, convw, ob, nct):
    b, d = xs[0].shape[0], xs[0].shape[2]
    ta = sum(a.shape[1] for a in xs)
    nt = ta // ROWS
    widths = (_W_GDN, _W_MLAZ, _W_GLA, LANES)
    row = lambda bi, j: (bi, j, 0)
    full2 = lambda bi, j: (0, 0)
    hm = lambda bi, j: (bi, 0, j, 0)
    consts = (w_p, wabt)
    mla_consts = (qw, kvw, wuq, wuk, wuv, convw, ob)
    return pl.pallas_call(
        functools.partial(_inproj_kernel, nct, len(xs)),
        grid=(b, nt),
        in_specs=_token_specs(xs, nct, d) + _halo_specs(xs, nct, d, False)
        + _halo_specs(xs, nct, d, True)
        + [pl.BlockSpec((1, 3, d), lambda bi, j: (jnp.where(j < nct, 0, 1 + bi), 0, 0))]
        + [pl.BlockSpec(a.shape, full2) for a in consts]
        + [pl.BlockSpec((3, ROWS, LANES), lambda bi, j: (0, j, 0))]
        + [pl.BlockSpec(a.shape, full2) for a in mla_consts],
        out_specs=[pl.BlockSpec((1, ROWS, n), row) for n in widths]
        + [pl.BlockSpec((1, 4 * GDN_HEADS, ROWS), lambda bi, j: (bi, 0, j)),
           pl.BlockSpec((1, MLA_HEADS, ROWS, LANES), hm),
           pl.BlockSpec((1, MLA_HEADS, ROWS, LANES), hm),
           pl.BlockSpec((1, MLA_HEADS // 2, ROWS, MXU_N), hm)],
        out_shape=[jax.ShapeDtypeStruct((b, ta, n), F32) for n in widths]
        + [jax.ShapeDtypeStruct((b, 4 * GDN_HEADS, ta), F32),
           jax.ShapeDtypeStruct((b, MLA_HEADS, ta, LANES), BF16),
           jax.ShapeDtypeStruct((b, MLA_HEADS, ta, LANES), BF16),
           jax.ShapeDtypeStruct((b, MLA_HEADS // 2, ta, MXU_N), BF16)],
        scratch_shapes=[pltpu.VMEM((ROWS + 2 * HALO, GDN_QKV), F32)],
        compiler_params=_params("arbitrary", "arbitrary"),
        name="in_proj",
    )(*xs, *xs, *xs, mod, *consts, tab, *mla_consts)


def _scan_block_index(j, nct, nt, rev):
    if not rev:
        return j
    return jnp.where(j < nct, nct - 1 - j, nt - 1 - (j - nct))


def _head_lane_mask(n, width, h):
    lane = lax.broadcasted_iota(jnp.int32, (1, n), 1)
    return (lane >= h * width) & (lane < (h + 1) * width)


def _gdn_prep(d, x_ref, small_ref, abt_ref, prow_ref, pcol_ref, tri, exp_ref):
    q = x_ref[0, :, :GDN_QK]
    k = x_ref[0, :, GDN_QK:2 * GDN_QK]
    v = x_ref[0, :, 2 * GDN_QK:GDN_QKV]
    sm = small_ref[0]
    g_all = -jnp.exp(prow_ref[0:1, :]) * _softplus(sm + prow_ref[1:2, :])
    beta_all = jax.nn.sigmoid(sm)
    gc_all = _sel_dot(tri, g_all)
    g_t = -jnp.exp(pcol_ref[:, 0:1]) * _softplus(abt_ref[0] + pcol_ref[:, 1:2])
    gc_t = _dot_sel_nt(g_t, tri)
    gc_w = _dot_sel2(gc_all, exp_ref[d])
    beta_w = _dot_sel2(beta_all, exp_ref[2 + d])
    last = 0 if d == 1 else CHUNK - 1

    def chunk_last(x):
        return jnp.concatenate(
            [jnp.broadcast_to(x[c * CHUNK + last:c * CHUNK + last + 1, :], (CHUNK, x.shape[1]))
             for c in range(ROWS // CHUNK)], axis=0)

    gl_all = chunk_last(gc_all)
    gl_w = chunk_last(gc_w)
    kb = k * beta_w
    return dict(q=q, kbf=k.astype(BF16), kb=kb, vb=v * beta_w, kbg=kb * jnp.exp(gc_w),
                qg=q * jnp.exp(gc_w), kdec=k * jnp.exp(gl_w - gc_w),
                gc_all=gc_all, gc_t=gc_t, gl_all=gl_all)


def _gdn_stages(xf_ref, smf_ref, abtf_ref, xb_ref, smb_ref, abtb_ref, prow_ref, pcol_ref, tri_ref,
                ones_ref, negm_ref, lvm_ref, exp_ref, of_ref, obk_ref, s_scr):
    nh, dk = GDN_HEADS, GDN_DK
    nchunk = ROWS // CHUNK
    nlev = len(LEVELS)
    ones = ones_ref[...]
    dir_refs = ((xf_ref, smf_ref, abtf_ref), (xb_ref, smb_ref, abtb_ref))
    prep = []
    for d in range(2):
        prep.append(_gdn_prep(d, *dir_refs[d], prow_ref, pcol_ref, tri_ref[d], exp_ref))
        yield
    ri = lax.broadcasted_iota(jnp.int32, (ROWS, ROWS), 0)
    ci = lax.broadcasted_iota(jnp.int32, (ROWS, ROWS), 1)
    eye = (ri == ci).astype(BF16)
    chains = [(d, h) for d in range(2) for h in range(nh)]
    nc = len(chains)

    def compact(m):
        return functools.reduce(lambda a, b: a + b,
                                [m[c * CHUNK:(c + 1) * CHUNK] for c in range(nchunk)])

    def expand(mc):
        return jnp.concatenate([mc] * nchunk, axis=0) * ones

    def side_by_side(x):
        return jnp.concatenate([x[c * CHUNK:(c + 1) * CHUNK] for c in range(nchunk)], axis=1)

    low_c, a_intra = [], []
    for d, h in chains:
        p = prep[d]
        idx = d * nh + h
        sl = slice(h * dk, (h + 1) * dk)
        gc_col = side_by_side(jnp.broadcast_to(p["gc_all"][:, idx:idx + 1], (ROWS, CHUNK)))
        decay = jnp.exp(gc_col - p["gc_t"][idx:idx + 1, :] + negm_ref[d])
        lhs = jnp.concatenate([side_by_side(p["kb"][:, sl]), side_by_side(p["q"][:, sl])], axis=0)
        kt = jnp.concatenate([p["kbf"][:, sl]] * nchunk, axis=1) * ones
        kq = _dot_nt(lhs.astype(BF16), kt)
        low_c.append((kq[:CHUNK] * decay).astype(BF16))
        a_intra.append(expand((kq[CHUNK:] * decay).astype(BF16)))
        if h == nh - 1:
            yield

    eye_c = compact(eye)
    t_c = [eye_c - low_c[i] * lvm_ref[d, nlev - 1] for i, (d, h) in enumerate(chains)]
    t_inv = [expand(t) for t in t_c]
    for li in reversed(range(nlev - 1)):
        ys = [expand(_dot(low_c[i] * lvm_ref[d, li], t_inv[i]).astype(BF16))
              for i, (d, h) in enumerate(chains)]
        t_c = [t_c[i] - _dot(t_c[i], ys[i]).astype(BF16) for i in range(nc)]
        t_inv = [expand(t) for t in t_c]
        yield
    uw = []
    for i, (d, h) in enumerate(chains):
        sl = slice(h * dk, (h + 1) * dk)
        rhs = jnp.concatenate([prep[d]["vb"][:, sl], prep[d]["kbg"][:, sl]], axis=1)
        uw.append(_dot(t_inv[i], rhs.astype(BF16)))

    q2, ou, gb = [], [], []
    for i, (d, h) in enumerate(chains):
        sl = slice(h * dk, (h + 1) * dk)
        uwb = uw[i].astype(BF16)
        auw = _dot(a_intra[i], uwb)
        ou.append(auw[:, :GDN_DV])
        q2.append((prep[d]["qg"][:, sl] - auw[:, GDN_DV:]).astype(BF16))
        kd = prep[d]["kdec"][:, sl].astype(BF16)
        gb.append([_dot_tn(kd[c * CHUNK:(c + 1) * CHUNK], uwb[c * CHUNK:(c + 1) * CHUNK])
                   for c in range(nchunk)])
    yield
    states = [s_scr[i] for i in range(nc)]
    o_parts = [[None] * nchunk for _ in chains]
    for step in range(nchunk):
        for i, (d, h) in enumerate(chains):
            c = nchunk - 1 - step if d == 1 else step
            idx = d * nh + h
            r0 = c * CHUNK
            rs = slice(r0, r0 + CHUNK)
            sb = states[i].astype(BF16)
            o_parts[i][c] = _dot(q2[i][rs], sb) + ou[i][rs]
            states[i] = (states[i] * jnp.exp(prep[d]["gl_all"][r0:r0 + 1, idx:idx + 1])
                         - _dot(gb[i][c][:, GDN_DV:].astype(BF16), sb) + gb[i][c][:, :GDN_DV])
        yield
    for i in range(len(chains)):
        s_scr[i] = states[i]
    outs = [jnp.concatenate(o_parts[i], axis=0) for i in range(len(chains))]
    of_ref[0] = jnp.concatenate(outs[:nh], axis=1)
    obk_ref[0] = jnp.concatenate(outs[nh:], axis=1)


def _scan_row_specs(nct, nt, rev, width):
    blk_of = functools.partial(_scan_block_index, nct=nct, nt=nt, rev=rev)
    return pl.BlockSpec((1, ROWS, width), lambda bi, j: (bi, blk_of(j), 0))


def _gdn_specs(nct, nt, gdn_in, small, abt, prow, pcol, exp_m):
    cf, cb = _scan_consts(False), _scan_consts(True)
    stack = lambda name: jnp.stack([cf[name], cb[name]])
    full2 = lambda bi, j: (0, 0)
    full3 = lambda bi, j: (0, 0, 0)

    def dir_specs(rev):
        blk_of = functools.partial(_scan_block_index, nct=nct, nt=nt, rev=rev)
        return [
            _scan_row_specs(nct, nt, rev, GDN_QKV),
            _scan_row_specs(nct, nt, rev, LANES),
            pl.BlockSpec((1, 4 * GDN_HEADS, ROWS), lambda bi, j: (bi, 0, blk_of(j))),
        ]

    in_specs = dir_specs(False) + dir_specs(True) + [
        pl.BlockSpec(prow.shape, full2),
        pl.BlockSpec(pcol.shape, full2),
        pl.BlockSpec((2, ROWS, ROWS), full3),
        pl.BlockSpec((ROWS, ROWS), full2),
        pl.BlockSpec((2, CHUNK, ROWS), full3),
        pl.BlockSpec((2, len(LEVELS), CHUNK, ROWS), lambda bi, j: (0, 0, 0, 0)),
        pl.BlockSpec(exp_m.shape, full3),
    ]
    dir_args = [gdn_in, small, abt]
    args = dir_args + dir_args + [prow, pcol, stack("tri"), cf["ones"], stack("negm"),
                                  stack("lvm"), exp_m]
    out_specs = [_scan_row_specs(nct, nt, False, GDN_W), _scan_row_specs(nct, nt, True, GDN_W)]
    scratch = [pltpu.VMEM((2 * GDN_HEADS, GDN_DK, GDN_DV), F32)]
    return in_specs, args, out_specs, scratch


def _gla_stages(xf_ref, smf_ref, xb_ref, smb_ref, wg_ref, bg_ref, stack_ref, lvc_ref, kmask_ref,
                same_ref, eqk_ref, sbd_ref, of_ref, obk_ref, s_scr):
    nh, dk, dv = GLA_HEADS, GLA_DK, GLA_DV
    nchunk = ROWS // CHUNK
    nlev = len(LEVELS)
    n = GLA_QK
    out_head = [_head_lane_mask(GLA_W, dv, h) for h in range(nh)]
    eqk = eqk_ref[...]
    sbd = sbd_ref[...]
    x_refs, sm_refs, o_refs = (xf_ref, xb_ref), (smf_ref, smb_ref), (of_ref, obk_ref)
    dirs = range(2)
    xs = [x_refs[d][0] for d in dirs]
    q = [xs[d][:, :GLA_QK] * (dk ** -0.5) for d in dirs]
    k = [xs[d][:, GLA_QK:2 * GLA_QK] for d in dirs]
    v = [xs[d][:, 2 * GLA_QK:2 * GLA_QK + GLA_W] for d in dirs]
    vb = [v[d].astype(BF16) for d in dirs]
    gk = [_dot(sm_refs[d][0].astype(BF16), wg_ref[d]) + bg_ref[d] for d in dirs]
    la = [-_softplus(-gk[d]) * (1.0 / GLA_GATE_NORM) for d in dirs]
    la2 = [jnp.concatenate(_split2(la[d]), axis=1) for d in dirs]
    ys = [_dot(stack_ref[d], la2[d]) for d in dirs]
    cums = [ys[d][:, :n] + ys[d][:, n:] for d in dirs]
    piece = lambda d, i: cums[d][i * ROWS:(i + 1) * ROWS]
    bcum = [piece(d, 0) for d in dirs]
    last_row = [0 if d == 1 else CHUNK - 1 for d in dirs]
    blast = [jnp.concatenate(
        [jnp.broadcast_to(bcum[d][c * CHUNK + last_row[d]:c * CHUNK + last_row[d] + 1, :], (CHUNK, n))
         for c in range(nchunk)], axis=0) for d in dirs]
    la_next = [pltpu.roll(la[d], 1 if d == 1 else ROWS - 1, 0) for d in dirs]
    qg = [(q[d] * jnp.exp(bcum[d])).astype(BF16) for d in dirs]
    kdec = [(k[d] * jnp.exp(blast[d] - bcum[d])).astype(BF16) for d in dirs]
    yield

    lane4 = lax.broadcasted_iota(jnp.int32, (1, nchunk * n), 1) % n
    lane_head4 = [(lane4 >= h * dk) & (lane4 < (h + 1) * dk) for h in range(nh)]
    kmask = kmask_ref[...]
    same = same_ref[...]
    lv4 = [jnp.concatenate([lvc_ref[d]] * nh, axis=0) for d in dirs]
    acc = [jnp.zeros((nh * CHUNK, ROWS), F32) for d in dirs]
    for li in range(nlev):
        if li < nlev - 1:
            ql = [q[d] * jnp.exp(piece(d, 1 + li)) for d in dirs]
            kl = [(k[d] * jnp.exp(piece(d, nlev + li))).astype(BF16) for d in dirs]
        else:
            ql = q
            kl = [(k[d] * jnp.exp(la_next[d])).astype(BF16) for d in dirs]
        qc = [jnp.concatenate([ql[d][c * CHUNK:(c + 1) * CHUNK] for c in range(nchunk)], axis=1)
              for d in dirs]
        qs = [jnp.concatenate([jnp.where(lane_head4[h], qc[d], 0.0) for h in range(nh)],
                              axis=0).astype(BF16) for d in dirs]
        kt = [jnp.concatenate([kl[d]] * nchunk, axis=1) * kmask for d in dirs]
        ps = [_dot_nt(qs[d], kt[d]) for d in dirs]
        acc = [jnp.where(lv4[d] == float(li), ps[d], acc[d]) for d in dirs]
        yield
    accb = [acc[d].astype(BF16) for d in dirs]
    acc_bd = [jnp.concatenate(
        [jnp.concatenate([accb[d][h * CHUNK:(h + 1) * CHUNK]] * nchunk, axis=0) * same
         for h in range(nh)], axis=0) for d in dirs]
    pv = [_dot(acc_bd[d], vb[d]) for d in dirs]
    o = [_dot_sel2(q[d] * k[d], eqk) * v[d] for d in dirs]
    for h in range(nh):
        o = [o[d] + jnp.where(out_head[h], pv[d][h * ROWS:(h + 1) * ROWS], 0.0) for d in dirs]
    yield

    states = [s_scr[d] for d in dirs]
    o_parts = [[None] * nchunk for d in dirs]
    for step in range(nchunk):
        for d in dirs:
            c = nchunk - 1 - step if d == 1 else step
            rs = slice(c * CHUNK, (c + 1) * CHUNK)
            o_parts[d][c] = _dot_nt(qg[d][rs], states[d].astype(BF16))
            states[d] = (states[d] * jnp.exp(blast[d][c * CHUNK:c * CHUNK + 1, :])
                         + sbd * _dot_tn(vb[d][rs], kdec[d][rs]))
        yield
    for d in dirs:
        s_scr[d] = states[d]
        o_refs[d][0] = o[d] + jnp.concatenate(o_parts[d], axis=0)


def _gla_specs(nct, nt, gla_in, small, wg, bg, eqk, sbd):
    cf, cb = _scan_consts(False), _scan_consts(True)
    stack = jnp.stack([cf["stack"], cb["stack"]])
    lvc = jnp.stack([cf["lvc"], cb["lvc"]])
    assert GLA_QK == LANES
    kmask = _chunk_lane_mask()
    full2 = lambda bi, j: (0, 0)
    full3 = lambda bi, j: (0, 0, 0)
    in_specs = [
        _scan_row_specs(nct, nt, False, _W_GLA), _scan_row_specs(nct, nt, False, LANES),
        _scan_row_specs(nct, nt, True, _W_GLA), _scan_row_specs(nct, nt, True, LANES),
        pl.BlockSpec(wg.shape, full3),
        pl.BlockSpec(bg.shape, full3),
        pl.BlockSpec(stack.shape, full3),
        pl.BlockSpec(lvc.shape, full3),
        pl.BlockSpec(kmask.shape, full2),
        pl.BlockSpec((ROWS, ROWS), full2),
        pl.BlockSpec(eqk.shape, full2),
        pl.BlockSpec(sbd.shape, full2),
    ]
    args = [gla_in, small, gla_in, small, wg, bg, stack, lvc, kmask, cf["ones"], eqk, sbd]
    out_specs = [_scan_row_specs(nct, nt, False, GLA_W), _scan_row_specs(nct, nt, True, GLA_W)]
    scratch = [pltpu.VMEM((2, GLA_W, GLA_QK), F32)]
    return in_specs, args, out_specs, scratch


_N_GDN_IN, _N_GLA_IN = 13, 12


_N_GDN_ROWS_IN, _N_GLA_ROWS_IN = 6, 4


def _scan_kernel(nb, *refs):
    gdn_in, refs = refs[:_N_GDN_IN], refs[_N_GDN_IN:]
    gla_in, refs = refs[:_N_GLA_IN], refs[_N_GLA_IN:]
    gdn_out, gla_out, (gdn_state, gla_state) = refs[:2], refs[2:4], refs[4:]

    @pl.when(pl.program_id(1) == 0)
    def _():
        gdn_state[...] = jnp.zeros_like(gdn_state)
        gla_state[...] = jnp.zeros_like(gla_state)

    def of_batch(refs, bb, n_rows):
        return [r.at[pl.ds(bb, 1)] if i < n_rows else r for i, r in enumerate(refs)]

    stages = []
    for bb in range(nb):
        stages.append(_gdn_stages(*of_batch(gdn_in, bb, _N_GDN_ROWS_IN),
                                  *of_batch(gdn_out, bb, 2), gdn_state.at[bb]))
        stages.append(_gla_stages(*of_batch(gla_in, bb, _N_GLA_ROWS_IN),
                                  *of_batch(gla_out, bb, 2), gla_state.at[bb]))
    while stages:
        for s in list(stages):
            if next(s, StopIteration) is StopIteration:
                stages.remove(s)


def _scans(nct, gdn_args, gla_args):
    gdn_in = gdn_args[0]
    b, ta, _ = gdn_in.shape
    nt = ta // ROWS
    g_in, g_args, g_out, g_scr = _gdn_specs(nct, nt, *gdn_args)
    a_in, a_args, a_out, a_scr = _gla_specs(nct, nt, *gla_args)
    assert len(g_in) == _N_GDN_IN and len(a_in) == _N_GLA_IN
    nb = SCAN_BATCH if b % SCAN_BATCH == 0 else 1

    def widen(specs, n_rows):
        return [pl.BlockSpec((nb,) + tuple(s.block_shape[1:]), s.index_map) if i < n_rows else s
                for i, s in enumerate(specs)]

    return pl.pallas_call(
        functools.partial(_scan_kernel, nb),
        grid=(b // nb, nt),
        in_specs=widen(g_in, _N_GDN_ROWS_IN) + widen(a_in, _N_GLA_ROWS_IN),
        out_specs=widen(g_out, 2) + widen(a_out, 2),
        out_shape=[jax.ShapeDtypeStruct((b, ta, GDN_W), F32)] * 2
        + [jax.ShapeDtypeStruct((b, ta, GLA_W), F32)] * 2,
        scratch_shapes=[pltpu.VMEM((nb,) + tuple(s.shape), s.dtype) for s in g_scr + a_scr],
        compiler_params=_params("arbitrary", "arbitrary"),
        name="gdn_gla_scan",
    )(*g_args, *a_args)


def _attn_kernel(nct, off, n_ctx, q_ref, k_ref, v_ref, z_ref, o_ref):
    i = pl.program_id(2) + off

    def body(nk):
        parts = []
        half = ROWS // 2

        def by_row_halves(dot, lhs, rhs):
            return jnp.concatenate([dot(lhs[:half], rhs), dot(lhs[half:], rhs)], axis=0)

        for hh in range(ATTN_HEADS):
            q = q_ref[0, hh]
            kk = k_ref[0, hh, :nk, :]
            s = by_row_halves(_dot_nt, q, kk) if hh == 0 else _dot_nt(q, kk)
            m = jnp.max(s, axis=-1, keepdims=True)
            p = jnp.exp2(s - m).astype(BF16)
            vv = v_ref[0, hh // 2, :nk, :]
            pv = by_row_halves(_dot, p, vv) if hh == ATTN_HEADS - 1 else _dot(p, vv)
            o = pv[:, (hh % 2) * MLA_DV:(hh % 2 + 1) * MLA_DV] / pv[:, LANES:LANES + 1]
            parts.append(o)
        o_ref[0] = (jnp.concatenate(parts, axis=1) * _silu(z_ref[0])).astype(BF16)

    if off == 0:
        @pl.when(i < nct)
        def _():
            body(n_ctx)

        @pl.when(i >= nct)
        def _():
            body(k_ref.shape[2])
    else:
        body(k_ref.shape[2])


def _attention(q, k, v, z, nct, with_ctx):
    b, nh, ta, _ = q.shape
    nt = ta // ROWS
    off = 0 if with_ctx else nct
    nq = nt - off
    hg = ATTN_HEADS
    wo = hg * MLA_DV
    return pl.pallas_call(
        functools.partial(_attn_kernel, nct, off, nct * ROWS),
        grid=(b, nh // hg, nq),
        in_specs=[pl.BlockSpec((1, hg, ROWS, LANES), lambda bi, hp, i: (bi, hp, i + off, 0)),
                  pl.BlockSpec((1, hg, ta, LANES), lambda bi, hp, i: (bi, hp, 0, 0)),
                  pl.BlockSpec((1, hg // 2, ta, MXU_N), lambda bi, hp, i: (bi, hp, 0, 0)),
                  pl.BlockSpec((1, ROWS, wo), lambda bi, hp, i: (bi, i + off, hp))],
        out_specs=pl.BlockSpec((1, ROWS, wo), lambda bi, hp, i: (bi, i, hp)),
        out_shape=jax.ShapeDtypeStruct((b, nq * ROWS, MLA_W), BF16),
        compiler_params=_params("arbitrary", "arbitrary", "arbitrary"),
        name="mla_attn",
    )(q, k, v, z)


def _outproj_kernel(final, nct, nx, gf_ref, gb_ref, gz_ref, gnw_ref, af_ref, ab_ref, az_ref, anw_ref,
                    ob_ref, m_ref, mod_ref, w_ref, *rest):
    x_refs, rest = rest[:nx], rest[nx:]
    if final:
        fw_ref, o_ref = rest
    else:
        (o_ref,) = rest
    ob = ob_ref[...]

    def gated_head_norm(o, nw_ref, z):
        ms = _dot_sel2(o * o, ob) * (1.0 / GDN_DV)
        return (o * lax.rsqrt(ms + EPS) * nw_ref[...] * _silu(z)).astype(BF16)

    g = gated_head_norm(gf_ref[0] + gb_ref[0], gnw_ref, gz_ref[0])
    a = gated_head_norm(af_ref[0] + ab_ref[0], anw_ref, az_ref[0])
    y = _dot(g, w_ref[0:GDN_W, :])
    y = y + _dot(m_ref[0], w_ref[GDN_W:GDN_W + MLA_W, :])
    y = y + _dot(a, w_ref[GDN_W + MLA_W:, :])
    xn = _token_block(nct, x_refs) + mod_ref[0, 2:3, :] * y
    if final:
        xn = xn * lax.rsqrt(jnp.mean(xn * xn, axis=-1, keepdims=True) + EPS) * fw_ref[...]
    o_ref[0] = xn


def _outproj(gdn_f, gdn_b, gdn_in, gnw, gla_f, gla_b, gla_in, anw, ob, mla_o, xs, mod, w_out,
             nct, final_w):
    assert GDN_DV == GLA_DV and GDN_W == GLA_W
    b, d = xs[0].shape[0], xs[0].shape[2]
    ta = sum(a.shape[1] for a in xs)
    nt = ta // ROWS
    final = final_w is not None
    assert not (final and len(xs) > 1)
    off = nct if final else 0
    moff = off - (ta - mla_o.shape[1]) // ROWS
    nq = nt - off
    row = lambda bi, j: (bi, j + off, 0)
    full2 = lambda bi, j: (0, 0)
    stream = pl.BlockSpec
    x_specs = [stream((1, ROWS, d), row)] if len(xs) == 1 else _token_specs(xs, nct, d)
    wide = stream((1, ROWS, GDN_W), row)
    gz_blk = GDN_QKV // GDN_W
    az_blk = (2 * GLA_QK + GLA_W) // GLA_W
    in_specs = [wide, wide,
                stream((1, ROWS, GDN_W), lambda bi, j: (bi, j + off, gz_blk)),
                pl.BlockSpec(gnw.shape, full2),
                wide, wide,
                stream((1, ROWS, GLA_W), lambda bi, j: (bi, j + off, az_blk)),
                pl.BlockSpec(anw.shape, full2),
                pl.BlockSpec(ob.shape, full2),
                stream((1, ROWS, MLA_W), lambda bi, j: (bi, j + moff, 0)),
                pl.BlockSpec((1, 3, d), lambda bi, j: (jnp.where(j + off < nct, 0, 1 + bi), 0, 0)),
                pl.BlockSpec(w_out.shape, full2)] + x_specs
    args = [gdn_f, gdn_b, gdn_in, gnw, gla_f, gla_b, gla_in, anw, ob, mla_o, mod, w_out, *xs]
    if final:
        in_specs.append(pl.BlockSpec(final_w.shape, full2))
        args.append(final_w)
    return pl.pallas_call(
        functools.partial(_outproj_kernel, final, nct, len(xs)),
        grid=(b, nq),
        in_specs=in_specs,
        out_specs=pl.BlockSpec((1, ROWS, d), lambda bi, j: (bi, j, 0)),
        out_shape=jax.ShapeDtypeStruct((b, nq * ROWS, d), F32),
        compiler_params=_params("arbitrary", "arbitrary"),
        name="out_proj",
    )(*args)


def kernel(x, c, ctx, c_ctx, w_ada, b_ada, w_in, gdn_conv_w, gdn_a_log, gdn_dt_bias, gdn_norm_w,
           mla_q_norm_w, mla_w_uq, mla_kv_norm_w, mla_w_ukv, gla_w_gk, gla_b_gk, gla_norm_w,
           w_out, final_norm_w):
    b, seq, d = x.shape
    n_ctx = ctx.shape[1]
    depth = w_in.shape[0]
    assert n_ctx % ROWS == 0 and seq % ROWS == 0 and seq % ROPE_GRID_W == 0
    nct = n_ctx // ROWS
    nh = GDN_HEADS

    xs = (ctx, x)
    pad_rows = (-(1 + b)) % 8
    cc = jnp.concatenate([c_ctx[None, :], c, jnp.zeros((pad_rows, d), F32)], axis=0)
    mod_all = _ada(cc, w_ada, b_ada).reshape(depth, cc.shape[0], 3, d)

    perm = _inproj_perm()
    qperm = _mla_q_perm()
    kperm, vperm = _mla_kv_perm()
    tab = _rope_tables(n_ctx, seq)
    ob64 = _head_block_ones(GDN_W, GDN_DV)
    exp_m = jnp.asarray(np.stack(
        [_expand_matrix(_S_A + dd * nh, nh, GDN_DK, GDN_QK) for dd in range(2)]
        + [_expand_matrix(_S_B + dd * nh, nh, GDN_DK, GDN_QK) for dd in range(2)]), BF16)
    eqk_np = np.zeros((GLA_QK, GLA_W), np.float32)
    sbd_np = np.zeros((GLA_W, GLA_QK), np.float32)
    for h in range(GLA_HEADS):
        eqk_np[h * GLA_DK:(h + 1) * GLA_DK, h * GLA_DV:(h + 1) * GLA_DV] = 1.0
        sbd_np[h * GLA_DV:(h + 1) * GLA_DV, h * GLA_DK:(h + 1) * GLA_DK] = 1.0
    eqk = jnp.asarray(eqk_np, BF16)
    sbd = jnp.asarray(sbd_np)

    out = None
    for layer in range(depth):
        last = layer == depth - 1
        w_p = _take_cols(w_in[layer], perm).astype(BF16)
        wabt = w_in[layer][:, _O_A:_O_A + 4 * nh].T.astype(BF16)
        mod = mod_all[layer]
        wuq = _take_cols(mla_w_uq[layer], qperm).astype(BF16)
        wuk = _take_cols(mla_w_ukv[layer], kperm).astype(BF16)
        wuv = jnp.take(mla_w_ukv[layer], jnp.asarray(vperm), axis=1).astype(BF16)
        convw = jnp.concatenate(
            [gdn_conv_w[layer], jnp.zeros((8 - GDN_CONV, GDN_QKV), F32)], axis=0)
        gdn_in, mla_z, gla_in, small, abt, qh, kh, vh = _inproj(
            xs, mod, w_p, wabt, tab, mla_q_norm_w[layer][None, :], mla_kv_norm_w[layer][None, :],
            wuq, wuk, wuv, convw, ob64, nct)

        a_flat = gdn_a_log[layer].reshape(-1)
        dt_flat = gdn_dt_bias[layer].reshape(-1)
        prow = jnp.zeros((8, LANES), F32)
        prow = prow.at[0, _S_A:_S_A + 2 * nh].set(a_flat).at[1, _S_A:_S_A + 2 * nh].set(dt_flat)
        pcol = jnp.zeros((4 * nh, LANES), F32)
        pcol = pcol.at[0:2 * nh, 0].set(a_flat).at[0:2 * nh, 1].set(dt_flat)
        gnw = jnp.tile(gdn_norm_w[layer], nh)[None, :]

        anw = jnp.tile(gla_norm_w[layer], GLA_HEADS)[None, :]
        wg = jnp.zeros((2, LANES, GLA_QK), F32)
        for dd in range(2):
            r0 = _S_GLOW + dd * GLA_GATE_RANK
            wg = wg.at[dd, r0:r0 + GLA_GATE_RANK, :].set(gla_w_gk[layer, dd])
        gdn_f, gdn_b, gla_f, gla_b = _scans(
            nct, (gdn_in, small, abt, prow, pcol, exp_m),
            (gla_in, small, wg.astype(BF16), gla_b_gk[layer][:, None, :], eqk, sbd))

        mla_o = _attention(qh, kh, vh, mla_z, nct, with_ctx=not last)

        if last and len(xs) > 1:
            xs = (jnp.concatenate(xs, axis=1),)
        res = _outproj(gdn_f, gdn_b, gdn_in, gnw, gla_f, gla_b, gla_in, anw, ob64, mla_o, xs,
                       mod, w_out[layer].astype(BF16), nct, final_norm_w[None, :] if last else None)
        if last:
            out = res
        else:
            xs = (res,)
    return out
```

```python
import functools
import math

import numpy as np
import jax
import jax.numpy as jnp
from jax import lax
from jax.experimental import pallas as pl
from jax.experimental.pallas import tpu as pltpu

F32 = jnp.float32
BF16 = jnp.bfloat16
EPS = 1e-6

GDN_HEADS, GDN_DK, GDN_DV, GDN_CONV = 4, 64, 64, 5
GDN_QK = GDN_HEADS * GDN_DK
GDN_W = GDN_HEADS * GDN_DV
GDN_QKV = 2 * GDN_QK + GDN_W
MLA_HEADS, MLA_Q_RANK, MLA_KV_RANK = 8, 384, 256
MLA_NOPE, MLA_ROPE, MLA_DV = 64, 32, 64
MLA_W = MLA_HEADS * MLA_DV
MLA_SCALE = (MLA_NOPE + MLA_ROPE) ** -0.5
ROPE_THETA = 10000.0
ROPE_GRID_W = 64
GLA_HEADS, GLA_DK, GLA_DV = 4, 32, 64
GLA_QK = GLA_HEADS * GLA_DK
GLA_W = GLA_HEADS * GLA_DV
GLA_GATE_RANK = 16
GLA_GATE_NORM = 16.0
CHUNK = 64
LEVELS = (32, 16, 8, 4, 2, 1)

LANES = 128
MXU_N = 256
ROWS = 256
HALO = 8
ATTN_HEADS = 8
SCAN_BATCH = 2
VMEM_LIMIT = 56 * 1024 * 1024

_O_GDN_QKV, _O_GDN_Z, _O_A, _O_B = 0, 768, 1024, 1032
_O_CQ, _O_CKV, _O_KR, _O_MLA_Z = 1040, 1424, 1680, 1712
_O_GLA_Q, _O_GLA_K, _O_GLA_V, _O_GLA_Z, _O_GLOW = 2224, 2352, 2480, 2736, 2992
_S_A, _S_B, _S_GLOW = 0, 8, 16
_W_GDN, _W_MLA, _W_MLAZ, _W_GLA = 1024, 640, 512, 768
_N_IN_PAD = _W_GDN + _W_MLA + _W_MLAZ + _W_GLA + 2 * LANES


def _dot(a, b):
    return jnp.dot(a, b, preferred_element_type=F32)


def _dot_nt(a, b):
    return lax.dot_general(a, b, (((1,), (1,)), ((), ())), preferred_element_type=F32)


def _dot_tn(a, b):
    return lax.dot_general(a, b, (((0,), (0,)), ((), ())), preferred_element_type=F32)


def _split2(x):
    x1 = x.astype(BF16)
    return x1, (x - x1.astype(F32)).astype(BF16)


def _split3(x):
    x1 = x.astype(BF16)
    r1 = x - x1.astype(F32)
    x2 = r1.astype(BF16)
    x3 = (r1 - x2.astype(F32)).astype(BF16)
    return x1, x2, x3


def _sel_dot(m01, x):
    n = x.shape[1]
    y = _dot(m01, jnp.concatenate(_split3(x), axis=1))
    return y[:, :n] + y[:, n:2 * n] + y[:, 2 * n:]


def _dot_sel2(x, m01):
    n = x.shape[0]
    y = _dot(jnp.concatenate(_split2(x), axis=0), m01)
    return y[:n] + y[n:]


def _dot_sel_nt(x, m01):
    n = x.shape[0]
    y = _dot_nt(jnp.concatenate(_split3(x), axis=0), m01)
    return y[:n] + y[n:2 * n] + y[2 * n:]


def _softplus(x):
    return jnp.maximum(x, 0.0) + jnp.log1p(jnp.exp(-jnp.abs(x)))


def _silu(x):
    return x * jax.nn.sigmoid(x)


def _params(*sem):
    return pltpu.CompilerParams(dimension_semantics=sem, vmem_limit_bytes=VMEM_LIMIT)


def _scan_consts(rev):
    t = np.arange(ROWS)
    ch = t // CHUNK
    p = (CHUNK - 1 - t % CHUNK) if rev else (t % CHUNK)
    same = ch[:, None] == ch[None, :]
    tri = same & (p[None, :] <= p[:, None])
    mq, mk = [], []
    lv = np.full((ROWS, ROWS), -1.0, np.float32)
    for li, s in enumerate(LEVELS):
        blk = p // s
        mq.append(same & (p[None, :] > (blk * s)[:, None]) & (p[None, :] <= p[:, None]))
        mk.append(same & (p[None, :] > p[:, None]) & (p[None, :] <= ((blk + 1) * s)[:, None]))
        pair = same & ((blk % 2) == 1)[:, None] & (blk[None, :] == (blk - 1)[:, None])
        lv[pair] = li
    stack = np.concatenate([tri] + mq[:-1] + mk[:-1], axis=0)
    lvm = np.stack([sum((lv == li)[c * CHUNK:(c + 1) * CHUNK] for c in range(ROWS // CHUNK))
                    for li in range(len(LEVELS))])
    def blocks_side_by_side(m):
        return np.concatenate([m[c * CHUNK:(c + 1) * CHUNK, c * CHUNK:(c + 1) * CHUNK]
                               for c in range(ROWS // CHUNK)], axis=1)

    negm = blocks_side_by_side(np.where(tri, 0.0, -np.inf).astype(np.float32))
    lvc = blocks_side_by_side(lv)
    return dict(tri=jnp.asarray(tri, BF16), ones=jnp.asarray(same, BF16), negm=jnp.asarray(negm),
                lvc=jnp.asarray(lvc), lvm=jnp.asarray(lvm, BF16), stack=jnp.asarray(stack, BF16))


def _chunk_lane_mask():
    nchunk = ROWS // CHUNK
    return jnp.asarray(np.repeat(np.repeat(np.eye(nchunk), CHUNK, axis=0), LANES, axis=1), BF16)


def _head_block_ones(n, width):
    i = np.arange(n)
    return jnp.asarray((i[:, None] // width) == (i[None, :] // width), BF16)


def _inproj_perm():
    perm = np.full((_N_IN_PAD,), -1, np.int64)

    def put(src, n, at):
        perm[at:at + n] = np.arange(src, src + n)

    put(_O_GDN_QKV, GDN_QKV, 0)
    put(_O_GDN_Z, GDN_W, GDN_QKV)
    pos = _W_GDN
    put(_O_CQ, MLA_Q_RANK, pos)
    put(_O_CKV, MLA_KV_RANK, pos + MLA_Q_RANK)
    pos += _W_MLA
    put(_O_MLA_Z, MLA_W, pos)
    pos += _W_MLAZ
    put(_O_GLA_Q, GLA_QK, pos)
    put(_O_GLA_K, GLA_QK, pos + GLA_QK)
    put(_O_GLA_V, GLA_W, pos + 2 * GLA_QK)
    put(_O_GLA_Z, GLA_W, pos + 2 * GLA_QK + GLA_W)
    pos += _W_GLA
    put(_O_A, 2 * GDN_HEADS, pos + _S_A)
    put(_O_B, 2 * GDN_HEADS, pos + _S_B)
    put(_O_GLOW, 2 * GLA_GATE_RANK, pos + _S_GLOW)
    pos += LANES
    put(_O_KR, MLA_ROPE, pos + MLA_NOPE)
    return perm


def _take_cols(w, perm):
    cols = jnp.take(w, jnp.asarray(np.maximum(perm, 0)), axis=1)
    return jnp.where(jnp.asarray(perm >= 0)[None, :], cols, 0.0)


def _mla_q_perm():
    perm = np.full((MLA_HEADS * LANES,), -1, np.int64)
    d = MLA_NOPE + MLA_ROPE
    for h in range(MLA_HEADS):
        perm[h * LANES:h * LANES + d] = np.arange(h * d, (h + 1) * d)
    return perm


def _mla_kv_perm():
    dk = MLA_NOPE + MLA_DV
    kperm = np.full((MLA_HEADS * LANES,), -1, np.int64)
    vperm = np.zeros((MLA_W,), np.int64)
    for h in range(MLA_HEADS):
        kperm[h * LANES:h * LANES + MLA_NOPE] = np.arange(h * dk, h * dk + MLA_NOPE)
        vperm[h * MLA_DV:(h + 1) * MLA_DV] = np.arange(h * dk + MLA_NOPE, (h + 1) * dk)
    return kperm, vperm


def _rope_tables(n_ctx, n_lat):
    rows = n_lat // ROPE_GRID_W
    row = np.repeat(np.arange(rows, dtype=np.float32), ROPE_GRID_W)
    col = np.tile(np.arange(ROPE_GRID_W, dtype=np.float32), rows)
    n_freq = MLA_ROPE // 4
    inv = (ROPE_THETA ** (-np.arange(n_freq, dtype=np.float32) / n_freq)).astype(np.float32)
    ang = np.concatenate([row[:, None] * inv, col[:, None] * inv], axis=-1)
    cos = np.concatenate([np.ones((n_ctx, 2 * n_freq), np.float32), np.cos(ang)], 0)
    sin = np.concatenate([np.zeros((n_ctx, 2 * n_freq), np.float32), np.sin(ang)], 0)
    n = n_ctx + n_lat
    half = MLA_ROPE // 2
    tab = np.zeros((3, n, LANES), np.float32)
    tab[0, :, :MLA_NOPE] = 1.0
    tab[0, :, MLA_NOPE:MLA_NOPE + half] = cos
    tab[0, :, MLA_NOPE + half:MLA_NOPE + MLA_ROPE] = cos
    tab[1, :, MLA_NOPE:MLA_NOPE + half] = -sin
    tab[2, :, MLA_NOPE + half:MLA_NOPE + MLA_ROPE] = sin
    return jnp.asarray(tab)


def _expand_matrix(src0, n_src, width, n_out):
    m = np.zeros((LANES, n_out), np.float32)
    for h in range(n_src):
        m[src0 + h, h * width:(h + 1) * width] = 1.0
    return m


def _ada_kernel(c_ref, w_ref, b_ref, o_ref):
    c = _silu(c_ref[...]).astype(BF16)
    o_ref[0] = _dot(c, w_ref[0].astype(BF16)) + b_ref[0]


def _ada(cc, w_ada, b_ada):
    nl, d, n3 = w_ada.shape
    r = cc.shape[0]
    tn = 1024
    return pl.pallas_call(
        _ada_kernel,
        grid=(nl, n3 // tn),
        in_specs=[pl.BlockSpec((r, d), lambda l, j: (0, 0)),
                  pl.BlockSpec((1, d, tn), lambda l, j: (l, 0, j)),
                  pl.BlockSpec((1, 1, tn), lambda l, j: (l, 0, j))],
        out_specs=pl.BlockSpec((1, r, tn), lambda l, j: (l, 0, j)),
        out_shape=jax.ShapeDtypeStruct((nl, r, n3), F32),
        compiler_params=_params("arbitrary", "arbitrary"),
        name="ada_mod",
    )(cc, w_ada, b_ada.reshape(nl, 1, n3))


def _token_block(nct, refs):
    if len(refs) == 1:
        return refs[0][0]
    return jnp.where(pl.program_id(1) < nct, refs[0][0], refs[1][0])


def _token_specs(arrays, nct, d):
    if len(arrays) == 1:
        return [pl.BlockSpec((1, ROWS, d), lambda bi, j: (bi, j, 0))]
    return [pl.BlockSpec((1, ROWS, d), lambda bi, j: (bi, jnp.minimum(j, nct - 1), 0)),
            pl.BlockSpec((1, ROWS, d), lambda bi, j: (bi, jnp.maximum(j - nct, 0), 0))]


def _halo_specs(arrays, nct, d, after):
    hb = ROWS // HALO
    offs = (0,) if len(arrays) == 1 else (0, nct)

    def spec(a, off):
        last = a.shape[1] // HALO - 1

        def index(bi, j):
            h = (j - off + 1) * hb if after else (j - off) * hb - 1
            return bi, jnp.clip(h, 0, last), 0
        return pl.BlockSpec((1, HALO, d), index)

    return [spec(a, off) for a, off in zip(arrays, offs)]


def _rope(x, tab_ref):
    half = MLA_ROPE // 2
    return (x * tab_ref[0] + pltpu.roll(x, LANES - half, 1) * tab_ref[1]
            + pltpu.roll(x, half, 1) * tab_ref[2])


def _inproj_kernel(nct, nx, *refs):
    x_refs, p_refs, n_refs, refs = refs[:nx], refs[nx:2 * nx], refs[2 * nx:3 * nx], refs[3 * nx:]
    (mod_ref, w_ref, wabt_ref, tab_ref, qw_ref, kvw_ref, wuq_ref, wuk_ref, wuv_ref, convw_ref,
     ob_ref, ogdn, omlaz, ogla, osmall, oabt, q_ref, k_ref, v_ref, xe_scr) = refs
    j = pl.program_id(1)
    nt = pl.num_programs(1)

    def modulated(x):
        h = x * lax.rsqrt(jnp.mean(x * x, axis=-1, keepdims=True) + EPS)
        return (h * (1.0 + mod_ref[0, 1:2, :]) + mod_ref[0, 0:1, :]).astype(BF16)

    hb = modulated(_token_block(nct, x_refs))
    halo = jnp.concatenate([_token_block(nct, p_refs), _token_block(nct, n_refs)], axis=0)
    ygdn = _dot(jnp.concatenate([hb, modulated(halo)], axis=0), w_ref[:, 0:_W_GDN])
    proj = {"gdn": ygdn[:ROWS]}
    yh = ygdn[ROWS:, :GDN_QKV]
    pos = _W_GDN
    for name, ref, n in (("mla", None, _W_MLA), ("mlaz", omlaz, _W_MLAZ), ("gla", ogla, _W_GLA),
                         ("small", osmall, LANES), ("kr", None, LANES)):
        y = _dot(hb, w_ref[:, pos:pos + n])
        if ref is None:
            proj[name] = y
        else:
            ref[0] = y
        pos += n
    oabt[0] = _dot_nt(wabt_ref[...], hb)

    has_prev = jnp.logical_and(j != 0, j != nct)
    has_next = jnp.logical_and(j != nct - 1, j != nt - 1)
    xe_scr[0:HALO, :] = jnp.where(has_prev, yh[:HALO], 0.0)
    xe_scr[HALO:HALO + ROWS, :] = proj["gdn"][:, :GDN_QKV]
    xe_scr[HALO + ROWS:, :] = jnp.where(has_next, yh[HALO:], 0.0)
    pad = (GDN_CONV - 1) // 2
    conv = jnp.zeros((ROWS, GDN_QKV), F32)
    for t in range(GDN_CONV):
        conv = conv + convw_ref[t:t + 1, :] * xe_scr[pl.ds(HALO - pad + t, ROWS), :]
    hqkv = _silu(conv)
    ob = ob_ref[...]
    qn = hqkv[:, :GDN_QK]
    kn = hqkv[:, GDN_QK:2 * GDN_QK]
    qn = qn * lax.rsqrt(_dot_sel2(qn * qn, ob) + EPS) * (GDN_DK ** -0.5)
    kn = kn * lax.rsqrt(_dot_sel2(kn * kn, ob) + EPS)
    ogdn[0] = jnp.concatenate([qn, kn, hqkv[:, 2 * GDN_QK:], proj["gdn"][:, GDN_QKV:]], axis=1)

    cq = proj["mla"][:, :MLA_Q_RANK]
    ckv = proj["mla"][:, MLA_Q_RANK:]
    cq = cq * lax.rsqrt(jnp.mean(cq * cq, axis=-1, keepdims=True) + EPS) * qw_ref[...]
    ckv = ckv * lax.rsqrt(jnp.mean(ckv * ckv, axis=-1, keepdims=True) + EPS) * kvw_ref[...]
    cqb = cq.astype(BF16)
    ckvb = ckv.astype(BF16)
    qf = _dot(cqb, wuq_ref[...])
    kf = _dot(ckvb, wuk_ref[...])
    vf = _dot(ckvb, wuv_ref[...])
    kr = _rope(proj["kr"], tab_ref)
    q_scale = MLA_SCALE * math.log2(math.e)
    for hd in range(MLA_HEADS):
        sl = slice(hd * LANES, (hd + 1) * LANES)
        q_ref[0, hd] = (_rope(qf[:, sl], tab_ref) * q_scale).astype(BF16)
        k_ref[0, hd] = (kf[:, sl] + kr).astype(BF16)
    ones = jnp.ones((ROWS, MXU_N - LANES), BF16)
    for hp in range(MLA_HEADS // 2):
        v_ref[0, hp] = jnp.concatenate(
            [vf[:, hp * LANES:(hp + 1) * LANES].astype(BF16), ones], axis=1)


def _inproj(xs, mod, w_p, wabt, tab, qw, kvw, wuq, wuk, wuv, convw, ob, nct):
    b, d = xs[0].shape[0], xs[0].shape[2]
    ta = sum(a.shape[1] for a in xs)
    nt = ta // ROWS
    widths = (_W_GDN, _W_MLAZ, _W_GLA, LANES)
    row = lambda bi, j: (bi, j, 0)
    full2 = lambda bi, j: (0, 0)
    hm = lambda bi, j: (bi, 0, j, 0)
    consts = (w_p, wabt)
    mla_consts = (qw, kvw, wuq, wuk, wuv, convw, ob)
    return pl.pallas_call(
        functools.partial(_inproj_kernel, nct, len(xs)),
        grid=(b, nt),
        in_specs=_token_specs(xs, nct, d) + _halo_specs(xs, nct, d, False)
        + _halo_specs(xs, nct, d, True)
        + [pl.BlockSpec((1, 3, d), lambda bi, j: (jnp.where(j < nct, 0, 1 + bi), 0, 0))]
        + [pl.BlockSpec(a.shape, full2) for a in consts]
        + [pl.BlockSpec((3, ROWS, LANES), lambda bi, j: (0, j, 0))]
        + [pl.BlockSpec(a.shape, full2) for a in mla_consts],
        out_specs=[pl.BlockSpec((1, ROWS, n), row) for n in widths]
        + [pl.BlockSpec((1, 4 * GDN_HEADS, ROWS), lambda bi, j: (bi, 0, j)),
           pl.BlockSpec((1, MLA_HEADS, ROWS, LANES), hm),
           pl.BlockSpec((1, MLA_HEADS, ROWS, LANES), hm),
           pl.BlockSpec((1, MLA_HEADS // 2, ROWS, MXU_N), hm)],
        out_shape=[jax.ShapeDtypeStruct((b, ta, n), F32) for n in widths]
        + [jax.ShapeDtypeStruct((b, 4 * GDN_HEADS, ta), F32),
           jax.ShapeDtypeStruct((b, MLA_HEADS, ta, LANES), BF16),
           jax.ShapeDtypeStruct((b, MLA_HEADS, ta, LANES), BF16),
           jax.ShapeDtypeStruct((b, MLA_HEADS // 2, ta, MXU_N), BF16)],
        scratch_shapes=[pltpu.VMEM((ROWS + 2 * HALO, GDN_QKV), F32)],
        compiler_params=_params("arbitrary", "arbitrary"),
        name="in_proj",
    )(*xs, *xs, *xs, mod, *consts, tab, *mla_consts)


def _scan_block_index(j, nct, nt, rev):
    if not rev:
        return j
    return jnp.where(j < nct, nct - 1 - j, nt - 1 - (j - nct))


def _head_lane_mask(n, width, h):
    lane = lax.broadcasted_iota(jnp.int32, (1, n), 1)
    return (lane >= h * width) & (lane < (h + 1) * width)


def _gdn_prep(d, x_ref, small_ref, abt_ref, prow_ref, pcol_ref, tri, exp_ref):
    q = x_ref[0, :, :GDN_QK]
    k = x_ref[0, :, GDN_QK:2 * GDN_QK]
    v = x_ref[0, :, 2 * GDN_QK:GDN_QKV]
    sm = small_ref[0]
    g_all = -jnp.exp(prow_ref[0:1, :]) * _softplus(sm + prow_ref[1:2, :])
    beta_all = jax.nn.sigmoid(sm)
    gc_all = _sel_dot(tri, g_all)
    g_t = -jnp.exp(pcol_ref[:, 0:1]) * _softplus(abt_ref[0] + pcol_ref[:, 1:2])
    gc_t = _dot_sel_nt(g_t, tri)
    gc_w = _dot_sel2(gc_all, exp_ref[d])
    beta_w = _dot_sel2(beta_all, exp_ref[2 + d])
    last = 0 if d == 1 else CHUNK - 1

    def chunk_last(x):
        return jnp.concatenate(
            [jnp.broadcast_to(x[c * CHUNK + last:c * CHUNK + last + 1, :], (CHUNK, x.shape[1]))
             for c in range(ROWS // CHUNK)], axis=0)

    gl_all = chunk_last(gc_all)
    gl_w = chunk_last(gc_w)
    kb = k * beta_w
    return dict(q=q, kbf=k.astype(BF16), kb=kb, vb=v * beta_w, kbg=kb * jnp.exp(gc_w),
                qg=q * jnp.exp(gc_w), kdec=k * jnp.exp(gl_w - gc_w),
                gc_all=gc_all, gc_t=gc_t, gl_all=gl_all)


def _gdn_stages(xf_ref, smf_ref, abtf_ref, xb_ref, smb_ref, abtb_ref, prow_ref, pcol_ref, tri_ref,
                ones_ref, negm_ref, lvm_ref, exp_ref, of_ref, obk_ref, s_scr):
    nh, dk = GDN_HEADS, GDN_DK
    nchunk = ROWS // CHUNK
    nlev = len(LEVELS)
    ones = ones_ref[...]
    dir_refs = ((xf_ref, smf_ref, abtf_ref), (xb_ref, smb_ref, abtb_ref))
    prep = []
    for d in range(2):
        prep.append(_gdn_prep(d, *dir_refs[d], prow_ref, pcol_ref, tri_ref[d], exp_ref))
        yield
    ri = lax.broadcasted_iota(jnp.int32, (ROWS, ROWS), 0)
    ci = lax.broadcasted_iota(jnp.int32, (ROWS, ROWS), 1)
    eye = (ri == ci).astype(BF16)
    chains = [(d, h) for d in range(2) for h in range(nh)]
    nc = len(chains)

    def compact(m):
        return functools.reduce(lambda a, b: a + b,
                                [m[c * CHUNK:(c + 1) * CHUNK] for c in range(nchunk)])

    def expand(mc):
        return jnp.concatenate([mc] * nchunk, axis=0) * ones

    def side_by_side(x):
        return jnp.concatenate([x[c * CHUNK:(c + 1) * CHUNK] for c in range(nchunk)], axis=1)

    low_c, a_intra = [], []
    for d, h in chains:
        p = prep[d]
        idx = d * nh + h
        sl = slice(h * dk, (h + 1) * dk)
        gc_col = side_by_side(jnp.broadcast_to(p["gc_all"][:, idx:idx + 1], (ROWS, CHUNK)))
        decay = jnp.exp(gc_col - p["gc_t"][idx:idx + 1, :] + negm_ref[d])
        lhs = jnp.concatenate([side_by_side(p["kb"][:, sl]), side_by_side(p["q"][:, sl])], axis=0)
        kt = jnp.concatenate([p["kbf"][:, sl]] * nchunk, axis=1) * ones
        kq = _dot_nt(lhs.astype(BF16), kt)
        low_c.append((kq[:CHUNK] * decay).astype(BF16))
        a_intra.append(expand((kq[CHUNK:] * decay).astype(BF16)))
        if h == nh - 1:
            yield

    eye_c = compact(eye)
    t_c = [eye_c - low_c[i] * lvm_ref[d, nlev - 1] for i, (d, h) in enumerate(chains)]
    t_inv = [expand(t) for t in t_c]
    for li in reversed(range(nlev - 1)):
        ys = [expand(_dot(low_c[i] * lvm_ref[d, li], t_inv[i]).astype(BF16))
              for i, (d, h) in enumerate(chains)]
        t_c = [t_c[i] - _dot(t_c[i], ys[i]).astype(BF16) for i in range(nc)]
        t_inv = [expand(t) for t in t_c]
        yield
    uw = []
    for i, (d, h) in enumerate(chains):
        sl = slice(h * dk, (h + 1) * dk)
        rhs = jnp.concatenate([prep[d]["vb"][:, sl], prep[d]["kbg"][:, sl]], axis=1)
        uw.append(_dot(t_inv[i], rhs.astype(BF16)))

    q2, ou, gb = [], [], []
    for i, (d, h) in enumerate(chains):
        sl = slice(h * dk, (h + 1) * dk)
        uwb = uw[i].astype(BF16)
        auw = _dot(a_intra[i], uwb)
        ou.append(auw[:, :GDN_DV])
        q2.append((prep[d]["qg"][:, sl] - auw[:, GDN_DV:]).astype(BF16))
        kd = prep[d]["kdec"][:, sl].astype(BF16)
        gb.append([_dot_tn(kd[c * CHUNK:(c + 1) * CHUNK], uwb[c * CHUNK:(c + 1) * CHUNK])
                   for c in range(nchunk)])
    yield
    states = [s_scr[i] for i in range(nc)]
    o_parts = [[None] * nchunk for _ in chains]
    for step in range(nchunk):
        for i, (d, h) in enumerate(chains):
            c = nchunk - 1 - step if d == 1 else step
            idx = d * nh + h
            r0 = c * CHUNK
            rs = slice(r0, r0 + CHUNK)
            sb = states[i].astype(BF16)
            o_parts[i][c] = _dot(q2[i][rs], sb) + ou[i][rs]
            states[i] = (states[i] * jnp.exp(prep[d]["gl_all"][r0:r0 + 1, idx:idx + 1])
                         - _dot(gb[i][c][:, GDN_DV:].astype(BF16), sb) + gb[i][c][:, :GDN_DV])
        yield
    for i in range(len(chains)):
        s_scr[i] = states[i]
    outs = [jnp.concatenate(o_parts[i], axis=0) for i in range(len(chains))]
    of_ref[0] = jnp.concatenate(outs[:nh], axis=1)
    obk_ref[0] = jnp.concatenate(outs[nh:], axis=1)


def _scan_row_specs(nct, nt, rev, width):
    blk_of = functools.partial(_scan_block_index, nct=nct, nt=nt, rev=rev)
    return pl.BlockSpec((1, ROWS, width), lambda bi, j: (bi, blk_of(j), 0))


def _gdn_specs(nct, nt, gdn_in, small, abt, prow, pcol, exp_m):
    cf, cb = _scan_consts(False), _scan_consts(True)
    stack = lambda name: jnp.stack([cf[name], cb[name]])
    full2 = lambda bi, j: (0, 0)
    full3 = lambda bi, j: (0, 0, 0)

    def dir_specs(rev):
        blk_of = functools.partial(_scan_block_index, nct=nct, nt=nt, rev=rev)
        return [
            _scan_row_specs(nct, nt, rev, GDN_QKV),
            _scan_row_specs(nct, nt, rev, LANES),
            pl.BlockSpec((1, 4 * GDN_HEADS, ROWS), lambda bi, j: (bi, 0, blk_of(j))),
        ]

    in_specs = dir_specs(False) + dir_specs(True) + [
        pl.BlockSpec(prow.shape, full2),
        pl.BlockSpec(pcol.shape, full2),
        pl.BlockSpec((2, ROWS, ROWS), full3),
        pl.BlockSpec((ROWS, ROWS), full2),
        pl.BlockSpec((2, CHUNK, ROWS), full3),
        pl.BlockSpec((2, len(LEVELS), CHUNK, ROWS), lambda bi, j: (0, 0, 0, 0)),
        pl.BlockSpec(exp_m.shape, full3),
    ]
    dir_args = [gdn_in, small, abt]
    args = dir_args + dir_args + [prow, pcol, stack("tri"), cf["ones"], stack("negm"),
                                  stack("lvm"), exp_m]
    out_specs = [_scan_row_specs(nct, nt, False, GDN_W), _scan_row_specs(nct, nt, True, GDN_W)]
    scratch = [pltpu.VMEM((2 * GDN_HEADS, GDN_DK, GDN_DV), F32)]
    return in_specs, args, out_specs, scratch


def _gla_stages(xf_ref, smf_ref, xb_ref, smb_ref, wg_ref, bg_ref, stack_ref, lvc_ref, kmask_ref,
                same_ref, eqk_ref, sbd_ref, of_ref, obk_ref, s_scr):
    nh, dk, dv = GLA_HEADS, GLA_DK, GLA_DV
    nchunk = ROWS // CHUNK
    nlev = len(LEVELS)
    n = GLA_QK
    out_head = [_head_lane_mask(GLA_W, dv, h) for h in range(nh)]
    eqk = eqk_ref[...]
    sbd = sbd_ref[...]
    x_refs, sm_refs, o_refs = (xf_ref, xb_ref), (smf_ref, smb_ref), (of_ref, obk_ref)
    dirs = range(2)
    xs = [x_refs[d][0] for d in dirs]
    q = [xs[d][:, :GLA_QK] * (dk ** -0.5) for d in dirs]
    k = [xs[d][:, GLA_QK:2 * GLA_QK] for d in dirs]
    v = [xs[d][:, 2 * GLA_QK:2 * GLA_QK + GLA_W] for d in dirs]
    vb = [v[d].astype(BF16) for d in dirs]
    gk = [_dot(sm_refs[d][0].astype(BF16), wg_ref[d]) + bg_ref[d] for d in dirs]
    la = [-_softplus(-gk[d]) * (1.0 / GLA_GATE_NORM) for d in dirs]
    la2 = [jnp.concatenate(_split2(la[d]), axis=1) for d in dirs]
    ys = [_dot(stack_ref[d], la2[d]) for d in dirs]
    cums = [ys[d][:, :n] + ys[d][:, n:] for d in dirs]
    piece = lambda d, i: cums[d][i * ROWS:(i + 1) * ROWS]
    bcum = [piece(d, 0) for d in dirs]
    last_row = [0 if d == 1 else CHUNK - 1 for d in dirs]
    blast = [jnp.concatenate(
        [jnp.broadcast_to(bcum[d][c * CHUNK + last_row[d]:c * CHUNK + last_row[d] + 1, :], (CHUNK, n))
         for c in range(nchunk)], axis=0) for d in dirs]
    la_next = [pltpu.roll(la[d], 1 if d == 1 else ROWS - 1, 0) for d in dirs]
    qg = [(q[d] * jnp.exp(bcum[d])).astype(BF16) for d in dirs]
    kdec = [(k[d] * jnp.exp(blast[d] - bcum[d])).astype(BF16) for d in dirs]
    yield

    lane4 = lax.broadcasted_iota(jnp.int32, (1, nchunk * n), 1) % n
    lane_head4 = [(lane4 >= h * dk) & (lane4 < (h + 1) * dk) for h in range(nh)]
    kmask = kmask_ref[...]
    same = same_ref[...]
    lv4 = [jnp.concatenate([lvc_ref[d]] * nh, axis=0) for d in dirs]
    acc = [jnp.zeros((nh * CHUNK, ROWS), F32) for d in dirs]
    for li in range(nlev):
        if li < nlev - 1:
            ql = [q[d] * jnp.exp(piece(d, 1 + li)) for d in dirs]
            kl = [(k[d] * jnp.exp(piece(d, nlev + li))).astype(BF16) for d in dirs]
        else:
            ql = q
            kl = [(k[d] * jnp.exp(la_next[d])).astype(BF16) for d in dirs]
        qc = [jnp.concatenate([ql[d][c * CHUNK:(c + 1) * CHUNK] for c in range(nchunk)], axis=1)
              for d in dirs]
        qs = [jnp.concatenate([jnp.where(lane_head4[h], qc[d], 0.0) for h in range(nh)],
                              axis=0).astype(BF16) for d in dirs]
        kt = [jnp.concatenate([kl[d]] * nchunk, axis=1) * kmask for d in dirs]
        ps = [_dot_nt(qs[d], kt[d]) for d in dirs]
        acc = [jnp.where(lv4[d] == float(li), ps[d], acc[d]) for d in dirs]
        yield
    accb = [acc[d].astype(BF16) for d in dirs]
    acc_bd = [jnp.concatenate(
        [jnp.concatenate([accb[d][h * CHUNK:(h + 1) * CHUNK]] * nchunk, axis=0) * same
         for h in range(nh)], axis=0) for d in dirs]
    pv = [_dot(acc_bd[d], vb[d]) for d in dirs]
    o = [_dot_sel2(q[d] * k[d], eqk) * v[d] for d in dirs]
    for h in range(nh):
        o = [o[d] + jnp.where(out_head[h], pv[d][h * ROWS:(h + 1) * ROWS], 0.0) for d in dirs]
    yield

    states = [s_scr[d] for d in dirs]
    o_parts = [[None] * nchunk for d in dirs]
    for step in range(nchunk):
        for d in dirs:
            c = nchunk - 1 - step if d == 1 else step
            rs = slice(c * CHUNK, (c + 1) * CHUNK)
            o_parts[d][c] = _dot_nt(qg[d][rs], states[d].astype(BF16))
            states[d] = (states[d] * jnp.exp(blast[d][c * CHUNK:c * CHUNK + 1, :])
                         + sbd * _dot_tn(vb[d][rs], kdec[d][rs]))
        yield
    for d in dirs:
        s_scr[d] = states[d]
        o_refs[d][0] = o[d] + jnp.concatenate(o_parts[d], axis=0)


def _gla_specs(nct, nt, gla_in, small, wg, bg, eqk, sbd):
    cf, cb = _scan_consts(False), _scan_consts(True)
    stack = jnp.stack([cf["stack"], cb["stack"]])
    lvc = jnp.stack([cf["lvc"], cb["lvc"]])
    assert GLA_QK == LANES
    kmask = _chunk_lane_mask()
    full2 = lambda bi, j: (0, 0)
    full3 = lambda bi, j: (0, 0, 0)
    in_specs = [
        _scan_row_specs(nct, nt, False, _W_GLA), _scan_row_specs(nct, nt, False, LANES),
        _scan_row_specs(nct, nt, True, _W_GLA), _scan_row_specs(nct, nt, True, LANES),
        pl.BlockSpec(wg.shape, full3),
        pl.BlockSpec(bg.shape, full3),
        pl.BlockSpec(stack.shape, full3),
        pl.BlockSpec(lvc.shape, full3),
        pl.BlockSpec(kmask.shape, full2),
        pl.BlockSpec((ROWS, ROWS), full2),
        pl.BlockSpec(eqk.shape, full2),
        pl.BlockSpec(sbd.shape, full2),
    ]
    args = [gla_in, small, gla_in, small, wg, bg, stack, lvc, kmask, cf["ones"], eqk, sbd]
    out_specs = [_scan_row_specs(nct, nt, False, GLA_W), _scan_row_specs(nct, nt, True, GLA_W)]
    scratch = [pltpu.VMEM((2, GLA_W, GLA_QK), F32)]
    return in_specs, args, out_specs, scratch


_N_GDN_IN, _N_GLA_IN = 13, 12


_N_GDN_ROWS_IN, _N_GLA_ROWS_IN = 6, 4


def _scan_kernel(nb, *refs):
    gdn_in, refs = refs[:_N_GDN_IN], refs[_N_GDN_IN:]
    gla_in, refs = refs[:_N_GLA_IN], refs[_N_GLA_IN:]
    gdn_out, gla_out, (gdn_state, gla_state) = refs[:2], refs[2:4], refs[4:]

    @pl.when(pl.program_id(1) == 0)
    def _():
        gdn_state[...] = jnp.zeros_like(gdn_state)
        gla_state[...] = jnp.zeros_like(gla_state)

    def of_batch(refs, bb, n_rows):
        return [r.at[pl.ds(bb, 1)] if i < n_rows else r for i, r in enumerate(refs)]

    stages = []
    for bb in range(nb):
        stages.append(_gdn_stages(*of_batch(gdn_in, bb, _N_GDN_ROWS_IN),
                                  *of_batch(gdn_out, bb, 2), gdn_state.at[bb]))
        stages.append(_gla_stages(*of_batch(gla_in, bb, _N_GLA_ROWS_IN),
                                  *of_batch(gla_out, bb, 2), gla_state.at[bb]))
    while stages:
        for s in list(stages):
            if next(s, StopIteration) is StopIteration:
                stages.remove(s)


def _scans(nct, gdn_args, gla_args):
    gdn_in = gdn_args[0]
    b, ta, _ = gdn_in.shape
    nt = ta // ROWS
    g_in, g_args, g_out, g_scr = _gdn_specs(nct, nt, *gdn_args)
    a_in, a_args, a_out, a_scr = _gla_specs(nct, nt, *gla_args)
    assert len(g_in) == _N_GDN_IN and len(a_in) == _N_GLA_IN
    nb = SCAN_BATCH if b % SCAN_BATCH == 0 else 1

    def widen(specs, n_rows):
        return [pl.BlockSpec((nb,) + tuple(s.block_shape[1:]), s.index_map) if i < n_rows else s
                for i, s in enumerate(specs)]

    return pl.pallas_call(
        functools.partial(_scan_kernel, nb),
        grid=(b // nb, nt),
        in_specs=widen(g_in, _N_GDN_ROWS_IN) + widen(a_in, _N_GLA_ROWS_IN),
        out_specs=widen(g_out, 2) + widen(a_out, 2),
        out_shape=[jax.ShapeDtypeStruct((b, ta, GDN_W), F32)] * 2
        + [jax.ShapeDtypeStruct((b, ta, GLA_W), F32)] * 2,
        scratch_shapes=[pltpu.VMEM((nb,) + tuple(s.shape), s.dtype) for s in g_scr + a_scr],
        compiler_params=_params("arbitrary", "arbitrary"),
        name="gdn_gla_scan",
    )(*g_args, *a_args)


def _attn_kernel(nct, off, n_ctx, q_ref, k_ref, v_ref, z_ref, o_ref):
    i = pl.program_id(2) + off

    def body(nk):
        parts = []
        half = ROWS // 2

        def by_row_halves(dot, lhs, rhs):
            return jnp.concatenate([dot(lhs[:half], rhs), dot(lhs[half:], rhs)], axis=0)

        for hh in range(ATTN_HEADS):
            q = q_ref[0, hh]
            kk = k_ref[0, hh, :nk, :]
            s = by_row_halves(_dot_nt, q, kk) if hh == 0 else _dot_nt(q, kk)
            m = jnp.max(s, axis=-1, keepdims=True)
            p = jnp.exp2(s - m).astype(BF16)
            vv = v_ref[0, hh // 2, :nk, :]
            pv = by_row_halves(_dot, p, vv) if hh == ATTN_HEADS - 1 else _dot(p, vv)
            o = pv[:, (hh % 2) * MLA_DV:(hh % 2 + 1) * MLA_DV] / pv[:, LANES:LANES + 1]
            parts.append(o)
        o_ref[0] = (jnp.concatenate(parts, axis=1) * _silu(z_ref[0])).astype(BF16)

    if off == 0:
        @pl.when(i < nct)
        def _():
            body(n_ctx)

        @pl.when(i >= nct)
        def _():
            body(k_ref.shape[2])
    else:
        body(k_ref.shape[2])


def _attention(q, k, v, z, nct, with_ctx):
    b, nh, ta, _ = q.shape
    nt = ta // ROWS
    off = 0 if with_ctx else nct
    nq = nt - off
    hg = ATTN_HEADS
    wo = hg * MLA_DV
    return pl.pallas_call(
        functools.partial(_attn_kernel, nct, off, nct * ROWS),
        grid=(b, nh // hg, nq),
        in_specs=[pl.BlockSpec((1, hg, ROWS, LANES), lambda bi, hp, i: (bi, hp, i + off, 0)),
                  pl.BlockSpec((1, hg, ta, LANES), lambda bi, hp, i: (bi, hp, 0, 0)),
                  pl.BlockSpec((1, hg // 2, ta, MXU_N), lambda bi, hp, i: (bi, hp, 0, 0)),
                  pl.BlockSpec((1, ROWS, wo), lambda bi, hp, i: (bi, i + off, hp))],
        out_specs=pl.BlockSpec((1, ROWS, wo), lambda bi, hp, i: (bi, i, hp)),
        out_shape=jax.ShapeDtypeStruct((b, nq * ROWS, MLA_W), BF16),
        compiler_params=_params("arbitrary", "arbitrary", "arbitrary"),
        name="mla_attn",
    )(q, k, v, z)


def _outproj_kernel(final, nct, nx, off, nb, gf_ref, gb_ref, gz_ref, gnw_ref, af_ref, ab_ref, az_ref,
                    anw_ref, ob_ref, m_ref, mod_ref, w_ref, *rest):
    x_refs, rest = rest[:nx], rest[nx:]
    if final:
        fw_ref, o_ref = rest
    else:
        (o_ref,) = rest
    ob = ob_ref[...]
    is_ctx = pl.program_id(1) + off < nct

    def gated_head_norm(o, nw_ref, z):
        ms = _dot_sel2(o * o, ob) * (1.0 / GDN_DV)
        return (o * lax.rsqrt(ms + EPS) * nw_ref[...] * _silu(z)).astype(BF16)

    for bb in range(nb):
        at = lambda r: r.at[pl.ds(bb, 1)]
        g = gated_head_norm(at(gf_ref)[0] + at(gb_ref)[0], gnw_ref, at(gz_ref)[0])
        a = gated_head_norm(at(af_ref)[0] + at(ab_ref)[0], anw_ref, at(az_ref)[0])
        y = _dot(g, w_ref[0:GDN_W, :])
        y = y + _dot(at(m_ref)[0], w_ref[GDN_W:GDN_W + MLA_W, :])
        y = y + _dot(a, w_ref[GDN_W + MLA_W:, :])
        mod_row = jnp.where(is_ctx, 0, 1 + pl.program_id(0) * nb + bb)
        gate = mod_ref[pl.ds(mod_row, 1), 2:3, :][0]
        xn = _token_block(nct, [at(r) for r in x_refs]) + gate * y
        if final:
            xn = xn * lax.rsqrt(jnp.mean(xn * xn, axis=-1, keepdims=True) + EPS) * fw_ref[...]
        o_ref[bb] = xn


def _outproj(gdn_f, gdn_b, gdn_in, gnw, gla_f, gla_b, gla_in, anw, ob, mla_o, xs, mod, w_out,
             nct, final_w):
    assert GDN_DV == GLA_DV and GDN_W == GLA_W
    b, d = xs[0].shape[0], xs[0].shape[2]
    ta = sum(a.shape[1] for a in xs)
    nt = ta // ROWS
    final = final_w is not None
    assert not (final and len(xs) > 1)
    off = nct if final else 0
    moff = off - (ta - mla_o.shape[1]) // ROWS
    nq = nt - off
    row = lambda bi, j: (bi, j + off, 0)
    full2 = lambda bi, j: (0, 0)
    stream = pl.BlockSpec
    x_specs = [stream((1, ROWS, d), row)] if len(xs) == 1 else _token_specs(xs, nct, d)
    wide = stream((1, ROWS, GDN_W), row)
    gz_blk = GDN_QKV // GDN_W
    az_blk = (2 * GLA_QK + GLA_W) // GLA_W
    in_specs = [wide, wide,
                stream((1, ROWS, GDN_W), lambda bi, j: (bi, j + off, gz_blk)),
                pl.BlockSpec(gnw.shape, full2),
                wide, wide,
                stream((1, ROWS, GLA_W), lambda bi, j: (bi, j + off, az_blk)),
                pl.BlockSpec(anw.shape, full2),
                pl.BlockSpec(ob.shape, full2),
                stream((1, ROWS, MLA_W), lambda bi, j: (bi, j + moff, 0)),
                pl.BlockSpec(mod.shape, lambda bi, j: (0, 0, 0)),
                pl.BlockSpec(w_out.shape, full2)] + x_specs
    args = [gdn_f, gdn_b, gdn_in, gnw, gla_f, gla_b, gla_in, anw, ob, mla_o, mod, w_out, *xs]
    if final:
        in_specs.append(pl.BlockSpec(final_w.shape, full2))
        args.append(final_w)
    nb = SCAN_BATCH if b % SCAN_BATCH == 0 else 1
    in_specs = [pl.BlockSpec((nb,) + tuple(sp.block_shape[1:]), sp.index_map)
                if len(sp.block_shape) == 3 and sp.block_shape[1] == ROWS else sp for sp in in_specs]
    return pl.pallas_call(
        functools.partial(_outproj_kernel, final, nct, len(xs), off, nb),
        grid=(b // nb, nq),
        in_specs=in_specs,
        out_specs=pl.BlockSpec((nb, ROWS, d), lambda bi, j: (bi, j, 0)),
        out_shape=jax.ShapeDtypeStruct((b, nq * ROWS, d), F32),
        compiler_params=_params("arbitrary", "arbitrary"),
        name="out_proj",
    )(*args)


def kernel(x, c, ctx, c_ctx, w_ada, b_ada, w_in, gdn_conv_w, gdn_a_log, gdn_dt_bias, gdn_norm_w,
           mla_q_norm_w, mla_w_uq, mla_kv_norm_w, mla_w_ukv, gla_w_gk, gla_b_gk, gla_norm_w,
           w_out, final_norm_w):
    b, seq, d = x.shape
    n_ctx = ctx.shape[1]
    depth = w_in.shape[0]
    assert n_ctx % ROWS == 0 and seq % ROWS == 0 and seq % ROPE_GRID_W == 0
    nct = n_ctx // ROWS
    nh = GDN_HEADS

    xs = (ctx, x)
    pad_rows = (-(1 + b)) % 8
    cc = jnp.concatenate([c_ctx[None, :], c, jnp.zeros((pad_rows, d), F32)], axis=0)
    mod_all = _ada(cc, w_ada, b_ada).reshape(depth, cc.shape[0], 3, d)

    perm = _inproj_perm()
    qperm = _mla_q_perm()
    kperm, vperm = _mla_kv_perm()
    tab = _rope_tables(n_ctx, seq)
    ob64 = _head_block_ones(GDN_W, GDN_DV)
    exp_m = jnp.asarray(np.stack(
        [_expand_matrix(_S_A + dd * nh, nh, GDN_DK, GDN_QK) for dd in range(2)]
        + [_expand_matrix(_S_B + dd * nh, nh, GDN_DK, GDN_QK) for dd in range(2)]), BF16)
    eqk_np = np.zeros((GLA_QK, GLA_W), np.float32)
    sbd_np = np.zeros((GLA_W, GLA_QK), np.float32)
    for h in range(GLA_HEADS):
        eqk_np[h * GLA_DK:(h + 1) * GLA_DK, h * GLA_DV:(h + 1) * GLA_DV] = 1.0
        sbd_np[h * GLA_DV:(h + 1) * GLA_DV, h * GLA_DK:(h + 1) * GLA_DK] = 1.0
    eqk = jnp.asarray(eqk_np, BF16)
    sbd = jnp.asarray(sbd_np)

    out = None
    for layer in range(depth):
        last = layer == depth - 1
        w_p = _take_cols(w_in[layer], perm).astype(BF16)
        wabt = w_in[layer][:, _O_A:_O_A + 4 * nh].T.astype(BF16)
        mod = mod_all[layer]
        wuq = _take_cols(mla_w_uq[layer], qperm).astype(BF16)
        wuk = _take_cols(mla_w_ukv[layer], kperm).astype(BF16)
        wuv = jnp.take(mla_w_ukv[layer], jnp.asarray(vperm), axis=1).astype(BF16)
        convw = jnp.concatenate(
            [gdn_conv_w[layer], jnp.zeros((8 - GDN_CONV, GDN_QKV), F32)], axis=0)
        gdn_in, mla_z, gla_in, small, abt, qh, kh, vh = _inproj(
            xs, mod, w_p, wabt, tab, mla_q_norm_w[layer][None, :], mla_kv_norm_w[layer][None, :],
            wuq, wuk, wuv, convw, ob64, nct)

        a_flat = gdn_a_log[layer].reshape(-1)
        dt_flat = gdn_dt_bias[layer].reshape(-1)
        prow = jnp.zeros((8, LANES), F32)
        prow = prow.at[0, _S_A:_S_A + 2 * nh].set(a_flat).at[1, _S_A:_S_A + 2 * nh].set(dt_flat)
        pcol = jnp.zeros((4 * nh, LANES), F32)
        pcol = pcol.at[0:2 * nh, 0].set(a_flat).at[0:2 * nh, 1].set(dt_flat)
        gnw = jnp.tile(gdn_norm_w[layer], nh)[None, :]

        anw = jnp.tile(gla_norm_w[layer], GLA_HEADS)[None, :]
        wg = jnp.zeros((2, LANES, GLA_QK), F32)
        for dd in range(2):
            r0 = _S_GLOW + dd * GLA_GATE_RANK
            wg = wg.at[dd, r0:r0 + GLA_GATE_RANK, :].set(gla_w_gk[layer, dd])
        gdn_f, gdn_b, gla_f, gla_b = _scans(
            nct, (gdn_in, small, abt, prow, pcol, exp_m),
            (gla_in, small, wg.astype(BF16), gla_b_gk[layer][:, None, :], eqk, sbd))

        mla_o = _attention(qh, kh, vh, mla_z, nct, with_ctx=not last)

        if last and len(xs) > 1:
            xs = (jnp.concatenate(xs, axis=1),)
        res = _outproj(gdn_f, gdn_b, gdn_in, gnw, gla_f, gla_b, gla_in, anw, ob64, mla_o, xs,
                       mod, w_out[layer].astype(BF16), nct, final_norm_w[None, :] if last else None)
        if last:
            out = res
        else:
            xs = (res,)
    return out
```

```python
import functools
import math

import numpy as np
import jax
import jax.numpy as jnp
from jax import lax
from jax.experimental import pallas as pl
from jax.experimental.pallas import tpu as pltpu

F32 = jnp.float32
BF16 = jnp.bfloat16
EPS = 1e-6

GDN_HEADS, GDN_DK, GDN_DV, GDN_CONV = 4, 64, 64, 5
GDN_QK = GDN_HEADS * GDN_DK
GDN_W = GDN_HEADS * GDN_DV
GDN_QKV = 2 * GDN_QK + GDN_W
MLA_HEADS, MLA_Q_RANK, MLA_KV_RANK = 8, 384, 256
MLA_NOPE, MLA_ROPE, MLA_DV = 64, 32, 64
MLA_W = MLA_HEADS * MLA_DV
MLA_SCALE = (MLA_NOPE + MLA_ROPE) ** -0.5
ROPE_THETA = 10000.0
ROPE_GRID_W = 64
GLA_HEADS, GLA_DK, GLA_DV = 4, 32, 64
GLA_QK = GLA_HEADS * GLA_DK
GLA_W = GLA_HEADS * GLA_DV
GLA_GATE_RANK = 16
GLA_GATE_NORM = 16.0
CHUNK = 64
LEVELS = (32, 16, 8, 4, 2, 1)

LANES = 128
MXU_N = 256
ROWS = 256
HALO = 8
ATTN_HEADS = 8
SCAN_BATCH = 2
OUTPROJ_BATCH = 4
VMEM_LIMIT = 56 * 1024 * 1024

_O_GDN_QKV, _O_GDN_Z, _O_A, _O_B = 0, 768, 1024, 1032
_O_CQ, _O_CKV, _O_KR, _O_MLA_Z = 1040, 1424, 1680, 1712
_O_GLA_Q, _O_GLA_K, _O_GLA_V, _O_GLA_Z, _O_GLOW = 2224, 2352, 2480, 2736, 2992
_S_A, _S_B, _S_GLOW = 0, 8, 16
_W_GDN, _W_MLA, _W_MLAZ, _W_GLA = 1024, 640, 512, 768
_N_IN_PAD = _W_GDN + _W_MLA + _W_MLAZ + _W_GLA + 2 * LANES


def _dot(a, b):
    return jnp.dot(a, b, preferred_element_type=F32)


def _dot_nt(a, b):
    return lax.dot_general(a, b, (((1,), (1,)), ((), ())), preferred_element_type=F32)


def _dot_tn(a, b):
    return lax.dot_general(a, b, (((0,), (0,)), ((), ())), preferred_element_type=F32)


def _split2(x):
    x1 = x.astype(BF16)
    return x1, (x - x1.astype(F32)).astype(BF16)


def _split3(x):
    x1 = x.astype(BF16)
    r1 = x - x1.astype(F32)
    x2 = r1.astype(BF16)
    x3 = (r1 - x2.astype(F32)).astype(BF16)
    return x1, x2, x3


def _sel_dot(m01, x):
    n = x.shape[1]
    y = _dot(m01, jnp.concatenate(_split3(x), axis=1))
    return y[:, :n] + y[:, n:2 * n] + y[:, 2 * n:]


def _dot_sel2(x, m01):
    n = x.shape[0]
    y = _dot(jnp.concatenate(_split2(x), axis=0), m01)
    return y[:n] + y[n:]


def _dot_sel_nt(x, m01):
    n = x.shape[0]
    y = _dot_nt(jnp.concatenate(_split3(x), axis=0), m01)
    return y[:n] + y[n:2 * n] + y[2 * n:]


def _softplus(x):
    return jnp.maximum(x, 0.0) + jnp.log1p(jnp.exp(-jnp.abs(x)))


def _silu(x):
    return x * jax.nn.sigmoid(x)


def _params(*sem):
    return pltpu.CompilerParams(dimension_semantics=sem, vmem_limit_bytes=VMEM_LIMIT)


def _scan_consts(rev):
    t = np.arange(ROWS)
    ch = t // CHUNK
    p = (CHUNK - 1 - t % CHUNK) if rev else (t % CHUNK)
    same = ch[:, None] == ch[None, :]
    tri = same & (p[None, :] <= p[:, None])
    mq, mk = [], []
    lv = np.full((ROWS, ROWS), -1.0, np.float32)
    for li, s in enumerate(LEVELS):
        blk = p // s
        mq.append(same & (p[None, :] > (blk * s)[:, None]) & (p[None, :] <= p[:, None]))
        mk.append(same & (p[None, :] > p[:, None]) & (p[None, :] <= ((blk + 1) * s)[:, None]))
        pair = same & ((blk % 2) == 1)[:, None] & (blk[None, :] == (blk - 1)[:, None])
        lv[pair] = li
    stack = np.concatenate([tri] + mq[:-1] + mk[:-1], axis=0)
    lvm = np.stack([sum((lv == li)[c * CHUNK:(c + 1) * CHUNK] for c in range(ROWS // CHUNK))
                    for li in range(len(LEVELS))])
    def blocks_side_by_side(m):
        return np.concatenate([m[c * CHUNK:(c + 1) * CHUNK, c * CHUNK:(c + 1) * CHUNK]
                               for c in range(ROWS // CHUNK)], axis=1)

    negm = blocks_side_by_side(np.where(tri, 0.0, -np.inf).astype(np.float32))
    lvc = blocks_side_by_side(lv)
    return dict(tri=jnp.asarray(tri, BF16), ones=jnp.asarray(same, BF16), negm=jnp.asarray(negm),
                lvc=jnp.asarray(lvc), lvm=jnp.asarray(lvm, BF16), stack=jnp.asarray(stack, BF16))


def _chunk_lane_mask():
    nchunk = ROWS // CHUNK
    return jnp.asarray(np.repeat(np.repeat(np.eye(nchunk), CHUNK, axis=0), LANES, axis=1), BF16)


def _head_block_ones(n, width):
    i = np.arange(n)
    return jnp.asarray((i[:, None] // width) == (i[None, :] // width), BF16)


def _inproj_perm():
    perm = np.full((_N_IN_PAD,), -1, np.int64)

    def put(src, n, at):
        perm[at:at + n] = np.arange(src, src + n)

    put(_O_GDN_QKV, GDN_QKV, 0)
    put(_O_GDN_Z, GDN_W, GDN_QKV)
    pos = _W_GDN
    put(_O_CQ, MLA_Q_RANK, pos)
    put(_O_CKV, MLA_KV_RANK, pos + MLA_Q_RANK)
    pos += _W_MLA
    put(_O_MLA_Z, MLA_W, pos)
    pos += _W_MLAZ
    put(_O_GLA_Q, GLA_QK, pos)
    put(_O_GLA_K, GLA_QK, pos + GLA_QK)
    put(_O_GLA_V, GLA_W, pos + 2 * GLA_QK)
    put(_O_GLA_Z, GLA_W, pos + 2 * GLA_QK + GLA_W)
    pos += _W_GLA
    put(_O_A, 2 * GDN_HEADS, pos + _S_A)
    put(_O_B, 2 * GDN_HEADS, pos + _S_B)
    put(_O_GLOW, 2 * GLA_GATE_RANK, pos + _S_GLOW)
    pos += LANES
    put(_O_KR, MLA_ROPE, pos + MLA_NOPE)
    return perm


def _take_cols(w, perm):
    cols = jnp.take(w, jnp.asarray(np.maximum(perm, 0)), axis=1)
    return jnp.where(jnp.asarray(perm >= 0)[None, :], cols, 0.0)


def _mla_q_perm():
    perm = np.full((MLA_HEADS * LANES,), -1, np.int64)
    d = MLA_NOPE + MLA_ROPE
    for h in range(MLA_HEADS):
        perm[h * LANES:h * LANES + d] = np.arange(h * d, (h + 1) * d)
    return perm


def _mla_kv_perm():
    dk = MLA_NOPE + MLA_DV
    kperm = np.full((MLA_HEADS * LANES,), -1, np.int64)
    vperm = np.zeros((MLA_W,), np.int64)
    for h in range(MLA_HEADS):
        kperm[h * LANES:h * LANES + MLA_NOPE] = np.arange(h * dk, h * dk + MLA_NOPE)
        vperm[h * MLA_DV:(h + 1) * MLA_DV] = np.arange(h * dk + MLA_NOPE, (h + 1) * dk)
    return kperm, vperm


def _rope_tables(n_ctx, n_lat):
    rows = n_lat // ROPE_GRID_W
    row = np.repeat(np.arange(rows, dtype=np.float32), ROPE_GRID_W)
    col = np.tile(np.arange(ROPE_GRID_W, dtype=np.float32), rows)
    n_freq = MLA_ROPE // 4
    inv = (ROPE_THETA ** (-np.arange(n_freq, dtype=np.float32) / n_freq)).astype(np.float32)
    ang = np.concatenate([row[:, None] * inv, col[:, None] * inv], axis=-1)
    cos = np.concatenate([np.ones((n_ctx, 2 * n_freq), np.float32), np.cos(ang)], 0)
    sin = np.concatenate([np.zeros((n_ctx, 2 * n_freq), np.float32), np.sin(ang)], 0)
    n = n_ctx + n_lat
    half = MLA_ROPE // 2
    tab = np.zeros((3, n, LANES), np.float32)
    tab[0, :, :MLA_NOPE] = 1.0
    tab[0, :, MLA_NOPE:MLA_NOPE + half] = cos
    tab[0, :, MLA_NOPE + half:MLA_NOPE + MLA_ROPE] = cos
    tab[1, :, MLA_NOPE:MLA_NOPE + half] = -sin
    tab[2, :, MLA_NOPE + half:MLA_NOPE + MLA_ROPE] = sin
    return jnp.asarray(tab)


def _expand_matrix(src0, n_src, width, n_out):
    m = np.zeros((LANES, n_out), np.float32)
    for h in range(n_src):
        m[src0 + h, h * width:(h + 1) * width] = 1.0
    return m


def _ada_kernel(c_ref, w_ref, b_ref, o_ref):
    c = _silu(c_ref[...]).astype(BF16)
    o_ref[0] = _dot(c, w_ref[0].astype(BF16)) + b_ref[0]


def _ada(cc, w_ada, b_ada):
    nl, d, n3 = w_ada.shape
    r = cc.shape[0]
    tn = 1024
    return pl.pallas_call(
        _ada_kernel,
        grid=(nl, n3 // tn),
        in_specs=[pl.BlockSpec((r, d), lambda l, j: (0, 0)),
                  pl.BlockSpec((1, d, tn), lambda l, j: (l, 0, j)),
                  pl.BlockSpec((1, 1, tn), lambda l, j: (l, 0, j))],
        out_specs=pl.BlockSpec((1, r, tn), lambda l, j: (l, 0, j)),
        out_shape=jax.ShapeDtypeStruct((nl, r, n3), F32),
        compiler_params=_params("arbitrary", "arbitrary"),
        name="ada_mod",
    )(cc, w_ada, b_ada.reshape(nl, 1, n3))


def _token_block(nct, refs):
    if len(refs) == 1:
        return refs[0][0]
    return jnp.where(pl.program_id(1) < nct, refs[0][0], refs[1][0])


def _token_specs(arrays, nct, d):
    if len(arrays) == 1:
        return [pl.BlockSpec((1, ROWS, d), lambda bi, j: (bi, j, 0))]
    return [pl.BlockSpec((1, ROWS, d), lambda bi, j: (bi, jnp.minimum(j, nct - 1), 0)),
            pl.BlockSpec((1, ROWS, d), lambda bi, j: (bi, jnp.maximum(j - nct, 0), 0))]


def _halo_specs(arrays, nct, d, after):
    hb = ROWS // HALO
    offs = (0,) if len(arrays) == 1 else (0, nct)

    def spec(a, off):
        last = a.shape[1] // HALO - 1

        def index(bi, j):
            h = (j - off + 1) * hb if after else (j - off) * hb - 1
            return bi, jnp.clip(h, 0, last), 0
        return pl.BlockSpec((1, HALO, d), index)

    return [spec(a, off) for a, off in zip(arrays, offs)]


def _rope(x, tab_ref):
    half = MLA_ROPE // 2
    return (x * tab_ref[0] + pltpu.roll(x, LANES - half, 1) * tab_ref[1]
            + pltpu.roll(x, half, 1) * tab_ref[2])


def _inproj_kernel(nct, nx, *refs):
    x_refs, p_refs, n_refs, refs = refs[:nx], refs[nx:2 * nx], refs[2 * nx:3 * nx], refs[3 * nx:]
    (mod_ref, w_ref, wabt_ref, tab_ref, qw_ref, kvw_ref, wuq_ref, wuk_ref, wuv_ref, convw_ref,
     ob_ref, ogdn, omlaz, ogla, osmall, oabt, q_ref, k_ref, v_ref, xe_scr) = refs
    j = pl.program_id(1)
    nt = pl.num_programs(1)

    def modulated(x):
        h = x * lax.rsqrt(jnp.mean(x * x, axis=-1, keepdims=True) + EPS)
        return (h * (1.0 + mod_ref[0, 1:2, :]) + mod_ref[0, 0:1, :]).astype(BF16)

    hb = modulated(_token_block(nct, x_refs))
    halo = jnp.concatenate([_token_block(nct, p_refs), _token_block(nct, n_refs)], axis=0)
    ygdn = _dot(jnp.concatenate([hb, modulated(halo)], axis=0), w_ref[:, 0:_W_GDN])
    proj = {"gdn": ygdn[:ROWS]}
    yh = ygdn[ROWS:, :GDN_QKV]
    pos = _W_GDN
    for name, ref, n in (("mla", None, _W_MLA), ("mlaz", omlaz, _W_MLAZ), ("gla", ogla, _W_GLA),
                         ("small", osmall, LANES), ("kr", None, LANES)):
        y = _dot(hb, w_ref[:, pos:pos + n])
        if ref is None:
            proj[name] = y
        else:
            ref[0] = y
        pos += n
    oabt[0] = _dot_nt(wabt_ref[...], hb)

    has_prev = jnp.logical_and(j != 0, j != nct)
    has_next = jnp.logical_and(j != nct - 1, j != nt - 1)
    xe_scr[0:HALO, :] = jnp.where(has_prev, yh[:HALO], 0.0)
    xe_scr[HALO:HALO + ROWS, :] = proj["gdn"][:, :GDN_QKV]
    xe_scr[HALO + ROWS:, :] = jnp.where(has_next, yh[HALO:], 0.0)
    pad = (GDN_CONV - 1) // 2
    conv = jnp.zeros((ROWS, GDN_QKV), F32)
    for t in range(GDN_CONV):
        conv = conv + convw_ref[t:t + 1, :] * xe_scr[pl.ds(HALO - pad + t, ROWS), :]
    hqkv = _silu(conv)
    ob = ob_ref[...]
    qn = hqkv[:, :GDN_QK]
    kn = hqkv[:, GDN_QK:2 * GDN_QK]
    qn = qn * lax.rsqrt(_dot_sel2(qn * qn, ob) + EPS) * (GDN_DK ** -0.5)
    kn = kn * lax.rsqrt(_dot_sel2(kn * kn, ob) + EPS)
    ogdn[0] = jnp.concatenate([qn, kn, hqkv[:, 2 * GDN_QK:], proj["gdn"][:, GDN_QKV:]], axis=1)

    cq = proj["mla"][:, :MLA_Q_RANK]
    ckv = proj["mla"][:, MLA_Q_RANK:]
    cq = cq * lax.rsqrt(jnp.mean(cq * cq, axis=-1, keepdims=True) + EPS) * qw_ref[...]
    ckv = ckv * lax.rsqrt(jnp.mean(ckv * ckv, axis=-1, keepdims=True) + EPS) * kvw_ref[...]
    cqb = cq.astype(BF16)
    ckvb = ckv.astype(BF16)
    qf = _dot(cqb, wuq_ref[...])
    kf = _dot(ckvb, wuk_ref[...])
    vf = _dot(ckvb, wuv_ref[...])
    kr = _rope(proj["kr"], tab_ref)
    q_scale = MLA_SCALE * math.log2(math.e)
    for hd in range(MLA_HEADS):
        sl = slice(hd * LANES, (hd + 1) * LANES)
        q_ref[0, hd] = (_rope(qf[:, sl], tab_ref) * q_scale).astype(BF16)
        k_ref[0, hd] = (kf[:, sl] + kr).astype(BF16)
    ones = jnp.ones((ROWS, MXU_N - LANES), BF16)
    for hp in range(MLA_HEADS // 2):
        v_ref[0, hp] = jnp.concatenate(
            [vf[:, hp * LANES:(hp + 1) * LANES].astype(BF16), ones], axis=1)


def _inproj(xs, mod, w_p, wabt, tab, qw, kvw, wuq, wuk, wuv, convw, ob, nct):
    b, d = xs[0].shape[0], xs[0].shape[2]
    ta = sum(a.shape[1] for a in xs)
    nt = ta // ROWS
    widths = (_W_GDN, _W_MLAZ, _W_GLA, LANES)
    row = lambda bi, j: (bi, j, 0)
    full2 = lambda bi, j: (0, 0)
    hm = lambda bi, j: (bi, 0, j, 0)
    consts = (w_p, wabt)
    mla_consts = (qw, kvw, wuq, wuk, wuv, convw, ob)
    return pl.pallas_call(
        functools.partial(_inproj_kernel, nct, len(xs)),
        grid=(b, nt),
        in_specs=_token_specs(xs, nct, d) + _halo_specs(xs, nct, d, False)
        + _halo_specs(xs, nct, d, True)
        + [pl.BlockSpec((1, 3, d), lambda bi, j: (jnp.where(j < nct, 0, 1 + bi), 0, 0))]
        + [pl.BlockSpec(a.shape, full2) for a in consts]
        + [pl.BlockSpec((3, ROWS, LANES), lambda bi, j: (0, j, 0))]
        + [pl.BlockSpec(a.shape, full2) for a in mla_consts],
        out_specs=[pl.BlockSpec((1, ROWS, n), row) for n in widths]
        + [pl.BlockSpec((1, 4 * GDN_HEADS, ROWS), lambda bi, j: (bi, 0, j)),
           pl.BlockSpec((1, MLA_HEADS, ROWS, LANES), hm),
           pl.BlockSpec((1, MLA_HEADS, ROWS, LANES), hm),
           pl.BlockSpec((1, MLA_HEADS // 2, ROWS, MXU_N), hm)],
        out_shape=[jax.ShapeDtypeStruct((b, ta, n), F32) for n in widths]
        + [jax.ShapeDtypeStruct((b, 4 * GDN_HEADS, ta), F32),
           jax.ShapeDtypeStruct((b, MLA_HEADS, ta, LANES), BF16),
           jax.ShapeDtypeStruct((b, MLA_HEADS, ta, LANES), BF16),
           jax.ShapeDtypeStruct((b, MLA_HEADS // 2, ta, MXU_N), BF16)],
        scratch_shapes=[pltpu.VMEM((ROWS + 2 * HALO, GDN_QKV), F32)],
        compiler_params=_params("arbitrary", "arbitrary"),
        name="in_proj",
    )(*xs, *xs, *xs, mod, *consts, tab, *mla_consts)


def _scan_block_index(j, nct, nt, rev):
    if not rev:
        return j
    return jnp.where(j < nct, nct - 1 - j, nt - 1 - (j - nct))


def _head_lane_mask(n, width, h):
    lane = lax.broadcasted_iota(jnp.int32, (1, n), 1)
    return (lane >= h * width) & (lane < (h + 1) * width)


def _gdn_prep(d, x_ref, small_ref, abt_ref, prow_ref, pcol_ref, tri, exp_ref):
    q = x_ref[0, :, :GDN_QK]
    k = x_ref[0, :, GDN_QK:2 * GDN_QK]
    v = x_ref[0, :, 2 * GDN_QK:GDN_QKV]
    sm = small_ref[0]
    g_all = -jnp.exp(prow_ref[0:1, :]) * _softplus(sm + prow_ref[1:2, :])
    beta_all = jax.nn.sigmoid(sm)
    gc_all = _sel_dot(tri, g_all)
    g_t = -jnp.exp(pcol_ref[:, 0:1]) * _softplus(abt_ref[0] + pcol_ref[:, 1:2])
    gc_t = _dot_sel_nt(g_t, tri)
    gc_w = _dot_sel2(gc_all, exp_ref[d])
    beta_w = _dot_sel2(beta_all, exp_ref[2 + d])
    last = 0 if d == 1 else CHUNK - 1

    def chunk_last(x):
        return jnp.concatenate(
            [jnp.broadcast_to(x[c * CHUNK + last:c * CHUNK + last + 1, :], (CHUNK, x.shape[1]))
             for c in range(ROWS // CHUNK)], axis=0)

    gl_all = chunk_last(gc_all)
    gl_w = chunk_last(gc_w)
    kb = k * beta_w
    return dict(q=q, kbf=k.astype(BF16), kb=kb, vb=v * beta_w, kbg=kb * jnp.exp(gc_w),
                qg=q * jnp.exp(gc_w), kdec=k * jnp.exp(gl_w - gc_w),
                gc_all=gc_all, gc_t=gc_t, gl_all=gl_all)


def _gdn_stages(xf_ref, smf_ref, abtf_ref, xb_ref, smb_ref, abtb_ref, prow_ref, pcol_ref, tri_ref,
                ones_ref, negm_ref, lvm_ref, exp_ref, of_ref, obk_ref, s_scr):
    nh, dk = GDN_HEADS, GDN_DK
    nchunk = ROWS // CHUNK
    nlev = len(LEVELS)
    ones = ones_ref[...]
    dir_refs = ((xf_ref, smf_ref, abtf_ref), (xb_ref, smb_ref, abtb_ref))
    prep = []
    for d in range(2):
        prep.append(_gdn_prep(d, *dir_refs[d], prow_ref, pcol_ref, tri_ref[d], exp_ref))
        yield
    ri = lax.broadcasted_iota(jnp.int32, (ROWS, ROWS), 0)
    ci = lax.broadcasted_iota(jnp.int32, (ROWS, ROWS), 1)
    eye = (ri == ci).astype(BF16)
    chains = [(d, h) for d in range(2) for h in range(nh)]
    nc = len(chains)

    def compact(m):
        return functools.reduce(lambda a, b: a + b,
                                [m[c * CHUNK:(c + 1) * CHUNK] for c in range(nchunk)])

    def expand(mc):
        return jnp.concatenate([mc] * nchunk, axis=0) * ones

    def side_by_side(x):
        return jnp.concatenate([x[c * CHUNK:(c + 1) * CHUNK] for c in range(nchunk)], axis=1)

    low_c, a_intra = [], []
    for d, h in chains:
        p = prep[d]
        idx = d * nh + h
        sl = slice(h * dk, (h + 1) * dk)
        gc_col = side_by_side(jnp.broadcast_to(p["gc_all"][:, idx:idx + 1], (ROWS, CHUNK)))
        decay = jnp.exp(gc_col - p["gc_t"][idx:idx + 1, :] + negm_ref[d])
        lhs = jnp.concatenate([side_by_side(p["kb"][:, sl]), side_by_side(p["q"][:, sl])], axis=0)
        kt = jnp.concatenate([p["kbf"][:, sl]] * nchunk, axis=1) * ones
        kq = _dot_nt(lhs.astype(BF16), kt)
        low_c.append((kq[:CHUNK] * decay).astype(BF16))
        a_intra.append(expand((kq[CHUNK:] * decay).astype(BF16)))
        if h == nh - 1:
            yield

    eye_c = compact(eye)
    t_c = [eye_c - low_c[i] * lvm_ref[d, nlev - 1] for i, (d, h) in enumerate(chains)]
    t_inv = [expand(t) for t in t_c]
    for li in reversed(range(nlev - 1)):
        ys = [expand(_dot(low_c[i] * lvm_ref[d, li], t_inv[i]).astype(BF16))
              for i, (d, h) in enumerate(chains)]
        t_c = [t_c[i] - _dot(t_c[i], ys[i]).astype(BF16) for i in range(nc)]
        t_inv = [expand(t) for t in t_c]
        yield
    uw = []
    for i, (d, h) in enumerate(chains):
        sl = slice(h * dk, (h + 1) * dk)
        rhs = jnp.concatenate([prep[d]["vb"][:, sl], prep[d]["kbg"][:, sl]], axis=1)
        uw.append(_dot(t_inv[i], rhs.astype(BF16)))

    q2, ou, gb = [], [], []
    for i, (d, h) in enumerate(chains):
        sl = slice(h * dk, (h + 1) * dk)
        uwb = uw[i].astype(BF16)
        auw = _dot(a_intra[i], uwb)
        ou.append(auw[:, :GDN_DV])
        q2.append((prep[d]["qg"][:, sl] - auw[:, GDN_DV:]).astype(BF16))
        kd = prep[d]["kdec"][:, sl].astype(BF16)
        gb.append([_dot_tn(kd[c * CHUNK:(c + 1) * CHUNK], uwb[c * CHUNK:(c + 1) * CHUNK])
                   for c in range(nchunk)])
    yield
    states = [s_scr[i] for i in range(nc)]
    o_parts = [[None] * nchunk for _ in chains]
    for step in range(nchunk):
        for i, (d, h) in enumerate(chains):
            c = nchunk - 1 - step if d == 1 else step
            idx = d * nh + h
            r0 = c * CHUNK
            rs = slice(r0, r0 + CHUNK)
            sb = states[i].astype(BF16)
            o_parts[i][c] = _dot(q2[i][rs], sb) + ou[i][rs]
            states[i] = (states[i] * jnp.exp(prep[d]["gl_all"][r0:r0 + 1, idx:idx + 1])
                         - _dot(gb[i][c][:, GDN_DV:].astype(BF16), sb) + gb[i][c][:, :GDN_DV])
        yield
    for i in range(len(chains)):
        s_scr[i] = states[i]
    outs = [jnp.concatenate(o_parts[i], axis=0) for i in range(len(chains))]
    of_ref[0] = jnp.concatenate(outs[:nh], axis=1)
    obk_ref[0] = jnp.concatenate(outs[nh:], axis=1)


def _scan_row_specs(nct, nt, rev, width):
    blk_of = functools.partial(_scan_block_index, nct=nct, nt=nt, rev=rev)
    return pl.BlockSpec((1, ROWS, width), lambda bi, j: (bi, blk_of(j), 0))


def _gdn_specs(nct, nt, gdn_in, small, abt, prow, pcol, exp_m):
    cf, cb = _scan_consts(False), _scan_consts(True)
    stack = lambda name: jnp.stack([cf[name], cb[name]])
    full2 = lambda bi, j: (0, 0)
    full3 = lambda bi, j: (0, 0, 0)

    def dir_specs(rev):
        blk_of = functools.partial(_scan_block_index, nct=nct, nt=nt, rev=rev)
        return [
            _scan_row_specs(nct, nt, rev, GDN_QKV),
            _scan_row_specs(nct, nt, rev, LANES),
            pl.BlockSpec((1, 4 * GDN_HEADS, ROWS), lambda bi, j: (bi, 0, blk_of(j))),
        ]

    in_specs = dir_specs(False) + dir_specs(True) + [
        pl.BlockSpec(prow.shape, full2),
        pl.BlockSpec(pcol.shape, full2),
        pl.BlockSpec((2, ROWS, ROWS), full3),
        pl.BlockSpec((ROWS, ROWS), full2),
        pl.BlockSpec((2, CHUNK, ROWS), full3),
        pl.BlockSpec((2, len(LEVELS), CHUNK, ROWS), lambda bi, j: (0, 0, 0, 0)),
        pl.BlockSpec(exp_m.shape, full3),
    ]
    dir_args = [gdn_in, small, abt]
    args = dir_args + dir_args + [prow, pcol, stack("tri"), cf["ones"], stack("negm"),
                                  stack("lvm"), exp_m]
    out_specs = [_scan_row_specs(nct, nt, False, GDN_W), _scan_row_specs(nct, nt, True, GDN_W)]
    scratch = [pltpu.VMEM((2 * GDN_HEADS, GDN_DK, GDN_DV), F32)]
    return in_specs, args, out_specs, scratch


def _gla_stages(xf_ref, smf_ref, xb_ref, smb_ref, wg_ref, bg_ref, stack_ref, lvc_ref, kmask_ref,
                same_ref, eqk_ref, sbd_ref, of_ref, obk_ref, s_scr):
    nh, dk, dv = GLA_HEADS, GLA_DK, GLA_DV
    nchunk = ROWS // CHUNK
    nlev = len(LEVELS)
    n = GLA_QK
    out_head = [_head_lane_mask(GLA_W, dv, h) for h in range(nh)]
    eqk = eqk_ref[...]
    sbd = sbd_ref[...]
    x_refs, sm_refs, o_refs = (xf_ref, xb_ref), (smf_ref, smb_ref), (of_ref, obk_ref)
    dirs = range(2)
    xs = [x_refs[d][0] for d in dirs]
    q = [xs[d][:, :GLA_QK] * (dk ** -0.5) for d in dirs]
    k = [xs[d][:, GLA_QK:2 * GLA_QK] for d in dirs]
    v = [xs[d][:, 2 * GLA_QK:2 * GLA_QK + GLA_W] for d in dirs]
    vb = [v[d].astype(BF16) for d in dirs]
    gk = [_dot(sm_refs[d][0].astype(BF16), wg_ref[d]) + bg_ref[d] for d in dirs]
    la = [-_softplus(-gk[d]) * (1.0 / GLA_GATE_NORM) for d in dirs]
    la2 = [jnp.concatenate(_split2(la[d]), axis=1) for d in dirs]
    ys = [_dot(stack_ref[d], la2[d]) for d in dirs]
    cums = [ys[d][:, :n] + ys[d][:, n:] for d in dirs]
    piece = lambda d, i: cums[d][i * ROWS:(i + 1) * ROWS]
    bcum = [piece(d, 0) for d in dirs]
    last_row = [0 if d == 1 else CHUNK - 1 for d in dirs]
    blast = [jnp.concatenate(
        [jnp.broadcast_to(bcum[d][c * CHUNK + last_row[d]:c * CHUNK + last_row[d] + 1, :], (CHUNK, n))
         for c in range(nchunk)], axis=0) for d in dirs]
    la_next = [pltpu.roll(la[d], 1 if d == 1 else ROWS - 1, 0) for d in dirs]
    qg = [(q[d] * jnp.exp(bcum[d])).astype(BF16) for d in dirs]
    kdec = [(k[d] * jnp.exp(blast[d] - bcum[d])).astype(BF16) for d in dirs]
    yield

    lane4 = lax.broadcasted_iota(jnp.int32, (1, nchunk * n), 1) % n
    lane_head4 = [(lane4 >= h * dk) & (lane4 < (h + 1) * dk) for h in range(nh)]
    kmask = kmask_ref[...]
    same = same_ref[...]
    lv4 = [jnp.concatenate([lvc_ref[d]] * nh, axis=0) for d in dirs]
    acc = [jnp.zeros((nh * CHUNK, ROWS), F32) for d in dirs]
    for li in range(nlev):
        if li < nlev - 1:
            ql = [q[d] * jnp.exp(piece(d, 1 + li)) for d in dirs]
            kl = [(k[d] * jnp.exp(piece(d, nlev + li))).astype(BF16) for d in dirs]
        else:
            ql = q
            kl = [(k[d] * jnp.exp(la_next[d])).astype(BF16) for d in dirs]
        qc = [jnp.concatenate([ql[d][c * CHUNK:(c + 1) * CHUNK] for c in range(nchunk)], axis=1)
              for d in dirs]
        qs = [jnp.concatenate([jnp.where(lane_head4[h], qc[d], 0.0) for h in range(nh)],
                              axis=0).astype(BF16) for d in dirs]
        kt = [jnp.concatenate([kl[d]] * nchunk, axis=1) * kmask for d in dirs]
        ps = [_dot_nt(qs[d], kt[d]) for d in dirs]
        acc = [jnp.where(lv4[d] == float(li), ps[d], acc[d]) for d in dirs]
        yield
    accb = [acc[d].astype(BF16) for d in dirs]
    acc_bd = [jnp.concatenate(
        [jnp.concatenate([accb[d][h * CHUNK:(h + 1) * CHUNK]] * nchunk, axis=0) * same
         for h in range(nh)], axis=0) for d in dirs]
    pv = [_dot(acc_bd[d], vb[d]) for d in dirs]
    o = [_dot_sel2(q[d] * k[d], eqk) * v[d] for d in dirs]
    for h in range(nh):
        o = [o[d] + jnp.where(out_head[h], pv[d][h * ROWS:(h + 1) * ROWS], 0.0) for d in dirs]
    yield

    states = [s_scr[d] for d in dirs]
    o_parts = [[None] * nchunk for d in dirs]
    for step in range(nchunk):
        for d in dirs:
            c = nchunk - 1 - step if d == 1 else step
            rs = slice(c * CHUNK, (c + 1) * CHUNK)
            o_parts[d][c] = _dot_nt(qg[d][rs], states[d].astype(BF16))
            states[d] = (states[d] * jnp.exp(blast[d][c * CHUNK:c * CHUNK + 1, :])
                         + sbd * _dot_tn(vb[d][rs], kdec[d][rs]))
        yield
    for d in dirs:
        s_scr[d] = states[d]
        o_refs[d][0] = o[d] + jnp.concatenate(o_parts[d], axis=0)


def _gla_specs(nct, nt, gla_in, small, wg, bg, eqk, sbd):
    cf, cb = _scan_consts(False), _scan_consts(True)
    stack = jnp.stack([cf["stack"], cb["stack"]])
    lvc = jnp.stack([cf["lvc"], cb["lvc"]])
    assert GLA_QK == LANES
    kmask = _chunk_lane_mask()
    full2 = lambda bi, j: (0, 0)
    full3 = lambda bi, j: (0, 0, 0)
    in_specs = [
        _scan_row_specs(nct, nt, False, _W_GLA), _scan_row_specs(nct, nt, False, LANES),
        _scan_row_specs(nct, nt, True, _W_GLA), _scan_row_specs(nct, nt, True, LANES),
        pl.BlockSpec(wg.shape, full3),
        pl.BlockSpec(bg.shape, full3),
        pl.BlockSpec(stack.shape, full3),
        pl.BlockSpec(lvc.shape, full3),
        pl.BlockSpec(kmask.shape, full2),
        pl.BlockSpec((ROWS, ROWS), full2),
        pl.BlockSpec(eqk.shape, full2),
        pl.BlockSpec(sbd.shape, full2),
    ]
    args = [gla_in, small, gla_in, small, wg, bg, stack, lvc, kmask, cf["ones"], eqk, sbd]
    out_specs = [_scan_row_specs(nct, nt, False, GLA_W), _scan_row_specs(nct, nt, True, GLA_W)]
    scratch = [pltpu.VMEM((2, GLA_W, GLA_QK), F32)]
    return in_specs, args, out_specs, scratch


_N_GDN_IN, _N_GLA_IN = 13, 12


_N_GDN_ROWS_IN, _N_GLA_ROWS_IN = 6, 4


def _scan_kernel(nb, *refs):
    gdn_in, refs = refs[:_N_GDN_IN], refs[_N_GDN_IN:]
    gla_in, refs = refs[:_N_GLA_IN], refs[_N_GLA_IN:]
    gdn_out, gla_out, (gdn_state, gla_state) = refs[:2], refs[2:4], refs[4:]

    @pl.when(pl.program_id(1) == 0)
    def _():
        gdn_state[...] = jnp.zeros_like(gdn_state)
        gla_state[...] = jnp.zeros_like(gla_state)

    def of_batch(refs, bb, n_rows):
        return [r.at[pl.ds(bb, 1)] if i < n_rows else r for i, r in enumerate(refs)]

    stages = []
    for bb in range(nb):
        stages.append(_gdn_stages(*of_batch(gdn_in, bb, _N_GDN_ROWS_IN),
                                  *of_batch(gdn_out, bb, 2), gdn_state.at[bb]))
        stages.append(_gla_stages(*of_batch(gla_in, bb, _N_GLA_ROWS_IN),
                                  *of_batch(gla_out, bb, 2), gla_state.at[bb]))
    while stages:
        for s in list(stages):
            if next(s, StopIteration) is StopIteration:
                stages.remove(s)


def _scans(nct, gdn_args, gla_args):
    gdn_in = gdn_args[0]
    b, ta, _ = gdn_in.shape
    nt = ta // ROWS
    g_in, g_args, g_out, g_scr = _gdn_specs(nct, nt, *gdn_args)
    a_in, a_args, a_out, a_scr = _gla_specs(nct, nt, *gla_args)
    assert len(g_in) == _N_GDN_IN and len(a_in) == _N_GLA_IN
    nb = SCAN_BATCH if b % SCAN_BATCH == 0 else 1

    def widen(specs, n_rows):
        return [pl.BlockSpec((nb,) + tuple(s.block_shape[1:]), s.index_map) if i < n_rows else s
                for i, s in enumerate(specs)]

    return pl.pallas_call(
        functools.partial(_scan_kernel, nb),
        grid=(b // nb, nt),
        in_specs=widen(g_in, _N_GDN_ROWS_IN) + widen(a_in, _N_GLA_ROWS_IN),
        out_specs=widen(g_out, 2) + widen(a_out, 2),
        out_shape=[jax.ShapeDtypeStruct((b, ta, GDN_W), F32)] * 2
        + [jax.ShapeDtypeStruct((b, ta, GLA_W), F32)] * 2,
        scratch_shapes=[pltpu.VMEM((nb,) + tuple(s.shape), s.dtype) for s in g_scr + a_scr],
        compiler_params=_params("arbitrary", "arbitrary"),
        name="gdn_gla_scan",
    )(*g_args, *a_args)


def _attn_kernel(nct, off, n_ctx, q_ref, k_ref, v_ref, z_ref, o_ref):
    i = pl.program_id(2) + off

    def body(nk):
        parts = []
        half = ROWS // 2

        def by_row_halves(dot, lhs, rhs):
            return jnp.concatenate([dot(lhs[:half], rhs), dot(lhs[half:], rhs)], axis=0)

        for hh in range(ATTN_HEADS):
            q = q_ref[0, hh]
            kk = k_ref[0, hh, :nk, :]
            s = by_row_halves(_dot_nt, q, kk) if hh == 0 else _dot_nt(q, kk)
            m = jnp.max(s, axis=-1, keepdims=True)
            p = jnp.exp2(s - m).astype(BF16)
            vv = v_ref[0, hh // 2, :nk, :]
            pv = by_row_halves(_dot, p, vv) if hh == ATTN_HEADS - 1 else _dot(p, vv)
            o = pv[:, (hh % 2) * MLA_DV:(hh % 2 + 1) * MLA_DV] / pv[:, LANES:LANES + 1]
            parts.append(o)
        o_ref[0] = (jnp.concatenate(parts, axis=1) * _silu(z_ref[0])).astype(BF16)

    if off == 0:
        @pl.when(i < nct)
        def _():
            body(n_ctx)

        @pl.when(i >= nct)
        def _():
            body(k_ref.shape[2])
    else:
        body(k_ref.shape[2])


def _attention(q, k, v, z, nct, with_ctx):
    b, nh, ta, _ = q.shape
    nt = ta // ROWS
    off = 0 if with_ctx else nct
    nq = nt - off
    hg = ATTN_HEADS
    wo = hg * MLA_DV
    return pl.pallas_call(
        functools.partial(_attn_kernel, nct, off, nct * ROWS),
        grid=(b, nh // hg, nq),
        in_specs=[pl.BlockSpec((1, hg, ROWS, LANES), lambda bi, hp, i: (bi, hp, i + off, 0)),
                  pl.BlockSpec((1, hg, ta, LANES), lambda bi, hp, i: (bi, hp, 0, 0)),
                  pl.BlockSpec((1, hg // 2, ta, MXU_N), lambda bi, hp, i: (bi, hp, 0, 0)),
                  pl.BlockSpec((1, ROWS, wo), lambda bi, hp, i: (bi, i + off, hp))],
        out_specs=pl.BlockSpec((1, ROWS, wo), lambda bi, hp, i: (bi, i, hp)),
        out_shape=jax.ShapeDtypeStruct((b, nq * ROWS, MLA_W), BF16),
        compiler_params=_params("arbitrary", "arbitrary", "arbitrary"),
        name="mla_attn",
    )(q, k, v, z)


def _outproj_kernel(final, nct, nx, off, nb, gf_ref, gb_ref, gz_ref, gnw_ref, af_ref, ab_ref, az_ref,
                    anw_ref, ob_ref, m_ref, mod_ref, w_ref, *rest):
    x_refs, rest = rest[:nx], rest[nx:]
    if final:
        fw_ref, o_ref = rest
    else:
        (o_ref,) = rest
    ob = ob_ref[...]
    is_ctx = pl.program_id(1) + off < nct

    def gated_head_norm(o, nw_ref, z):
        ms = _dot_sel2(o * o, ob) * (1.0 / GDN_DV)
        return (o * lax.rsqrt(ms + EPS) * nw_ref[...] * _silu(z)).astype(BF16)

    for bb in range(nb):
        at = lambda r: r.at[pl.ds(bb, 1)]
        g = gated_head_norm(at(gf_ref)[0] + at(gb_ref)[0], gnw_ref, at(gz_ref)[0])
        a = gated_head_norm(at(af_ref)[0] + at(ab_ref)[0], anw_ref, at(az_ref)[0])
        y = _dot(g, w_ref[0:GDN_W, :])
        y = y + _dot(at(m_ref)[0], w_ref[GDN_W:GDN_W + MLA_W, :])
        y = y + _dot(a, w_ref[GDN_W + MLA_W:, :])
        mod_row = jnp.where(is_ctx, 0, 1 + pl.program_id(0) * nb + bb)
        gate = mod_ref[pl.ds(mod_row, 1), 2:3, :][0]
        xn = _token_block(nct, [at(r) for r in x_refs]) + gate * y
        if final:
            xn = xn * lax.rsqrt(jnp.mean(xn * xn, axis=-1, keepdims=True) + EPS) * fw_ref[...]
        o_ref[bb] = xn


def _outproj(gdn_f, gdn_b, gdn_in, gnw, gla_f, gla_b, gla_in, anw, ob, mla_o, xs, mod, w_out,
             nct, final_w):
    assert GDN_DV == GLA_DV and GDN_W == GLA_W
    b, d = xs[0].shape[0], xs[0].shape[2]
    ta = sum(a.shape[1] for a in xs)
    nt = ta // ROWS
    final = final_w is not None
    assert not (final and len(xs) > 1)
    off = nct if final else 0
    moff = off - (ta - mla_o.shape[1]) // ROWS
    nq = nt - off
    row = lambda bi, j: (bi, j + off, 0)
    full2 = lambda bi, j: (0, 0)
    stream = pl.BlockSpec
    x_specs = [stream((1, ROWS, d), row)] if len(xs) == 1 else _token_specs(xs, nct, d)
    wide = stream((1, ROWS, GDN_W), row)
    gz_blk = GDN_QKV // GDN_W
    az_blk = (2 * GLA_QK + GLA_W) // GLA_W
    in_specs = [wide, wide,
                stream((1, ROWS, GDN_W), lambda bi, j: (bi, j + off, gz_blk)),
                pl.BlockSpec(gnw.shape, full2),
                wide, wide,
                stream((1, ROWS, GLA_W), lambda bi, j: (bi, j + off, az_blk)),
                pl.BlockSpec(anw.shape, full2),
                pl.BlockSpec(ob.shape, full2),
                stream((1, ROWS, MLA_W), lambda bi, j: (bi, j + moff, 0)),
                pl.BlockSpec(mod.shape, lambda bi, j: (0, 0, 0)),
                pl.BlockSpec(w_out.shape, full2)] + x_specs
    args = [gdn_f, gdn_b, gdn_in, gnw, gla_f, gla_b, gla_in, anw, ob, mla_o, mod, w_out, *xs]
    if final:
        in_specs.append(pl.BlockSpec(final_w.shape, full2))
        args.append(final_w)
    nb = OUTPROJ_BATCH if b % OUTPROJ_BATCH == 0 else 1
    in_specs = [pl.BlockSpec((nb,) + tuple(sp.block_shape[1:]), sp.index_map)
                if len(sp.block_shape) == 3 and sp.block_shape[1] == ROWS else sp for sp in in_specs]
    return pl.pallas_call(
        functools.partial(_outproj_kernel, final, nct, len(xs), off, nb),
        grid=(b // nb, nq),
        in_specs=in_specs,
        out_specs=pl.BlockSpec((nb, ROWS, d), lambda bi, j: (bi, j, 0)),
        out_shape=jax.ShapeDtypeStruct((b, nq * ROWS, d), F32),
        compiler_params=_params("arbitrary", "arbitrary"),
        name="out_proj",
    )(*args)


def kernel(x, c, ctx, c_ctx, w_ada, b_ada, w_in, gdn_conv_w, gdn_a_log, gdn_dt_bias, gdn_norm_w,
           mla_q_norm_w, mla_w_uq, mla_kv_norm_w, mla_w_ukv, gla_w_gk, gla_b_gk, gla_norm_w,
           w_out, final_norm_w):
    b, seq, d = x.shape
    n_ctx = ctx.shape[1]
    depth = w_in.shape[0]
    assert n_ctx % ROWS == 0 and seq % ROWS == 0 and seq % ROPE_GRID_W == 0
    nct = n_ctx // ROWS
    nh = GDN_HEADS

    xs = (ctx, x)
    pad_rows = (-(1 + b)) % 8
    cc = jnp.concatenate([c_ctx[None, :], c, jnp.zeros((pad_rows, d), F32)], axis=0)
    mod_all = _ada(cc, w_ada, b_ada).reshape(depth, cc.shape[0], 3, d)

    perm = _inproj_perm()
    qperm = _mla_q_perm()
    kperm, vperm = _mla_kv_perm()
    tab = _rope_tables(n_ctx, seq)
    ob64 = _head_block_ones(GDN_W, GDN_DV)
    exp_m = jnp.asarray(np.stack(
        [_expand_matrix(_S_A + dd * nh, nh, GDN_DK, GDN_QK) for dd in range(2)]
        + [_expand_matrix(_S_B + dd * nh, nh, GDN_DK, GDN_QK) for dd in range(2)]), BF16)
    eqk_np = np.zeros((GLA_QK, GLA_W), np.float32)
    sbd_np = np.zeros((GLA_W, GLA_QK), np.float32)
    for h in range(GLA_HEADS):
        eqk_np[h * GLA_DK:(h + 1) * GLA_DK, h * GLA_DV:(h + 1) * GLA_DV] = 1.0
        sbd_np[h * GLA_DV:(h + 1) * GLA_DV, h * GLA_DK:(h + 1) * GLA_DK] = 1.0
    eqk = jnp.asarray(eqk_np, BF16)
    sbd = jnp.asarray(sbd_np)

    out = None
    for layer in range(depth):
        last = layer == depth - 1
        w_p = _take_cols(w_in[layer], perm).astype(BF16)
        wabt = w_in[layer][:, _O_A:_O_A + 4 * nh].T.astype(BF16)
        mod = mod_all[layer]
        wuq = _take_cols(mla_w_uq[layer], qperm).astype(BF16)
        wuk = _take_cols(mla_w_ukv[layer], kperm).astype(BF16)
        wuv = jnp.take(mla_w_ukv[layer], jnp.asarray(vperm), axis=1).astype(BF16)
        convw = jnp.concatenate(
            [gdn_conv_w[layer], jnp.zeros((8 - GDN_CONV, GDN_QKV), F32)], axis=0)
        gdn_in, mla_z, gla_in, small, abt, qh, kh, vh = _inproj(
            xs, mod, w_p, wabt, tab, mla_q_norm_w[layer][None, :], mla_kv_norm_w[layer][None, :],
            wuq, wuk, wuv, convw, ob64, nct)

        a_flat = gdn_a_log[layer].reshape(-1)
        dt_flat = gdn_dt_bias[layer].reshape(-1)
        prow = jnp.zeros((8, LANES), F32)
        prow = prow.at[0, _S_A:_S_A + 2 * nh].set(a_flat).at[1, _S_A:_S_A + 2 * nh].set(dt_flat)
        pcol = jnp.zeros((4 * nh, LANES), F32)
        pcol = pcol.at[0:2 * nh, 0].set(a_flat).at[0:2 * nh, 1].set(dt_flat)
        gnw = jnp.tile(gdn_norm_w[layer], nh)[None, :]

        anw = jnp.tile(gla_norm_w[layer], GLA_HEADS)[None, :]
        wg = jnp.zeros((2, LANES, GLA_QK), F32)
        for dd in range(2):
            r0 = _S_GLOW + dd * GLA_GATE_RANK
            wg = wg.at[dd, r0:r0 + GLA_GATE_RANK, :].set(gla_w_gk[layer, dd])
        gdn_f, gdn_b, gla_f, gla_b = _scans(
            nct, (gdn_in, small, abt, prow, pcol, exp_m),
            (gla_in, small, wg.astype(BF16), gla_b_gk[layer][:, None, :], eqk, sbd))

        mla_o = _attention(qh, kh, vh, mla_z, nct, with_ctx=not last)

        if last and len(xs) > 1:
            xs = (jnp.concatenate(xs, axis=1),)
        res = _outproj(gdn_f, gdn_b, gdn_in, gnw, gla_f, gla_b, gla_in, anw, ob64, mla_o, xs,
                       mod, w_out[layer].astype(BF16), nct, final_norm_w[None, :] if last else None)
        if last:
            out = res
        else:
            xs = (res,)
    return out
```
